```python
import math
import jax
import jax.numpy as jnp
from jax import lax
import numpy as np

D_MODEL = 4096
BATCH = 4
SEQ = 2048
DEPTH = 2
DEC_BATCH = 8
DEC_SEQ = 8
PAST_LEN = 16384
PAGE_SIZE = 128

GROUP_W = D_MODEL // 4
MIX_W = 4 * GROUP_W
GDN_HEADS = 8
GDN_DK = GROUP_W // GDN_HEADS
GDN_DV = GROUP_W // GDN_HEADS
GDN_QKV = 3 * GROUP_W
GDN_CONV = 4
GDN_CHUNK = 64
RWKV_HS = 64
RWKV_HEADS = GROUP_W // RWKV_HS
RWKV_W_LORA = 64
RWKV_A_LORA = 64
RWKV_G_LORA = 160
RWKV_PROJ = 3 * GROUP_W + RWKV_W_LORA + RWKV_A_LORA + RWKV_G_LORA
RWKV_GN_EPS = 64e-5
SSM_HEADDIM = 64
SSM_HEADS = GROUP_W // SSM_HEADDIM
SSM_GROUPS = 2
SSM_STATE = 128
SSM_CONV = 4
SSM_CHUNK = 128
SSM_XBC = GROUP_W + 2 * SSM_GROUPS * SSM_STATE
SWA_HEADS = 8
SWA_HD = GROUP_W // SWA_HEADS
SWA_PATTERNS = ((128, 1), (512, 4), (2048, 16))
SWA_MAX_WINDOW = 2048
D_FF = 256 * ((8 * D_MODEL // 3 + 255) // 256)
FFN_CONV = 3
NORM_EPS = 1e-6
NEG_INF = -1e30
IN_SIZES = (GDN_QKV, GROUP_W, GDN_HEADS, GDN_HEADS, RWKV_PROJ, GROUP_W, SSM_XBC, SSM_HEADS, 3 * GROUP_W)
N_IN = sum(IN_SIZES)

kernel_name = 'hybrid_parallel_heads_decode_step'


def rmsnorm(x, w):
    x32 = x.astype(jnp.float32)
    y = x32 * lax.rsqrt(jnp.mean(x32 * x32, axis=-1, keepdims=True) + NORM_EPS)
    return (y * w.astype(jnp.float32)).astype(x.dtype)


def l2norm(x):
    return x * lax.rsqrt(jnp.sum(x * x, axis=-1, keepdims=True) + 1e-6)


def split_cols(h, sizes):
    out, o = [], 0
    for s in sizes:
        out.append(h[..., o:o + s])
        o += s
    return out


def causal_dwconv(x, buf, w):
    n_tap, l = w.shape[0], x.shape[1]
    xp = jnp.concatenate([buf.astype(x.dtype), x], axis=1)
    y = xp[:, 0:l] * w[0]
    for i in range(1, n_tap):
        y = y + xp[:, i:i + l] * w[i]
    return y, xp[:, l:]


def alibi_slopes():
    return jnp.asarray(2.0 ** (-8.0 * np.arange(1, SWA_HEADS + 1) / SWA_HEADS), dtype=jnp.float32)


def gated_delta_chunked(q, k, v, g, beta, S0):
    bsz, l, H, dk = q.shape
    dv = v.shape[-1]
    C = GDN_CHUNK
    n = -(-l // C)
    pad = n * C - l

    def blocks(t):
        t = jnp.pad(t, [(0, 0), (0, pad)] + [(0, 0)] * (t.ndim - 2))
        t = jnp.moveaxis(t.reshape(bsz, n, C, *t.shape[2:]), 2, 3)
        return jnp.moveaxis(t, 1, 0)

    qc, kc, vc, gc, bc = [blocks(t) for t in (q, k, v, g, beta)]
    gcum = jnp.cumsum(gc, axis=-1)
    tri = jnp.tril(jnp.ones((C, C), bool))
    strict = jnp.tril(jnp.ones((C, C), bool), -1)
    gam = jnp.exp(jnp.where(tri, gcum[..., :, None] - gcum[..., None, :], -jnp.inf))
    kbeta = kc * bc[..., None]
    M = jnp.where(strict, jnp.einsum('nbhid,nbhjd->nbhij', kbeta, kc) * gam, 0.0)
    A = M + jnp.eye(C, dtype=jnp.float32)
    W = lax.linalg.triangular_solve(A, kbeta * jnp.exp(gcum)[..., None], left_side=True, lower=True, unit_diagonal=True)
    U = lax.linalg.triangular_solve(A, vc * bc[..., None], left_side=True, lower=True, unit_diagonal=True)
    Aqk = jnp.einsum('nbhid,nbhjd->nbhij', qc, kc) * gam
    qd = qc * jnp.exp(gcum)[..., None]
    kd = kc * jnp.exp(gcum[..., -1:] - gcum)[..., None]
    glast = jnp.exp(gcum[..., -1])

    def step(S, inp):
        W_i, U_i, Aqk_i, qd_i, kd_i, gl_i = inp
        v_new = U_i - jnp.einsum('bhcd,bhde->bhce', W_i, S)
        o = jnp.einsum('bhcd,bhde->bhce', qd_i, S) + jnp.einsum('bhij,bhje->bhie', Aqk_i, v_new)
        S = S * gl_i[..., None, None] + jnp.einsum('bhcd,bhce->bhde', kd_i, v_new)
        return S, o

    S, o = lax.scan(step, S0.astype(jnp.float32), (W, U, Aqk, qd, kd, glast))
    o = jnp.moveaxis(jnp.moveaxis(o, 0, 1), 2, 3).reshape(bsz, n * C, H, dv)[:, :l]
    return o, S


def gdn_mixer(qkv, z, b_raw, a_raw, conv_buf, S0, conv_w, A_log, dt_bias, norm_w):
    bsz, l, _ = qkv.shape
    y, new_buf = causal_dwconv(qkv, conv_buf, conv_w)
    y = jax.nn.silu(y.astype(jnp.float32))
    q, k, v = split_cols(y, (GROUP_W, GROUP_W, GROUP_W))
    q = l2norm(q.reshape(bsz, l, GDN_HEADS, GDN_DK)) * (GDN_DK ** -0.5)
    k = l2norm(k.reshape(bsz, l, GDN_HEADS, GDN_DK))
    v = v.reshape(bsz, l, GDN_HEADS, GDN_DV)
    beta = jax.nn.sigmoid(b_raw.astype(jnp.float32))
    g = -jnp.exp(A_log.astype(jnp.float32)) * jax.nn.softplus(a_raw.astype(jnp.float32) + dt_bias)
    o, S = gated_delta_chunked(q, k, v, g, beta, S0)
    o = o * lax.rsqrt(jnp.mean(o * o, -1, keepdims=True) + NORM_EPS) * norm_w
    o = o * jax.nn.silu(z.astype(jnp.float32).reshape(bsz, l, GDN_HEADS, GDN_DV))
    return o.reshape(bsz, l, GROUP_W), new_buf, S


def rwkv7_scan(r, w, k, v, a, b, S0):
    def step(S, inp):
        r_t, w_t, k_t, v_t, a_t, b_t = inp
        sa = jnp.einsum('bhij,bhj->bhi', S, a_t)
        S = S * w_t[:, :, None, :] + sa[..., None] * b_t[:, :, None, :] + v_t[..., None] * k_t[:, :, None, :]
        return S, jnp.einsum('bhij,bhj->bhi', S, r_t)

    xs = tuple(jnp.moveaxis(t, 1, 0) for t in (r, w, k, v, a, b))
    S, ys = lax.scan(step, S0.astype(jnp.float32), xs)
    return jnp.moveaxis(ys, 0, 1), S


def rwkv7_mixer(zb, shift_prev, S0, mu, w0, w2, a0, a2, g2, k_k, k_a, r_k, ln_w, ln_b):
    bsz, l, _ = zb.shape
    z32 = zb.astype(jnp.float32)
    prev = jnp.concatenate([shift_prev[:, None].astype(jnp.float32), z32[:, :-1]], axis=1)
    zm = z32 + (prev - z32) * mu
    r, k, v, wd, ad, gd = split_cols(zm, (GROUP_W, GROUP_W, GROUP_W, RWKV_W_LORA, RWKV_A_LORA, RWKV_G_LORA))
    w_log = -jax.nn.softplus(-(w0 + jnp.tanh(wd) @ w2)) - 0.5
    decay = jnp.exp(-jnp.exp(w_log))
    a = jax.nn.sigmoid(a0 + ad @ a2)
    gate = jax.nn.sigmoid(gd) @ g2

    def hs(t):
        return t.reshape(bsz, l, RWKV_HEADS, RWKV_HS)

    kk = l2norm(hs(k * k_k))
    k = k * (1.0 + (a - 1.0) * k_a)
    r_h, k_h, v_h, a_h = hs(r), hs(k), hs(v), hs(a)
    y, S = rwkv7_scan(r_h, hs(decay), k_h, v_h, -kk, kk * a_h, S0)
    mean = jnp.mean(y, -1, keepdims=True)
    var = jnp.mean(jnp.square(y - mean), -1, keepdims=True)
    yn = ((y - mean) * lax.rsqrt(var + RWKV_GN_EPS)).reshape(bsz, l, GROUP_W) * ln_w + ln_b
    bonus = jnp.sum(r_h * k_h * r_k, -1, keepdims=True) * v_h
    out = (yn + bonus.reshape(bsz, l, GROUP_W)) * gate
    return out, zb[:, -1], S


def ssd_chunked(X, dA, Bm, Cm, S0):
    bsz, l, H, P = X.shape
    G, N = Bm.shape[2], Bm.shape[3]
    R = H // G
    Q = SSM_CHUNK
    n = -(-l // Q)
    pad = n * Q - l

    def padl(t):
        return jnp.pad(t, [(0, 0), (0, pad)] + [(0, 0)] * (t.ndim - 2))

    Xc = padl(X).reshape(bsz, n, Q, G, R, P)
    Ac = padl(dA).reshape(bsz, n, Q, G, R)
    Bc = padl(Bm).reshape(bsz, n, Q, G, N)
    Cc = padl(Cm).reshape(bsz, n, Q, G, N)
    Acs = jnp.cumsum(Ac, axis=2)
    Acs_t = jnp.moveaxis(Acs, 2, -1)
    tri = jnp.tril(jnp.ones((Q, Q), bool))
    Lmat = jnp.exp(jnp.where(tri, Acs_t[..., :, None] - Acs_t[..., None, :], -jnp.inf))
    CB = jnp.einsum('bclgn,bcsgn->bcgls', Cc, Bc)
    y_diag = jnp.einsum('bcgls,bcgrls,bcsgrp->bclgrp', CB, Lmat, Xc)
    decay_st = jnp.exp(Acs[:, :, -1:] - Acs)
    chunk_states = jnp.einsum('bclgn,bclgr,bclgrp->bcgrpn', Bc, decay_st, Xc)
    chunk_decay = jnp.exp(Acs[:, :, -1])

    def step(S, inp):
        st, dec = inp
        return S * dec[..., None, None] + st, S

    S_fin, S_in = lax.scan(step, S0.reshape(bsz, G, R, P, N).astype(jnp.float32),
                           (jnp.moveaxis(chunk_states, 1, 0), jnp.moveaxis(chunk_decay, 1, 0)))
    S_in = jnp.moveaxis(S_in, 0, 1)
    y_off = jnp.einsum('bclgn,bcgrpn,bclgr->bclgrp', Cc, S_in, jnp.exp(Acs))
    y = (y_diag + y_off).reshape(bsz, n * Q, H, P)[:, :l]
    return y, S_fin.reshape(bsz, H, P, N)


def mamba2_mixer(z, xbc, dt_raw, conv_buf, S0, conv_w, conv_b, dt_bias, A_log, d_skip, norm_w):
    bsz, l, _ = xbc.shape
    y, new_buf = causal_dwconv(xbc, conv_buf, conv_w)
    y = jax.nn.silu((y + conv_b).astype(jnp.float32))
    xs, Bm, Cm = split_cols(y, (GROUP_W, SSM_GROUPS * SSM_STATE, SSM_GROUPS * SSM_STATE))
    xs = xs.reshape(bsz, l, SSM_HEADS, SSM_HEADDIM)
    Bm = Bm.reshape(bsz, l, SSM_GROUPS, SSM_STATE)
    Cm = Cm.reshape(bsz, l, SSM_GROUPS, SSM_STATE)
    dt = jax.nn.softplus(dt_raw.astype(jnp.float32) + dt_bias)
    A = -jnp.exp(A_log.astype(jnp.float32))
    yh, S = ssd_chunked(xs * dt[..., None], dt * A, Bm, Cm, S0)
    yh = yh + xs * d_skip[:, None]
    yg = yh.reshape(bsz, l, GROUP_W) * jax.nn.silu(z.astype(jnp.float32))
    yg = yg.reshape(bsz, l, SSM_GROUPS, GROUP_W // SSM_GROUPS)
    yg = yg * lax.rsqrt(jnp.mean(yg * yg, -1, keepdims=True) + NORM_EPS)
    return yg.reshape(bsz, l, GROUP_W) * norm_w, new_buf, S


def dilated_band(q, k, v, slopes, window, dil):
    bsz, S, H, E = q.shape
    band = window // dil
    L = S // dil
    nb = -(-L // band)
    Lp = nb * band
    q, k, v = [t.astype(jnp.float32) for t in (q, k, v)]

    def sub(t):
        return jnp.moveaxis(t.reshape(bsz, L, dil, H, E), 2, 1)

    def pad_seq(t, front, back):
        return jnp.pad(t, ((0, 0), (0, 0), (front, back), (0, 0), (0, 0)))

    qb = pad_seq(sub(q), 0, Lp - L).reshape(bsz, dil, nb, band, H, E)

    def key_blocks(t):
        tp = pad_seq(sub(t), band, Lp - L).reshape(bsz, dil, nb + 1, band, H, E)
        return jnp.concatenate([tp[:, :, :-1], tp[:, :, 1:]], axis=3)

    kb, vb = key_blocks(k), key_blocks(v)
    s = jnp.einsum('bdnqhe,bdnkhe->bdnhqk', qb, kb) * (E ** -0.5)
    qi = jnp.arange(band)[:, None]
    ki = jnp.arange(2 * band)[None, :]
    dist = qi + band - ki
    blk = jnp.arange(nb)[:, None, None]
    valid = (dist >= 0) & (dist <= band) & (blk * band - band + ki >= 0)
    s = s - slopes[:, None, None] * (dist * dil).astype(jnp.float32)
    s = jnp.where(valid[:, None], s, NEG_INF)
    m = jnp.max(s, axis=-1)
    p = jnp.exp(s - m[..., None])
    den = jnp.sum(p, axis=-1)
    num = jnp.einsum('bdnhqk,bdnkhe->bdnqhe', p, vb)

    def unblock(t):
        t = t.reshape(bsz, dil, Lp, *t.shape[4:])[:, :, :L]
        return jnp.moveaxis(t, 1, 2).reshape(bsz, S, *t.shape[3:])

    return unblock(jnp.moveaxis(m, 3, 4)), unblock(jnp.moveaxis(den, 3, 4)), unblock(num)


def dilated_sample(q, k_all, v_all, slopes, wb):
    bsz, T, H, E = q.shape
    q = q.astype(jnp.float32)
    parts = []
    for window, dil in SWA_PATTERNS:
        steps = np.arange(window // dil + 1)
        idx = wb + np.arange(T)[:, None] - steps[None, :] * dil
        valid = jnp.asarray(idx >= 0)
        idx = np.maximum(idx, 0)
        kg = k_all[:, idx].astype(jnp.float32)
        vg = v_all[:, idx].astype(jnp.float32)
        s = jnp.einsum('bthe,btnhe->bthn', q, kg) * (E ** -0.5)
        s = s - slopes[:, None] * jnp.asarray(steps * dil, jnp.float32)
        s = jnp.where(valid[:, None, :], s, NEG_INF)
        m = jnp.max(s, axis=-1)
        p = jnp.exp(s - m[..., None])
        parts.append((m, jnp.sum(p, axis=-1), jnp.einsum('bthn,btnhe->bthe', p, vg)))
    return parts


def combine_groups(parts):
    M = parts[0][0]
    for m, _, _ in parts[1:]:
        M = jnp.maximum(M, m)
    num, den = 0.0, 0.0
    for m, s, n in parts:
        c = jnp.exp(m - M)
        num = num + c[..., None] * n
        den = den + c * s
    return num / den[..., None]


def zero_past(bsz, dtype):
    f32 = jnp.float32
    return {
        'gdn': jnp.zeros((bsz, GDN_HEADS, GDN_DK, GDN_DV), f32),
        'gdn_conv': jnp.zeros((bsz, GDN_CONV - 1, GDN_QKV), dtype),
        'rwkv': jnp.zeros((bsz, RWKV_HEADS, RWKV_HS, RWKV_HS), f32),
        'rwkv_shift': jnp.zeros((bsz, RWKV_PROJ), dtype),
        'ssm': jnp.zeros((bsz, SSM_HEADS, SSM_HEADDIM, SSM_STATE), f32),
        'ssm_conv': jnp.zeros((bsz, SSM_CONV - 1, SSM_XBC), dtype),
        'ffn_conv': jnp.zeros((bsz, FFN_CONV - 1, D_FF), dtype),
    }


def decoder_layer(x, prm, slopes, past, swa_cache):
    bsz, l, _ = x.shape
    h = rmsnorm(x, prm['norm_mix_pre'])
    proj = h @ prm['w_in']
    gdn_qkv, gdn_z, gdn_b, gdn_a, rwkv_in, ssm_z, ssm_xbc, ssm_dt, swa_qkv = split_cols(proj, IN_SIZES)

    o_a, gdn_conv, gdn_S = gdn_mixer(gdn_qkv, gdn_z, gdn_b, gdn_a, past['gdn_conv'], past['gdn'],
                                     prm['gdn_conv_w'], prm['gdn_A_log'], prm['gdn_dt_bias'], prm['gdn_norm_w'])
    o_b, rwkv_shift, rwkv_S = rwkv7_mixer(rwkv_in, past['rwkv_shift'], past['rwkv'], prm['rwkv_mu'],
                                          prm['rwkv_w0'], prm['rwkv_w2'], prm['rwkv_a0'], prm['rwkv_a2'],
                                          prm['rwkv_g2'], prm['rwkv_k_k'], prm['rwkv_k_a'], prm['rwkv_r_k'],
                                          prm['rwkv_ln_w'], prm['rwkv_ln_b'])
    o_c, ssm_conv, ssm_S = mamba2_mixer(ssm_z, ssm_xbc, ssm_dt, past['ssm_conv'], past['ssm'],
                                        prm['ssm_conv_w'], prm['ssm_conv_b'], prm['ssm_dt_bias'],
                                        prm['ssm_A_log'], prm['ssm_D'], prm['ssm_norm_w'])
    q, k, v = [t.reshape(bsz, l, SWA_HEADS, SWA_HD) for t in split_cols(swa_qkv, (GROUP_W, GROUP_W, GROUP_W))]
    if swa_cache is None:
        o_d = combine_groups([dilated_band(q, k, v, slopes, w, d) for (w, d) in SWA_PATTERNS])
        keep = min(SWA_MAX_WINDOW, l)
        k_rows, v_rows = k[:, l - keep:], v[:, l - keep:]
    else:
        ck, cv = swa_cache
        wb = ck.shape[1]
        k_all = jnp.concatenate([ck.astype(k.dtype), k], axis=1)
        v_all = jnp.concatenate([cv.astype(v.dtype), v], axis=1)
        o_d = combine_groups(dilated_sample(q, k_all, v_all, slopes, wb))
        k_rows, v_rows = k, v

    mix = jnp.concatenate([o_a, o_b, o_c, o_d.reshape(bsz, l, GROUP_W)], axis=-1).astype(x.dtype)
    x = x + rmsnorm(mix @ prm['w_out'], prm['norm_mix_post'])

    h2 = rmsnorm(x, prm['norm_ffn_pre'])
    up = h2 @ prm['ffn_w_up']
    gate, val = up[..., :D_FF], up[..., D_FF:]
    gconv, ffn_conv = causal_dwconv(gate, past['ffn_conv'], prm['ffn_conv_w'])
    act = jax.nn.silu((gconv + prm['ffn_conv_b']).astype(jnp.float32)) * val.astype(jnp.float32)
    x = x + rmsnorm(act.astype(x.dtype) @ prm['ffn_w_down'], prm['norm_ffn_post'])
    return x, (gdn_S, gdn_conv, rwkv_S, rwkv_shift, ssm_S, ssm_conv, k_rows, v_rows, ffn_conv)


def setup_inputs(seed: int = 0) -> dict:
    key = jax.random.key(seed)
    keys = iter(jax.random.split(key, 64))
    f32 = jnp.float32

    def nrm(shape, scale):
        return scale * jax.random.normal(next(keys), shape, f32)

    def gain(shape):
        return 1.0 + 0.02 * jax.random.normal(next(keys), shape, f32)

    def unif(shape, lo, hi):
        return jax.random.uniform(next(keys), shape, f32, lo, hi)

    def inv_softplus_dt(shape):
        dt = jnp.exp(unif(shape, math.log(1e-3), math.log(1e-1)))
        return dt + jnp.log(-jnp.expm1(-dt))

    L, DB = DEPTH, DEC_BATCH
    wb = min(SWA_MAX_WINDOW, PAST_LEN)
    return {
        'x_prompt': nrm((BATCH, SEQ, D_MODEL), 1.0),
        'x_sample': nrm((DEC_BATCH, DEC_SEQ, D_MODEL), 1.0),
        'state_gdn': nrm((L, DB, GDN_HEADS, GDN_DK, GDN_DV), 0.1),
        'state_gdn_conv': nrm((L, DB, GDN_CONV - 1, GDN_QKV), 1.0),
        'state_rwkv': nrm((L, DB, RWKV_HEADS, RWKV_HS, RWKV_HS), 0.1),
        'state_rwkv_shift': nrm((L, DB, RWKV_PROJ), 1.0),
        'state_ssm': nrm((L, DB, SSM_HEADS, SSM_HEADDIM, SSM_STATE), 0.1),
        'state_ssm_conv': nrm((L, DB, SSM_CONV - 1, SSM_XBC), 1.0),
        'cache_swa_k': nrm((L, DB, wb, SWA_HEADS, SWA_HD), 1.0),
        'cache_swa_v': nrm((L, DB, wb, SWA_HEADS, SWA_HD), 1.0),
        'state_ffn_conv': nrm((L, DB, FFN_CONV - 1, D_FF), 1.0),
        'norm_mix_pre': gain((L, D_MODEL)),
        'norm_mix_post': gain((L, D_MODEL)),
        'norm_ffn_pre': gain((L, D_MODEL)),
        'norm_ffn_post': gain((L, D_MODEL)),
        'w_in': nrm((L, D_MODEL, N_IN), D_MODEL ** -0.5),
        'w_out': nrm((L, MIX_W, D_MODEL), MIX_W ** -0.5),
        'gdn_conv_w': nrm((L, GDN_CONV, GDN_QKV), GDN_CONV ** -0.5),
        'gdn_A_log': jnp.log(unif((L, GDN_HEADS), 1.0, 16.0)),
        'gdn_dt_bias': inv_softplus_dt((L, GDN_HEADS)),
        'gdn_norm_w': gain((L, GDN_DV)),
        'rwkv_mu': unif((L, RWKV_PROJ), 0.0, 1.0),
        'rwkv_w0': unif((L, GROUP_W), -6.0, -1.0),
        'rwkv_w2': nrm((L, RWKV_W_LORA, GROUP_W), 0.1 * RWKV_W_LORA ** -0.5),
        'rwkv_a0': nrm((L, GROUP_W), 0.1),
        'rwkv_a2': nrm((L, RWKV_A_LORA, GROUP_W), RWKV_A_LORA ** -0.5),
        'rwkv_g2': nrm((L, RWKV_G_LORA, GROUP_W), RWKV_G_LORA ** -0.5),
        'rwkv_k_k': 0.85 + nrm((L, GROUP_W), 0.02),
        'rwkv_k_a': gain((L, GROUP_W)),
        'rwkv_r_k': nrm((L, RWKV_HEADS, RWKV_HS), 0.1),
        'rwkv_ln_w': gain((L, GROUP_W)),
        'rwkv_ln_b': nrm((L, GROUP_W), 0.02),
        'ssm_conv_w': nrm((L, SSM_CONV, SSM_XBC), SSM_CONV ** -0.5),
        'ssm_conv_b': nrm((L, SSM_XBC), 0.02),
        'ssm_dt_bias': inv_softplus_dt((L, SSM_HEADS)),
        'ssm_A_log': jnp.log(unif((L, SSM_HEADS), 1.0, 16.0)),
        'ssm_D': gain((L, SSM_HEADS)),
        'ssm_norm_w': gain((L, GROUP_W)),
        'ffn_w_up': nrm((L, D_MODEL, 2 * D_FF), D_MODEL ** -0.5),
        'ffn_conv_w': nrm((L, FFN_CONV, D_FF), FFN_CONV ** -0.5),
        'ffn_conv_b': nrm((L, D_FF), 0.02),
        'ffn_w_down': nrm((L, D_FF, D_MODEL), D_FF ** -0.5),
    }


def reference(x_prompt, x_sample, state_gdn, state_gdn_conv, state_rwkv, state_rwkv_shift, state_ssm,
              state_ssm_conv, cache_swa_k, cache_swa_v, state_ffn_conv, norm_mix_pre, norm_mix_post,
              norm_ffn_pre, norm_ffn_post, w_in, w_out, gdn_conv_w, gdn_A_log, gdn_dt_bias, gdn_norm_w,
              rwkv_mu, rwkv_w0, rwkv_w2, rwkv_a0, rwkv_a2, rwkv_g2, rwkv_k_k, rwkv_k_a, rwkv_r_k,
              rwkv_ln_w, rwkv_ln_b, ssm_conv_w, ssm_conv_b, ssm_dt_bias, ssm_A_log, ssm_D, ssm_norm_w,
              ffn_w_up, ffn_conv_w, ffn_conv_b, ffn_w_down):
    slopes = alibi_slopes()
    xp, xs = x_prompt, x_sample
    prompt_states, sample_states = [], []
    for li in range(DEPTH):
        prm = {
            'norm_mix_pre': norm_mix_pre[li], 'norm_mix_post': norm_mix_post[li],
            'norm_ffn_pre': norm_ffn_pre[li], 'norm_ffn_post': norm_ffn_post[li],
            'w_in': w_in[li], 'w_out': w_out[li],
            'gdn_conv_w': gdn_conv_w[li], 'gdn_A_log': gdn_A_log[li], 'gdn_dt_bias': gdn_dt_bias[li],
            'gdn_norm_w': gdn_norm_w[li],
            'rwkv_mu': rwkv_mu[li], 'rwkv_w0': rwkv_w0[li], 'rwkv_w2': rwkv_w2[li], 'rwkv_a0': rwkv_a0[li],
            'rwkv_a2': rwkv_a2[li], 'rwkv_g2': rwkv_g2[li], 'rwkv_k_k': rwkv_k_k[li], 'rwkv_k_a': rwkv_k_a[li],
            'rwkv_r_k': rwkv_r_k[li], 'rwkv_ln_w': rwkv_ln_w[li], 'rwkv_ln_b': rwkv_ln_b[li],
            'ssm_conv_w': ssm_conv_w[li], 'ssm_conv_b': ssm_conv_b[li], 'ssm_dt_bias': ssm_dt_bias[li],
            'ssm_A_log': ssm_A_log[li], 'ssm_D': ssm_D[li], 'ssm_norm_w': ssm_norm_w[li],
            'ffn_w_up': ffn_w_up[li], 'ffn_conv_w': ffn_conv_w[li], 'ffn_conv_b': ffn_conv_b[li],
            'ffn_w_down': ffn_w_down[li],
        }
        xp, stp = decoder_layer(xp, prm, slopes, zero_past(xp.shape[0], xp.dtype), None)
        past = {
            'gdn': state_gdn[li], 'gdn_conv': state_gdn_conv[li], 'rwkv': state_rwkv[li],
            'rwkv_shift': state_rwkv_shift[li], 'ssm': state_ssm[li], 'ssm_conv': state_ssm_conv[li],
            'ffn_conv': state_ffn_conv[li],
        }
        xs, sts = decoder_layer(xs, prm, slopes, past, (cache_swa_k[li], cache_swa_v[li]))
        prompt_states.append(stp)
        sample_states.append(sts)
    (p_gdn, p_gdn_conv, p_rwkv, p_rwkv_shift, p_ssm, p_ssm_conv, p_swa_k, p_swa_v,
     p_ffn_conv) = [jnp.stack(t) for t in zip(*prompt_states)]
    (s_gdn, s_gdn_conv, s_rwkv, s_rwkv_shift, s_ssm, s_ssm_conv, s_swa_k, s_swa_v,
     s_ffn_conv) = [jnp.stack(t) for t in zip(*sample_states)]
    return (xp, xs, p_gdn, p_gdn_conv, p_rwkv, p_rwkv_shift, p_ssm, p_ssm_conv, p_swa_k, p_swa_v, p_ffn_conv,
            s_gdn, s_gdn_conv, s_rwkv, s_rwkv_shift, s_ssm, s_ssm_conv, s_swa_k, s_swa_v, s_ffn_conv)
```

```python
import functools

import jax
import jax.numpy as jnp
from jax import lax
from jax.experimental import pallas as pl
from jax.experimental.pallas import tpu as pltpu

F32 = jnp.float32
BF16 = jnp.bfloat16
HI = lax.Precision.HIGHEST

D_MODEL = 4096
GROUP_W = D_MODEL // 4
GDN_HEADS = 8
GDN_D = GROUP_W // GDN_HEADS
GDN_TAPS = 4
RWKV_HS = 64
RWKV_HEADS = GROUP_W // RWKV_HS
RWKV_W_LORA = 64
RWKV_A_LORA = 64
RWKV_G_LORA = 160
RWKV_LORA = RWKV_W_LORA + RWKV_A_LORA + RWKV_G_LORA
RWKV_GN_EPS = 64e-5
SSM_P = 64
SSM_HEADS = GROUP_W // SSM_P
SSM_GROUPS = 2
SSM_N = 128
SSM_TAPS = 4
SSM_XBC = GROUP_W + 2 * SSM_GROUPS * SSM_N
SWA_HEADS = 8
SWA_HD = GROUP_W // SWA_HEADS
SWA_PATTERNS = ((128, 1), (512, 4), (2048, 16))
SWA_MAX_WINDOW = 2048
D_FF = 256 * ((8 * D_MODEL // 3 + 255) // 256)
FFN_TAPS = 3
NORM_EPS = 1e-6
NEG_INF = -1e30

LANE = 128
SUBLANE = 8
LORA_PAD = 384
G_LORA_PAD = LORA_PAD - RWKV_W_LORA - RWKV_A_LORA

COL_A_QKV = 0
COL_B_RKV = 3 * GROUP_W
COL_D_QKV = 6 * GROUP_W
COL_C_XBC = 9 * GROUP_W
COL_B_LORA = COL_C_XBC + SSM_XBC
COL_SMALL = COL_B_LORA + LORA_PAD
COL_A_Z = COL_SMALL + LANE
COL_C_Z = COL_A_Z + GROUP_W
N_PROJ = COL_C_Z + GROUP_W
SM_GDN_B = 0
SM_GDN_A = GDN_HEADS
SM_SSM_DT = 2 * GDN_HEADS

VMEM_LIMIT = 56 * 1024 * 1024


def _cparams(sem):
    return pltpu.CompilerParams(dimension_semantics=sem, vmem_limit_bytes=VMEM_LIMIT)


def _dot(a, b, dims, prec):
    return lax.dot_general(a, b, (dims, ((), ())), precision=prec, preferred_element_type=F32)


def dot_nn(a, b, prec=HI):
    return _dot(a, b, ((1,), (0,)), prec)


def dot_nt(a, b, prec=HI):
    return _dot(a, b, ((1,), (1,)), prec)


def dot_tn(a, b, prec=HI):
    return _dot(a, b, ((0,), (0,)), prec)


def _silu(x):
    return x * jax.nn.sigmoid(x)


def _iota2(shape, axis):
    return lax.broadcasted_iota(jnp.int32, shape, axis)


def _log2(n):
    s = n.bit_length() - 1
    assert 1 << s == n
    return s


def _inv_unit_lower(m, c):
    row = _iota2((c, c), 0)
    col = _iota2((c, c), 1)
    eye = jnp.where(row == col, 1.0, 0.0).astype(F32)
    base = min(SUBLANE, c)
    sb = _log2(base)
    n = jnp.where((row >> sb) == (col >> sb), -m, 0.0)
    t = eye + n
    p = n
    for _ in range(sb - 1):
        p = dot_nn(p, p)
        t = t + dot_nn(t, p)
    s = base
    while s < c:
        ls = _log2(s)
        off = ((row >> (ls + 1)) == (col >> (ls + 1))) & ((row >> ls) > (col >> ls))
        moff = jnp.where(off, m, 0.0)
        t = t - dot_nn(t, dot_nn(moff, t))
        s *= 2
    return t


def _lane_onehot(c, lane):
    return jnp.where(_iota2((c, LANE), 1) == lane, 1.0, 0.0).astype(F32)


def _cumsum_rows(x, c):
    tri = jnp.where(_iota2((c, c), 0) >= _iota2((c, c), 1), 1.0, 0.0).astype(F32)
    return dot_nn(tri, x)


def _shifted_taps(tail, x, taps):
    c = x.shape[0]
    xp = jnp.concatenate([tail, x], axis=0)
    out = []
    for s in range(taps - 1, 0, -1):
        out.append(pltpu.roll(xp, s, 0)[SUBLANE:SUBLANE + c])
    out.append(x)
    return out


def _rms_cast_kernel(x_ref, w_ref, o_ref):
    x = x_ref[...]
    y = x * lax.rsqrt(jnp.mean(x * x, axis=-1, keepdims=True) + NORM_EPS)
    o_ref[...] = (y * w_ref[...]).astype(o_ref.dtype)


def rms_cast(x, w, tr):
    m, d = x.shape
    return pl.pallas_call(
        _rms_cast_kernel,
        grid=(m // tr,),
        in_specs=[pl.BlockSpec((tr, d), lambda i: (i, 0)), pl.BlockSpec((1, d), lambda i: (0, 0))],
        out_specs=pl.BlockSpec((tr, d), lambda i: (i, 0)),
        out_shape=jax.ShapeDtypeStruct((m, d), BF16),
        compiler_params=_cparams(("parallel",)),
        name="rms_cast",
    )(x, w.reshape(1, d))


def _add_rms_kernel(x_ref, y_ref, w_ref, o_ref):
    y = y_ref[...]
    yn = y * lax.rsqrt(jnp.mean(y * y, axis=-1, keepdims=True) + NORM_EPS)
    o_ref[...] = x_ref[...] + yn * w_ref[...]


def add_rms(x, y, w, tr):
    m, d = x.shape
    return pl.pallas_call(
        _add_rms_kernel,
        grid=(m // tr,),
        in_specs=[pl.BlockSpec((tr, d), lambda i: (i, 0)), pl.BlockSpec((tr, d), lambda i: (i, 0)),
                  pl.BlockSpec((1, d), lambda i: (0, 0))],
        out_specs=pl.BlockSpec((tr, d), lambda i: (i, 0)),
        out_shape=jax.ShapeDtypeStruct((m, d), F32),
        compiler_params=_cparams(("parallel",)),
        name="add_rms",
    )(x, y, w.reshape(1, d))


def _mm_kernel(a_ref, w_ref, o_ref, *, nk):
    p = jnp.dot(a_ref[...], w_ref[...], preferred_element_type=F32)
    if nk == 1:
        o_ref[...] = p
    else:
        k = pl.program_id(2)

        @pl.when(k == 0)
        def _():
            o_ref[...] = p

        @pl.when(k > 0)
        def _():
            o_ref[...] += p


def matmul(a, w, tm, tn, tk, name):
    m, kd = a.shape
    n = w.shape[1]
    nk = kd // tk
    return pl.pallas_call(
        functools.partial(_mm_kernel, nk=nk),
        grid=(m // tm, n // tn, nk),
        in_specs=[pl.BlockSpec((tm, tk), lambda i, j, k: (i, k)), pl.BlockSpec((tk, tn), lambda i, j, k: (k, j))],
        out_specs=pl.BlockSpec((tm, tn), lambda i, j, k: (i, j)),
        out_shape=jax.ShapeDtypeStruct((m, n), F32),
        compiler_params=_cparams(("parallel", "parallel", "arbitrary")),
        name=name,
    )(a, w)


def _ffn_act_kernel(g_ref, v_ref, halo_ref, st_ref, cw_ref, cb_ref, o_ref):
    g = g_ref[...]
    tail = jnp.where(pl.program_id(1) == 0, st_ref[...], halo_ref[...])
    taps = _shifted_taps(tail, g, FFN_TAPS)
    cw = cw_ref[...]
    y = taps[0] * cw[0:1]
    for i in range(1, FFN_TAPS):
        y = y + taps[i] * cw[i:i + 1]
    y = y + cb_ref[...]
    o_ref[...] = (_silu(y) * v_ref[...]).astype(o_ref.dtype)


def ffn_act(up3, state8, conv_w, conv_b, ts, tn):
    b, l, _ = up3.shape
    nj = D_FF // tn
    hb = ts // SUBLANE
    return pl.pallas_call(
        _ffn_act_kernel,
        grid=(b, l // ts, nj),
        in_specs=[
            pl.BlockSpec((None, ts, tn), lambda bi, i, j: (bi, i, j)),
            pl.BlockSpec((None, ts, tn), lambda bi, i, j: (bi, i, j + nj)),
            pl.BlockSpec((None, SUBLANE, tn), lambda bi, i, j: (bi, jnp.maximum(i * hb - 1, 0), j)),
            pl.BlockSpec((None, SUBLANE, tn), lambda bi, i, j: (bi, 0, j)),
            pl.BlockSpec((FFN_TAPS, tn), lambda bi, i, j: (0, j)),
            pl.BlockSpec((1, tn), lambda bi, i, j: (0, j)),
        ],
        out_specs=pl.BlockSpec((None, ts, tn), lambda bi, i, j: (bi, i, j)),
        out_shape=jax.ShapeDtypeStruct((b, l, D_FF), BF16),
        compiler_params=_cparams(("parallel", "parallel", "parallel")),
        name="ffn_act",
    )(up3, up3, up3, state8, conv_w, conv_b.reshape(1, D_FF))


def _gdn_kernel(qkv_ref, z_ref, sm_ref, cbuf_ref, s0_ref, cw_ref, alog_ref, dtb_ref, nw_ref,
                o_ref, sout_ref, s_scr, tail_scr, *, c):
    ci = pl.program_id(1)

    @pl.when(ci == 0)
    def _():
        s_scr[...] = s0_ref[...]
        tail_scr[...] = cbuf_ref[...]

    x = qkv_ref[...]
    taps = _shifted_taps(tail_scr[...], x, GDN_TAPS)
    tail_scr[...] = x[c - SUBLANE:]
    cw = cw_ref[...]
    y = taps[0] * cw[0:1]
    for i in range(1, GDN_TAPS):
        y = y + taps[i] * cw[i:i + 1]
    y = _silu(y)

    sm = sm_ref[...]
    beta_all = jax.nn.sigmoid(sm)
    g_all = -jnp.exp(alog_ref[...]) * jax.nn.softplus(sm + dtb_ref[...])
    gcum_all = _cumsum_rows(g_all, c)

    row = _iota2((c, c), 0)
    col = _iota2((c, c), 1)
    tri = row >= col
    strict = row > col
    z = z_ref[...]
    nw = nw_ref[...]
    for h in range(GDN_HEADS):
        lo = h * GDN_D
        q = y[:, lo:lo + GDN_D]
        k = y[:, GROUP_W + lo:GROUP_W + lo + GDN_D]
        v = y[:, 2 * GROUP_W + lo:2 * GROUP_W + lo + GDN_D]
        q = q * lax.rsqrt(jnp.sum(q * q, axis=-1, keepdims=True) + 1e-6) * (GDN_D ** -0.5)
        k = k * lax.rsqrt(jnp.sum(k * k, axis=-1, keepdims=True) + 1e-6)
        beta = beta_all[:, SM_GDN_B + h:SM_GDN_B + h + 1]
        gc = gcum_all[:, SM_GDN_A + h:SM_GDN_A + h + 1]
        gc_row = dot_nt(_lane_onehot(c, SM_GDN_A + h), gcum_all)
        gam = jnp.exp(jnp.where(tri, gc - gc_row, -jnp.inf))
        kbeta = k * beta
        m = jnp.where(strict, dot_nt(kbeta, k) * gam, 0.0)
        t = _inv_unit_lower(m, c)
        eg = jnp.exp(gc)
        w = dot_nn(t, kbeta * eg)
        u = dot_nn(t, v * beta)
        aqk = dot_nt(q, k) * gam
        gl = gc[c - 1:c]
        s = s_scr[h]
        v_new = u - dot_nn(w, s)
        o = dot_nn(q * eg, s) + dot_nn(aqk, v_new)
        s_scr[h] = s * jnp.exp(gl) + dot_tn(k * jnp.exp(gl - gc), v_new)
        o = o * lax.rsqrt(jnp.mean(o * o, axis=-1, keepdims=True) + NORM_EPS) * nw
        o = o * _silu(z[:, lo:lo + GDN_D])
        o_ref[:, lo:lo + GDN_D] = o.astype(o_ref.dtype)

    @pl.when(ci == pl.num_programs(1) - 1)
    def _():
        sout_ref[...] = s_scr[...]


def gdn_mixer(proj3, cbuf8, s0, conv_w, a_log, dt_bias, norm_w, c):
    b, l, _ = proj3.shape
    alog_row = jnp.zeros((1, LANE), F32).at[0, SM_GDN_A:SM_GDN_A + GDN_HEADS].set(a_log)
    dtb_row = jnp.zeros((1, LANE), F32).at[0, SM_GDN_A:SM_GDN_A + GDN_HEADS].set(dt_bias)
    w3 = 3 * GROUP_W
    const2 = lambda bi, ci: (0, 0)
    return pl.pallas_call(
        functools.partial(_gdn_kernel, c=c),
        grid=(b, l // c),
        in_specs=[
            pl.BlockSpec((None, c, w3), lambda bi, ci: (bi, ci, COL_A_QKV // w3)),
            pl.BlockSpec((None, c, GROUP_W), lambda bi, ci: (bi, ci, COL_A_Z // GROUP_W)),
            pl.BlockSpec((None, c, LANE), lambda bi, ci: (bi, ci, COL_SMALL // LANE)),
            pl.BlockSpec((None, SUBLANE, w3), lambda bi, ci: (bi, 0, 0)),
            pl.BlockSpec((None, GDN_HEADS, GDN_D, GDN_D), lambda bi, ci: (bi, 0, 0, 0)),
            pl.BlockSpec((GDN_TAPS, w3), const2),
            pl.BlockSpec((1, LANE), const2),
            pl.BlockSpec((1, LANE), const2),
            pl.BlockSpec((1, GDN_D), const2),
        ],
        out_specs=[
            pl.BlockSpec((None, c, GROUP_W), lambda bi, ci: (bi, ci, 0)),
            pl.BlockSpec((None, GDN_HEADS, GDN_D, GDN_D), lambda bi, ci: (bi, 0, 0, 0)),
        ],
        out_shape=[jax.ShapeDtypeStruct((b, l, GROUP_W), BF16),
                   jax.ShapeDtypeStruct((b, GDN_HEADS, GDN_D, GDN_D), F32)],
        scratch_shapes=[pltpu.VMEM((GDN_HEADS, GDN_D, GDN_D), F32), pltpu.VMEM((SUBLANE, w3), F32)],
        compiler_params=_cparams(("parallel", "arbitrary")),
        name="gdn_mixer",
    )(proj3, proj3, proj3, cbuf8, s0, conv_w, alog_row, dtb_row, norm_w.reshape(1, GDN_D))


def _rwkv_kernel(rkv_ref, lora_ref, sh_rkv_ref, sh_lora_ref, s0_ref, mu_rkv_ref, mu_lora_ref,
                 w0_ref, w2_ref, a0_ref, a2_ref, g2_ref, kk_ref, ka_ref, rk_ref, lnw_ref, lnb_ref,
                 o_ref, sout_ref, s_scr, tail_rkv, tail_lora, *, c):
    ci = pl.program_id(1)

    @pl.when(ci == 0)
    def _():
        s_scr[...] = s0_ref[...]
        tail_rkv[...] = sh_rkv_ref[...]
        tail_lora[...] = sh_lora_ref[...]

    x = rkv_ref[...]
    xl = lora_ref[...]
    prev = _shifted_taps(tail_rkv[...], x, 2)[0]
    prev_l = _shifted_taps(tail_lora[...], xl, 2)[0]
    tail_rkv[...] = x[c - SUBLANE:]
    tail_lora[...] = xl[c - SUBLANE:]
    zm = x + (prev - x) * mu_rkv_ref[...]
    zl = xl + (prev_l - xl) * mu_lora_ref[...]
    r = zm[:, 0:GROUP_W]
    k = zm[:, GROUP_W:2 * GROUP_W]
    v = zm[:, 2 * GROUP_W:3 * GROUP_W]
    wd = zl[:, 0:RWKV_W_LORA]
    ad = zl[:, RWKV_W_LORA:RWKV_W_LORA + RWKV_A_LORA]
    gd = zl[:, RWKV_W_LORA + RWKV_A_LORA:LORA_PAD]

    w_log = -jax.nn.softplus(-(w0_ref[...] + dot_nn(jnp.tanh(wd), w2_ref[...]))) - 0.5
    logw = -jnp.exp(w_log)
    a = jax.nn.sigmoid(a0_ref[...] + dot_nn(ad, a2_ref[...]))
    gate = dot_nn(jax.nn.sigmoid(gd), g2_ref[...])
    kkv = k * kk_ref[...]
    k2 = k * (1.0 + (a - 1.0) * ka_ref[...])
    lcum = _cumsum_rows(logw, c)

    row = _iota2((c, c), 0)
    col = _iota2((c, c), 1)
    tri = row >= col
    strict = row > col
    rk = rk_ref[...]
    lnw = lnw_ref[...]
    lnb = lnb_ref[...]
    for h in range(RWKV_HEADS):
        lo = h * RWKV_HS
        hi = lo + RWKV_HS
        r_h, k_h, v_h, a_h = r[:, lo:hi], k2[:, lo:hi], v[:, lo:hi], a[:, lo:hi]
        kk = kkv[:, lo:hi]
        kk = kk * lax.rsqrt(jnp.sum(kk * kk, axis=-1, keepdims=True) + 1e-6)
        lc = lcum[:, lo:hi]
        lw = logw[:, lo:hi]
        l_last = lc[c - 1:c]
        p_in = jnp.exp(lc)
        p_inv = jnp.exp(-lc)
        a_t = -kk * jnp.exp(lc - lw)
        b_vec = kk * a_h
        b_t = b_vec * p_inv
        k_t = k_h * p_inv
        r_t = r_h * p_in
        p_end = jnp.exp(l_last - lc)
        s = s_scr[h]
        a_ab = jnp.where(strict, dot_nt(a_t, b_t), 0.0)
        a_ak = jnp.where(strict, dot_nt(a_t, k_t), 0.0)
        t = _inv_unit_lower(-a_ab, c)
        u = dot_nn(t, dot_nt(a_t, s) + dot_nn(a_ak, v_h))
        y = (dot_nt(r_t, s) + dot_nn(jnp.where(tri, dot_nt(r_t, b_t), 0.0), u)
             + dot_nn(jnp.where(tri, dot_nt(r_t, k_t), 0.0), v_h))
        s_scr[h] = s * jnp.exp(l_last) + dot_tn(u, b_vec * p_end) + dot_tn(v_h, k_h * p_end)
        mean = jnp.mean(y, axis=-1, keepdims=True)
        var = jnp.mean(jnp.square(y - mean), axis=-1, keepdims=True)
        yn = (y - mean) * lax.rsqrt(var + RWKV_GN_EPS) * lnw[:, lo:hi] + lnb[:, lo:hi]
        bonus = jnp.sum(r_h * k_h * rk[:, lo:hi], axis=-1, keepdims=True) * v_h
        o_ref[:, lo:hi] = ((yn + bonus) * gate[:, lo:hi]).astype(o_ref.dtype)

    @pl.when(ci == pl.num_programs(1) - 1)
    def _():
        sout_ref[...] = s_scr[...]


def rwkv_mixer(proj3, sh_rkv8, sh_lora8, s0, prm, c):
    b, l, _ = proj3.shape
    w3 = 3 * GROUP_W
    mu = prm['rwkv_mu']
    mu_rkv = mu[:w3].reshape(1, w3)
    mu_lora = jnp.pad(mu[w3:], (0, LORA_PAD - RWKV_LORA)).reshape(1, LORA_PAD)
    g2 = jnp.pad(prm['rwkv_g2'], ((0, G_LORA_PAD - RWKV_G_LORA), (0, 0)))
    row = lambda t: t.reshape(1, GROUP_W)
    const2 = lambda bi, ci: (0, 0)
    vec = pl.BlockSpec((1, GROUP_W), const2)
    return pl.pallas_call(
        functools.partial(_rwkv_kernel, c=c),
        grid=(b, l // c),
        in_specs=[
            pl.BlockSpec((None, c, w3), lambda bi, ci: (bi, ci, COL_B_RKV // w3)),
            pl.BlockSpec((None, c, LORA_PAD), lambda bi, ci: (bi, ci, COL_B_LORA // LORA_PAD)),
            pl.BlockSpec((None, SUBLANE, w3), lambda bi, ci: (bi, 0, 0)),
            pl.BlockSpec((None, SUBLANE, LORA_PAD), lambda bi, ci: (bi, 0, 0)),
            pl.BlockSpec((None, RWKV_HEADS, RWKV_HS, RWKV_HS), lambda bi, ci: (bi, 0, 0, 0)),
            pl.BlockSpec((1, w3), const2),
            pl.BlockSpec((1, LORA_PAD), const2),
            vec,
            pl.BlockSpec((RWKV_W_LORA, GROUP_W), const2),
            vec,
            pl.BlockSpec((RWKV_A_LORA, GROUP_W), const2),
            pl.BlockSpec((G_LORA_PAD, GROUP_W), const2),
            vec, vec, vec, vec, vec,
        ],
        out_specs=[
            pl.BlockSpec((None, c, GROUP_W), lambda bi, ci: (bi, ci, 0)),
            pl.BlockSpec((None, RWKV_HEADS, RWKV_HS, RWKV_HS), lambda bi, ci: (bi, 0, 0, 0)),
        ],
        out_shape=[jax.ShapeDtypeStruct((b, l, GROUP_W), BF16),
                   jax.ShapeDtypeStruct((b, RWKV_HEADS, RWKV_HS, RWKV_HS), F32)],
        scratch_shapes=[pltpu.VMEM((RWKV_HEADS, RWKV_HS, RWKV_HS), F32),
                        pltpu.VMEM((SUBLANE, w3), F32), pltpu.VMEM((SUBLANE, LORA_PAD), F32)],
        compiler_params=_cparams(("parallel", "arbitrary")),
        name="rwkv_mixer",
    )(proj3, proj3, sh_rkv8, sh_lora8, s0, mu_rkv, mu_lora,
      row(prm['rwkv_w0']), prm['rwkv_w2'], row(prm['rwkv_a0']), prm['rwkv_a2'], g2,
      row(prm['rwkv_k_k']), row(prm['rwkv_k_a']), row(prm['rwkv_r_k']),
      row(prm['rwkv_ln_w']), row(prm['rwkv_ln_b']))


def _ssd_kernel(z_ref, xbc_ref, sm_ref, cbuf_ref, s0_ref, cw_ref, cb_ref, dtb_ref, alog_ref, dsk_ref, nw_ref,
                o_ref, sout_ref, s_scr, tail_scr, y_scr, *, c):
    ci = pl.program_id(1)

    @pl.when(ci == 0)
    def _():
        s_scr[...] = s0_ref[...]
        tail_scr[...] = cbuf_ref[...]

    x = xbc_ref[...]
    taps = _shifted_taps(tail_scr[...], x, SSM_TAPS)
    tail_scr[...] = x[c - SUBLANE:]
    cw = cw_ref[...]
    y = taps[0] * cw[0:1]
    for i in range(1, SSM_TAPS):
        y = y + taps[i] * cw[i:i + 1]
    y = _silu(y + cb_ref[...])
    xs = y[:, 0:GROUP_W]
    gn = SSM_GROUPS * SSM_N

    sm = sm_ref[...]
    dt_all = jax.nn.softplus(sm + dtb_ref[...])
    da_all = dt_all * (-jnp.exp(alog_ref[...]))
    acs_all = _cumsum_rows(da_all, c)

    row = _iota2((c, c), 0)
    col = _iota2((c, c), 1)
    tri = row >= col
    z = z_ref[...]
    dsk = dsk_ref[...]
    hpg = SSM_HEADS // SSM_GROUPS
    for g in range(SSM_GROUPS):
        bm = y[:, GROUP_W + g * SSM_N:GROUP_W + (g + 1) * SSM_N]
        cm = y[:, GROUP_W + gn + g * SSM_N:GROUP_W + gn + (g + 1) * SSM_N]
        cb = dot_nt(cm, bm)
        for rr in range(hpg):
            h = g * hpg + rr
            lo = h * SSM_P
            lane = SM_SSM_DT + h
            xs_h = xs[:, lo:lo + SSM_P]
            dt = dt_all[:, lane:lane + 1]
            acs = acs_all[:, lane:lane + 1]
            acs_row = dot_nt(_lane_onehot(c, lane), acs_all)
            lmat = jnp.exp(jnp.where(tri, acs - acs_row, -jnp.inf))
            xd = xs_h * dt
            a_last = acs[c - 1:c]
            s = s_scr[h]
            yh = dot_nn(cb * lmat, xd) + dot_nt(cm, s) * jnp.exp(acs)
            s_scr[h] = s * jnp.exp(a_last) + dot_tn(xd * jnp.exp(a_last - acs), bm)
            yh = yh + xs_h * dsk[:, lane:lane + 1]
            y_scr[:, lo:lo + SSM_P] = yh * _silu(z[:, lo:lo + SSM_P])

    gw = GROUP_W // SSM_GROUPS
    nw = nw_ref[...]
    for g in range(SSM_GROUPS):
        yg = y_scr[:, g * gw:(g + 1) * gw]
        yg = yg * lax.rsqrt(jnp.mean(yg * yg, axis=-1, keepdims=True) + NORM_EPS)
        o_ref[:, g * gw:(g + 1) * gw] = (yg * nw[:, g * gw:(g + 1) * gw]).astype(o_ref.dtype)

    @pl.when(ci == pl.num_programs(1) - 1)
    def _():
        sout_ref[...] = s_scr[...]


def ssd_mixer(proj3, cbuf8, s0, prm, c):
    b, l, _ = proj3.shape

    def lanes(t):
        return jnp.zeros((1, LANE), F32).at[0, SM_SSM_DT:SM_SSM_DT + SSM_HEADS].set(t)

    const2 = lambda bi, ci: (0, 0)
    small = pl.BlockSpec((1, LANE), const2)
    return pl.pallas_call(
        functools.partial(_ssd_kernel, c=c),
        grid=(b, l // c),
        in_specs=[
            pl.BlockSpec((None, c, GROUP_W), lambda bi, ci: (bi, ci, COL_C_Z // GROUP_W)),
            pl.BlockSpec((None, c, SSM_XBC), lambda bi, ci: (bi, ci, COL_C_XBC // SSM_XBC)),
            pl.BlockSpec((None, c, LANE), lambda bi, ci: (bi, ci, COL_SMALL // LANE)),
            pl.BlockSpec((None, SUBLANE, SSM_XBC), lambda bi, ci: (bi, 0, 0)),
            pl.BlockSpec((None, SSM_HEADS, SSM_P, SSM_N), lambda bi, ci: (bi, 0, 0, 0)),
            pl.BlockSpec((SSM_TAPS, SSM_XBC), const2),
            pl.BlockSpec((1, SSM_XBC), const2),
            small, small, small,
            pl.BlockSpec((1, GROUP_W), const2),
        ],
        out_specs=[
            pl.BlockSpec((None, c, GROUP_W), lambda bi, ci: (bi, ci, 0)),
            pl.BlockSpec((None, SSM_HEADS, SSM_P, SSM_N), lambda bi, ci: (bi, 0, 0, 0)),
        ],
        out_shape=[jax.ShapeDtypeStruct((b, l, GROUP_W), BF16),
                   jax.ShapeDtypeStruct((b, SSM_HEADS, SSM_P, SSM_N), F32)],
        scratch_shapes=[pltpu.VMEM((SSM_HEADS, SSM_P, SSM_N), F32), pltpu.VMEM((SUBLANE, SSM_XBC), F32),
                        pltpu.VMEM((c, GROUP_W), F32)],
        compiler_params=_cparams(("parallel", "arbitrary")),
        name="ssd_mixer",
    )(proj3, proj3, proj3, cbuf8, s0, prm['ssm_conv_w'], prm['ssm_conv_b'].reshape(1, SSM_XBC),
      lanes(prm['ssm_dt_bias']), lanes(prm['ssm_A_log']), lanes(prm['ssm_D']),
      prm['ssm_norm_w'].reshape(1, GROUP_W))


def _swa_weight(d):
    mult = jnp.zeros(d.shape, F32)
    for window, dil in SWA_PATTERNS:
        ok = (d >= 0) & (d <= window) & ((d & (dil - 1)) == 0)
        mult = mult + jnp.where(ok, 1.0, 0.0)
    return mult


def _swa_scores(q, k, d, slope):
    s = dot_nt(q.astype(BF16), k.astype(BF16), None) * (SWA_HD ** -0.5)
    mult = _swa_weight(d)
    s = s - slope * d.astype(F32)
    return jnp.where(mult > 0.0, s, NEG_INF), mult


def _swa_prompt_kernel(slopes_ref, q_ref, k_ref, v_ref, o_ref, m_scr, l_scr, acc_scr, *, tq, tk):
    h = pl.program_id(1)
    qi = pl.program_id(2)
    ki = pl.program_id(3)

    @pl.when(ki == 0)
    def _():
        m_scr[...] = jnp.full(m_scr.shape, NEG_INF, F32)
        l_scr[...] = jnp.zeros(l_scr.shape, F32)
        acc_scr[...] = jnp.zeros(acc_scr.shape, F32)

    @pl.when(ki * tk <= qi * tq + tq - 1)
    def _():
        d = (qi * tq + _iota2((tq, tk), 0)) - (ki * tk + _iota2((tq, tk), 1))
        s, mult = _swa_scores(q_ref[...], k_ref[...], d, slopes_ref[h])
        m_old = m_scr[...]
        m_new = jnp.maximum(m_old, jnp.max(s, axis=-1, keepdims=True))
        alpha = jnp.exp(m_old - m_new)
        p = jnp.exp(s - m_new) * mult
        l_scr[...] = alpha * l_scr[...] + jnp.sum(p, axis=-1, keepdims=True)
        acc_scr[...] = alpha * acc_scr[...] + dot_nn(p.astype(BF16), v_ref[...].astype(BF16), None)
        m_scr[...] = m_new

    @pl.when(ki == pl.num_programs(3) - 1)
    def _():
        o_ref[...] = (acc_scr[...] / l_scr[...]).astype(o_ref.dtype)


def _alibi_slopes():
    return jnp.asarray([2.0 ** (-8.0 * (i + 1) / SWA_HEADS) for i in range(SWA_HEADS)], F32)


def swa_prompt(proj3, tq, tk):
    b, l, _ = proj3.shape
    qc = COL_D_QKV // SWA_HD

    def kv_map(off):
        return lambda bi, h, qi, ki: (bi, jnp.minimum(ki, (qi * tq + tq - 1) // tk), qc + off + h)

    return pl.pallas_call(
        functools.partial(_swa_prompt_kernel, tq=tq, tk=tk),
        grid=(b, SWA_HEADS, l // tq, l // tk),
        in_specs=[
            pl.BlockSpec(memory_space=pltpu.SMEM),
            pl.BlockSpec((None, tq, SWA_HD), lambda bi, h, qi, ki: (bi, qi, qc + h)),
            pl.BlockSpec((None, tk, SWA_HD), kv_map(SWA_HEADS)),
            pl.BlockSpec((None, tk, SWA_HD), kv_map(2 * SWA_HEADS)),
        ],
        out_specs=pl.BlockSpec((None, tq, SWA_HD), lambda bi, h, qi, ki: (bi, qi, h)),
        out_shape=jax.ShapeDtypeStruct((b, l, GROUP_W), BF16),
        scratch_shapes=[pltpu.VMEM((tq, 1), F32), pltpu.VMEM((tq, 1), F32), pltpu.VMEM((tq, SWA_HD), F32)],
        compiler_params=_cparams(("parallel", "parallel", "parallel", "arbitrary")),
        name="swa_prompt",
    )(_alibi_slopes(), proj3, proj3, proj3)


def _swa_sample_kernel(slopes_ref, q_ref, k_ref, v_ref, ck_ref, cv_ref, o_ref, *, t, wb):
    slope = slopes_ref[pl.program_id(1)]
    q = q_ref[...]
    d_c = (wb + _iota2((t, wb), 0)) - _iota2((t, wb), 1)
    d_n = _iota2((t, t), 0) - _iota2((t, t), 1)
    s_c, mult_c = _swa_scores(q, ck_ref[...], d_c, slope)
    s_n, mult_n = _swa_scores(q, k_ref[...], d_n, slope)
    m = jnp.maximum(jnp.max(s_c, axis=-1, keepdims=True), jnp.max(s_n, axis=-1, keepdims=True))
    p_c = jnp.exp(s_c - m) * mult_c
    p_n = jnp.exp(s_n - m) * mult_n
    den = jnp.sum(p_c, axis=-1, keepdims=True) + jnp.sum(p_n, axis=-1, keepdims=True)
    num = (dot_nn(p_c.astype(BF16), cv_ref[...].astype(BF16), None)
           + dot_nn(p_n.astype(BF16), v_ref[...].astype(BF16), None))
    o_ref[...] = (num / den).astype(o_ref.dtype)


def swa_sample(proj3, cache_k, cache_v):
    b, t, _ = proj3.shape
    wb = cache_k.shape[1]
    qc = COL_D_QKV // SWA_HD
    ck = cache_k.reshape(b, wb, GROUP_W)
    cv = cache_v.reshape(b, wb, GROUP_W)
    return pl.pallas_call(
        functools.partial(_swa_sample_kernel, t=t, wb=wb),
        grid=(b, SWA_HEADS),
        in_specs=[
            pl.BlockSpec(memory_space=pltpu.SMEM),
            pl.BlockSpec((None, t, SWA_HD), lambda bi, h: (bi, 0, qc + h)),
            pl.BlockSpec((None, t, SWA_HD), lambda bi, h: (bi, 0, qc + SWA_HEADS + h)),
            pl.BlockSpec((None, t, SWA_HD), lambda bi, h: (bi, 0, qc + 2 * SWA_HEADS + h)),
            pl.BlockSpec((None, wb, SWA_HD), lambda bi, h: (bi, 0, h)),
            pl.BlockSpec((None, wb, SWA_HD), lambda bi, h: (bi, 0, h)),
        ],
        out_specs=pl.BlockSpec((None, t, SWA_HD), lambda bi, h: (bi, 0, h)),
        out_shape=jax.ShapeDtypeStruct((b, t, GROUP_W), BF16),
        compiler_params=_cparams(("parallel", "parallel")),
        name="swa_sample",
    )(_alibi_slopes(), proj3, proj3, proj3, ck, cv)


def _tiles(m):
    return (256, 1024) if m >= 1024 else (m, m)


def _front_pad_rows(t, rows=SUBLANE):
    return jnp.pad(t, ((0, 0), (rows - t.shape[1], 0), (0, 0)))


def prep_weights(p):
    w = p['w_in']
    o = 0
    seg = {}
    for name, n in (('a_qkv', 3 * GROUP_W), ('a_z', GROUP_W), ('a_b', GDN_HEADS), ('a_a', GDN_HEADS),
                    ('b_rkv', 3 * GROUP_W), ('b_lora', RWKV_LORA), ('c_z', GROUP_W), ('c_xbc', SSM_XBC),
                    ('c_dt', SSM_HEADS), ('d_qkv', 3 * GROUP_W)):
        seg[name] = w[:, o:o + n]
        o += n
    zeros = lambda n: jnp.zeros((D_MODEL, n), w.dtype)
    small_used = 2 * GDN_HEADS + SSM_HEADS
    w_in = jnp.concatenate([
        seg['a_qkv'], seg['b_rkv'], seg['d_qkv'], seg['c_xbc'],
        seg['b_lora'], zeros(LORA_PAD - RWKV_LORA),
        seg['a_b'], seg['a_a'], seg['c_dt'], zeros(LANE - small_used),
        seg['a_z'], seg['c_z']], axis=1).astype(BF16)
    return {'w_in': w_in, 'w_out': p['w_out'].astype(BF16), 'ffn_w_up': p['ffn_w_up'].astype(BF16),
            'ffn_w_down': p['ffn_w_down'].astype(BF16)}


def decoder_layer(x, prm, wts, past, swa_cache, chunks):
    b, l, _ = x.shape
    m = b * l
    tr, tm = _tiles(m)
    x2 = x.reshape(m, D_MODEL)
    c_gdn, c_rwkv, c_ssd = chunks

    h = rms_cast(x2, prm['norm_mix_pre'], tr)
    proj = matmul(h, wts['w_in'], tm, 512, D_MODEL, "mm_in")
    proj3 = proj.reshape(b, l, N_PROJ)

    o_a, gdn_s = gdn_mixer(proj3, _front_pad_rows(past['gdn_conv']), past['gdn'], prm['gdn_conv_w'],
                           prm['gdn_A_log'], prm['gdn_dt_bias'], prm['gdn_norm_w'], c_gdn)
    shift = past['rwkv_shift'][:, None, :]
    sh_rkv8 = _front_pad_rows(shift[:, :, :3 * GROUP_W])
    sh_lora8 = _front_pad_rows(jnp.pad(shift[:, :, 3 * GROUP_W:], ((0, 0), (0, 0), (0, LORA_PAD - RWKV_LORA))))
    o_b, rwkv_s = rwkv_mixer(proj3, sh_rkv8, sh_lora8, past['rwkv'], prm, c_rwkv)
    o_c, ssm_s = ssd_mixer(proj3, _front_pad_rows(past['ssm_conv']), past['ssm'], prm, c_ssd)
    if swa_cache is None:
        o_d = swa_prompt(proj3, min(l, 512), min(l, 512))
        keep = min(SWA_MAX_WINDOW, l)
    else:
        o_d = swa_sample(proj3, swa_cache[0], swa_cache[1])
        keep = l

    gdn_conv = proj3[:, l - (GDN_TAPS - 1):, COL_A_QKV:COL_A_QKV + 3 * GROUP_W]
    ssm_conv = proj3[:, l - (SSM_TAPS - 1):, COL_C_XBC:COL_C_XBC + SSM_XBC]
    rwkv_shift = jnp.concatenate([proj3[:, l - 1, COL_B_RKV:COL_B_RKV + 3 * GROUP_W],
                                  proj3[:, l - 1, COL_B_LORA:COL_B_LORA + RWKV_LORA]], axis=-1)
    kcol = COL_D_QKV + GROUP_W
    k_rows = proj3[:, l - keep:, kcol:kcol + GROUP_W].reshape(b, keep, SWA_HEADS, SWA_HD)
    v_rows = proj3[:, l - keep:, kcol + GROUP_W:kcol + 2 * GROUP_W].reshape(b, keep, SWA_HEADS, SWA_HD)

    mix = jnp.concatenate([o_a, o_b, o_c, o_d], axis=-1).reshape(m, D_MODEL)
    y = matmul(mix, wts['w_out'], tm, 512, D_MODEL, "mm_out")
    x2 = add_rms(x2, y, prm['norm_mix_post'], tr)

    h2 = rms_cast(x2, prm['norm_ffn_pre'], tr)
    up = matmul(h2, wts['ffn_w_up'], tm, 512, D_MODEL, "mm_up")
    up3 = up.reshape(b, l, 2 * D_FF)
    act = ffn_act(up3, _front_pad_rows(past['ffn_conv']), prm['ffn_conv_w'], prm['ffn_conv_b'],
                  min(l, 1024), 256)
    ffn_conv = up3[:, l - (FFN_TAPS - 1):, :D_FF]
    y2 = matmul(act.reshape(m, D_FF), wts['ffn_w_down'], min(tm, 512), 512, D_FF // 2, "mm_down")
    x2 = add_rms(x2, y2, prm['norm_ffn_post'], tr)
    return x2.reshape(b, l, D_MODEL), (gdn_s, gdn_conv, rwkv_s, rwkv_shift, ssm_s, ssm_conv, k_rows, v_rows,
                                       ffn_conv)


def _zero_past(bsz):
    return {
        'gdn': jnp.zeros((bsz, GDN_HEADS, GDN_D, GDN_D), F32),
        'gdn_conv': jnp.zeros((bsz, GDN_TAPS - 1, 3 * GROUP_W), F32),
        'rwkv': jnp.zeros((bsz, RWKV_HEADS, RWKV_HS, RWKV_HS), F32),
        'rwkv_shift': jnp.zeros((bsz, 3 * GROUP_W + RWKV_LORA), F32),
        'ssm': jnp.zeros((bsz, SSM_HEADS, SSM_P, SSM_N), F32),
        'ssm_conv': jnp.zeros((bsz, SSM_TAPS - 1, SSM_XBC), F32),
        'ffn_conv': jnp.zeros((bsz, FFN_TAPS - 1, D_FF), F32),
    }


PARAM_NAMES = ('norm_mix_pre', 'norm_mix_post', 'norm_ffn_pre', 'norm_ffn_post', 'w_in', 'w_out', 'gdn_conv_w',
               'gdn_A_log', 'gdn_dt_bias', 'gdn_norm_w', 'rwkv_mu', 'rwkv_w0', 'rwkv_w2', 'rwkv_a0', 'rwkv_a2',
               'rwkv_g2', 'rwkv_k_k', 'rwkv_k_a', 'rwkv_r_k', 'rwkv_ln_w', 'rwkv_ln_b', 'ssm_conv_w', 'ssm_conv_b',
               'ssm_dt_bias', 'ssm_A_log', 'ssm_D', 'ssm_norm_w', 'ffn_w_up', 'ffn_conv_w', 'ffn_conv_b',
               'ffn_w_down')


def kernel(x_prompt, x_sample, state_gdn, state_gdn_conv, state_rwkv, state_rwkv_shift, state_ssm, state_ssm_conv, cache_swa_k, cache_swa_v, state_ffn_conv, norm_mix_pre, norm_mix_post, norm_ffn_pre, norm_ffn_post, w_in, w_out, gdn_conv_w, gdn_A_log, gdn_dt_bias, gdn_norm_w, rwkv_mu, rwkv_w0, rwkv_w2, rwkv_a0, rwkv_a2, rwkv_g2, rwkv_k_k, rwkv_k_a, rwkv_r_k, rwkv_ln_w, rwkv_ln_b, ssm_conv_w, ssm_conv_b, ssm_dt_bias, ssm_A_log, ssm_D, ssm_norm_w, ffn_w_up, ffn_conv_w, ffn_conv_b, ffn_w_down):
    params = dict(zip(PARAM_NAMES, (norm_mix_pre, norm_mix_post, norm_ffn_pre, norm_ffn_post, w_in, w_out,
                                    gdn_conv_w, gdn_A_log, gdn_dt_bias, gdn_norm_w, rwkv_mu, rwkv_w0, rwkv_w2,
                                    rwkv_a0, rwkv_a2, rwkv_g2, rwkv_k_k, rwkv_k_a, rwkv_r_k, rwkv_ln_w, rwkv_ln_b,
                                    ssm_conv_w, ssm_conv_b, ssm_dt_bias, ssm_A_log, ssm_D, ssm_norm_w, ffn_w_up,
                                    ffn_conv_w, ffn_conv_b, ffn_w_down)))
    depth = w_in.shape[0]
    xp, xs = x_prompt, x_sample
    t_dec = x_sample.shape[1]
    prompt_states, sample_states = [], []
    for li in range(depth):
        prm = {k: v[li] for k, v in params.items()}
        wts = prep_weights(prm)
        xp, stp = decoder_layer(xp, prm, wts, _zero_past(xp.shape[0]), None, (64, 64, 128))
        past = {'gdn': state_gdn[li], 'gdn_conv': state_gdn_conv[li], 'rwkv': state_rwkv[li],
                'rwkv_shift': state_rwkv_shift[li], 'ssm': state_ssm[li], 'ssm_conv': state_ssm_conv[li],
                'ffn_conv': state_ffn_conv[li]}
        xs, sts = decoder_layer(xs, prm, wts, past, (cache_swa_k[li], cache_swa_v[li]), (t_dec, t_dec, t_dec))
        prompt_states.append(stp)
        sample_states.append(sts)
    p_out = [jnp.stack(t) for t in zip(*prompt_states)]
    s_out = [jnp.stack(t) for t in zip(*sample_states)]
    return (xp, xs, *p_out, *s_out)
```

```python
import functools

import jax
import jax.numpy as jnp
from jax import lax
from jax.experimental import pallas as pl
from jax.experimental.pallas import tpu as pltpu

F32 = jnp.float32
BF16 = jnp.bfloat16
HI = lax.Precision.HIGHEST

D_MODEL = 4096
GROUP_W = D_MODEL // 4
GDN_HEADS = 8
GDN_D = GROUP_W // GDN_HEADS
GDN_TAPS = 4
RWKV_HS = 64
RWKV_HEADS = GROUP_W // RWKV_HS
RWKV_W_LORA = 64
RWKV_A_LORA = 64
RWKV_G_LORA = 160
RWKV_LORA = RWKV_W_LORA + RWKV_A_LORA + RWKV_G_LORA
RWKV_GN_EPS = 64e-5
SSM_P = 64
SSM_HEADS = GROUP_W // SSM_P
SSM_GROUPS = 2
SSM_N = 128
SSM_TAPS = 4
SSM_XBC = GROUP_W + 2 * SSM_GROUPS * SSM_N
SWA_HEADS = 8
SWA_HD = GROUP_W // SWA_HEADS
SWA_PATTERNS = ((128, 1), (512, 4), (2048, 16))
SWA_MAX_WINDOW = 2048
D_FF = 256 * ((8 * D_MODEL // 3 + 255) // 256)
FFN_TAPS = 3
NORM_EPS = 1e-6
NEG_INF = -1e30

LANE = 128
SUBLANE = 8
LORA_PAD = 384
G_LORA_PAD = LORA_PAD - RWKV_W_LORA - RWKV_A_LORA

COL_A_QKV = 0
COL_B_RKV = 3 * GROUP_W
COL_D_QKV = 6 * GROUP_W
COL_C_XBC = 9 * GROUP_W
COL_B_LORA = COL_C_XBC + SSM_XBC
COL_SMALL = COL_B_LORA + LORA_PAD
COL_A_Z = COL_SMALL + LANE
COL_C_Z = COL_A_Z + GROUP_W
N_PROJ = COL_C_Z + GROUP_W
SM_GDN_B = 0
SM_GDN_A = GDN_HEADS
SM_SSM_DT = 2 * GDN_HEADS

VMEM_LIMIT = 56 * 1024 * 1024
SWA_TILE = 512


def _cparams(sem):
    return pltpu.CompilerParams(dimension_semantics=sem, vmem_limit_bytes=VMEM_LIMIT)


def _dot(a, b, dims, prec):
    return lax.dot_general(a, b, (dims, ((), ())), precision=prec, preferred_element_type=F32)


def dot_nn(a, b, prec=HI):
    return _dot(a, b, ((1,), (0,)), prec)


def dot_nt(a, b, prec=HI):
    return _dot(a, b, ((1,), (1,)), prec)


def dot_tn(a, b, prec=HI):
    return _dot(a, b, ((0,), (0,)), prec)


def _silu(x):
    return x * jax.nn.sigmoid(x)


def _iota2(shape, axis):
    return lax.broadcasted_iota(jnp.int32, shape, axis)


def _log2(n):
    s = n.bit_length() - 1
    assert 1 << s == n
    return s


NN = ((1,), (0,))
NT = ((1,), (1,))
TN = ((0,), (0,))


def _split(x):
    hi = x.astype(BF16)
    return hi, (x - hi.astype(F32)).astype(BF16)


def dot3(ap, bp, dims):
    return (_dot(ap[0], bp[0], dims, None) + _dot(ap[0], bp[1], dims, None)
            + _dot(ap[1], bp[0], dims, None))


def dot1(a, b, dims):
    return _dot(a.astype(BF16), b.astype(BF16), dims, None)


def _inv_unit_lower_multi(ms, c):
    row = _iota2((c, c), 0)
    col = _iota2((c, c), 1)
    eye = jnp.where(row == col, 1.0, 0.0).astype(F32)
    base = min(SUBLANE, c)
    sb = _log2(base)
    blk = (row >> sb) == (col >> sb)
    ps = [jnp.where(blk, -m, 0.0) for m in ms]
    ts = [eye + p for p in ps]
    for _ in range(sb - 1):
        pss = [_split(p) for p in ps]
        ps = [dot3(x, x, NN) for x in pss]
        ts = [t + dot3(_split(t), _split(p), NN) for t, p in zip(ts, ps)]
    s = base
    while s < c:
        ls = _log2(s)
        off = ((row >> (ls + 1)) == (col >> (ls + 1))) & ((row >> ls) > (col >> ls))
        tss = [_split(t) for t in ts]
        inner = [dot3(_split(jnp.where(off, m, 0.0)), t2, NN) for m, t2 in zip(ms, tss)]
        ts = [t - dot3(t2, _split(x), NN) for t, t2, x in zip(ts, tss, inner)]
        s *= 2
    return ts


def _row_getter(x, c):
    if c % LANE:
        x = jnp.concatenate([x, jnp.zeros((LANE - c % LANE, LANE), F32)], axis=0)
    xt = x.T
    return lambda lane: xt[lane:lane + 1, :c]


def _cumsum_rows(x, c):
    tri = jnp.where(_iota2((c, c), 0) >= _iota2((c, c), 1), 1.0, 0.0).astype(F32)
    return dot_nn(tri, x)


def _shifted_taps(tail, x, taps):
    c = x.shape[0]
    xp = jnp.concatenate([tail, x], axis=0)
    out = []
    for s in range(taps - 1, 0, -1):
        out.append(pltpu.roll(xp, s, 0)[SUBLANE:SUBLANE + c])
    out.append(x)
    return out


def _rms_cast_kernel(x_ref, w_ref, o_ref):
    x = x_ref[...]
    y = x * lax.rsqrt(jnp.mean(x * x, axis=-1, keepdims=True) + NORM_EPS)
    o_ref[...] = (y * w_ref[...]).astype(o_ref.dtype)


def rms_cast(x, w, tr):
    m, d = x.shape
    return pl.pallas_call(
        _rms_cast_kernel,
        grid=(m // tr,),
        in_specs=[pl.BlockSpec((tr, d), lambda i: (i, 0)), pl.BlockSpec((1, d), lambda i: (0, 0))],
        out_specs=pl.BlockSpec((tr, d), lambda i: (i, 0)),
        out_shape=jax.ShapeDtypeStruct((m, d), BF16),
        compiler_params=_cparams(("parallel",)),
        name="rms_cast",
    )(x, w.reshape(1, d))


def _add_rms_kernel(x_ref, y_ref, w_ref, o_ref):
    y = y_ref[...]
    yn = y * lax.rsqrt(jnp.mean(y * y, axis=-1, keepdims=True) + NORM_EPS)
    o_ref[...] = x_ref[...] + yn * w_ref[...]


def add_rms(x, y, w, tr):
    m, d = x.shape
    return pl.pallas_call(
        _add_rms_kernel,
        grid=(m // tr,),
        in_specs=[pl.BlockSpec((tr, d), lambda i: (i, 0)), pl.BlockSpec((tr, d), lambda i: (i, 0)),
                  pl.BlockSpec((1, d), lambda i: (0, 0))],
        out_specs=pl.BlockSpec((tr, d), lambda i: (i, 0)),
        out_shape=jax.ShapeDtypeStruct((m, d), F32),
        compiler_params=_cparams(("parallel",)),
        name="add_rms",
    )(x, y, w.reshape(1, d))


def _mm_kernel(a_ref, w_ref, o_ref, *, nk):
    p = jnp.dot(a_ref[...], w_ref[...], preferred_element_type=F32)
    if nk == 1:
        o_ref[...] = p
    else:
        k = pl.program_id(2)

        @pl.when(k == 0)
        def _():
            o_ref[...] = p

        @pl.when(k > 0)
        def _():
            o_ref[...] += p


def matmul(a, w, tm, tn, tk, name):
    m, kd = a.shape
    n = w.shape[1]
    nk = kd // tk
    return pl.pallas_call(
        functools.partial(_mm_kernel, nk=nk),
        grid=(m // tm, n // tn, nk),
        in_specs=[pl.BlockSpec((tm, tk), lambda i, j, k: (i, k)), pl.BlockSpec((tk, tn), lambda i, j, k: (k, j))],
        out_specs=pl.BlockSpec((tm, tn), lambda i, j, k: (i, j)),
        out_shape=jax.ShapeDtypeStruct((m, n), F32),
        compiler_params=_cparams(("parallel", "parallel", "arbitrary")),
        name=name,
    )(a, w)


def _mm_groups_kernel(*refs):
    *a_refs, w_ref, o_ref = refs
    kg = a_refs[0].shape[1]
    acc = jnp.dot(a_refs[0][...], w_ref[0:kg, :], preferred_element_type=F32)
    for g in range(1, len(a_refs)):
        acc = acc + jnp.dot(a_refs[g][...], w_ref[g * kg:(g + 1) * kg, :], preferred_element_type=F32)
    o_ref[...] = acc


def matmul_groups(parts, w, tm, tn, name):
    m, kg = parts[0].shape
    kd, n = w.shape
    return pl.pallas_call(
        _mm_groups_kernel,
        grid=(m // tm, n // tn),
        in_specs=[pl.BlockSpec((tm, kg), lambda i, j: (i, 0)) for _ in parts]
        + [pl.BlockSpec((kd, tn), lambda i, j: (0, j))],
        out_specs=pl.BlockSpec((tm, tn), lambda i, j: (i, j)),
        out_shape=jax.ShapeDtypeStruct((m, n), F32),
        compiler_params=_cparams(("parallel", "parallel")),
        name=name,
    )(*parts, w)


def _ffn_act_kernel(g_ref, v_ref, halo_ref, st_ref, cw_ref, cb_ref, o_ref):
    g = g_ref[...]
    tail = jnp.where(pl.program_id(1) == 0, st_ref[...], halo_ref[...])
    taps = _shifted_taps(tail, g, FFN_TAPS)
    cw = cw_ref[...]
    y = taps[0] * cw[0:1]
    for i in range(1, FFN_TAPS):
        y = y + taps[i] * cw[i:i + 1]
    y = y + cb_ref[...]
    o_ref[...] = (_silu(y) * v_ref[...]).astype(o_ref.dtype)


def ffn_act(up3, state8, conv_w, conv_b, ts, tn):
    b, l, _ = up3.shape
    nj = D_FF // tn
    hb = ts // SUBLANE
    return pl.pallas_call(
        _ffn_act_kernel,
        grid=(b, l // ts, nj),
        in_specs=[
            pl.BlockSpec((None, ts, tn), lambda bi, i, j: (bi, i, j)),
            pl.BlockSpec((None, ts, tn), lambda bi, i, j: (bi, i, j + nj)),
            pl.BlockSpec((None, SUBLANE, tn), lambda bi, i, j: (bi, jnp.maximum(i * hb - 1, 0), j)),
            pl.BlockSpec((None, SUBLANE, tn), lambda bi, i, j: (bi, 0, j)),
            pl.BlockSpec((FFN_TAPS, tn), lambda bi, i, j: (0, j)),
            pl.BlockSpec((1, tn), lambda bi, i, j: (0, j)),
        ],
        out_specs=pl.BlockSpec((None, ts, tn), lambda bi, i, j: (bi, i, j)),
        out_shape=jax.ShapeDtypeStruct((b, l, D_FF), BF16),
        compiler_params=_cparams(("parallel", "parallel", "parallel")),
        name="ffn_act",
    )(up3, up3, up3, state8, conv_w, conv_b.reshape(1, D_FF))


def _gdn_kernel(qkv_ref, z_ref, sm_ref, cbuf_ref, s0_ref, cw_ref, alog_ref, dtb_ref, nw_ref,
                o_ref, sout_ref, s_scr, tail_scr, *, c):
    ci = pl.program_id(1)

    @pl.when(ci == 0)
    def _():
        s_scr[...] = s0_ref[...]
        tail_scr[...] = cbuf_ref[...]

    x = qkv_ref[...]
    taps = _shifted_taps(tail_scr[...], x, GDN_TAPS)
    tail_scr[...] = x[c - SUBLANE:]
    cw = cw_ref[...]
    y = taps[0] * cw[0:1]
    for i in range(1, GDN_TAPS):
        y = y + taps[i] * cw[i:i + 1]
    y = _silu(y)

    sm = sm_ref[...]
    beta_all = jax.nn.sigmoid(sm)
    g_all = -jnp.exp(alog_ref[...]) * jax.nn.softplus(sm + dtb_ref[...])
    gcum_all = _cumsum_rows(g_all, c)
    rows_of = _row_getter(gcum_all, c)

    row = _iota2((c, c), 0)
    col = _iota2((c, c), 1)
    tri = row >= col
    strict = row > col
    z = z_ref[...]
    nw = nw_ref[...]
    heads = range(GDN_HEADS)
    ms, aqk, rhs, qd, kd, gls = [], [], [], [], [], []
    for h in heads:
        lo = h * GDN_D
        q = y[:, lo:lo + GDN_D]
        k = y[:, GROUP_W + lo:GROUP_W + lo + GDN_D]
        v = y[:, 2 * GROUP_W + lo:2 * GROUP_W + lo + GDN_D]
        q = q * lax.rsqrt(jnp.sum(q * q, axis=-1, keepdims=True) + 1e-6) * (GDN_D ** -0.5)
        k = k * lax.rsqrt(jnp.sum(k * k, axis=-1, keepdims=True) + 1e-6)
        beta = beta_all[:, SM_GDN_B + h:SM_GDN_B + h + 1]
        gc = gcum_all[:, SM_GDN_A + h:SM_GDN_A + h + 1]
        gam = jnp.exp(jnp.where(tri, gc - rows_of(SM_GDN_A + h), -jnp.inf))
        kbeta = k * beta
        mq = dot3(_split(jnp.concatenate([kbeta, q], axis=0)), _split(k), NT)
        ms.append(jnp.where(strict, mq[:c] * gam, 0.0))
        aqk.append(mq[c:] * gam)
        eg = jnp.exp(gc)
        gl = gc[c - 1:c]
        rhs.append(jnp.concatenate([kbeta * eg, v * beta], axis=1))
        qd.append(q * eg)
        kd.append(k * jnp.exp(gl - gc))
        gls.append(gl)
    ts = _inv_unit_lower_multi(ms, c)
    wu = [dot3(_split(t), _split(x), NN) for t, x in zip(ts, rhs)]
    wqs = [dot3(_split(jnp.concatenate([x[:, :GDN_D], qd[h]], axis=0)), _split(s_scr[h]), NN)
           for h, x in enumerate(wu)]
    for h in heads:
        lo = h * GDN_D
        v_new = wu[h][:, GDN_D:] - wqs[h][:c]
        o = wqs[h][c:] + dot1(aqk[h], v_new, NN)
        s_scr[h] = s_scr[h] * jnp.exp(gls[h]) + dot3(_split(kd[h]), _split(v_new), TN)
        o = o * lax.rsqrt(jnp.mean(o * o, axis=-1, keepdims=True) + NORM_EPS) * nw
        o = o * _silu(z[:, lo:lo + GDN_D])
        o_ref[:, lo:lo + GDN_D] = o.astype(o_ref.dtype)

    @pl.when(ci == pl.num_programs(1) - 1)
    def _():
        sout_ref[...] = s_scr[...]


def gdn_mixer(proj3, cbuf8, s0, conv_w, a_log, dt_bias, norm_w, c):
    b, l, _ = proj3.shape
    alog_row = jnp.zeros((1, LANE), F32).at[0, SM_GDN_A:SM_GDN_A + GDN_HEADS].set(a_log)
    dtb_row = jnp.zeros((1, LANE), F32).at[0, SM_GDN_A:SM_GDN_A + GDN_HEADS].set(dt_bias)
    w3 = 3 * GROUP_W
    const2 = lambda bi, ci: (0, 0)
    return pl.pallas_call(
        functools.partial(_gdn_kernel, c=c),
        grid=(b, l // c),
        in_specs=[
            pl.BlockSpec((None, c, w3), lambda bi, ci: (bi, ci, COL_A_QKV // w3)),
            pl.BlockSpec((None, c, GROUP_W), lambda bi, ci: (bi, ci, COL_A_Z // GROUP_W)),
            pl.BlockSpec((None, c, LANE), lambda bi, ci: (bi, ci, COL_SMALL // LANE)),
            pl.BlockSpec((None, SUBLANE, w3), lambda bi, ci: (bi, 0, 0)),
            pl.BlockSpec((None, GDN_HEADS, GDN_D, GDN_D), lambda bi, ci: (bi, 0, 0, 0)),
            pl.BlockSpec((GDN_TAPS, w3), const2),
            pl.BlockSpec((1, LANE), const2),
            pl.BlockSpec((1, LANE), const2),
            pl.BlockSpec((1, GDN_D), const2),
        ],
        out_specs=[
            pl.BlockSpec((None, c, GROUP_W), lambda bi, ci: (bi, ci, 0)),
            pl.BlockSpec((None, GDN_HEADS, GDN_D, GDN_D), lambda bi, ci: (bi, 0, 0, 0)),
        ],
        out_shape=[jax.ShapeDtypeStruct((b, l, GROUP_W), BF16),
                   jax.ShapeDtypeStruct((b, GDN_HEADS, GDN_D, GDN_D), F32)],
        scratch_shapes=[pltpu.VMEM((GDN_HEADS, GDN_D, GDN_D), F32), pltpu.VMEM((SUBLANE, w3), F32)],
        compiler_params=_cparams(("parallel", "arbitrary")),
        name="gdn_mixer",
    )(proj3, proj3, proj3, cbuf8, s0, conv_w, alog_row, dtb_row, norm_w.reshape(1, GDN_D))


def _rwkv_kernel(rkv_ref, lora_ref, sh_rkv_ref, sh_lora_ref, s0_ref, mu_rkv_ref, mu_lora_ref,
                 w0_ref, w2_ref, a0_ref, a2_ref, g2_ref, kk_ref, ka_ref, rk_ref, lnw_ref, lnb_ref,
                 o_ref, sout_ref, s_scr, tail_rkv, tail_lora, *, c):
    ci = pl.program_id(1)

    @pl.when(ci == 0)
    def _():
        s_scr[...] = s0_ref[...]
        tail_rkv[...] = sh_rkv_ref[...]
        tail_lora[...] = sh_lora_ref[...]

    x = rkv_ref[...]
    xl = lora_ref[...]
    prev = _shifted_taps(tail_rkv[...], x, 2)[0]
    prev_l = _shifted_taps(tail_lora[...], xl, 2)[0]
    tail_rkv[...] = x[c - SUBLANE:]
    tail_lora[...] = xl[c - SUBLANE:]
    zm = x + (prev - x) * mu_rkv_ref[...]
    zl = xl + (prev_l - xl) * mu_lora_ref[...]
    r = zm[:, 0:GROUP_W]
    k = zm[:, GROUP_W:2 * GROUP_W]
    v = zm[:, 2 * GROUP_W:3 * GROUP_W]
    wd = zl[:, 0:RWKV_W_LORA]
    ad = zl[:, RWKV_W_LORA:RWKV_W_LORA + RWKV_A_LORA]
    gd = zl[:, RWKV_W_LORA + RWKV_A_LORA:LORA_PAD]

    w_log = -jax.nn.softplus(-(w0_ref[...] + dot3(_split(jnp.tanh(wd)), _split(w2_ref[...]), NN))) - 0.5
    logw = -jnp.exp(w_log)
    a = jax.nn.sigmoid(a0_ref[...] + dot3(_split(ad), _split(a2_ref[...]), NN))
    gate = dot1(jax.nn.sigmoid(gd), g2_ref[...], NN)
    kkv = k * kk_ref[...]
    k2 = k * (1.0 + (a - 1.0) * ka_ref[...])
    lcum = _cumsum_rows(logw, c)

    row = _iota2((c, c), 0)
    col = _iota2((c, c), 1)
    strict = row > col
    tri2 = _iota2((c, 2 * c), 0) >= (_iota2((c, 2 * c), 1) & (c - 1))
    rk = rk_ref[...]
    lnw = lnw_ref[...]
    lnb = lnb_ref[...]
    heads = range(RWKV_HEADS)
    sl = [slice(h * RWKV_HS, (h + 1) * RWKV_HS) for h in heads]
    acat, ars, rcat, av, bk_end, l_last, vs = [], [], [], [], [], [], []
    for h in heads:
        r_h, k_h, v_h, a_h = r[:, sl[h]], k2[:, sl[h]], v[:, sl[h]], a[:, sl[h]]
        kk = kkv[:, sl[h]]
        kk = kk * lax.rsqrt(jnp.sum(kk * kk, axis=-1, keepdims=True) + 1e-6)
        lc = lcum[:, sl[h]]
        ll = lc[c - 1:c]
        p_inv = jnp.exp(-lc)
        a_t = -kk * jnp.exp(lc - logw[:, sl[h]])
        b_vec = kk * a_h
        r_t = r_h * jnp.exp(lc)
        p_end = jnp.exp(ll - lc)
        bk_s = _split(jnp.concatenate([b_vec * p_inv, k_h * p_inv], axis=0))
        acat.append(dot3(_split(a_t), bk_s, NT))
        ars.append(dot3(_split(jnp.concatenate([a_t, r_t], axis=0)), _split(s_scr[h]), NT))
        rcat.append(jnp.where(tri2, _dot(r_t.astype(BF16), bk_s[0], NT, None), 0.0))
        bk_end.append(jnp.concatenate([b_vec * p_end, k_h * p_end], axis=0))
        l_last.append(ll)
        vs.append(v_h)
    a_ab = [jnp.where(strict, x[:, :c], 0.0) for x in acat]
    av = [dot3(_split(jnp.where(strict, x[:, c:], 0.0)), _split(v_h), NN) for x, v_h in zip(acat, vs)]
    ts = _inv_unit_lower_multi([-x for x in a_ab], c)
    us = [dot3(_split(t), _split(x[:c] + y), NN) for t, x, y in zip(ts, ars, av)]
    for h in heads:
        uv_s = _split(jnp.concatenate([us[h], vs[h]], axis=0))
        y = ars[h][c:] + _dot(rcat[h].astype(BF16), uv_s[0], NN, None)
        s_scr[h] = s_scr[h] * jnp.exp(l_last[h]) + dot3(uv_s, _split(bk_end[h]), TN)
        mean = jnp.mean(y, axis=-1, keepdims=True)
        var = jnp.mean(jnp.square(y - mean), axis=-1, keepdims=True)
        yn = (y - mean) * lax.rsqrt(var + RWKV_GN_EPS) * lnw[:, sl[h]] + lnb[:, sl[h]]
        bonus = jnp.sum(r[:, sl[h]] * k2[:, sl[h]] * rk[:, sl[h]], axis=-1, keepdims=True) * vs[h]
        o_ref[:, sl[h]] = ((yn + bonus) * gate[:, sl[h]]).astype(o_ref.dtype)

    @pl.when(ci == pl.num_programs(1) - 1)
    def _():
        sout_ref[...] = s_scr[...]


def rwkv_mixer(proj3, sh_rkv8, sh_lora8, s0, prm, c):
    b, l, _ = proj3.shape
    w3 = 3 * GROUP_W
    mu = prm['rwkv_mu']
    mu_rkv = mu[:w3].reshape(1, w3)
    mu_lora = jnp.pad(mu[w3:], (0, LORA_PAD - RWKV_LORA)).reshape(1, LORA_PAD)
    g2 = jnp.pad(prm['rwkv_g2'], ((0, G_LORA_PAD - RWKV_G_LORA), (0, 0)))
    row = lambda t: t.reshape(1, GROUP_W)
    const2 = lambda bi, ci: (0, 0)
    vec = pl.BlockSpec((1, GROUP_W), const2)
    return pl.pallas_call(
        functools.partial(_rwkv_kernel, c=c),
        grid=(b, l // c),
        in_specs=[
            pl.BlockSpec((None, c, w3), lambda bi, ci: (bi, ci, COL_B_RKV // w3)),
            pl.BlockSpec((None, c, LORA_PAD), lambda bi, ci: (bi, ci, COL_B_LORA // LORA_PAD)),
            pl.BlockSpec((None, SUBLANE, w3), lambda bi, ci: (bi, 0, 0)),
            pl.BlockSpec((None, SUBLANE, LORA_PAD), lambda bi, ci: (bi, 0, 0)),
            pl.BlockSpec((None, RWKV_HEADS, RWKV_HS, RWKV_HS), lambda bi, ci: (bi, 0, 0, 0)),
            pl.BlockSpec((1, w3), const2),
            pl.BlockSpec((1, LORA_PAD), const2),
            vec,
            pl.BlockSpec((RWKV_W_LORA, GROUP_W), const2),
            vec,
            pl.BlockSpec((RWKV_A_LORA, GROUP_W), const2),
            pl.BlockSpec((G_LORA_PAD, GROUP_W), const2),
            vec, vec, vec, vec, vec,
        ],
        out_specs=[
            pl.BlockSpec((None, c, GROUP_W), lambda bi, ci: (bi, ci, 0)),
            pl.BlockSpec((None, RWKV_HEADS, RWKV_HS, RWKV_HS), lambda bi, ci: (bi, 0, 0, 0)),
        ],
        out_shape=[jax.ShapeDtypeStruct((b, l, GROUP_W), BF16),
                   jax.ShapeDtypeStruct((b, RWKV_HEADS, RWKV_HS, RWKV_HS), F32)],
        scratch_shapes=[pltpu.VMEM((RWKV_HEADS, RWKV_HS, RWKV_HS), F32),
                        pltpu.VMEM((SUBLANE, w3), F32), pltpu.VMEM((SUBLANE, LORA_PAD), F32)],
        compiler_params=_cparams(("parallel", "arbitrary")),
        name="rwkv_mixer",
    )(proj3, proj3, sh_rkv8, sh_lora8, s0, mu_rkv, mu_lora,
      row(prm['rwkv_w0']), prm['rwkv_w2'], row(prm['rwkv_a0']), prm['rwkv_a2'], g2,
      row(prm['rwkv_k_k']), row(prm['rwkv_k_a']), row(prm['rwkv_r_k']),
      row(prm['rwkv_ln_w']), row(prm['rwkv_ln_b']))


def _ssd_kernel(z_ref, xbc_ref, sm_ref, cbuf_ref, s0_ref, cw_ref, cb_ref, dtb_ref, alog_ref, dsk_ref, nw_ref,
                o_ref, sout_ref, s_scr, tail_scr, y_scr, *, c):
    ci = pl.program_id(1)

    @pl.when(ci == 0)
    def _():
        s_scr[...] = s0_ref[...]
        tail_scr[...] = cbuf_ref[...]

    x = xbc_ref[...]
    taps = _shifted_taps(tail_scr[...], x, SSM_TAPS)
    tail_scr[...] = x[c - SUBLANE:]
    cw = cw_ref[...]
    y = taps[0] * cw[0:1]
    for i in range(1, SSM_TAPS):
        y = y + taps[i] * cw[i:i + 1]
    y = _silu(y + cb_ref[...])
    xs = y[:, 0:GROUP_W]
    gn = SSM_GROUPS * SSM_N

    sm = sm_ref[...]
    dt_all = jax.nn.softplus(sm + dtb_ref[...])
    da_all = dt_all * (-jnp.exp(alog_ref[...]))
    acs_all = _cumsum_rows(da_all, c)

    row = _iota2((c, c), 0)
    col = _iota2((c, c), 1)
    tri = row >= col
    z = z_ref[...]
    dsk = dsk_ref[...]
    hpg = SSM_HEADS // SSM_GROUPS
    gp = hpg * SSM_P
    rows_of = _row_getter(acs_all, c)
    for g in range(SSM_GROUPS):
        bm = y[:, GROUP_W + g * SSM_N:GROUP_W + (g + 1) * SSM_N]
        cm = y[:, GROUP_W + gn + g * SSM_N:GROUP_W + gn + (g + 1) * SSM_N]
        cb = dot1(cm, bm, NT)
        sg = s_scr[g]
        y_off = dot1(cm, sg, NT)
        xdec = []
        for rr in range(hpg):
            h = g * hpg + rr
            lo = h * SSM_P
            lane = SM_SSM_DT + h
            xs_h = xs[:, lo:lo + SSM_P]
            dt = dt_all[:, lane:lane + 1]
            acs = acs_all[:, lane:lane + 1]
            lmat = jnp.exp(jnp.where(tri, acs - rows_of(lane), -jnp.inf))
            xd = xs_h * dt
            a_last = acs[c - 1:c]
            xdec.append(xd * jnp.exp(a_last - acs))
            yh = dot1(cb * lmat, xd, NN) + y_off[:, rr * SSM_P:(rr + 1) * SSM_P] * jnp.exp(acs)
            yh = yh + xs_h * dsk[:, lane:lane + 1]
            y_scr[:, lo:lo + SSM_P] = yh * _silu(z[:, lo:lo + SSM_P])
        upd = dot3(_split(jnp.concatenate(xdec, axis=1)), _split(bm), TN)
        for rr in range(hpg):
            lane = SM_SSM_DT + g * hpg + rr
            dec = jnp.exp(acs_all[c - 1:c, lane:lane + 1])
            s_scr[g, rr * SSM_P:(rr + 1) * SSM_P, :] = (sg[rr * SSM_P:(rr + 1) * SSM_P] * dec
                                                         + upd[rr * SSM_P:(rr + 1) * SSM_P])

    gw = GROUP_W // SSM_GROUPS
    nw = nw_ref[...]
    for g in range(SSM_GROUPS):
        yg = y_scr[:, g * gw:(g + 1) * gw]
        yg = yg * lax.rsqrt(jnp.mean(yg * yg, axis=-1, keepdims=True) + NORM_EPS)
        o_ref[:, g * gw:(g + 1) * gw] = (yg * nw[:, g * gw:(g + 1) * gw]).astype(o_ref.dtype)

    @pl.when(ci == pl.num_programs(1) - 1)
    def _():
        sout_ref[...] = s_scr[...]


def ssd_mixer(proj3, cbuf8, s0, prm, c):
    b, l, _ = proj3.shape

    def lanes(t):
        return jnp.zeros((1, LANE), F32).at[0, SM_SSM_DT:SM_SSM_DT + SSM_HEADS].set(t)

    const2 = lambda bi, ci: (0, 0)
    small = pl.BlockSpec((1, LANE), const2)
    gshape = (SSM_GROUPS, SSM_HEADS // SSM_GROUPS * SSM_P, SSM_N)
    state_spec = pl.BlockSpec((None,) + gshape, lambda bi, ci: (bi, 0, 0, 0))
    o, s_new = pl.pallas_call(
        functools.partial(_ssd_kernel, c=c),
        grid=(b, l // c),
        in_specs=[
            pl.BlockSpec((None, c, GROUP_W), lambda bi, ci: (bi, ci, COL_C_Z // GROUP_W)),
            pl.BlockSpec((None, c, SSM_XBC), lambda bi, ci: (bi, ci, COL_C_XBC // SSM_XBC)),
            pl.BlockSpec((None, c, LANE), lambda bi, ci: (bi, ci, COL_SMALL // LANE)),
            pl.BlockSpec((None, SUBLANE, SSM_XBC), lambda bi, ci: (bi, 0, 0)),
            state_spec,
            pl.BlockSpec((SSM_TAPS, SSM_XBC), const2),
            pl.BlockSpec((1, SSM_XBC), const2),
            small, small, small,
            pl.BlockSpec((1, GROUP_W), const2),
        ],
        out_specs=[
            pl.BlockSpec((None, c, GROUP_W), lambda bi, ci: (bi, ci, 0)),
            state_spec,
        ],
        out_shape=[jax.ShapeDtypeStruct((b, l, GROUP_W), BF16),
                   jax.ShapeDtypeStruct((b,) + gshape, F32)],
        scratch_shapes=[pltpu.VMEM(gshape, F32), pltpu.VMEM((SUBLANE, SSM_XBC), F32),
                        pltpu.VMEM((c, GROUP_W), F32)],
        compiler_params=_cparams(("parallel", "arbitrary")),
        name="ssd_mixer",
    )(proj3, proj3, proj3, cbuf8, s0.reshape((b,) + gshape), prm['ssm_conv_w'],
      prm['ssm_conv_b'].reshape(1, SSM_XBC),
      lanes(prm['ssm_dt_bias']), lanes(prm['ssm_A_log']), lanes(prm['ssm_D']),
      prm['ssm_norm_w'].reshape(1, GROUP_W))
    return o, s_new.reshape(b, SSM_HEADS, SSM_P, SSM_N)


def _swa_weight(d):
    mult = jnp.zeros(d.shape, F32)
    for window, dil in SWA_PATTERNS:
        ok = (d >= 0) & (d <= window) & ((d & (dil - 1)) == 0)
        mult = mult + jnp.where(ok, 1.0, 0.0)
    return mult


def _swa_scores(q, k, d, slope):
    s = dot_nt(q.astype(BF16), k.astype(BF16), None) * (SWA_HD ** -0.5)
    mult = _swa_weight(d)
    s = s - slope * d.astype(F32)
    return jnp.where(mult > 0.0, s, NEG_INF), mult


def _swa_prompt_kernel(slopes_ref, q_ref, k_ref, v_ref, lw_ref, o_ref, m_scr, l_scr, acc_scr, *, t):
    h = pl.program_id(1)
    qi = pl.program_id(2)
    ki = pl.program_id(3)

    @pl.when(ki == 0)
    def _():
        m_scr[...] = jnp.full(m_scr.shape, NEG_INF, F32)
        l_scr[...] = jnp.zeros(l_scr.shape, F32)
        acc_scr[...] = jnp.zeros(acc_scr.shape, F32)

    @pl.when(ki <= qi)
    def _():
        q = (q_ref[...] * (SWA_HD ** -0.5)).astype(BF16)
        col = (ki * t + _iota2((1, t), 1)).astype(F32) * slopes_ref[h]
        s = _dot(q, k_ref[...].astype(BF16), NT, None) + (lw_ref[qi - ki] + col)
        m_old = m_scr[...]
        m_new = jnp.maximum(m_old, jnp.max(s, axis=-1, keepdims=True))
        alpha = jnp.exp(m_old - m_new)
        p = jnp.exp(s - m_new)
        l_scr[...] = alpha * l_scr[...] + jnp.sum(p, axis=-1, keepdims=True)
        acc_scr[...] = alpha * acc_scr[...] + _dot(p.astype(BF16), v_ref[...].astype(BF16), NN, None)
        m_scr[...] = m_new

    @pl.when(ki == pl.num_programs(3) - 1)
    def _():
        o_ref[...] = (acc_scr[...] / l_scr[...]).astype(o_ref.dtype)


def _alibi_slopes():
    return jnp.asarray([2.0 ** (-8.0 * (i + 1) / SWA_HEADS) for i in range(SWA_HEADS)], F32)


def _swa_log_weight_tiles(n, t):
    d = (jnp.arange(n)[:, None, None] * t + jnp.arange(t)[None, :, None]) - jnp.arange(t)[None, None, :]
    mult = _swa_weight(d.astype(jnp.int32))
    return jnp.where(mult > 0.0, jnp.log(jnp.maximum(mult, 1.0)), NEG_INF)


def swa_prompt(proj3, t):
    b, l, _ = proj3.shape
    qc = COL_D_QKV // SWA_HD
    n = l // t

    def kv_map(off):
        return lambda bi, h, qi, ki: (bi, jnp.minimum(ki, qi), qc + off + h)

    return pl.pallas_call(
        functools.partial(_swa_prompt_kernel, t=t),
        grid=(b, SWA_HEADS, n, n),
        in_specs=[
            pl.BlockSpec(memory_space=pltpu.SMEM),
            pl.BlockSpec((None, t, SWA_HD), lambda bi, h, qi, ki: (bi, qi, qc + h)),
            pl.BlockSpec((None, t, SWA_HD), kv_map(SWA_HEADS)),
            pl.BlockSpec((None, t, SWA_HD), kv_map(2 * SWA_HEADS)),
            pl.BlockSpec((n, t, t), lambda bi, h, qi, ki: (0, 0, 0)),
        ],
        out_specs=pl.BlockSpec((None, t, SWA_HD), lambda bi, h, qi, ki: (bi, qi, h)),
        out_shape=jax.ShapeDtypeStruct((b, l, GROUP_W), BF16),
        scratch_shapes=[pltpu.VMEM((t, 1), F32), pltpu.VMEM((t, 1), F32), pltpu.VMEM((t, SWA_HD), F32)],
        compiler_params=_cparams(("parallel", "parallel", "parallel", "arbitrary")),
        name="swa_prompt",
    )(_alibi_slopes(), proj3, proj3, proj3, _swa_log_weight_tiles(n, t))


def _swa_sample_kernel(slopes_ref, q_ref, k_ref, v_ref, ck_ref, cv_ref, o_ref, *, t, wb):
    slope = slopes_ref[pl.program_id(1)]
    q = q_ref[...]
    d_c = (wb + _iota2((t, wb), 0)) - _iota2((t, wb), 1)
    d_n = _iota2((t, t), 0) - _iota2((t, t), 1)
    s_c, mult_c = _swa_scores(q, ck_ref[...], d_c, slope)
    s_n, mult_n = _swa_scores(q, k_ref[...], d_n, slope)
    m = jnp.maximum(jnp.max(s_c, axis=-1, keepdims=True), jnp.max(s_n, axis=-1, keepdims=True))
    p_c = jnp.exp(s_c - m) * mult_c
    p_n = jnp.exp(s_n - m) * mult_n
    den = jnp.sum(p_c, axis=-1, keepdims=True) + jnp.sum(p_n, axis=-1, keepdims=True)
    num = (dot_nn(p_c.astype(BF16), cv_ref[...].astype(BF16), None)
           + dot_nn(p_n.astype(BF16), v_ref[...].astype(BF16), None))
    o_ref[...] = (num / den).astype(o_ref.dtype)


def swa_sample(proj3, cache_k, cache_v):
    b, t, _ = proj3.shape
    wb = cache_k.shape[1]
    qc = COL_D_QKV // SWA_HD
    ck = cache_k.reshape(b, wb, GROUP_W)
    cv = cache_v.reshape(b, wb, GROUP_W)
    return pl.pallas_call(
        functools.partial(_swa_sample_kernel, t=t, wb=wb),
        grid=(b, SWA_HEADS),
        in_specs=[
            pl.BlockSpec(memory_space=pltpu.SMEM),
            pl.BlockSpec((None, t, SWA_HD), lambda bi, h: (bi, 0, qc + h)),
            pl.BlockSpec((None, t, SWA_HD), lambda bi, h: (bi, 0, qc + SWA_HEADS + h)),
            pl.BlockSpec((None, t, SWA_HD), lambda bi, h: (bi, 0, qc + 2 * SWA_HEADS + h)),
            pl.BlockSpec((None, wb, SWA_HD), lambda bi, h: (bi, 0, h)),
            pl.BlockSpec((None, wb, SWA_HD), lambda bi, h: (bi, 0, h)),
        ],
        out_specs=pl.BlockSpec((None, t, SWA_HD), lambda bi, h: (bi, 0, h)),
        out_shape=jax.ShapeDtypeStruct((b, t, GROUP_W), BF16),
        compiler_params=_cparams(("parallel", "parallel")),
        name="swa_sample",
    )(_alibi_slopes(), proj3, proj3, proj3, ck, cv)


def _tiles(m):
    return (256, 1024) if m >= 1024 else (m, m)


def _front_pad_rows(t, rows=SUBLANE):
    return jnp.pad(t, ((0, 0), (rows - t.shape[1], 0), (0, 0)))


def prep_weights(p):
    w = p['w_in']
    o = 0
    seg = {}
    for name, n in (('a_qkv', 3 * GROUP_W), ('a_z', GROUP_W), ('a_b', GDN_HEADS), ('a_a', GDN_HEADS),
                    ('b_rkv', 3 * GROUP_W), ('b_lora', RWKV_LORA), ('c_z', GROUP_W), ('c_xbc', SSM_XBC),
                    ('c_dt', SSM_HEADS), ('d_qkv', 3 * GROUP_W)):
        seg[name] = w[:, o:o + n]
        o += n
    zeros = lambda n: jnp.zeros((D_MODEL, n), w.dtype)
    small_used = 2 * GDN_HEADS + SSM_HEADS
    w_in = jnp.concatenate([
        seg['a_qkv'], seg['b_rkv'], seg['d_qkv'], seg['c_xbc'],
        seg['b_lora'], zeros(LORA_PAD - RWKV_LORA),
        seg['a_b'], seg['a_a'], seg['c_dt'], zeros(LANE - small_used),
        seg['a_z'], seg['c_z']], axis=1).astype(BF16)
    return {'w_in': w_in, 'w_out': p['w_out'].astype(BF16), 'ffn_w_up': p['ffn_w_up'].astype(BF16),
            'ffn_w_down': p['ffn_w_down'].astype(BF16)}


def decoder_layer(x, prm, wts, past, swa_cache, chunks):
    b, l, _ = x.shape
    m = b * l
    tr, tm = _tiles(m)
    x2 = x.reshape(m, D_MODEL)
    c_gdn, c_rwkv, c_ssd = chunks

    h = rms_cast(x2, prm['norm_mix_pre'], tr)
    proj = matmul(h, wts['w_in'], tm, 512, D_MODEL, "mm_in")
    proj3 = proj.reshape(b, l, N_PROJ)

    o_a, gdn_s = gdn_mixer(proj3, _front_pad_rows(past['gdn_conv']), past['gdn'], prm['gdn_conv_w'],
                           prm['gdn_A_log'], prm['gdn_dt_bias'], prm['gdn_norm_w'], c_gdn)
    shift = past['rwkv_shift'][:, None, :]
    sh_rkv8 = _front_pad_rows(shift[:, :, :3 * GROUP_W])
    sh_lora8 = _front_pad_rows(jnp.pad(shift[:, :, 3 * GROUP_W:], ((0, 0), (0, 0), (0, LORA_PAD - RWKV_LORA))))
    o_b, rwkv_s = rwkv_mixer(proj3, sh_rkv8, sh_lora8, past['rwkv'], prm, c_rwkv)
    o_c, ssm_s = ssd_mixer(proj3, _front_pad_rows(past['ssm_conv']), past['ssm'], prm, c_ssd)
    if swa_cache is None:
        o_d = swa_prompt(proj3, min(l, SWA_TILE))
        keep = min(SWA_MAX_WINDOW, l)
    else:
        o_d = swa_sample(proj3, swa_cache[0], swa_cache[1])
        keep = l

    gdn_conv = proj3[:, l - (GDN_TAPS - 1):, COL_A_QKV:COL_A_QKV + 3 * GROUP_W]
    ssm_conv = proj3[:, l - (SSM_TAPS - 1):, COL_C_XBC:COL_C_XBC + SSM_XBC]
    rwkv_shift = jnp.concatenate([proj3[:, l - 1, COL_B_RKV:COL_B_RKV + 3 * GROUP_W],
                                  proj3[:, l - 1, COL_B_LORA:COL_B_LORA + RWKV_LORA]], axis=-1)
    kcol = COL_D_QKV + GROUP_W
    k_rows = proj3[:, l - keep:, kcol:kcol + GROUP_W].reshape(b, keep, SWA_HEADS, SWA_HD)
    v_rows = proj3[:, l - keep:, kcol + GROUP_W:kcol + 2 * GROUP_W].reshape(b, keep, SWA_HEADS, SWA_HD)

    y = matmul_groups([o.reshape(m, GROUP_W) for o in (o_a, o_b, o_c, o_d)], wts['w_out'], tm, 512, "mm_out")
    x2 = add_rms(x2, y, prm['norm_mix_post'], tr)

    h2 = rms_cast(x2, prm['norm_ffn_pre'], tr)
    up = matmul(h2, wts['ffn_w_up'], tm, 512, D_MODEL, "mm_up")
    up3 = up.reshape(b, l, 2 * D_FF)
    act = ffn_act(up3, _front_pad_rows(past['ffn_conv']), prm['ffn_conv_w'], prm['ffn_conv_b'],
                  min(l, 1024), 256 if m >= 1024 else D_FF // 2)
    ffn_conv = up3[:, l - (FFN_TAPS - 1):, :D_FF]
    y2 = matmul(act.reshape(m, D_FF), wts['ffn_w_down'], tm, 512, D_FF // 2, "mm_down")
    x2 = add_rms(x2, y2, prm['norm_ffn_post'], tr)
    return x2.reshape(b, l, D_MODEL), (gdn_s, gdn_conv, rwkv_s, rwkv_shift, ssm_s, ssm_conv, k_rows, v_rows,
                                       ffn_conv)


def _zero_past(bsz):
    return {
        'gdn': jnp.zeros((bsz, GDN_HEADS, GDN_D, GDN_D), F32),
        'gdn_conv': jnp.zeros((bsz, GDN_TAPS - 1, 3 * GROUP_W), F32),
        'rwkv': jnp.zeros((bsz, RWKV_HEADS, RWKV_HS, RWKV_HS), F32),
        'rwkv_shift': jnp.zeros((bsz, 3 * GROUP_W + RWKV_LORA), F32),
        'ssm': jnp.zeros((bsz, SSM_HEADS, SSM_P, SSM_N), F32),
        'ssm_conv': jnp.zeros((bsz, SSM_TAPS - 1, SSM_XBC), F32),
        'ffn_conv': jnp.zeros((bsz, FFN_TAPS - 1, D_FF), F32),
    }


PARAM_NAMES = ('norm_mix_pre', 'norm_mix_post', 'norm_ffn_pre', 'norm_ffn_post', 'w_in', 'w_out', 'gdn_conv_w',
               'gdn_A_log', 'gdn_dt_bias', 'gdn_norm_w', 'rwkv_mu', 'rwkv_w0', 'rwkv_w2', 'rwkv_a0', 'rwkv_a2',
               'rwkv_g2', 'rwkv_k_k', 'rwkv_k_a', 'rwkv_r_k', 'rwkv_ln_w', 'rwkv_ln_b', 'ssm_conv_w', 'ssm_conv_b',
               'ssm_dt_bias', 'ssm_A_log', 'ssm_D', 'ssm_norm_w', 'ffn_w_up', 'ffn_conv_w', 'ffn_conv_b',
               'ffn_w_down')


def kernel(x_prompt, x_sample, state_gdn, state_gdn_conv, state_rwkv, state_rwkv_shift, state_ssm, state_ssm_conv, cache_swa_k, cache_swa_v, state_ffn_conv, norm_mix_pre, norm_mix_post, norm_ffn_pre, norm_ffn_post, w_in, w_out, gdn_conv_w, gdn_A_log, gdn_dt_bias, gdn_norm_w, rwkv_mu, rwkv_w0, rwkv_w2, rwkv_a0, rwkv_a2, rwkv_g2, rwkv_k_k, rwkv_k_a, rwkv_r_k, rwkv_ln_w, rwkv_ln_b, ssm_conv_w, ssm_conv_b, ssm_dt_bias, ssm_A_log, ssm_D, ssm_norm_w, ffn_w_up, ffn_conv_w, ffn_conv_b, ffn_w_down):
    params = dict(zip(PARAM_NAMES, (norm_mix_pre, norm_mix_post, norm_ffn_pre, norm_ffn_post, w_in, w_out,
                                    gdn_conv_w, gdn_A_log, gdn_dt_bias, gdn_norm_w, rwkv_mu, rwkv_w0, rwkv_w2,
                                    rwkv_a0, rwkv_a2, rwkv_g2, rwkv_k_k, rwkv_k_a, rwkv_r_k, rwkv_ln_w, rwkv_ln_b,
                                    ssm_conv_w, ssm_conv_b, ssm_dt_bias, ssm_A_log, ssm_D, ssm_norm_w, ffn_w_up,
                                    ffn_conv_w, ffn_conv_b, ffn_w_down)))
    depth = w_in.shape[0]
    xp, xs = x_prompt, x_sample
    t_dec = x_sample.shape[1]
    prompt_states, sample_states = [], []
    for li in range(depth):
        prm = {k: v[li] for k, v in params.items()}
        wts = prep_weights(prm)
        xp, stp = decoder_layer(xp, prm, wts, _zero_past(xp.shape[0]), None, (64, 64, 128))
        past = {'gdn': state_gdn[li], 'gdn_conv': state_gdn_conv[li], 'rwkv': state_rwkv[li],
                'rwkv_shift': state_rwkv_shift[li], 'ssm': state_ssm[li], 'ssm_conv': state_ssm_conv[li],
                'ffn_conv': state_ffn_conv[li]}
        xs, sts = decoder_layer(xs, prm, wts, past, (cache_swa_k[li], cache_swa_v[li]), (t_dec, t_dec, t_dec))
        prompt_states.append(stp)
        sample_states.append(sts)
    p_out = [jnp.stack(t) for t in zip(*prompt_states)]
    s_out = [jnp.stack(t) for t in zip(*sample_states)]
    return (xp, xs, *p_out, *s_out)
```

```python
import functools

import jax
import jax.numpy as jnp
from jax import lax
from jax.experimental import pallas as pl
from jax.experimental.pallas import tpu as pltpu

F32 = jnp.float32
BF16 = jnp.bfloat16
HI = lax.Precision.HIGHEST

D_MODEL = 4096
GROUP_W = D_MODEL // 4
GDN_HEADS = 8
GDN_D = GROUP_W // GDN_HEADS
GDN_TAPS = 4
RWKV_HS = 64
RWKV_HEADS = GROUP_W // RWKV_HS
RWKV_W_LORA = 64
RWKV_A_LORA = 64
RWKV_G_LORA = 160
RWKV_LORA = RWKV_W_LORA + RWKV_A_LORA + RWKV_G_LORA
RWKV_GN_EPS = 64e-5
SSM_P = 64
SSM_HEADS = GROUP_W // SSM_P
SSM_GROUPS = 2
SSM_N = 128
SSM_TAPS = 4
SSM_XBC = GROUP_W + 2 * SSM_GROUPS * SSM_N
SWA_HEADS = 8
SWA_HD = GROUP_W // SWA_HEADS
SWA_PATTERNS = ((128, 1), (512, 4), (2048, 16))
SWA_MAX_WINDOW = 2048
D_FF = 256 * ((8 * D_MODEL // 3 + 255) // 256)
FFN_TAPS = 3
NORM_EPS = 1e-6
NEG_INF = -1e30

LANE = 128
SUBLANE = 8
LORA_PAD = 384
G_LORA_PAD = LORA_PAD - RWKV_W_LORA - RWKV_A_LORA

COL_A_QKV = 0
COL_B_RKV = 3 * GROUP_W
COL_D_QKV = 6 * GROUP_W
COL_C_XBC = 9 * GROUP_W
COL_B_LORA = COL_C_XBC + SSM_XBC
COL_SMALL = COL_B_LORA + LORA_PAD
COL_A_Z = COL_SMALL + LANE
COL_C_Z = COL_A_Z + GROUP_W
N_PROJ = COL_C_Z + GROUP_W
SM_GDN_B = 0
SM_GDN_A = GDN_HEADS
SM_SSM_DT = 2 * GDN_HEADS

VMEM_LIMIT = 56 * 1024 * 1024
SWA_TILE = 1024


def _cparams(sem):
    return pltpu.CompilerParams(dimension_semantics=sem, vmem_limit_bytes=VMEM_LIMIT)


def _dot(a, b, dims, prec):
    return lax.dot_general(a, b, (dims, ((), ())), precision=prec, preferred_element_type=F32)


def dot_nn(a, b, prec=HI):
    return _dot(a, b, ((1,), (0,)), prec)


def dot_nt(a, b, prec=HI):
    return _dot(a, b, ((1,), (1,)), prec)


def dot_tn(a, b, prec=HI):
    return _dot(a, b, ((0,), (0,)), prec)


def _silu(x):
    return x * jax.nn.sigmoid(x)


def _iota2(shape, axis):
    return lax.broadcasted_iota(jnp.int32, shape, axis)


def _log2(n):
    s = n.bit_length() - 1
    assert 1 << s == n
    return s


NN = ((1,), (0,))
NT = ((1,), (1,))
TN = ((0,), (0,))


def _split(x):
    hi = x.astype(BF16)
    return hi, (x - hi.astype(F32)).astype(BF16)


def _split_rows(x):
    hi = x.astype(BF16)
    hif = hi.astype(F32)
    return jnp.concatenate([hif, x - hif], axis=0).astype(BF16), hi


def dot3(ap, bp, dims):
    return (_dot(ap[0], bp[0], dims, None) + _dot(ap[0], bp[1], dims, None)
            + _dot(ap[1], bp[0], dims, None))


def dot3s(a, bp, dims):
    r = a.shape[0]
    stacked, hi = _split_rows(a)
    both = _dot(stacked, bp[0], dims, None)
    return both[:r] + both[r:] + _dot(hi, bp[1], dims, None)


def dot1(a, b, dims):
    return _dot(a.astype(BF16), b.astype(BF16), dims, None)


def _inv_unit_lower_multi(ms, c):
    row = _iota2((c, c), 0)
    col = _iota2((c, c), 1)
    eye = jnp.where(row == col, 1.0, 0.0).astype(F32)
    base = min(SUBLANE, c)
    sb = _log2(base)
    blk = (row >> sb) == (col >> sb)
    ps = [jnp.where(blk, -m, 0.0) for m in ms]
    ts = [eye + p for p in ps]
    if sb > 1:
        ps = [_dot(p.astype(BF16), p.astype(BF16), NN, None) for p in ps]
        for _ in range(sb - 2):
            both = [_dot(jnp.concatenate([t, p], axis=0).astype(BF16), p.astype(BF16), NN, None)
                    for t, p in zip(ts, ps)]
            ts = [t + x[:c] for t, x in zip(ts, both)]
            ps = [x[c:] for x in both]
        ts = [t + _dot(t.astype(BF16), p.astype(BF16), NN, None) for t, p in zip(ts, ps)]
    s = base
    while s < c:
        ls = _log2(s)
        off = ((row >> (ls + 1)) == (col >> (ls + 1))) & ((row >> ls) > (col >> ls))
        tbs = [t.astype(BF16) for t in ts]
        inner = [_dot(jnp.where(off, m, 0.0).astype(BF16), tb, NN, None) for m, tb in zip(ms, tbs)]
        ts = [t - _dot(tb, x.astype(BF16), NN, None) for t, tb, x in zip(ts, tbs, inner)]
        s *= 2
    res = [eye - t - dot3s(m, _split(t), NN) for m, t in zip(ms, ts)]
    return [t + _dot(t.astype(BF16), r.astype(BF16), NN, None) for t, r in zip(ts, res)]


def _row_getter(x, c):
    if c % LANE:
        x = jnp.concatenate([x, jnp.zeros((LANE - c % LANE, LANE), F32)], axis=0)
    xt = x.T
    return lambda lane: xt[lane:lane + 1, :c]


def _cumsum_rows(x, c):
    tri = jnp.where(_iota2((c, c), 0) >= _iota2((c, c), 1), 1.0, 0.0).astype(F32)
    return dot_nn(tri, x)


def _shifted_taps(tail, x, taps):
    c = x.shape[0]
    xp = jnp.concatenate([tail, x], axis=0)
    out = []
    for s in range(taps - 1, 0, -1):
        out.append(pltpu.roll(xp, s, 0)[SUBLANE:SUBLANE + c])
    out.append(x)
    return out


def _rms_cast_kernel(x_ref, w_ref, o_ref):
    x = x_ref[...]
    y = x * lax.rsqrt(jnp.mean(x * x, axis=-1, keepdims=True) + NORM_EPS)
    o_ref[...] = (y * w_ref[...]).astype(o_ref.dtype)


def rms_cast(x, w, tr):
    m, d = x.shape
    return pl.pallas_call(
        _rms_cast_kernel,
        grid=(m // tr,),
        in_specs=[pl.BlockSpec((tr, d), lambda i: (i, 0)), pl.BlockSpec((1, d), lambda i: (0, 0))],
        out_specs=pl.BlockSpec((tr, d), lambda i: (i, 0)),
        out_shape=jax.ShapeDtypeStruct((m, d), BF16),
        compiler_params=_cparams(("parallel",)),
        name="rms_cast",
    )(x, w.reshape(1, d))


def _add_rms_kernel(x_ref, y_ref, w_ref, o_ref):
    y = y_ref[...]
    yn = y * lax.rsqrt(jnp.mean(y * y, axis=-1, keepdims=True) + NORM_EPS)
    o_ref[...] = x_ref[...] + yn * w_ref[...]


def add_rms(x, y, w, tr):
    m, d = x.shape
    return pl.pallas_call(
        _add_rms_kernel,
        grid=(m // tr,),
        in_specs=[pl.BlockSpec((tr, d), lambda i: (i, 0)), pl.BlockSpec((tr, d), lambda i: (i, 0)),
                  pl.BlockSpec((1, d), lambda i: (0, 0))],
        out_specs=pl.BlockSpec((tr, d), lambda i: (i, 0)),
        out_shape=jax.ShapeDtypeStruct((m, d), F32),
        compiler_params=_cparams(("parallel",)),
        name="add_rms",
    )(x, y, w.reshape(1, d))


def _add_rms_cast_kernel(x_ref, y_ref, w_ref, wn_ref, o_ref, h_ref):
    y = y_ref[...]
    yn = y * lax.rsqrt(jnp.mean(y * y, axis=-1, keepdims=True) + NORM_EPS)
    x = x_ref[...] + yn * w_ref[...]
    o_ref[...] = x
    xn = x * lax.rsqrt(jnp.mean(x * x, axis=-1, keepdims=True) + NORM_EPS)
    h_ref[...] = (xn * wn_ref[...]).astype(h_ref.dtype)


def add_rms_cast(x, y, w, w_next, tr):
    m, d = x.shape
    row = pl.BlockSpec((tr, d), lambda i: (i, 0))
    vec = pl.BlockSpec((1, d), lambda i: (0, 0))
    return pl.pallas_call(
        _add_rms_cast_kernel,
        grid=(m // tr,),
        in_specs=[row, row, vec, vec],
        out_specs=[row, row],
        out_shape=[jax.ShapeDtypeStruct((m, d), F32), jax.ShapeDtypeStruct((m, d), BF16)],
        compiler_params=_cparams(("parallel",)),
        name="add_rms_cast",
    )(x, y, w.reshape(1, d), w_next.reshape(1, d))


def _mm_kernel(a_ref, w_ref, o_ref, *, nk):
    p = jnp.dot(a_ref[...], w_ref[...], preferred_element_type=F32)
    if nk == 1:
        o_ref[...] = p
    else:
        k = pl.program_id(2)

        @pl.when(k == 0)
        def _():
            o_ref[...] = p

        @pl.when(k > 0)
        def _():
            o_ref[...] += p


def matmul(a, w, tm, tn, tk, name):
    m, kd = a.shape
    n = w.shape[1]
    nk = kd // tk
    return pl.pallas_call(
        functools.partial(_mm_kernel, nk=nk),
        grid=(m // tm, n // tn, nk),
        in_specs=[pl.BlockSpec((tm, tk), lambda i, j, k: (i, k)), pl.BlockSpec((tk, tn), lambda i, j, k: (k, j))],
        out_specs=pl.BlockSpec((tm, tn), lambda i, j, k: (i, j)),
        out_shape=jax.ShapeDtypeStruct((m, n), F32),
        compiler_params=_cparams(("parallel", "parallel", "arbitrary")),
        name=name,
    )(a, w)


def _mm_groups_kernel(*refs):
    *a_refs, w_ref, o_ref = refs
    kg = a_refs[0].shape[1]
    acc = jnp.dot(a_refs[0][...], w_ref[0:kg, :], preferred_element_type=F32)
    for g in range(1, len(a_refs)):
        acc = acc + jnp.dot(a_refs[g][...], w_ref[g * kg:(g + 1) * kg, :], preferred_element_type=F32)
    o_ref[...] = acc


def matmul_groups(parts, w, tm, tn, name):
    m, kg = parts[0].shape
    kd, n = w.shape
    return pl.pallas_call(
        _mm_groups_kernel,
        grid=(m // tm, n // tn),
        in_specs=[pl.BlockSpec((tm, kg), lambda i, j: (i, 0)) for _ in parts]
        + [pl.BlockSpec((kd, tn), lambda i, j: (0, j))],
        out_specs=pl.BlockSpec((tm, tn), lambda i, j: (i, j)),
        out_shape=jax.ShapeDtypeStruct((m, n), F32),
        compiler_params=_cparams(("parallel", "parallel")),
        name=name,
    )(*parts, w)


def _ffn_act_kernel(g_ref, v_ref, halo_ref, st_ref, cw_ref, cb_ref, o_ref):
    g = g_ref[...]
    tail = jnp.where(pl.program_id(1) == 0, st_ref[...], halo_ref[...])
    taps = _shifted_taps(tail, g, FFN_TAPS)
    cw = cw_ref[...]
    y = taps[0] * cw[0:1]
    for i in range(1, FFN_TAPS):
        y = y + taps[i] * cw[i:i + 1]
    y = y + cb_ref[...]
    o_ref[...] = (_silu(y) * v_ref[...]).astype(o_ref.dtype)


def _ffn_up_act_kernel(x_ref, wg_ref, wv_ref, st_ref, cw_ref, cb_ref, act_ref, cst_ref, tail_scr,
                       *, tiles_per_seq, sub):
    i = pl.program_id(0)
    j = pl.program_id(1)
    wg = wg_ref[...]
    wv = wv_ref[...]
    cw = cw_ref[...]
    cb = cb_ref[...]
    tm = x_ref.shape[0]
    @pl.when(i % tiles_per_seq == 0)
    def _():
        tail_scr[j] = st_ref[...]

    tail = tail_scr[j]

    def project(r):
        x = x_ref[r * sub:(r + 1) * sub, :]
        return jnp.dot(x, wg, preferred_element_type=F32), jnp.dot(x, wv, preferred_element_type=F32)

    nxt = project(0)
    for r in range(tm // sub):
        g, v = nxt
        if r + 1 < tm // sub:
            nxt = project(r + 1)
        taps = _shifted_taps(tail, g, FFN_TAPS)
        y = taps[0] * cw[0:1]
        for t in range(1, FFN_TAPS):
            y = y + taps[t] * cw[t:t + 1]
        act_ref[r * sub:(r + 1) * sub, :] = (_silu(y + cb) * v).astype(act_ref.dtype)
        tail = g[sub - SUBLANE:]
    tail_scr[j] = tail
    cst_ref[...] = tail


def ffn_up_act(h2, w_up, state8, conv_w, conv_b, l, tm, tn, sub):
    m, kd = h2.shape
    b = m // l
    nj = D_FF // tn
    tiles_per_seq = l // tm
    act, tails = pl.pallas_call(
        functools.partial(_ffn_up_act_kernel, tiles_per_seq=tiles_per_seq, sub=sub),
        grid=(m // tm, nj),
        in_specs=[
            pl.BlockSpec((tm, kd), lambda i, j: (i, 0)),
            pl.BlockSpec((kd, tn), lambda i, j: (0, j)),
            pl.BlockSpec((kd, tn), lambda i, j: (0, j + nj)),
            pl.BlockSpec((None, SUBLANE, tn), lambda i, j: (i // tiles_per_seq, 0, j)),
            pl.BlockSpec((FFN_TAPS, tn), lambda i, j: (0, j)),
            pl.BlockSpec((1, tn), lambda i, j: (0, j)),
        ],
        out_specs=[
            pl.BlockSpec((tm, tn), lambda i, j: (i, j)),
            pl.BlockSpec((None, SUBLANE, tn), lambda i, j: (i, 0, j)),
        ],
        out_shape=[jax.ShapeDtypeStruct((m, D_FF), BF16), jax.ShapeDtypeStruct((m // tm, SUBLANE, D_FF), F32)],
        scratch_shapes=[pltpu.VMEM((nj, SUBLANE, tn), F32)],
        compiler_params=_cparams(("arbitrary", "arbitrary")),
        name="ffn_up_act",
    )(h2, w_up, w_up, state8, conv_w, conv_b.reshape(1, D_FF))
    return act, tails.reshape(b, tiles_per_seq, SUBLANE, D_FF)[:, tiles_per_seq - 1]


def ffn_act(up3, state8, conv_w, conv_b, ts, tn):
    b, l, _ = up3.shape
    nj = D_FF // tn
    hb = ts // SUBLANE
    return pl.pallas_call(
        _ffn_act_kernel,
        grid=(b, l // ts, nj),
        in_specs=[
            pl.BlockSpec((None, ts, tn), lambda bi, i, j: (bi, i, j)),
            pl.BlockSpec((None, ts, tn), lambda bi, i, j: (bi, i, j + nj)),
            pl.BlockSpec((None, SUBLANE, tn), lambda bi, i, j: (bi, jnp.maximum(i * hb - 1, 0), j)),
            pl.BlockSpec((None, SUBLANE, tn), lambda bi, i, j: (bi, 0, j)),
            pl.BlockSpec((FFN_TAPS, tn), lambda bi, i, j: (0, j)),
            pl.BlockSpec((1, tn), lambda bi, i, j: (0, j)),
        ],
        out_specs=pl.BlockSpec((None, ts, tn), lambda bi, i, j: (bi, i, j)),
        out_shape=jax.ShapeDtypeStruct((b, l, D_FF), BF16),
        compiler_params=_cparams(("parallel", "parallel", "parallel")),
        name="ffn_act",
    )(up3, up3, up3, state8, conv_w, conv_b.reshape(1, D_FF))


def _gdn_kernel(qkv_ref, z_ref, sm_ref, cbuf_ref, s0_ref, cw_ref, alog_ref, dtb_ref, nw_ref,
                o_ref, sout_ref, s_scr, tail_scr, *, c):
    ci = pl.program_id(1)

    @pl.when(ci == 0)
    def _():
        s_scr[...] = s0_ref[...]
        tail_scr[...] = cbuf_ref[...]

    x = qkv_ref[...]
    taps = _shifted_taps(tail_scr[...], x, GDN_TAPS)
    tail_scr[...] = x[c - SUBLANE:]
    cw = cw_ref[...]
    y = taps[0] * cw[0:1]
    for i in range(1, GDN_TAPS):
        y = y + taps[i] * cw[i:i + 1]
    y = _silu(y)

    sm = sm_ref[...]
    beta_all = jax.nn.sigmoid(sm)
    g_all = -jnp.exp(alog_ref[...]) * jax.nn.softplus(sm + dtb_ref[...])
    gcum_all = _cumsum_rows(g_all, c)
    rows_of = _row_getter(gcum_all, c)

    row = _iota2((c, c), 0)
    col = _iota2((c, c), 1)
    tri = row >= col
    strict = row > col
    z = z_ref[...]
    nw = nw_ref[...]
    heads = range(GDN_HEADS)
    ms, aqk, rhs, qd, kd, gls = [], [], [], [], [], []
    for h in heads:
        lo = h * GDN_D
        q = y[:, lo:lo + GDN_D]
        k = y[:, GROUP_W + lo:GROUP_W + lo + GDN_D]
        v = y[:, 2 * GROUP_W + lo:2 * GROUP_W + lo + GDN_D]
        q = q * lax.rsqrt(jnp.sum(q * q, axis=-1, keepdims=True) + 1e-6) * (GDN_D ** -0.5)
        k = k * lax.rsqrt(jnp.sum(k * k, axis=-1, keepdims=True) + 1e-6)
        beta = beta_all[:, SM_GDN_B + h:SM_GDN_B + h + 1]
        gc = gcum_all[:, SM_GDN_A + h:SM_GDN_A + h + 1]
        gam = jnp.exp(jnp.where(tri, gc - rows_of(SM_GDN_A + h), -jnp.inf))
        kbeta = k * beta
        mq = dot3s(jnp.concatenate([kbeta, q], axis=0), _split(k), NT)
        ms.append(jnp.where(strict, mq[:c] * gam, 0.0))
        aqk.append(mq[c:] * gam)
        eg = jnp.exp(gc)
        gl = gc[c - 1:c]
        rhs.append(jnp.concatenate([kbeta * eg, v * beta], axis=1))
        qd.append(q * eg)
        kd.append(k * jnp.exp(gl - gc))
        gls.append(gl)
    ts = _inv_unit_lower_multi(ms, c)
    wu = [dot3s(t, _split(x), NN) for t, x in zip(ts, rhs)]
    wqs = [dot3s(jnp.concatenate([x[:, :GDN_D], qd[h]], axis=0), _split(s_scr[h]), NN)
           for h, x in enumerate(wu)]
    for h in heads:
        lo = h * GDN_D
        v_new = wu[h][:, GDN_D:] - wqs[h][:c]
        o = wqs[h][c:] + dot1(aqk[h], v_new, NN)
        s_scr[h] = s_scr[h] * jnp.exp(gls[h]) + dot3(_split(kd[h]), _split(v_new), TN)
        o = o * lax.rsqrt(jnp.mean(o * o, axis=-1, keepdims=True) + NORM_EPS) * nw
        o = o * _silu(z[:, lo:lo + GDN_D])
        o_ref[:, lo:lo + GDN_D] = o.astype(o_ref.dtype)

    @pl.when(ci == pl.num_programs(1) - 1)
    def _():
        sout_ref[...] = s_scr[...]


def gdn_mixer(proj3, cbuf8, s0, conv_w, a_log, dt_bias, norm_w, c):
    b, l, _ = proj3.shape
    alog_row = jnp.zeros((1, LANE), F32).at[0, SM_GDN_A:SM_GDN_A + GDN_HEADS].set(a_log)
    dtb_row = jnp.zeros((1, LANE), F32).at[0, SM_GDN_A:SM_GDN_A + GDN_HEADS].set(dt_bias)
    w3 = 3 * GROUP_W
    const2 = lambda bi, ci: (0, 0)
    return pl.pallas_call(
        functools.partial(_gdn_kernel, c=c),
        grid=(b, l // c),
        in_specs=[
            pl.BlockSpec((None, c, w3), lambda bi, ci: (bi, ci, COL_A_QKV // w3)),
            pl.BlockSpec((None, c, GROUP_W), lambda bi, ci: (bi, ci, COL_A_Z // GROUP_W)),
            pl.BlockSpec((None, c, LANE), lambda bi, ci: (bi, ci, COL_SMALL // LANE)),
            pl.BlockSpec((None, SUBLANE, w3), lambda bi, ci: (bi, 0, 0)),
            pl.BlockSpec((None, GDN_HEADS, GDN_D, GDN_D), lambda bi, ci: (bi, 0, 0, 0)),
            pl.BlockSpec((GDN_TAPS, w3), const2),
            pl.BlockSpec((1, LANE), const2),
            pl.BlockSpec((1, LANE), const2),
            pl.BlockSpec((1, GDN_D), const2),
        ],
        out_specs=[
            pl.BlockSpec((None, c, GROUP_W), lambda bi, ci: (bi, ci, 0)),
            pl.BlockSpec((None, GDN_HEADS, GDN_D, GDN_D), lambda bi, ci: (bi, 0, 0, 0)),
        ],
        out_shape=[jax.ShapeDtypeStruct((b, l, GROUP_W), BF16),
                   jax.ShapeDtypeStruct((b, GDN_HEADS, GDN_D, GDN_D), F32)],
        scratch_shapes=[pltpu.VMEM((GDN_HEADS, GDN_D, GDN_D), F32), pltpu.VMEM((SUBLANE, w3), F32)],
        compiler_params=_cparams(("parallel", "arbitrary")),
        name="gdn_mixer",
    )(proj3, proj3, proj3, cbuf8, s0, conv_w, alog_row, dtb_row, norm_w.reshape(1, GDN_D))


def _rwkv_kernel(rkv_ref, lora_ref, sh_rkv_ref, sh_lora_ref, s0_ref, mu_rkv_ref, mu_lora_ref,
                 w0_ref, w2_ref, a0_ref, a2_ref, g2_ref, kk_ref, ka_ref, rk_ref, lnw_ref, lnb_ref,
                 o_ref, sout_ref, s_scr, tail_rkv, tail_lora, *, c):
    ci = pl.program_id(1)

    @pl.when(ci == 0)
    def _():
        s_scr[...] = s0_ref[...]
        tail_rkv[...] = sh_rkv_ref[...]
        tail_lora[...] = sh_lora_ref[...]

    x = rkv_ref[...]
    xl = lora_ref[...]
    prev = _shifted_taps(tail_rkv[...], x, 2)[0]
    prev_l = _shifted_taps(tail_lora[...], xl, 2)[0]
    tail_rkv[...] = x[c - SUBLANE:]
    tail_lora[...] = xl[c - SUBLANE:]
    zm = x + (prev - x) * mu_rkv_ref[...]
    zl = xl + (prev_l - xl) * mu_lora_ref[...]
    r = zm[:, 0:GROUP_W]
    k = zm[:, GROUP_W:2 * GROUP_W]
    v = zm[:, 2 * GROUP_W:3 * GROUP_W]
    wd = zl[:, 0:RWKV_W_LORA]
    ad = zl[:, RWKV_W_LORA:RWKV_W_LORA + RWKV_A_LORA]
    gd = zl[:, RWKV_W_LORA + RWKV_A_LORA:LORA_PAD]

    w_log = -jax.nn.softplus(-(w0_ref[...] + dot3s(jnp.tanh(wd), _split(w2_ref[...]), NN))) - 0.5
    logw = -jnp.exp(w_log)
    a = jax.nn.sigmoid(a0_ref[...] + dot3s(ad, _split(a2_ref[...]), NN))
    gate = dot1(jax.nn.sigmoid(gd), g2_ref[...], NN)
    kkv = k * kk_ref[...]
    k2 = k * (1.0 + (a - 1.0) * ka_ref[...])
    lcum = _cumsum_rows(logw, c)

    row = _iota2((c, c), 0)
    col = _iota2((c, c), 1)
    strict = row > col
    tri2 = _iota2((c, 2 * c), 0) >= (_iota2((c, 2 * c), 1) & (c - 1))
    rk = rk_ref[...]
    lnw = lnw_ref[...]
    lnb = lnb_ref[...]
    heads = range(RWKV_HEADS)
    sl = [slice(h * RWKV_HS, (h + 1) * RWKV_HS) for h in heads]
    acat, ars, rcat, av, bk_end, l_last, vs = [], [], [], [], [], [], []
    for h in heads:
        r_h, k_h, v_h, a_h = r[:, sl[h]], k2[:, sl[h]], v[:, sl[h]], a[:, sl[h]]
        kk = kkv[:, sl[h]]
        kk = kk * lax.rsqrt(jnp.sum(kk * kk, axis=-1, keepdims=True) + 1e-6)
        lc = lcum[:, sl[h]]
        ll = lc[c - 1:c]
        p_inv = jnp.exp(-lc)
        a_t = -kk * jnp.exp(lc - logw[:, sl[h]])
        b_vec = kk * a_h
        r_t = r_h * jnp.exp(lc)
        p_end = jnp.exp(ll - lc)
        bk_s = _split(jnp.concatenate([b_vec * p_inv, k_h * p_inv], axis=0))
        acat.append(dot3s(a_t, bk_s, NT))
        ars.append(dot3s(jnp.concatenate([a_t, r_t], axis=0), _split(s_scr[h]), NT))
        rcat.append(jnp.where(tri2, _dot(r_t.astype(BF16), bk_s[0], NT, None), 0.0))
        bk_end.append(jnp.concatenate([b_vec * p_end, k_h * p_end], axis=0))
        l_last.append(ll)
        vs.append(v_h)
    a_ab = [jnp.where(strict, x[:, :c], 0.0) for x in acat]
    av = [dot3s(jnp.where(strict, x[:, c:], 0.0), _split(v_h), NN) for x, v_h in zip(acat, vs)]
    ts = _inv_unit_lower_multi([-x for x in a_ab], c)
    us = [dot3s(t, _split(x[:c] + y), NN) for t, x, y in zip(ts, ars, av)]
    for h in heads:
        uv_s = _split(jnp.concatenate([us[h], vs[h]], axis=0))
        y = ars[h][c:] + _dot(rcat[h].astype(BF16), uv_s[0], NN, None)
        s_scr[h] = s_scr[h] * jnp.exp(l_last[h]) + dot3(uv_s, _split(bk_end[h]), TN)
        mean = jnp.mean(y, axis=-1, keepdims=True)
        var = jnp.mean(jnp.square(y - mean), axis=-1, keepdims=True)
        yn = (y - mean) * lax.rsqrt(var + RWKV_GN_EPS) * lnw[:, sl[h]] + lnb[:, sl[h]]
        bonus = jnp.sum(r[:, sl[h]] * k2[:, sl[h]] * rk[:, sl[h]], axis=-1, keepdims=True) * vs[h]
        o_ref[:, sl[h]] = ((yn + bonus) * gate[:, sl[h]]).astype(o_ref.dtype)

    @pl.when(ci == pl.num_programs(1) - 1)
    def _():
        sout_ref[...] = s_scr[...]


def rwkv_mixer(proj3, sh_rkv8, sh_lora8, s0, prm, c):
    b, l, _ = proj3.shape
    w3 = 3 * GROUP_W
    mu = prm['rwkv_mu']
    mu_rkv = mu[:w3].reshape(1, w3)
    mu_lora = jnp.pad(mu[w3:], (0, LORA_PAD - RWKV_LORA)).reshape(1, LORA_PAD)
    g2 = jnp.pad(prm['rwkv_g2'], ((0, G_LORA_PAD - RWKV_G_LORA), (0, 0)))
    row = lambda t: t.reshape(1, GROUP_W)
    const2 = lambda bi, ci: (0, 0)
    vec = pl.BlockSpec((1, GROUP_W), const2)
    return pl.pallas_call(
        functools.partial(_rwkv_kernel, c=c),
        grid=(b, l // c),
        in_specs=[
            pl.BlockSpec((None, c, w3), lambda bi, ci: (bi, ci, COL_B_RKV // w3)),
            pl.BlockSpec((None, c, LORA_PAD), lambda bi, ci: (bi, ci, COL_B_LORA // LORA_PAD)),
            pl.BlockSpec((None, SUBLANE, w3), lambda bi, ci: (bi, 0, 0)),
            pl.BlockSpec((None, SUBLANE, LORA_PAD), lambda bi, ci: (bi, 0, 0)),
            pl.BlockSpec((None, RWKV_HEADS, RWKV_HS, RWKV_HS), lambda bi, ci: (bi, 0, 0, 0)),
            pl.BlockSpec((1, w3), const2),
            pl.BlockSpec((1, LORA_PAD), const2),
            vec,
            pl.BlockSpec((RWKV_W_LORA, GROUP_W), const2),
            vec,
            pl.BlockSpec((RWKV_A_LORA, GROUP_W), const2),
            pl.BlockSpec((G_LORA_PAD, GROUP_W), const2),
            vec, vec, vec, vec, vec,
        ],
        out_specs=[
            pl.BlockSpec((None, c, GROUP_W), lambda bi, ci: (bi, ci, 0)),
            pl.BlockSpec((None, RWKV_HEADS, RWKV_HS, RWKV_HS), lambda bi, ci: (bi, 0, 0, 0)),
        ],
        out_shape=[jax.ShapeDtypeStruct((b, l, GROUP_W), BF16),
                   jax.ShapeDtypeStruct((b, RWKV_HEADS, RWKV_HS, RWKV_HS), F32)],
        scratch_shapes=[pltpu.VMEM((RWKV_HEADS, RWKV_HS, RWKV_HS), F32),
                        pltpu.VMEM((SUBLANE, w3), F32), pltpu.VMEM((SUBLANE, LORA_PAD), F32)],
        compiler_params=_cparams(("parallel", "arbitrary")),
        name="rwkv_mixer",
    )(proj3, proj3, sh_rkv8, sh_lora8, s0, mu_rkv, mu_lora,
      row(prm['rwkv_w0']), prm['rwkv_w2'], row(prm['rwkv_a0']), prm['rwkv_a2'], g2,
      row(prm['rwkv_k_k']), row(prm['rwkv_k_a']), row(prm['rwkv_r_k']),
      row(prm['rwkv_ln_w']), row(prm['rwkv_ln_b']))


def _ssd_kernel(z_ref, xbc_ref, sm_ref, cbuf_ref, s0_ref, cw_ref, cb_ref, dtb_ref, alog_ref, dsk_ref, nw_ref,
                o_ref, sout_ref, s_scr, tail_scr, y_scr, *, c):
    ci = pl.program_id(1)

    @pl.when(ci == 0)
    def _():
        s_scr[...] = s0_ref[...]
        tail_scr[...] = cbuf_ref[...]

    x = xbc_ref[...]
    taps = _shifted_taps(tail_scr[...], x, SSM_TAPS)
    tail_scr[...] = x[c - SUBLANE:]
    cw = cw_ref[...]
    y = taps[0] * cw[0:1]
    for i in range(1, SSM_TAPS):
        y = y + taps[i] * cw[i:i + 1]
    y = _silu(y + cb_ref[...])
    xs = y[:, 0:GROUP_W]
    gn = SSM_GROUPS * SSM_N

    sm = sm_ref[...]
    dt_all = jax.nn.softplus(sm + dtb_ref[...])
    da_all = dt_all * (-jnp.exp(alog_ref[...]))
    acs_all = _cumsum_rows(da_all, c)

    row = _iota2((c, c), 0)
    col = _iota2((c, c), 1)
    tri = row >= col
    z = z_ref[...]
    dsk = dsk_ref[...]
    hpg = SSM_HEADS // SSM_GROUPS
    gp = hpg * SSM_P
    rows_of = _row_getter(acs_all, c)
    for g in range(SSM_GROUPS):
        bm = y[:, GROUP_W + g * SSM_N:GROUP_W + (g + 1) * SSM_N]
        cm = y[:, GROUP_W + gn + g * SSM_N:GROUP_W + gn + (g + 1) * SSM_N]
        cb = dot1(cm, bm, NT)
        sg = s_scr[g]
        y_off = dot1(cm, sg, NT)
        xdec = []
        for rr in range(hpg):
            h = g * hpg + rr
            lo = h * SSM_P
            lane = SM_SSM_DT + h
            xs_h = xs[:, lo:lo + SSM_P]
            dt = dt_all[:, lane:lane + 1]
            acs = acs_all[:, lane:lane + 1]
            lmat = jnp.exp(jnp.where(tri, acs - rows_of(lane), -jnp.inf))
            xd = xs_h * dt
            a_last = acs[c - 1:c]
            xdec.append(xd * jnp.exp(a_last - acs))
            yh = dot1(cb * lmat, xd, NN) + y_off[:, rr * SSM_P:(rr + 1) * SSM_P] * jnp.exp(acs)
            yh = yh + xs_h * dsk[:, lane:lane + 1]
            y_scr[:, lo:lo + SSM_P] = yh * _silu(z[:, lo:lo + SSM_P])
        upd = dot3(_split(jnp.concatenate(xdec, axis=1)), _split(bm), TN)
        for rr in range(hpg):
            lane = SM_SSM_DT + g * hpg + rr
            dec = jnp.exp(acs_all[c - 1:c, lane:lane + 1])
            s_scr[g, rr * SSM_P:(rr + 1) * SSM_P, :] = (sg[rr * SSM_P:(rr + 1) * SSM_P] * dec
                                                         + upd[rr * SSM_P:(rr + 1) * SSM_P])

    gw = GROUP_W // SSM_GROUPS
    nw = nw_ref[...]
    for g in range(SSM_GROUPS):
        yg = y_scr[:, g * gw:(g + 1) * gw]
        yg = yg * lax.rsqrt(jnp.mean(yg * yg, axis=-1, keepdims=True) + NORM_EPS)
        o_ref[:, g * gw:(g + 1) * gw] = (yg * nw[:, g * gw:(g + 1) * gw]).astype(o_ref.dtype)

    @pl.when(ci == pl.num_programs(1) - 1)
    def _():
        sout_ref[...] = s_scr[...]


def ssd_mixer(proj3, cbuf8, s0, prm, c):
    b, l, _ = proj3.shape

    def lanes(t):
        return jnp.zeros((1, LANE), F32).at[0, SM_SSM_DT:SM_SSM_DT + SSM_HEADS].set(t)

    const2 = lambda bi, ci: (0, 0)
    small = pl.BlockSpec((1, LANE), const2)
    gshape = (SSM_GROUPS, SSM_HEADS // SSM_GROUPS * SSM_P, SSM_N)
    state_spec = pl.BlockSpec((None,) + gshape, lambda bi, ci: (bi, 0, 0, 0))
    o, s_new = pl.pallas_call(
        functools.partial(_ssd_kernel, c=c),
        grid=(b, l // c),
        in_specs=[
            pl.BlockSpec((None, c, GROUP_W), lambda bi, ci: (bi, ci, COL_C_Z // GROUP_W)),
            pl.BlockSpec((None, c, SSM_XBC), lambda bi, ci: (bi, ci, COL_C_XBC // SSM_XBC)),
            pl.BlockSpec((None, c, LANE), lambda bi, ci: (bi, ci, COL_SMALL // LANE)),
            pl.BlockSpec((None, SUBLANE, SSM_XBC), lambda bi, ci: (bi, 0, 0)),
            state_spec,
            pl.BlockSpec((SSM_TAPS, SSM_XBC), const2),
            pl.BlockSpec((1, SSM_XBC), const2),
            small, small, small,
            pl.BlockSpec((1, GROUP_W), const2),
        ],
        out_specs=[
            pl.BlockSpec((None, c, GROUP_W), lambda bi, ci: (bi, ci, 0)),
            state_spec,
        ],
        out_shape=[jax.ShapeDtypeStruct((b, l, GROUP_W), BF16),
                   jax.ShapeDtypeStruct((b,) + gshape, F32)],
        scratch_shapes=[pltpu.VMEM(gshape, F32), pltpu.VMEM((SUBLANE, SSM_XBC), F32),
                        pltpu.VMEM((c, GROUP_W), F32)],
        compiler_params=_cparams(("parallel", "arbitrary")),
        name="ssd_mixer",
    )(proj3, proj3, proj3, cbuf8, s0.reshape((b,) + gshape), prm['ssm_conv_w'],
      prm['ssm_conv_b'].reshape(1, SSM_XBC),
      lanes(prm['ssm_dt_bias']), lanes(prm['ssm_A_log']), lanes(prm['ssm_D']),
      prm['ssm_norm_w'].reshape(1, GROUP_W))
    return o, s_new.reshape(b, SSM_HEADS, SSM_P, SSM_N)


def _swa_weight(d):
    mult = jnp.zeros(d.shape, F32)
    for window, dil in SWA_PATTERNS:
        ok = (d >= 0) & (d <= window) & ((d & (dil - 1)) == 0)
        mult = mult + jnp.where(ok, 1.0, 0.0)
    return mult


def _swa_scores(q, k, d, slope):
    s = dot_nt(q.astype(BF16), k.astype(BF16), None) * (SWA_HD ** -0.5)
    mult = _swa_weight(d)
    s = s - slope * d.astype(F32)
    return jnp.where(mult > 0.0, s, NEG_INF), mult


def _swa_prompt_kernel(slopes_ref, q_ref, k_ref, v_ref, lw_ref, o_ref, m_scr, l_scr, acc_scr, *, t, sub):
    h = pl.program_id(1)
    qi = pl.program_id(2)
    ki = pl.program_id(3)

    @pl.when(ki == 0)
    def _():
        m_scr[...] = jnp.full(m_scr.shape, NEG_INF, F32)
        l_scr[...] = jnp.zeros(l_scr.shape, F32)
        acc_scr[...] = jnp.zeros(acc_scr.shape, F32)

    @pl.when(ki <= qi)
    def _():
        col = (ki * t + _iota2((1, t), 1)).astype(F32) * slopes_ref[h]
        kb = k_ref[...].astype(BF16)
        vb = v_ref[...].astype(BF16)
        def qk(r):
            q = (q_ref[r:r + sub, :] * (SWA_HD ** -0.5)).astype(BF16)
            return _dot(q, kb, NT, None)

        nxt = qk(0)
        for r in range(0, t, sub):
            rs = slice(r, r + sub)
            s = nxt + (lw_ref[qi - ki, rs, :] + col)
            if r + sub < t:
                nxt = qk(r + sub)
            m_old = m_scr[rs, :]
            m_new = jnp.maximum(m_old, jnp.max(s, axis=-1, keepdims=True))
            alpha = jnp.exp(m_old - m_new)
            p = jnp.exp(s - m_new)
            l_scr[rs, :] = alpha * l_scr[rs, :] + jnp.sum(p, axis=-1, keepdims=True)
            acc_scr[rs, :] = alpha * acc_scr[rs, :] + _dot(p.astype(BF16), vb, NN, None)
            m_scr[rs, :] = m_new

    @pl.when(ki == pl.num_programs(3) - 1)
    def _():
        o_ref[...] = (acc_scr[...] / l_scr[...]).astype(o_ref.dtype)


def _alibi_slopes():
    return jnp.asarray([2.0 ** (-8.0 * (i + 1) / SWA_HEADS) for i in range(SWA_HEADS)], F32)


def _swa_log_weight_tiles(n, t):
    d = (jnp.arange(n)[:, None, None] * t + jnp.arange(t)[None, :, None]) - jnp.arange(t)[None, None, :]
    mult = _swa_weight(d.astype(jnp.int32))
    return jnp.where(mult > 0.0, jnp.log(jnp.maximum(mult, 1.0)), NEG_INF)


def swa_prompt(proj3, t):
    b, l, _ = proj3.shape
    qc = COL_D_QKV // SWA_HD
    n = l // t

    def kv_map(off):
        return lambda bi, h, qi, ki: (bi, jnp.minimum(ki, qi), qc + off + h)

    return pl.pallas_call(
        functools.partial(_swa_prompt_kernel, t=t, sub=min(t, 256)),
        grid=(b, SWA_HEADS, n, n),
        in_specs=[
            pl.BlockSpec(memory_space=pltpu.SMEM),
            pl.BlockSpec((None, t, SWA_HD), lambda bi, h, qi, ki: (bi, qi, qc + h)),
            pl.BlockSpec((None, t, SWA_HD), kv_map(SWA_HEADS)),
            pl.BlockSpec((None, t, SWA_HD), kv_map(2 * SWA_HEADS)),
            pl.BlockSpec((n, t, t), lambda bi, h, qi, ki: (0, 0, 0)),
        ],
        out_specs=pl.BlockSpec((None, t, SWA_HD), lambda bi, h, qi, ki: (bi, qi, h)),
        out_shape=jax.ShapeDtypeStruct((b, l, GROUP_W), BF16),
        scratch_shapes=[pltpu.VMEM((t, 1), F32), pltpu.VMEM((t, 1), F32), pltpu.VMEM((t, SWA_HD), F32)],
        compiler_params=_cparams(("parallel", "parallel", "parallel", "arbitrary")),
        name="swa_prompt",
    )(_alibi_slopes(), proj3, proj3, proj3, _swa_log_weight_tiles(n, t))


def _swa_sample_kernel(slopes_ref, q_ref, k_ref, v_ref, ck_ref, cv_ref, o_ref, *, t, wb):
    slope = slopes_ref[pl.program_id(1)]
    q = q_ref[...]
    d_c = (wb + _iota2((t, wb), 0)) - _iota2((t, wb), 1)
    d_n = _iota2((t, t), 0) - _iota2((t, t), 1)
    s_c, mult_c = _swa_scores(q, ck_ref[...], d_c, slope)
    s_n, mult_n = _swa_scores(q, k_ref[...], d_n, slope)
    m = jnp.maximum(jnp.max(s_c, axis=-1, keepdims=True), jnp.max(s_n, axis=-1, keepdims=True))
    p_c = jnp.exp(s_c - m) * mult_c
    p_n = jnp.exp(s_n - m) * mult_n
    den = jnp.sum(p_c, axis=-1, keepdims=True) + jnp.sum(p_n, axis=-1, keepdims=True)
    num = (dot_nn(p_c.astype(BF16), cv_ref[...].astype(BF16), None)
           + dot_nn(p_n.astype(BF16), v_ref[...].astype(BF16), None))
    o_ref[...] = (num / den).astype(o_ref.dtype)


def swa_sample(proj3, cache_k, cache_v):
    b, t, _ = proj3.shape
    wb = cache_k.shape[1]
    qc = COL_D_QKV // SWA_HD
    ck = cache_k.reshape(b, wb, GROUP_W)
    cv = cache_v.reshape(b, wb, GROUP_W)
    return pl.pallas_call(
        functools.partial(_swa_sample_kernel, t=t, wb=wb),
        grid=(b, SWA_HEADS),
        in_specs=[
            pl.BlockSpec(memory_space=pltpu.SMEM),
            pl.BlockSpec((None, t, SWA_HD), lambda bi, h: (bi, 0, qc + h)),
            pl.BlockSpec((None, t, SWA_HD), lambda bi, h: (bi, 0, qc + SWA_HEADS + h)),
            pl.BlockSpec((None, t, SWA_HD), lambda bi, h: (bi, 0, qc + 2 * SWA_HEADS + h)),
            pl.BlockSpec((None, wb, SWA_HD), lambda bi, h: (bi, 0, h)),
            pl.BlockSpec((None, wb, SWA_HD), lambda bi, h: (bi, 0, h)),
        ],
        out_specs=pl.BlockSpec((None, t, SWA_HD), lambda bi, h: (bi, 0, h)),
        out_shape=jax.ShapeDtypeStruct((b, t, GROUP_W), BF16),
        compiler_params=_cparams(("parallel", "parallel")),
        name="swa_sample",
    )(_alibi_slopes(), proj3, proj3, proj3, ck, cv)


def _tiles(m):
    return (256, 1024) if m >= 1024 else (m, m)


def _front_pad_rows(t, rows=SUBLANE):
    return jnp.pad(t, ((0, 0), (rows - t.shape[1], 0), (0, 0)))


def prep_weights(p):
    w = p['w_in'].astype(BF16)
    o = 0
    seg = {}
    for name, n in (('a_qkv', 3 * GROUP_W), ('a_z', GROUP_W), ('a_b', GDN_HEADS), ('a_a', GDN_HEADS),
                    ('b_rkv', 3 * GROUP_W), ('b_lora', RWKV_LORA), ('c_z', GROUP_W), ('c_xbc', SSM_XBC),
                    ('c_dt', SSM_HEADS), ('d_qkv', 3 * GROUP_W)):
        seg[name] = w[:, o:o + n]
        o += n
    zeros = lambda n: jnp.zeros((D_MODEL, n), w.dtype)
    small_used = 2 * GDN_HEADS + SSM_HEADS
    w_in = jnp.concatenate([
        seg['a_qkv'], seg['b_rkv'], seg['d_qkv'], seg['c_xbc'],
        seg['b_lora'], zeros(LORA_PAD - RWKV_LORA),
        seg['a_b'], seg['a_a'], seg['c_dt'], zeros(LANE - small_used),
        seg['a_z'], seg['c_z']], axis=1)
    return {'w_in': w_in, 'w_out': p['w_out'].astype(BF16), 'ffn_w_up': p['ffn_w_up'].astype(BF16),
            'ffn_w_down': p['ffn_w_down'].astype(BF16)}


def decoder_layer(x, prm, wts, past, swa_cache, chunks, h=None, next_pre_w=None):
    b, l, _ = x.shape
    m = b * l
    tr, tm = _tiles(m)
    x2 = x.reshape(m, D_MODEL)
    c_gdn, c_rwkv, c_ssd = chunks

    if h is None:
        h = rms_cast(x2, prm['norm_mix_pre'], tr)
    proj = matmul(h, wts['w_in'], tm, 512, D_MODEL, "mm_in")
    proj3 = proj.reshape(b, l, N_PROJ)

    o_a, gdn_s = gdn_mixer(proj3, _front_pad_rows(past['gdn_conv']), past['gdn'], prm['gdn_conv_w'],
                           prm['gdn_A_log'], prm['gdn_dt_bias'], prm['gdn_norm_w'], c_gdn)
    shift = past['rwkv_shift'][:, None, :]
    sh_rkv8 = _front_pad_rows(shift[:, :, :3 * GROUP_W])
    sh_lora8 = _front_pad_rows(jnp.pad(shift[:, :, 3 * GROUP_W:], ((0, 0), (0, 0), (0, LORA_PAD - RWKV_LORA))))
    o_b, rwkv_s = rwkv_mixer(proj3, sh_rkv8, sh_lora8, past['rwkv'], prm, c_rwkv)
    o_c, ssm_s = ssd_mixer(proj3, _front_pad_rows(past['ssm_conv']), past['ssm'], prm, c_ssd)
    if swa_cache is None:
        o_d = swa_prompt(proj3, min(l, SWA_TILE))
        keep = min(SWA_MAX_WINDOW, l)
    else:
        o_d = swa_sample(proj3, swa_cache[0], swa_cache[1])
        keep = l

    gdn_conv = proj3[:, l - (GDN_TAPS - 1):, COL_A_QKV:COL_A_QKV + 3 * GROUP_W]
    ssm_conv = proj3[:, l - (SSM_TAPS - 1):, COL_C_XBC:COL_C_XBC + SSM_XBC]
    rwkv_shift = jnp.concatenate([proj3[:, l - 1, COL_B_RKV:COL_B_RKV + 3 * GROUP_W],
                                  proj3[:, l - 1, COL_B_LORA:COL_B_LORA + RWKV_LORA]], axis=-1)
    kcol = COL_D_QKV + GROUP_W
    k_rows = proj3[:, l - keep:, kcol:kcol + GROUP_W].reshape(b, keep, SWA_HEADS, SWA_HD)
    v_rows = proj3[:, l - keep:, kcol + GROUP_W:kcol + 2 * GROUP_W].reshape(b, keep, SWA_HEADS, SWA_HD)

    y = matmul_groups([o.reshape(m, GROUP_W) for o in (o_a, o_b, o_c, o_d)], wts['w_out'], tm, 512, "mm_out")
    x2, h2 = add_rms_cast(x2, y, prm['norm_mix_post'], prm['norm_ffn_pre'], tr)
    state8 = _front_pad_rows(past['ffn_conv'])
    if l % tm == 0:
        act, cst = ffn_up_act(h2, wts['ffn_w_up'], state8, prm['ffn_conv_w'], prm['ffn_conv_b'], l, tm, 256, 256)
        ffn_conv = cst[:, SUBLANE - (FFN_TAPS - 1):, :]
    else:
        up3 = matmul(h2, wts['ffn_w_up'], tm, 512, D_MODEL, "mm_up").reshape(b, l, 2 * D_FF)
        act = ffn_act(up3, state8, prm['ffn_conv_w'], prm['ffn_conv_b'], l, D_FF // 2).reshape(m, D_FF)
        ffn_conv = up3[:, l - (FFN_TAPS - 1):, :D_FF]
    y2 = matmul(act, wts['ffn_w_down'], tm, 512, D_FF // 2, "mm_down")
    if next_pre_w is None:
        x2, h_next = add_rms(x2, y2, prm['norm_ffn_post'], tr), None
    else:
        x2, h_next = add_rms_cast(x2, y2, prm['norm_ffn_post'], next_pre_w, tr)
    return x2.reshape(b, l, D_MODEL), h_next, (gdn_s, gdn_conv, rwkv_s, rwkv_shift, ssm_s, ssm_conv, k_rows, v_rows,
                                       ffn_conv)


def _zero_past(bsz):
    return {
        'gdn': jnp.zeros((bsz, GDN_HEADS, GDN_D, GDN_D), F32),
        'gdn_conv': jnp.zeros((bsz, GDN_TAPS - 1, 3 * GROUP_W), F32),
        'rwkv': jnp.zeros((bsz, RWKV_HEADS, RWKV_HS, RWKV_HS), F32),
        'rwkv_shift': jnp.zeros((bsz, 3 * GROUP_W + RWKV_LORA), F32),
        'ssm': jnp.zeros((bsz, SSM_HEADS, SSM_P, SSM_N), F32),
        'ssm_conv': jnp.zeros((bsz, SSM_TAPS - 1, SSM_XBC), F32),
        'ffn_conv': jnp.zeros((bsz, FFN_TAPS - 1, D_FF), F32),
    }


PARAM_NAMES = ('norm_mix_pre', 'norm_mix_post', 'norm_ffn_pre', 'norm_ffn_post', 'w_in', 'w_out', 'gdn_conv_w',
               'gdn_A_log', 'gdn_dt_bias', 'gdn_norm_w', 'rwkv_mu', 'rwkv_w0', 'rwkv_w2', 'rwkv_a0', 'rwkv_a2',
               'rwkv_g2', 'rwkv_k_k', 'rwkv_k_a', 'rwkv_r_k', 'rwkv_ln_w', 'rwkv_ln_b', 'ssm_conv_w', 'ssm_conv_b',
               'ssm_dt_bias', 'ssm_A_log', 'ssm_D', 'ssm_norm_w', 'ffn_w_up', 'ffn_conv_w', 'ffn_conv_b',
               'ffn_w_down')


def kernel(x_prompt, x_sample, state_gdn, state_gdn_conv, state_rwkv, state_rwkv_shift, state_ssm, state_ssm_conv, cache_swa_k, cache_swa_v, state_ffn_conv, norm_mix_pre, norm_mix_post, norm_ffn_pre, norm_ffn_post, w_in, w_out, gdn_conv_w, gdn_A_log, gdn_dt_bias, gdn_norm_w, rwkv_mu, rwkv_w0, rwkv_w2, rwkv_a0, rwkv_a2, rwkv_g2, rwkv_k_k, rwkv_k_a, rwkv_r_k, rwkv_ln_w, rwkv_ln_b, ssm_conv_w, ssm_conv_b, ssm_dt_bias, ssm_A_log, ssm_D, ssm_norm_w, ffn_w_up, ffn_conv_w, ffn_conv_b, ffn_w_down):
    params = dict(zip(PARAM_NAMES, (norm_mix_pre, norm_mix_post, norm_ffn_pre, norm_ffn_post, w_in, w_out,
                                    gdn_conv_w, gdn_A_log, gdn_dt_bias, gdn_norm_w, rwkv_mu, rwkv_w0, rwkv_w2,
                                    rwkv_a0, rwkv_a2, rwkv_g2, rwkv_k_k, rwkv_k_a, rwkv_r_k, rwkv_ln_w, rwkv_ln_b,
                                    ssm_conv_w, ssm_conv_b, ssm_dt_bias, ssm_A_log, ssm_D, ssm_norm_w, ffn_w_up,
                                    ffn_conv_w, ffn_conv_b, ffn_w_down)))
    depth = w_in.shape[0]
    xp, xs = x_prompt, x_sample
    t_dec = x_sample.shape[1]
    prompt_states, sample_states = [], []
    hp = hs = None
    for li in range(depth):
        prm = {k: v[li] for k, v in params.items()}
        wts = prep_weights(prm)
        nxt = norm_mix_pre[li + 1] if li + 1 < depth else None
        xp, hp, stp = decoder_layer(xp, prm, wts, _zero_past(xp.shape[0]), None, (64, 64, 128), hp, nxt)
        past = {'gdn': state_gdn[li], 'gdn_conv': state_gdn_conv[li], 'rwkv': state_rwkv[li],
                'rwkv_shift': state_rwkv_shift[li], 'ssm': state_ssm[li], 'ssm_conv': state_ssm_conv[li],
                'ffn_conv': state_ffn_conv[li]}
        xs, hs, sts = decoder_layer(xs, prm, wts, past, (cache_swa_k[li], cache_swa_v[li]),
                                    (t_dec, t_dec, t_dec), hs, nxt)
        prompt_states.append(stp)
        sample_states.append(sts)
    p_out = [jnp.stack(t) for t in zip(*prompt_states)]
    s_out = [jnp.stack(t) for t in zip(*sample_states)]
    return (xp, xs, *p_out, *s_out)
```

```python
import functools

import jax
import jax.numpy as jnp
from jax import lax
from jax.experimental import pallas as pl
from jax.experimental.pallas import tpu as pltpu

F32 = jnp.float32
BF16 = jnp.bfloat16
HI = lax.Precision.HIGHEST

D_MODEL = 4096
GROUP_W = D_MODEL // 4
GDN_HEADS = 8
GDN_D = GROUP_W // GDN_HEADS
GDN_TAPS = 4
RWKV_HS = 64
RWKV_HEADS = GROUP_W // RWKV_HS
RWKV_W_LORA = 64
RWKV_A_LORA = 64
RWKV_G_LORA = 160
RWKV_LORA = RWKV_W_LORA + RWKV_A_LORA + RWKV_G_LORA
RWKV_GN_EPS = 64e-5
SSM_P = 64
SSM_HEADS = GROUP_W // SSM_P
SSM_GROUPS = 2
SSM_N = 128
SSM_TAPS = 4
SSM_XBC = GROUP_W + 2 * SSM_GROUPS * SSM_N
SWA_HEADS = 8
SWA_HD = GROUP_W // SWA_HEADS
SWA_PATTERNS = ((128, 1), (512, 4), (2048, 16))
SWA_MAX_WINDOW = 2048
D_FF = 256 * ((8 * D_MODEL // 3 + 255) // 256)
FFN_TAPS = 3
NORM_EPS = 1e-6
NEG_INF = -1e30

LANE = 128
SUBLANE = 8
LORA_PAD = 384
G_LORA_PAD = LORA_PAD - RWKV_W_LORA - RWKV_A_LORA

COL_A_QKV = 0
COL_B_RKV = 3 * GROUP_W
COL_D_Q = 6 * GROUP_W
COL_A_Z = 7 * GROUP_W
COL_C_Z = 8 * GROUP_W
COL_C_XBC = 9 * GROUP_W
COL_B_LORA = COL_C_XBC + SSM_XBC
COL_SMALL = COL_B_LORA + LORA_PAD
N_PROJ = COL_SMALL + LANE
SM_GDN_B = 0
SM_GDN_A = GDN_HEADS
SM_SSM_DT = 2 * GDN_HEADS

VMEM_LIMIT = 56 * 1024 * 1024
SWA_TILE = 1024


def _cparams(sem):
    return pltpu.CompilerParams(dimension_semantics=sem, vmem_limit_bytes=VMEM_LIMIT)


def _dot(a, b, dims, prec):
    return lax.dot_general(a, b, (dims, ((), ())), precision=prec, preferred_element_type=F32)


def dot_nn(a, b, prec=HI):
    return _dot(a, b, ((1,), (0,)), prec)


def dot_nt(a, b, prec=HI):
    return _dot(a, b, ((1,), (1,)), prec)


def dot_tn(a, b, prec=HI):
    return _dot(a, b, ((0,), (0,)), prec)


def _silu(x):
    return x * jax.nn.sigmoid(x)


def _iota2(shape, axis):
    return lax.broadcasted_iota(jnp.int32, shape, axis)


def _log2(n):
    s = n.bit_length() - 1
    assert 1 << s == n
    return s


NN = ((1,), (0,))
NT = ((1,), (1,))
TN = ((0,), (0,))


def _split(x):
    hi = x.astype(BF16)
    return hi, (x - hi.astype(F32)).astype(BF16)


def _split_rows(x):
    hi = x.astype(BF16)
    hif = hi.astype(F32)
    return jnp.concatenate([hif, x - hif], axis=0).astype(BF16), hi


def dot3(ap, bp, dims):
    return (_dot(ap[0], bp[0], dims, None) + _dot(ap[0], bp[1], dims, None)
            + _dot(ap[1], bp[0], dims, None))


def dot3s(a, bp, dims):
    r = a.shape[0]
    stacked, hi = _split_rows(a)
    both = _dot(stacked, bp[0], dims, None)
    return both[:r] + both[r:] + _dot(hi, bp[1], dims, None)


def dot1(a, b, dims):
    return _dot(a.astype(BF16), b.astype(BF16), dims, None)


def _inv_unit_lower_multi(ms, c):
    row = _iota2((c, c), 0)
    col = _iota2((c, c), 1)
    eye = jnp.where(row == col, 1.0, 0.0).astype(F32)
    base = min(SUBLANE, c)
    sb = _log2(base)
    blk = (row >> sb) == (col >> sb)
    ps = [jnp.where(blk, -m, 0.0) for m in ms]
    ts = [eye + p for p in ps]
    if sb > 1:
        ps = [_dot(p.astype(BF16), p.astype(BF16), NN, None) for p in ps]
        for _ in range(sb - 2):
            both = [_dot(jnp.concatenate([t, p], axis=0).astype(BF16), p.astype(BF16), NN, None)
                    for t, p in zip(ts, ps)]
            ts = [t + x[:c] for t, x in zip(ts, both)]
            ps = [x[c:] for x in both]
        ts = [t + _dot(t.astype(BF16), p.astype(BF16), NN, None) for t, p in zip(ts, ps)]
    s = base
    while s < c:
        ls = _log2(s)
        off = ((row >> (ls + 1)) == (col >> (ls + 1))) & ((row >> ls) > (col >> ls))
        tbs = [t.astype(BF16) for t in ts]
        inner = [_dot(jnp.where(off, m, 0.0).astype(BF16), tb, NN, None) for m, tb in zip(ms, tbs)]
        ts = [t - _dot(tb, x.astype(BF16), NN, None) for t, tb, x in zip(ts, tbs, inner)]
        s *= 2
    res = [eye - t - dot3s(m, _split(t), NN) for m, t in zip(ms, ts)]
    return [t + _dot(t.astype(BF16), r.astype(BF16), NN, None) for t, r in zip(ts, res)]


def _row_getter(x, c):
    if c % LANE:
        x = jnp.concatenate([x, jnp.zeros((LANE - c % LANE, LANE), F32)], axis=0)
    xt = x.T
    return lambda lane: xt[lane:lane + 1, :c]


def _cumsum_rows(x, c):
    tri = jnp.where(_iota2((c, c), 0) >= _iota2((c, c), 1), 1.0, 0.0).astype(F32)
    return dot_nn(tri, x)


def _shifted_taps(tail, x, taps):
    c = x.shape[0]
    xp = jnp.concatenate([tail, x], axis=0)
    out = []
    for s in range(taps - 1, 0, -1):
        out.append(pltpu.roll(xp, s, 0)[SUBLANE:SUBLANE + c])
    out.append(x)
    return out


def _rms_cast_kernel(x_ref, w_ref, o_ref):
    x = x_ref[...]
    y = x * lax.rsqrt(jnp.mean(x * x, axis=-1, keepdims=True) + NORM_EPS)
    o_ref[...] = (y * w_ref[...]).astype(o_ref.dtype)


def rms_cast(x, w, tr):
    m, d = x.shape
    return pl.pallas_call(
        _rms_cast_kernel,
        grid=(m // tr,),
        in_specs=[pl.BlockSpec((tr, d), lambda i: (i, 0)), pl.BlockSpec((1, d), lambda i: (0, 0))],
        out_specs=pl.BlockSpec((tr, d), lambda i: (i, 0)),
        out_shape=jax.ShapeDtypeStruct((m, d), BF16),
        compiler_params=_cparams(("parallel",)),
        name="rms_cast",
    )(x, w.reshape(1, d))


def _add_rms_kernel(x_ref, y_ref, w_ref, o_ref):
    y = y_ref[...]
    yn = y * lax.rsqrt(jnp.mean(y * y, axis=-1, keepdims=True) + NORM_EPS)
    o_ref[...] = x_ref[...] + yn * w_ref[...]


def add_rms(x, y, w, tr):
    m, d = x.shape
    return pl.pallas_call(
        _add_rms_kernel,
        grid=(m // tr,),
        in_specs=[pl.BlockSpec((tr, d), lambda i: (i, 0)), pl.BlockSpec((tr, d), lambda i: (i, 0)),
                  pl.BlockSpec((1, d), lambda i: (0, 0))],
        out_specs=pl.BlockSpec((tr, d), lambda i: (i, 0)),
        out_shape=jax.ShapeDtypeStruct((m, d), F32),
        compiler_params=_cparams(("parallel",)),
        name="add_rms",
    )(x, y, w.reshape(1, d))


def _add_rms_cast_kernel(x_ref, y_ref, w_ref, wn_ref, o_ref, h_ref):
    y = y_ref[...]
    yn = y * lax.rsqrt(jnp.mean(y * y, axis=-1, keepdims=True) + NORM_EPS)
    x = x_ref[...] + yn * w_ref[...]
    o_ref[...] = x
    xn = x * lax.rsqrt(jnp.mean(x * x, axis=-1, keepdims=True) + NORM_EPS)
    h_ref[...] = (xn * wn_ref[...]).astype(h_ref.dtype)


def add_rms_cast(x, y, w, w_next, tr):
    m, d = x.shape
    row = pl.BlockSpec((tr, d), lambda i: (i, 0))
    vec = pl.BlockSpec((1, d), lambda i: (0, 0))
    return pl.pallas_call(
        _add_rms_cast_kernel,
        grid=(m // tr,),
        in_specs=[row, row, vec, vec],
        out_specs=[row, row],
        out_shape=[jax.ShapeDtypeStruct((m, d), F32), jax.ShapeDtypeStruct((m, d), BF16)],
        compiler_params=_cparams(("parallel",)),
        name="add_rms_cast",
    )(x, y, w.reshape(1, d), w_next.reshape(1, d))


def _mm_kernel(a_ref, w_ref, o_ref, *, nk):
    p = jnp.dot(a_ref[...], w_ref[...], preferred_element_type=F32)
    if nk == 1:
        o_ref[...] = p
    else:
        k = pl.program_id(2)

        @pl.when(k == 0)
        def _():
            o_ref[...] = p

        @pl.when(k > 0)
        def _():
            o_ref[...] += p


def matmul(a, w, tm, tn, tk, name):
    m, kd = a.shape
    n = w.shape[1]
    nk = kd // tk
    return pl.pallas_call(
        functools.partial(_mm_kernel, nk=nk),
        grid=(m // tm, n // tn, nk),
        in_specs=[pl.BlockSpec((tm, tk), lambda i, j, k: (i, k)), pl.BlockSpec((tk, tn), lambda i, j, k: (k, j))],
        out_specs=pl.BlockSpec((tm, tn), lambda i, j, k: (i, j)),
        out_shape=jax.ShapeDtypeStruct((m, n), F32),
        compiler_params=_cparams(("parallel", "parallel", "arbitrary")),
        name=name,
    )(a, w)


def _mm_into_kernel(a_ref, w_ref, *rest):
    rest[-1][...] = jnp.dot(a_ref[...], w_ref[...], preferred_element_type=F32)


def matmul_into(a, w, buf, li, depth, tm, tn, name):
    m, kd = a.shape
    n = w.shape[1]
    in_specs = [pl.BlockSpec((tm, kd), lambda i, j: (i, 0)), pl.BlockSpec((kd, tn), lambda i, j: (0, j))]
    args = [a, w]
    aliases = {}
    if buf is not None:
        in_specs.append(pl.BlockSpec(memory_space=pl.ANY))
        args.append(buf)
        aliases = {2: 0}
    return pl.pallas_call(
        _mm_into_kernel,
        grid=(m // tm, n // tn),
        in_specs=in_specs,
        out_specs=pl.BlockSpec((None, tm, tn), lambda i, j: (li, i, j)),
        out_shape=jax.ShapeDtypeStruct((depth, m, n), F32),
        input_output_aliases=aliases,
        compiler_params=_cparams(("parallel", "parallel")),
        name=name,
    )(*args)


def _mm_groups_kernel(*refs):
    *a_refs, w_ref, o_ref = refs
    kg = a_refs[0].shape[1]
    acc = jnp.dot(a_refs[0][...], w_ref[0:kg, :], preferred_element_type=F32)
    for g in range(1, len(a_refs)):
        acc = acc + jnp.dot(a_refs[g][...], w_ref[g * kg:(g + 1) * kg, :], preferred_element_type=F32)
    o_ref[...] = acc


def matmul_groups(parts, w, tm, tn, name):
    m, kg = parts[0].shape
    kd, n = w.shape
    return pl.pallas_call(
        _mm_groups_kernel,
        grid=(m // tm, n // tn),
        in_specs=[pl.BlockSpec((tm, kg), lambda i, j: (i, 0)) for _ in parts]
        + [pl.BlockSpec((kd, tn), lambda i, j: (0, j))],
        out_specs=pl.BlockSpec((tm, tn), lambda i, j: (i, j)),
        out_shape=jax.ShapeDtypeStruct((m, n), F32),
        compiler_params=_cparams(("parallel", "parallel")),
        name=name,
    )(*parts, w)


def _ffn_act_kernel(g_ref, v_ref, halo_ref, st_ref, cw_ref, cb_ref, o_ref):
    g = g_ref[...]
    tail = jnp.where(pl.program_id(1) == 0, st_ref[...], halo_ref[...])
    taps = _shifted_taps(tail, g, FFN_TAPS)
    cw = cw_ref[...]
    y = taps[0] * cw[0:1]
    for i in range(1, FFN_TAPS):
        y = y + taps[i] * cw[i:i + 1]
    y = y + cb_ref[...]
    o_ref[...] = (_silu(y) * v_ref[...]).astype(o_ref.dtype)


def _ffn_up_act_kernel(x_ref, wg_ref, wv_ref, st_ref, cw_ref, cb_ref, act_ref, cst_ref, tail_scr,
                       *, tiles_per_seq, sub):
    i = pl.program_id(0)
    j = pl.program_id(1)
    wg = wg_ref[...]
    wv = wv_ref[...]
    cw = cw_ref[...]
    cb = cb_ref[...]
    tm = x_ref.shape[0]
    @pl.when(i % tiles_per_seq == 0)
    def _():
        tail_scr[j] = st_ref[...]

    tail = tail_scr[j]

    def project(r):
        x = x_ref[r * sub:(r + 1) * sub, :]
        return jnp.dot(x, wg, preferred_element_type=F32), jnp.dot(x, wv, preferred_element_type=F32)

    nxt = project(0)
    for r in range(tm // sub):
        g, v = nxt
        if r + 1 < tm // sub:
            nxt = project(r + 1)
        taps = _shifted_taps(tail, g, FFN_TAPS)
        y = taps[0] * cw[0:1]
        for t in range(1, FFN_TAPS):
            y = y + taps[t] * cw[t:t + 1]
        act_ref[r * sub:(r + 1) * sub, :] = (_silu(y + cb) * v).astype(act_ref.dtype)
        tail = g[sub - SUBLANE:]
    tail_scr[j] = tail
    cst_ref[...] = tail


def ffn_up_act(h2, w_up, state8, conv_w, conv_b, l, tm, tn, sub):
    m, kd = h2.shape
    b = m // l
    nj = D_FF // tn
    tiles_per_seq = l // tm
    act, tails = pl.pallas_call(
        functools.partial(_ffn_up_act_kernel, tiles_per_seq=tiles_per_seq, sub=sub),
        grid=(m // tm, nj),
        in_specs=[
            pl.BlockSpec((tm, kd), lambda i, j: (i, 0)),
            pl.BlockSpec((kd, tn), lambda i, j: (0, j)),
            pl.BlockSpec((kd, tn), lambda i, j: (0, j + nj)),
            pl.BlockSpec((None, SUBLANE, tn), lambda i, j: (i // tiles_per_seq, 0, j)),
            pl.BlockSpec((FFN_TAPS, tn), lambda i, j: (0, j)),
            pl.BlockSpec((1, tn), lambda i, j: (0, j)),
        ],
        out_specs=[
            pl.BlockSpec((tm, tn), lambda i, j: (i, j)),
            pl.BlockSpec((None, SUBLANE, tn), lambda i, j: (i, 0, j)),
        ],
        out_shape=[jax.ShapeDtypeStruct((m, D_FF), BF16), jax.ShapeDtypeStruct((m // tm, SUBLANE, D_FF), F32)],
        scratch_shapes=[pltpu.VMEM((nj, SUBLANE, tn), F32)],
        compiler_params=_cparams(("arbitrary", "arbitrary")),
        name="ffn_up_act",
    )(h2, w_up, w_up, state8, conv_w, conv_b.reshape(1, D_FF))
    return act, tails.reshape(b, tiles_per_seq, SUBLANE, D_FF)[:, tiles_per_seq - 1]


def ffn_act(up3, state8, conv_w, conv_b, ts, tn):
    b, l, _ = up3.shape
    nj = D_FF // tn
    hb = ts // SUBLANE
    return pl.pallas_call(
        _ffn_act_kernel,
        grid=(b, l // ts, nj),
        in_specs=[
            pl.BlockSpec((None, ts, tn), lambda bi, i, j: (bi, i, j)),
            pl.BlockSpec((None, ts, tn), lambda bi, i, j: (bi, i, j + nj)),
            pl.BlockSpec((None, SUBLANE, tn), lambda bi, i, j: (bi, jnp.maximum(i * hb - 1, 0), j)),
            pl.BlockSpec((None, SUBLANE, tn), lambda bi, i, j: (bi, 0, j)),
            pl.BlockSpec((FFN_TAPS, tn), lambda bi, i, j: (0, j)),
            pl.BlockSpec((1, tn), lambda bi, i, j: (0, j)),
        ],
        out_specs=pl.BlockSpec((None, ts, tn), lambda bi, i, j: (bi, i, j)),
        out_shape=jax.ShapeDtypeStruct((b, l, D_FF), BF16),
        compiler_params=_cparams(("parallel", "parallel", "parallel")),
        name="ffn_act",
    )(up3, up3, up3, state8, conv_w, conv_b.reshape(1, D_FF))


def _gdn_kernel(qkv_ref, z_ref, sm_ref, cbuf_ref, s0_ref, cw_ref, alog_ref, dtb_ref, nw_ref,
                o_ref, sout_ref, s_scr, tail_scr, *, c):
    ci = pl.program_id(1)

    @pl.when(ci == 0)
    def _():
        s_scr[...] = s0_ref[...]
        tail_scr[...] = cbuf_ref[...]

    x = qkv_ref[...]
    taps = _shifted_taps(tail_scr[...], x, GDN_TAPS)
    tail_scr[...] = x[c - SUBLANE:]
    cw = cw_ref[...]
    y = taps[0] * cw[0:1]
    for i in range(1, GDN_TAPS):
        y = y + taps[i] * cw[i:i + 1]
    y = _silu(y)

    sm = sm_ref[...]
    beta_all = jax.nn.sigmoid(sm)
    g_all = -jnp.exp(alog_ref[...]) * jax.nn.softplus(sm + dtb_ref[...])
    gcum_all = _cumsum_rows(g_all, c)
    rows_of = _row_getter(gcum_all, c)

    row = _iota2((c, c), 0)
    col = _iota2((c, c), 1)
    tri = row >= col
    strict = row > col
    z = z_ref[...]
    nw = nw_ref[...]
    heads = range(GDN_HEADS)
    ms, aqk, rhs, qd, kd, gls = [], [], [], [], [], []
    for h in heads:
        lo = h * GDN_D
        q = y[:, lo:lo + GDN_D]
        k = y[:, GROUP_W + lo:GROUP_W + lo + GDN_D]
        v = y[:, 2 * GROUP_W + lo:2 * GROUP_W + lo + GDN_D]
        q = q * lax.rsqrt(jnp.sum(q * q, axis=-1, keepdims=True) + 1e-6) * (GDN_D ** -0.5)
        k = k * lax.rsqrt(jnp.sum(k * k, axis=-1, keepdims=True) + 1e-6)
        beta = beta_all[:, SM_GDN_B + h:SM_GDN_B + h + 1]
        gc = gcum_all[:, SM_GDN_A + h:SM_GDN_A + h + 1]
        gam = jnp.exp(jnp.where(tri, gc - rows_of(SM_GDN_A + h), -jnp.inf))
        kbeta = k * beta
        mq = dot3s(jnp.concatenate([kbeta, q], axis=0), _split(k), NT)
        ms.append(jnp.where(strict, mq[:c] * gam, 0.0))
        aqk.append(mq[c:] * gam)
        eg = jnp.exp(gc)
        gl = gc[c - 1:c]
        rhs.append(jnp.concatenate([kbeta * eg, v * beta], axis=1))
        qd.append(q * eg)
        kd.append(k * jnp.exp(gl - gc))
        gls.append(gl)
    ts = _inv_unit_lower_multi(ms, c)
    wu = [dot3s(t, _split(x), NN) for t, x in zip(ts, rhs)]
    wqs = [dot3s(jnp.concatenate([x[:, :GDN_D], qd[h]], axis=0), _split(s_scr[h]), NN)
           for h, x in enumerate(wu)]
    for h in heads:
        lo = h * GDN_D
        v_new = wu[h][:, GDN_D:] - wqs[h][:c]
        o = wqs[h][c:] + dot1(aqk[h], v_new, NN)
        s_scr[h] = s_scr[h] * jnp.exp(gls[h]) + dot3(_split(kd[h]), _split(v_new), TN)
        o = o * lax.rsqrt(jnp.mean(o * o, axis=-1, keepdims=True) + NORM_EPS) * nw
        o = o * _silu(z[:, lo:lo + GDN_D])
        o_ref[:, lo:lo + GDN_D] = o.astype(o_ref.dtype)

    @pl.when(ci == pl.num_programs(1) - 1)
    def _():
        sout_ref[...] = s_scr[...]


def gdn_mixer(proj3, cbuf8, s0, conv_w, a_log, dt_bias, norm_w, c):
    b, l, _ = proj3.shape
    alog_row = jnp.zeros((1, LANE), F32).at[0, SM_GDN_A:SM_GDN_A + GDN_HEADS].set(a_log)
    dtb_row = jnp.zeros((1, LANE), F32).at[0, SM_GDN_A:SM_GDN_A + GDN_HEADS].set(dt_bias)
    w3 = 3 * GROUP_W
    const2 = lambda bi, ci: (0, 0)
    return pl.pallas_call(
        functools.partial(_gdn_kernel, c=c),
        grid=(b, l // c),
        in_specs=[
            pl.BlockSpec((None, c, w3), lambda bi, ci: (bi, ci, COL_A_QKV // w3)),
            pl.BlockSpec((None, c, GROUP_W), lambda bi, ci: (bi, ci, COL_A_Z // GROUP_W)),
            pl.BlockSpec((None, c, LANE), lambda bi, ci: (bi, ci, COL_SMALL // LANE)),
            pl.BlockSpec((None, SUBLANE, w3), lambda bi, ci: (bi, 0, 0)),
            pl.BlockSpec((None, GDN_HEADS, GDN_D, GDN_D), lambda bi, ci: (bi, 0, 0, 0)),
            pl.BlockSpec((GDN_TAPS, w3), const2),
            pl.BlockSpec((1, LANE), const2),
            pl.BlockSpec((1, LANE), const2),
            pl.BlockSpec((1, GDN_D), const2),
        ],
        out_specs=[
            pl.BlockSpec((None, c, GROUP_W), lambda bi, ci: (bi, ci, 0)),
            pl.BlockSpec((None, GDN_HEADS, GDN_D, GDN_D), lambda bi, ci: (bi, 0, 0, 0)),
        ],
        out_shape=[jax.ShapeDtypeStruct((b, l, GROUP_W), BF16),
                   jax.ShapeDtypeStruct((b, GDN_HEADS, GDN_D, GDN_D), F32)],
        scratch_shapes=[pltpu.VMEM((GDN_HEADS, GDN_D, GDN_D), F32), pltpu.VMEM((SUBLANE, w3), F32)],
        compiler_params=_cparams(("parallel", "arbitrary")),
        name="gdn_mixer",
    )(proj3, proj3, proj3, cbuf8, s0, conv_w, alog_row, dtb_row, norm_w.reshape(1, GDN_D))


def _rwkv_kernel(rkv_ref, lora_ref, sh_rkv_ref, sh_lora_ref, s0_ref, mu_rkv_ref, mu_lora_ref,
                 w0_ref, w2_ref, a0_ref, a2_ref, g2_ref, kk_ref, ka_ref, rk_ref, lnw_ref, lnb_ref,
                 o_ref, sout_ref, s_scr, tail_rkv, tail_lora, *, c):
    ci = pl.program_id(1)

    @pl.when(ci == 0)
    def _():
        s_scr[...] = s0_ref[...]
        tail_rkv[...] = sh_rkv_ref[...]
        tail_lora[...] = sh_lora_ref[...]

    x = rkv_ref[...]
    xl = lora_ref[...]
    prev = _shifted_taps(tail_rkv[...], x, 2)[0]
    prev_l = _shifted_taps(tail_lora[...], xl, 2)[0]
    tail_rkv[...] = x[c - SUBLANE:]
    tail_lora[...] = xl[c - SUBLANE:]
    zm = x + (prev - x) * mu_rkv_ref[...]
    zl = xl + (prev_l - xl) * mu_lora_ref[...]
    r = zm[:, 0:GROUP_W]
    k = zm[:, GROUP_W:2 * GROUP_W]
    v = zm[:, 2 * GROUP_W:3 * GROUP_W]
    wd = zl[:, 0:RWKV_W_LORA]
    ad = zl[:, RWKV_W_LORA:RWKV_W_LORA + RWKV_A_LORA]
    gd = zl[:, RWKV_W_LORA + RWKV_A_LORA:LORA_PAD]

    w_log = -jax.nn.softplus(-(w0_ref[...] + dot3s(jnp.tanh(wd), _split(w2_ref[...]), NN))) - 0.5
    logw = -jnp.exp(w_log)
    a = jax.nn.sigmoid(a0_ref[...] + dot3s(ad, _split(a2_ref[...]), NN))
    gate = dot1(jax.nn.sigmoid(gd), g2_ref[...], NN)
    kkv = k * kk_ref[...]
    k2 = k * (1.0 + (a - 1.0) * ka_ref[...])
    lcum = _cumsum_rows(logw, c)

    row = _iota2((c, c), 0)
    col = _iota2((c, c), 1)
    strict = row > col
    tri2 = _iota2((c, 2 * c), 0) >= (_iota2((c, 2 * c), 1) & (c - 1))
    rk = rk_ref[...]
    lnw = lnw_ref[...]
    lnb = lnb_ref[...]
    heads = range(RWKV_HEADS)
    sl = [slice(h * RWKV_HS, (h + 1) * RWKV_HS) for h in heads]
    acat, ars, rcat, av, bk_end, l_last, vs = [], [], [], [], [], [], []
    for h in heads:
        r_h, k_h, v_h, a_h = r[:, sl[h]], k2[:, sl[h]], v[:, sl[h]], a[:, sl[h]]
        kk = kkv[:, sl[h]]
        kk = kk * lax.rsqrt(jnp.sum(kk * kk, axis=-1, keepdims=True) + 1e-6)
        lc = lcum[:, sl[h]]
        ll = lc[c - 1:c]
        p_inv = jnp.exp(-lc)
        a_t = -kk * jnp.exp(lc - logw[:, sl[h]])
        b_vec = kk * a_h
        r_t = r_h * jnp.exp(lc)
        p_end = jnp.exp(ll - lc)
        bk_s = _split(jnp.concatenate([b_vec * p_inv, k_h * p_inv], axis=0))
        acat.append(dot3s(a_t, bk_s, NT))
        ars.append(dot3s(jnp.concatenate([a_t, r_t], axis=0), _split(s_scr[h]), NT))
        rcat.append(jnp.where(tri2, _dot(r_t.astype(BF16), bk_s[0], NT, None), 0.0))
        bk_end.append(jnp.concatenate([b_vec * p_end, k_h * p_end], axis=0))
        l_last.append(ll)
        vs.append(v_h)
    a_ab = [jnp.where(strict, x[:, :c], 0.0) for x in acat]
    av = [dot3s(jnp.where(strict, x[:, c:], 0.0), _split(v_h), NN) for x, v_h in zip(acat, vs)]
    ts = _inv_unit_lower_multi([-x for x in a_ab], c)
    us = [dot3s(t, _split(x[:c] + y), NN) for t, x, y in zip(ts, ars, av)]
    for h in heads:
        uv_s = _split(jnp.concatenate([us[h], vs[h]], axis=0))
        y = ars[h][c:] + _dot(rcat[h].astype(BF16), uv_s[0], NN, None)
        s_scr[h] = s_scr[h] * jnp.exp(l_last[h]) + dot3(uv_s, _split(bk_end[h]), TN)
        mean = jnp.mean(y, axis=-1, keepdims=True)
        var = jnp.mean(jnp.square(y - mean), axis=-1, keepdims=True)
        yn = (y - mean) * lax.rsqrt(var + RWKV_GN_EPS) * lnw[:, sl[h]] + lnb[:, sl[h]]
        bonus = jnp.sum(r[:, sl[h]] * k2[:, sl[h]] * rk[:, sl[h]], axis=-1, keepdims=True) * vs[h]
        o_ref[:, sl[h]] = ((yn + bonus) * gate[:, sl[h]]).astype(o_ref.dtype)

    @pl.when(ci == pl.num_programs(1) - 1)
    def _():
        sout_ref[...] = s_scr[...]


def rwkv_mixer(proj3, sh_rkv8, sh_lora8, s0, prm, c):
    b, l, _ = proj3.shape
    w3 = 3 * GROUP_W
    mu = prm['rwkv_mu']
    mu_rkv = mu[:w3].reshape(1, w3)
    mu_lora = jnp.pad(mu[w3:], (0, LORA_PAD - RWKV_LORA)).reshape(1, LORA_PAD)
    g2 = jnp.pad(prm['rwkv_g2'], ((0, G_LORA_PAD - RWKV_G_LORA), (0, 0)))
    row = lambda t: t.reshape(1, GROUP_W)
    const2 = lambda bi, ci: (0, 0)
    vec = pl.BlockSpec((1, GROUP_W), const2)
    return pl.pallas_call(
        functools.partial(_rwkv_kernel, c=c),
        grid=(b, l // c),
        in_specs=[
            pl.BlockSpec((None, c, w3), lambda bi, ci: (bi, ci, COL_B_RKV // w3)),
            pl.BlockSpec((None, c, LORA_PAD), lambda bi, ci: (bi, ci, COL_B_LORA // LORA_PAD)),
            pl.BlockSpec((None, SUBLANE, w3), lambda bi, ci: (bi, 0, 0)),
            pl.BlockSpec((None, SUBLANE, LORA_PAD), lambda bi, ci: (bi, 0, 0)),
            pl.BlockSpec((None, RWKV_HEADS, RWKV_HS, RWKV_HS), lambda bi, ci: (bi, 0, 0, 0)),
            pl.BlockSpec((1, w3), const2),
            pl.BlockSpec((1, LORA_PAD), const2),
            vec,
            pl.BlockSpec((RWKV_W_LORA, GROUP_W), const2),
            vec,
            pl.BlockSpec((RWKV_A_LORA, GROUP_W), const2),
            pl.BlockSpec((G_LORA_PAD, GROUP_W), const2),
            vec, vec, vec, vec, vec,
        ],
        out_specs=[
            pl.BlockSpec((None, c, GROUP_W), lambda bi, ci: (bi, ci, 0)),
            pl.BlockSpec((None, RWKV_HEADS, RWKV_HS, RWKV_HS), lambda bi, ci: (bi, 0, 0, 0)),
        ],
        out_shape=[jax.ShapeDtypeStruct((b, l, GROUP_W), BF16),
                   jax.ShapeDtypeStruct((b, RWKV_HEADS, RWKV_HS, RWKV_HS), F32)],
        scratch_shapes=[pltpu.VMEM((RWKV_HEADS, RWKV_HS, RWKV_HS), F32),
                        pltpu.VMEM((SUBLANE, w3), F32), pltpu.VMEM((SUBLANE, LORA_PAD), F32)],
        compiler_params=_cparams(("parallel", "arbitrary")),
        name="rwkv_mixer",
    )(proj3, proj3, sh_rkv8, sh_lora8, s0, mu_rkv, mu_lora,
      row(prm['rwkv_w0']), prm['rwkv_w2'], row(prm['rwkv_a0']), prm['rwkv_a2'], g2,
      row(prm['rwkv_k_k']), row(prm['rwkv_k_a']), row(prm['rwkv_r_k']),
      row(prm['rwkv_ln_w']), row(prm['rwkv_ln_b']))


def _ssd_kernel(z_ref, xbc_ref, sm_ref, cbuf_ref, s0_ref, cw_ref, cb_ref, dtb_ref, alog_ref, dsk_ref, nw_ref,
                o_ref, sout_ref, s_scr, tail_scr, y_scr, *, c):
    ci = pl.program_id(1)

    @pl.when(ci == 0)
    def _():
        s_scr[...] = s0_ref[...]
        tail_scr[...] = cbuf_ref[...]

    x = xbc_ref[...]
    taps = _shifted_taps(tail_scr[...], x, SSM_TAPS)
    tail_scr[...] = x[c - SUBLANE:]
    cw = cw_ref[...]
    y = taps[0] * cw[0:1]
    for i in range(1, SSM_TAPS):
        y = y + taps[i] * cw[i:i + 1]
    y = _silu(y + cb_ref[...])
    xs = y[:, 0:GROUP_W]
    gn = SSM_GROUPS * SSM_N

    sm = sm_ref[...]
    dt_all = jax.nn.softplus(sm + dtb_ref[...])
    da_all = dt_all * (-jnp.exp(alog_ref[...]))
    acs_all = _cumsum_rows(da_all, c)

    row = _iota2((c, c), 0)
    col = _iota2((c, c), 1)
    tri = row >= col
    z = z_ref[...]
    dsk = dsk_ref[...]
    hpg = SSM_HEADS // SSM_GROUPS
    gp = hpg * SSM_P
    rows_of = _row_getter(acs_all, c)
    for g in range(SSM_GROUPS):
        bm = y[:, GROUP_W + g * SSM_N:GROUP_W + (g + 1) * SSM_N]
        cm = y[:, GROUP_W + gn + g * SSM_N:GROUP_W + gn + (g + 1) * SSM_N]
        cb = dot1(cm, bm, NT)
        sg = s_scr[g]
        y_off = dot1(cm, sg, NT)
        xdec = []
        for rr in range(hpg):
            h = g * hpg + rr
            lo = h * SSM_P
            lane = SM_SSM_DT + h
            xs_h = xs[:, lo:lo + SSM_P]
            dt = dt_all[:, lane:lane + 1]
            acs = acs_all[:, lane:lane + 1]
            lmat = jnp.exp(jnp.where(tri, acs - rows_of(lane), -jnp.inf))
            xd = xs_h * dt
            a_last = acs[c - 1:c]
            xdec.append(xd * jnp.exp(a_last - acs))
            yh = dot1(cb * lmat, xd, NN) + y_off[:, rr * SSM_P:(rr + 1) * SSM_P] * jnp.exp(acs)
            yh = yh + xs_h * dsk[:, lane:lane + 1]
            y_scr[:, lo:lo + SSM_P] = yh * _silu(z[:, lo:lo + SSM_P])
        upd = dot3(_split(jnp.concatenate(xdec, axis=1)), _split(bm), TN)
        for rr in range(hpg):
            lane = SM_SSM_DT + g * hpg + rr
            dec = jnp.exp(acs_all[c - 1:c, lane:lane + 1])
            s_scr[g, rr * SSM_P:(rr + 1) * SSM_P, :] = (sg[rr * SSM_P:(rr + 1) * SSM_P] * dec
                                                         + upd[rr * SSM_P:(rr + 1) * SSM_P])

    gw = GROUP_W // SSM_GROUPS
    nw = nw_ref[...]
    for g in range(SSM_GROUPS):
        yg = y_scr[:, g * gw:(g + 1) * gw]
        yg = yg * lax.rsqrt(jnp.mean(yg * yg, axis=-1, keepdims=True) + NORM_EPS)
        o_ref[:, g * gw:(g + 1) * gw] = (yg * nw[:, g * gw:(g + 1) * gw]).astype(o_ref.dtype)

    @pl.when(ci == pl.num_programs(1) - 1)
    def _():
        sout_ref[...] = s_scr[...]


def ssd_mixer(proj3, cbuf8, s0, prm, c):
    b, l, _ = proj3.shape

    def lanes(t):
        return jnp.zeros((1, LANE), F32).at[0, SM_SSM_DT:SM_SSM_DT + SSM_HEADS].set(t)

    const2 = lambda bi, ci: (0, 0)
    small = pl.BlockSpec((1, LANE), const2)
    gshape = (SSM_GROUPS, SSM_HEADS // SSM_GROUPS * SSM_P, SSM_N)
    state_spec = pl.BlockSpec((None,) + gshape, lambda bi, ci: (bi, 0, 0, 0))
    o, s_new = pl.pallas_call(
        functools.partial(_ssd_kernel, c=c),
        grid=(b, l // c),
        in_specs=[
            pl.BlockSpec((None, c, GROUP_W), lambda bi, ci: (bi, ci, COL_C_Z // GROUP_W)),
            pl.BlockSpec((None, c, SSM_XBC), lambda bi, ci: (bi, ci, COL_C_XBC // SSM_XBC)),
            pl.BlockSpec((None, c, LANE), lambda bi, ci: (bi, ci, COL_SMALL // LANE)),
            pl.BlockSpec((None, SUBLANE, SSM_XBC), lambda bi, ci: (bi, 0, 0)),
            state_spec,
            pl.BlockSpec((SSM_TAPS, SSM_XBC), const2),
            pl.BlockSpec((1, SSM_XBC), const2),
            small, small, small,
            pl.BlockSpec((1, GROUP_W), const2),
        ],
        out_specs=[
            pl.BlockSpec((None, c, GROUP_W), lambda bi, ci: (bi, ci, 0)),
            state_spec,
        ],
        out_shape=[jax.ShapeDtypeStruct((b, l, GROUP_W), BF16),
                   jax.ShapeDtypeStruct((b,) + gshape, F32)],
        scratch_shapes=[pltpu.VMEM(gshape, F32), pltpu.VMEM((SUBLANE, SSM_XBC), F32),
                        pltpu.VMEM((c, GROUP_W), F32)],
        compiler_params=_cparams(("parallel", "arbitrary")),
        name="ssd_mixer",
    )(proj3, proj3, proj3, cbuf8, s0.reshape((b,) + gshape), prm['ssm_conv_w'],
      prm['ssm_conv_b'].reshape(1, SSM_XBC),
      lanes(prm['ssm_dt_bias']), lanes(prm['ssm_A_log']), lanes(prm['ssm_D']),
      prm['ssm_norm_w'].reshape(1, GROUP_W))
    return o, s_new.reshape(b, SSM_HEADS, SSM_P, SSM_N)


def _swa_weight(d):
    mult = jnp.zeros(d.shape, F32)
    for window, dil in SWA_PATTERNS:
        ok = (d >= 0) & (d <= window) & ((d & (dil - 1)) == 0)
        mult = mult + jnp.where(ok, 1.0, 0.0)
    return mult


def _swa_scores(q, k, d, slope):
    s = dot_nt(q.astype(BF16), k.astype(BF16), None) * (SWA_HD ** -0.5)
    mult = _swa_weight(d)
    s = s - slope * d.astype(F32)
    return jnp.where(mult > 0.0, s, NEG_INF), mult


def _swa_prompt_kernel(slopes_ref, q_ref, k_ref, v_ref, lw_ref, o_ref, m_scr, l_scr, acc_scr, *, t, sub):
    h = pl.program_id(1)
    qi = pl.program_id(2)
    ki = pl.program_id(3)

    @pl.when(ki == 0)
    def _():
        m_scr[...] = jnp.full(m_scr.shape, NEG_INF, F32)
        l_scr[...] = jnp.zeros(l_scr.shape, F32)
        acc_scr[...] = jnp.zeros(acc_scr.shape, F32)

    @pl.when(ki <= qi)
    def _():
        col = (ki * t + _iota2((1, t), 1)).astype(F32) * slopes_ref[h]
        kb = k_ref[...].astype(BF16)
        vb = v_ref[...].astype(BF16)
        def qk(r):
            q = (q_ref[r:r + sub, :] * (SWA_HD ** -0.5)).astype(BF16)
            return _dot(q, kb, NT, None)

        nxt = qk(0)
        for r in range(0, t, sub):
            rs = slice(r, r + sub)
            s = nxt + (lw_ref[qi - ki, rs, :] + col)
            if r + sub < t:
                nxt = qk(r + sub)
            m_old = m_scr[rs, :]
            m_new = jnp.maximum(m_old, jnp.max(s, axis=-1, keepdims=True))
            alpha = jnp.exp(m_old - m_new)
            p = jnp.exp(s - m_new)
            l_scr[rs, :] = alpha * l_scr[rs, :] + jnp.sum(p, axis=-1, keepdims=True)
            acc_scr[rs, :] = alpha * acc_scr[rs, :] + _dot(p.astype(BF16), vb, NN, None)
            m_scr[rs, :] = m_new

    @pl.when(ki == pl.num_programs(3) - 1)
    def _():
        o_ref[...] = (acc_scr[...] / l_scr[...]).astype(o_ref.dtype)


def _alibi_slopes():
    return jnp.asarray([2.0 ** (-8.0 * (i + 1) / SWA_HEADS) for i in range(SWA_HEADS)], F32)


def _swa_log_weight_tiles(n, t):
    d = (jnp.arange(n)[:, None, None] * t + jnp.arange(t)[None, :, None]) - jnp.arange(t)[None, None, :]
    mult = _swa_weight(d.astype(jnp.int32))
    return jnp.where(mult > 0.0, jnp.log(jnp.maximum(mult, 1.0)), NEG_INF)


def swa_prompt(proj3, k4, v4, li, t):
    b, l, _ = proj3.shape
    qc = COL_D_Q // SWA_HD
    n = l // t
    kv_spec = pl.BlockSpec((None, None, t, SWA_HD), lambda bi, h, qi, ki: (li, bi, jnp.minimum(ki, qi), h))

    return pl.pallas_call(
        functools.partial(_swa_prompt_kernel, t=t, sub=min(t, 256)),
        grid=(b, SWA_HEADS, n, n),
        in_specs=[
            pl.BlockSpec(memory_space=pltpu.SMEM),
            pl.BlockSpec((None, t, SWA_HD), lambda bi, h, qi, ki: (bi, qi, qc + h)),
            kv_spec,
            kv_spec,
            pl.BlockSpec((n, t, t), lambda bi, h, qi, ki: (0, 0, 0)),
        ],
        out_specs=pl.BlockSpec((None, t, SWA_HD), lambda bi, h, qi, ki: (bi, qi, h)),
        out_shape=jax.ShapeDtypeStruct((b, l, GROUP_W), BF16),
        scratch_shapes=[pltpu.VMEM((t, 1), F32), pltpu.VMEM((t, 1), F32), pltpu.VMEM((t, SWA_HD), F32)],
        compiler_params=_cparams(("parallel", "parallel", "parallel", "arbitrary")),
        name="swa_prompt",
    )(_alibi_slopes(), proj3, k4, v4, _swa_log_weight_tiles(n, t))


def _swa_sample_kernel(slopes_ref, q_ref, k_ref, v_ref, ck_ref, cv_ref, o_ref, *, t, wb):
    slope = slopes_ref[pl.program_id(1)]
    q = q_ref[...]
    d_c = (wb + _iota2((t, wb), 0)) - _iota2((t, wb), 1)
    d_n = _iota2((t, t), 0) - _iota2((t, t), 1)
    s_c, mult_c = _swa_scores(q, ck_ref[...], d_c, slope)
    s_n, mult_n = _swa_scores(q, k_ref[...], d_n, slope)
    m = jnp.maximum(jnp.max(s_c, axis=-1, keepdims=True), jnp.max(s_n, axis=-1, keepdims=True))
    p_c = jnp.exp(s_c - m) * mult_c
    p_n = jnp.exp(s_n - m) * mult_n
    den = jnp.sum(p_c, axis=-1, keepdims=True) + jnp.sum(p_n, axis=-1, keepdims=True)
    num = (dot_nn(p_c.astype(BF16), cv_ref[...].astype(BF16), None)
           + dot_nn(p_n.astype(BF16), v_ref[...].astype(BF16), None))
    o_ref[...] = (num / den).astype(o_ref.dtype)


def swa_sample(proj3, k4, v4, li, cache_k, cache_v):
    b, t, _ = proj3.shape
    wb = cache_k.shape[1]
    qc = COL_D_Q // SWA_HD
    ck = cache_k.reshape(b, wb, GROUP_W)
    cv = cache_v.reshape(b, wb, GROUP_W)
    kv_spec = pl.BlockSpec((None, None, t, SWA_HD), lambda bi, h: (li, bi, 0, h))
    return pl.pallas_call(
        functools.partial(_swa_sample_kernel, t=t, wb=wb),
        grid=(b, SWA_HEADS),
        in_specs=[
            pl.BlockSpec(memory_space=pltpu.SMEM),
            pl.BlockSpec((None, t, SWA_HD), lambda bi, h: (bi, 0, qc + h)),
            kv_spec,
            kv_spec,
            pl.BlockSpec((None, wb, SWA_HD), lambda bi, h: (bi, 0, h)),
            pl.BlockSpec((None, wb, SWA_HD), lambda bi, h: (bi, 0, h)),
        ],
        out_specs=pl.BlockSpec((None, t, SWA_HD), lambda bi, h: (bi, 0, h)),
        out_shape=jax.ShapeDtypeStruct((b, t, GROUP_W), BF16),
        compiler_params=_cparams(("parallel", "parallel")),
        name="swa_sample",
    )(_alibi_slopes(), proj3, k4, v4, ck, cv)


def _tiles(m):
    return (256, 1024) if m >= 1024 else (m, m)


def _front_pad_rows(t, rows=SUBLANE):
    return jnp.pad(t, ((0, 0), (rows - t.shape[1], 0), (0, 0)))


def prep_weights(p):
    w = p['w_in']
    o = 0
    seg = {}
    for name, n in (('a_qkv', 3 * GROUP_W), ('a_z', GROUP_W), ('a_b', GDN_HEADS), ('a_a', GDN_HEADS),
                    ('b_rkv', 3 * GROUP_W), ('b_lora', RWKV_LORA), ('c_z', GROUP_W), ('c_xbc', SSM_XBC),
                    ('c_dt', SSM_HEADS), ('d_q', GROUP_W), ('d_k', GROUP_W), ('d_v', GROUP_W)):
        seg[name] = w[:, o:o + n].astype(BF16)
        o += n
    place = (('a_qkv', COL_A_QKV), ('b_rkv', COL_B_RKV), ('d_q', COL_D_Q), ('a_z', COL_A_Z), ('c_z', COL_C_Z),
             ('c_xbc', COL_C_XBC), ('b_lora', COL_B_LORA), ('a_b', COL_SMALL + SM_GDN_B),
             ('a_a', COL_SMALL + SM_GDN_A), ('c_dt', COL_SMALL + SM_SSM_DT))
    w_in = None
    for name, col in place:
        piece = jnp.pad(seg[name], ((0, 0), (col, N_PROJ - col - seg[name].shape[1])))
        w_in = piece if w_in is None else w_in + piece
    return {'w_in': w_in, 'w_k': seg['d_k'], 'w_v': seg['d_v'], 'w_out': p['w_out'].astype(BF16),
            'ffn_w_up': p['ffn_w_up'].astype(BF16), 'ffn_w_down': p['ffn_w_down'].astype(BF16)}


def decoder_layer(x, prm, wts, past, swa_cache, chunks, li, depth, kv=(None, None), h=None, next_pre_w=None):
    b, l, _ = x.shape
    m = b * l
    tr, tm = _tiles(m)
    x2 = x.reshape(m, D_MODEL)
    c_gdn, c_rwkv, c_ssd = chunks

    if h is None:
        h = rms_cast(x2, prm['norm_mix_pre'], tr)
    proj = matmul(h, wts['w_in'], tm, 512, D_MODEL, "mm_in")
    proj3 = proj.reshape(b, l, N_PROJ)
    kbuf = matmul_into(h, wts['w_k'], kv[0], li, depth, tm, 512, "mm_k")
    vbuf = matmul_into(h, wts['w_v'], kv[1], li, depth, tm, 512, "mm_v")
    k4 = kbuf.reshape(depth, b, l, GROUP_W)
    v4 = vbuf.reshape(depth, b, l, GROUP_W)

    o_a, gdn_s = gdn_mixer(proj3, _front_pad_rows(past['gdn_conv']), past['gdn'], prm['gdn_conv_w'],
                           prm['gdn_A_log'], prm['gdn_dt_bias'], prm['gdn_norm_w'], c_gdn)
    shift = past['rwkv_shift'][:, None, :]
    sh_rkv8 = _front_pad_rows(shift[:, :, :3 * GROUP_W])
    sh_lora8 = _front_pad_rows(jnp.pad(shift[:, :, 3 * GROUP_W:], ((0, 0), (0, 0), (0, LORA_PAD - RWKV_LORA))))
    o_b, rwkv_s = rwkv_mixer(proj3, sh_rkv8, sh_lora8, past['rwkv'], prm, c_rwkv)
    o_c, ssm_s = ssd_mixer(proj3, _front_pad_rows(past['ssm_conv']), past['ssm'], prm, c_ssd)
    if swa_cache is None:
        o_d = swa_prompt(proj3, k4, v4, li, min(l, SWA_TILE))
    else:
        o_d = swa_sample(proj3, k4, v4, li, swa_cache[0], swa_cache[1])

    gdn_conv = proj3[:, l - (GDN_TAPS - 1):, COL_A_QKV:COL_A_QKV + 3 * GROUP_W]
    ssm_conv = proj3[:, l - (SSM_TAPS - 1):, COL_C_XBC:COL_C_XBC + SSM_XBC]
    rwkv_shift = jnp.concatenate([proj3[:, l - 1, COL_B_RKV:COL_B_RKV + 3 * GROUP_W],
                                  proj3[:, l - 1, COL_B_LORA:COL_B_LORA + RWKV_LORA]], axis=-1)

    y = matmul_groups([o.reshape(m, GROUP_W) for o in (o_a, o_b, o_c, o_d)], wts['w_out'], tm, 512, "mm_out")
    x2, h2 = add_rms_cast(x2, y, prm['norm_mix_post'], prm['norm_ffn_pre'], tr)
    state8 = _front_pad_rows(past['ffn_conv'])
    if l % tm == 0:
        act, cst = ffn_up_act(h2, wts['ffn_w_up'], state8, prm['ffn_conv_w'], prm['ffn_conv_b'], l, tm, 256, 256)
        ffn_conv = cst[:, SUBLANE - (FFN_TAPS - 1):, :]
    else:
        up3 = matmul(h2, wts['ffn_w_up'], tm, 512, D_MODEL, "mm_up").reshape(b, l, 2 * D_FF)
        act = ffn_act(up3, state8, prm['ffn_conv_w'], prm['ffn_conv_b'], l, D_FF // 2).reshape(m, D_FF)
        ffn_conv = up3[:, l - (FFN_TAPS - 1):, :D_FF]
    y2 = matmul(act, wts['ffn_w_down'], tm, 512, D_FF // 2, "mm_down")
    if next_pre_w is None:
        x2, h_next = add_rms(x2, y2, prm['norm_ffn_post'], tr), None
    else:
        x2, h_next = add_rms_cast(x2, y2, prm['norm_ffn_post'], next_pre_w, tr)
    return x2.reshape(b, l, D_MODEL), h_next, (kbuf, vbuf), (gdn_s, gdn_conv, rwkv_s, rwkv_shift, ssm_s, ssm_conv,
                                                             ffn_conv)


def _zero_past(bsz):
    return {
        'gdn': jnp.zeros((bsz, GDN_HEADS, GDN_D, GDN_D), F32),
        'gdn_conv': jnp.zeros((bsz, GDN_TAPS - 1, 3 * GROUP_W), F32),
        'rwkv': jnp.zeros((bsz, RWKV_HEADS, RWKV_HS, RWKV_HS), F32),
        'rwkv_shift': jnp.zeros((bsz, 3 * GROUP_W + RWKV_LORA), F32),
        'ssm': jnp.zeros((bsz, SSM_HEADS, SSM_P, SSM_N), F32),
        'ssm_conv': jnp.zeros((bsz, SSM_TAPS - 1, SSM_XBC), F32),
        'ffn_conv': jnp.zeros((bsz, FFN_TAPS - 1, D_FF), F32),
    }


PARAM_NAMES = ('norm_mix_pre', 'norm_mix_post', 'norm_ffn_pre', 'norm_ffn_post', 'w_in', 'w_out', 'gdn_conv_w',
               'gdn_A_log', 'gdn_dt_bias', 'gdn_norm_w', 'rwkv_mu', 'rwkv_w0', 'rwkv_w2', 'rwkv_a0', 'rwkv_a2',
               'rwkv_g2', 'rwkv_k_k', 'rwkv_k_a', 'rwkv_r_k', 'rwkv_ln_w', 'rwkv_ln_b', 'ssm_conv_w', 'ssm_conv_b',
               'ssm_dt_bias', 'ssm_A_log', 'ssm_D', 'ssm_norm_w', 'ffn_w_up', 'ffn_conv_w', 'ffn_conv_b',
               'ffn_w_down')


def kernel(x_prompt, x_sample, state_gdn, state_gdn_conv, state_rwkv, state_rwkv_shift, state_ssm, state_ssm_conv, cache_swa_k, cache_swa_v, state_ffn_conv, norm_mix_pre, norm_mix_post, norm_ffn_pre, norm_ffn_post, w_in, w_out, gdn_conv_w, gdn_A_log, gdn_dt_bias, gdn_norm_w, rwkv_mu, rwkv_w0, rwkv_w2, rwkv_a0, rwkv_a2, rwkv_g2, rwkv_k_k, rwkv_k_a, rwkv_r_k, rwkv_ln_w, rwkv_ln_b, ssm_conv_w, ssm_conv_b, ssm_dt_bias, ssm_A_log, ssm_D, ssm_norm_w, ffn_w_up, ffn_conv_w, ffn_conv_b, ffn_w_down):
    params = dict(zip(PARAM_NAMES, (norm_mix_pre, norm_mix_post, norm_ffn_pre, norm_ffn_post, w_in, w_out,
                                    gdn_conv_w, gdn_A_log, gdn_dt_bias, gdn_norm_w, rwkv_mu, rwkv_w0, rwkv_w2,
                                    rwkv_a0, rwkv_a2, rwkv_g2, rwkv_k_k, rwkv_k_a, rwkv_r_k, rwkv_ln_w, rwkv_ln_b,
                                    ssm_conv_w, ssm_conv_b, ssm_dt_bias, ssm_A_log, ssm_D, ssm_norm_w, ffn_w_up,
                                    ffn_conv_w, ffn_conv_b, ffn_w_down)))
    depth = w_in.shape[0]
    xp, xs = x_prompt, x_sample
    t_dec = x_sample.shape[1]
    prompt_states, sample_states = [], []
    hp = hs = None
    kvp = kvs = (None, None)
    for li in range(depth):
        prm = {k: v[li] for k, v in params.items()}
        wts = prep_weights(prm)
        nxt = norm_mix_pre[li + 1] if li + 1 < depth else None
        xp, hp, kvp, stp = decoder_layer(xp, prm, wts, _zero_past(xp.shape[0]), None, (64, 64, 128),
                                         li, depth, kvp, hp, nxt)
        past = {'gdn': state_gdn[li], 'gdn_conv': state_gdn_conv[li], 'rwkv': state_rwkv[li],
                'rwkv_shift': state_rwkv_shift[li], 'ssm': state_ssm[li], 'ssm_conv': state_ssm_conv[li],
                'ffn_conv': state_ffn_conv[li]}
        xs, hs, kvs, sts = decoder_layer(xs, prm, wts, past, (cache_swa_k[li], cache_swa_v[li]),
                                         (t_dec, t_dec, t_dec), li, depth, kvs, hs, nxt)
        prompt_states.append(stp)
        sample_states.append(sts)

    def window_rows(buf, x):
        bsz, l = x.shape[0], x.shape[1]
        rows = buf.reshape(depth, bsz, l, SWA_HEADS, SWA_HD)
        return rows[:, :, max(l - SWA_MAX_WINDOW, 0):]

    def outputs(states, kv, x):
        st = [jnp.stack(t) for t in zip(*states)]
        return (*st[:6], window_rows(kv[0], x), window_rows(kv[1], x), st[6])

    return (xp, xs, *outputs(prompt_states, kvp, x_prompt), *outputs(sample_states, kvs, x_sample))
```

```python
import functools

import jax
import jax.numpy as jnp
from jax import lax
from jax.experimental import pallas as pl
from jax.experimental.pallas import tpu as pltpu

F32 = jnp.float32
BF16 = jnp.bfloat16
HI = lax.Precision.HIGHEST

D_MODEL = 4096
GROUP_W = D_MODEL // 4
GDN_HEADS = 8
GDN_D = GROUP_W // GDN_HEADS
GDN_TAPS = 4
RWKV_HS = 64
RWKV_HEADS = GROUP_W // RWKV_HS
RWKV_W_LORA = 64
RWKV_A_LORA = 64
RWKV_G_LORA = 160
RWKV_LORA = RWKV_W_LORA + RWKV_A_LORA + RWKV_G_LORA
RWKV_GN_EPS = 64e-5
SSM_P = 64
SSM_HEADS = GROUP_W // SSM_P
SSM_GROUPS = 2
SSM_N = 128
SSM_TAPS = 4
SSM_XBC = GROUP_W + 2 * SSM_GROUPS * SSM_N
SWA_HEADS = 8
SWA_HD = GROUP_W // SWA_HEADS
SWA_PATTERNS = ((128, 1), (512, 4), (2048, 16))
SWA_MAX_WINDOW = 2048
D_FF = 256 * ((8 * D_MODEL // 3 + 255) // 256)
FFN_TAPS = 3
NORM_EPS = 1e-6
NEG_INF = -1e30

LANE = 128
SUBLANE = 8
LORA_PAD = 384
G_LORA_PAD = LORA_PAD - RWKV_W_LORA - RWKV_A_LORA

COL_A_QKV = 0
COL_B_RKV = 3 * GROUP_W
COL_D_Q = 6 * GROUP_W
COL_A_Z = 7 * GROUP_W
COL_C_Z = 8 * GROUP_W
COL_C_XBC = 9 * GROUP_W
COL_B_LORA = COL_C_XBC + SSM_XBC
COL_SMALL = COL_B_LORA + LORA_PAD
N_PROJ = COL_SMALL + LANE
SM_GDN_B = 0
SM_GDN_A = GDN_HEADS
SM_SSM_DT = 2 * GDN_HEADS

VMEM_LIMIT = 56 * 1024 * 1024
SWA_TILE = 1024


def _cparams(sem):
    return pltpu.CompilerParams(dimension_semantics=sem, vmem_limit_bytes=VMEM_LIMIT)


def _dot(a, b, dims, prec):
    return lax.dot_general(a, b, (dims, ((), ())), precision=prec, preferred_element_type=F32)


def dot_nn(a, b, prec=HI):
    return _dot(a, b, ((1,), (0,)), prec)


def dot_nt(a, b, prec=HI):
    return _dot(a, b, ((1,), (1,)), prec)


def dot_tn(a, b, prec=HI):
    return _dot(a, b, ((0,), (0,)), prec)


def _silu(x):
    return x * jax.nn.sigmoid(x)


def _iota2(shape, axis):
    return lax.broadcasted_iota(jnp.int32, shape, axis)


def _log2(n):
    s = n.bit_length() - 1
    assert 1 << s == n
    return s


NN = ((1,), (0,))
NT = ((1,), (1,))
TN = ((0,), (0,))


def _split(x):
    hi = x.astype(BF16)
    return hi, (x - hi.astype(F32)).astype(BF16)


def _split_rows(x):
    hi = x.astype(BF16)
    hif = hi.astype(F32)
    return jnp.concatenate([hif, x - hif], axis=0).astype(BF16), hi


def dot3(ap, bp, dims):
    return (_dot(ap[0], bp[0], dims, None) + _dot(ap[0], bp[1], dims, None)
            + _dot(ap[1], bp[0], dims, None))


def dot3s(a, bp, dims):
    r = a.shape[0]
    stacked, hi = _split_rows(a)
    both = _dot(stacked, bp[0], dims, None)
    return both[:r] + both[r:] + _dot(hi, bp[1], dims, None)


def dot1(a, b, dims):
    return _dot(a.astype(BF16), b.astype(BF16), dims, None)


def _inv_unit_lower_multi(ms, c):
    row = _iota2((c, c), 0)
    col = _iota2((c, c), 1)
    eye = jnp.where(row == col, 1.0, 0.0).astype(F32)
    base = min(SUBLANE, c)
    sb = _log2(base)
    blk = (row >> sb) == (col >> sb)
    ps = [jnp.where(blk, -m, 0.0) for m in ms]
    ts = [eye + p for p in ps]
    if sb > 1:
        ps = [_dot(p.astype(BF16), p.astype(BF16), NN, None) for p in ps]
        for _ in range(sb - 2):
            both = [_dot(jnp.concatenate([t, p], axis=0).astype(BF16), p.astype(BF16), NN, None)
                    for t, p in zip(ts, ps)]
            ts = [t + x[:c] for t, x in zip(ts, both)]
            ps = [x[c:] for x in both]
        ts = [t + _dot(t.astype(BF16), p.astype(BF16), NN, None) for t, p in zip(ts, ps)]
    s = base
    while s < c:
        ls = _log2(s)
        off = ((row >> (ls + 1)) == (col >> (ls + 1))) & ((row >> ls) > (col >> ls))
        tbs = [t.astype(BF16) for t in ts]
        inner = [_dot(jnp.where(off, m, 0.0).astype(BF16), tb, NN, None) for m, tb in zip(ms, tbs)]
        ts = [t - _dot(tb, x.astype(BF16), NN, None) for t, tb, x in zip(ts, tbs, inner)]
        s *= 2
    res = [eye - t - dot3s(m, _split(t), NN) for m, t in zip(ms, ts)]
    return [t + _dot(t.astype(BF16), r.astype(BF16), NN, None) for t, r in zip(ts, res)]


def _row_getter(x, c):
    if c % LANE:
        x = jnp.concatenate([x, jnp.zeros((LANE - c % LANE, LANE), F32)], axis=0)
    xt = x.T
    return lambda lane: xt[lane:lane + 1, :c]


def _cumsum_rows(x, c):
    tri = jnp.where(_iota2((c, c), 0) >= _iota2((c, c), 1), 1.0, 0.0).astype(F32)
    return dot_nn(tri, x)


def _shifted_taps(tail, x, taps):
    c = x.shape[0]
    xp = jnp.concatenate([tail, x], axis=0)
    out = []
    for s in range(taps - 1, 0, -1):
        out.append(pltpu.roll(xp, s, 0)[SUBLANE:SUBLANE + c])
    out.append(x)
    return out


def _rms_cast_kernel(x_ref, w_ref, o_ref):
    x = x_ref[...]
    y = x * lax.rsqrt(jnp.mean(x * x, axis=-1, keepdims=True) + NORM_EPS)
    o_ref[...] = (y * w_ref[...]).astype(o_ref.dtype)


def rms_cast(x, w, tr):
    m, d = x.shape
    return pl.pallas_call(
        _rms_cast_kernel,
        grid=(m // tr,),
        in_specs=[pl.BlockSpec((tr, d), lambda i: (i, 0)), pl.BlockSpec((1, d), lambda i: (0, 0))],
        out_specs=pl.BlockSpec((tr, d), lambda i: (i, 0)),
        out_shape=jax.ShapeDtypeStruct((m, d), BF16),
        compiler_params=_cparams(("parallel",)),
        name="rms_cast",
    )(x, w.reshape(1, d))


def _add_rms_kernel(x_ref, y_ref, w_ref, o_ref):
    y = y_ref[...]
    yn = y * lax.rsqrt(jnp.mean(y * y, axis=-1, keepdims=True) + NORM_EPS)
    o_ref[...] = x_ref[...] + yn * w_ref[...]


def add_rms(x, y, w, tr):
    m, d = x.shape
    return pl.pallas_call(
        _add_rms_kernel,
        grid=(m // tr,),
        in_specs=[pl.BlockSpec((tr, d), lambda i: (i, 0)), pl.BlockSpec((tr, d), lambda i: (i, 0)),
                  pl.BlockSpec((1, d), lambda i: (0, 0))],
        out_specs=pl.BlockSpec((tr, d), lambda i: (i, 0)),
        out_shape=jax.ShapeDtypeStruct((m, d), F32),
        compiler_params=_cparams(("parallel",)),
        name="add_rms",
    )(x, y, w.reshape(1, d))


def _add_rms_cast_kernel(x_ref, y_ref, w_ref, wn_ref, o_ref, h_ref):
    y = y_ref[...]
    yn = y * lax.rsqrt(jnp.mean(y * y, axis=-1, keepdims=True) + NORM_EPS)
    x = x_ref[...] + yn * w_ref[...]
    o_ref[...] = x
    xn = x * lax.rsqrt(jnp.mean(x * x, axis=-1, keepdims=True) + NORM_EPS)
    h_ref[...] = (xn * wn_ref[...]).astype(h_ref.dtype)


def add_rms_cast(x, y, w, w_next, tr):
    m, d = x.shape
    row = pl.BlockSpec((tr, d), lambda i: (i, 0))
    vec = pl.BlockSpec((1, d), lambda i: (0, 0))
    return pl.pallas_call(
        _add_rms_cast_kernel,
        grid=(m // tr,),
        in_specs=[row, row, vec, vec],
        out_specs=[row, row],
        out_shape=[jax.ShapeDtypeStruct((m, d), F32), jax.ShapeDtypeStruct((m, d), BF16)],
        compiler_params=_cparams(("parallel",)),
        name="add_rms_cast",
    )(x, y, w.reshape(1, d), w_next.reshape(1, d))


def _mm_kernel(a_ref, w_ref, o_ref, *, nk):
    p = jnp.dot(a_ref[...], w_ref[...].astype(BF16), preferred_element_type=F32)
    if nk == 1:
        o_ref[...] = p
    else:
        k = pl.program_id(2)

        @pl.when(k == 0)
        def _():
            o_ref[...] = p

        @pl.when(k > 0)
        def _():
            o_ref[...] += p


def matmul(a, w, tm, tn, tk, name, li=None):
    m, kd = a.shape
    n = w.shape[-1]
    nk = kd // tk
    if li is None:
        w_spec = pl.BlockSpec((tk, tn), lambda i, j, k: (k, j))
    else:
        w_spec = pl.BlockSpec((None, tk, tn), lambda i, j, k: (li, k, j))
    return pl.pallas_call(
        functools.partial(_mm_kernel, nk=nk),
        grid=(m // tm, n // tn, nk),
        in_specs=[pl.BlockSpec((tm, tk), lambda i, j, k: (i, k)), w_spec],
        out_specs=pl.BlockSpec((tm, tn), lambda i, j, k: (i, j)),
        out_shape=jax.ShapeDtypeStruct((m, n), F32),
        compiler_params=_cparams(("parallel", "parallel", "arbitrary")),
        name=name,
    )(a, w)


def _mm_into_kernel(a_ref, w_ref, *rest):
    rest[-1][...] = jnp.dot(a_ref[...], w_ref[...], preferred_element_type=F32)


def matmul_into(a, w, buf, li, depth, tm, tn, name):
    m, kd = a.shape
    n = w.shape[1]
    in_specs = [pl.BlockSpec((tm, kd), lambda i, j: (i, 0)), pl.BlockSpec((kd, tn), lambda i, j: (0, j))]
    args = [a, w]
    aliases = {}
    if buf is not None:
        in_specs.append(pl.BlockSpec(memory_space=pl.ANY))
        args.append(buf)
        aliases = {2: 0}
    return pl.pallas_call(
        _mm_into_kernel,
        grid=(m // tm, n // tn),
        in_specs=in_specs,
        out_specs=pl.BlockSpec((None, tm, tn), lambda i, j: (li, i, j)),
        out_shape=jax.ShapeDtypeStruct((depth, m, n), F32),
        input_output_aliases=aliases,
        compiler_params=_cparams(("parallel", "parallel")),
        name=name,
    )(*args)


def _mm_groups_kernel(*refs):
    *a_refs, w_ref, o_ref, w_scr = refs
    kg = a_refs[0].shape[1]

    @pl.when(pl.program_id(1) == 0)
    def _():
        w_scr[...] = w_ref[...].astype(BF16)

    acc = jnp.dot(a_refs[0][...], w_scr[0:kg, :], preferred_element_type=F32)
    for g in range(1, len(a_refs)):
        acc = acc + jnp.dot(a_refs[g][...], w_scr[g * kg:(g + 1) * kg, :], preferred_element_type=F32)
    o_ref[...] = acc


def matmul_groups(parts, w_all, li, tm, tn, name):
    m, kg = parts[0].shape
    _, kd, n = w_all.shape
    return pl.pallas_call(
        _mm_groups_kernel,
        grid=(n // tn, m // tm),
        in_specs=[pl.BlockSpec((tm, kg), lambda j, i: (i, 0)) for _ in parts]
        + [pl.BlockSpec((None, kd, tn), lambda j, i: (li, 0, j))],
        out_specs=pl.BlockSpec((tm, tn), lambda j, i: (i, j)),
        out_shape=jax.ShapeDtypeStruct((m, n), F32),
        scratch_shapes=[pltpu.VMEM((kd, tn), BF16)],
        compiler_params=_cparams(("arbitrary", "arbitrary")),
        name=name,
    )(*parts, w_all)


def _ffn_act_kernel(g_ref, v_ref, halo_ref, st_ref, cw_ref, cb_ref, o_ref):
    g = g_ref[...]
    tail = jnp.where(pl.program_id(1) == 0, st_ref[...], halo_ref[...])
    taps = _shifted_taps(tail, g, FFN_TAPS)
    cw = cw_ref[...]
    y = taps[0] * cw[0:1]
    for i in range(1, FFN_TAPS):
        y = y + taps[i] * cw[i:i + 1]
    y = y + cb_ref[...]
    o_ref[...] = (_silu(y) * v_ref[...]).astype(o_ref.dtype)


def _ffn_up_act_kernel(x_ref, wg_ref, wv_ref, st_ref, cw_ref, cb_ref, act_ref, cst_ref, wg_scr, wv_scr, tail_scr,
                       *, tiles_per_seq, sub):
    i = pl.program_id(1)

    @pl.when(i == 0)
    def _():
        wg_scr[...] = wg_ref[...].astype(BF16)
        wv_scr[...] = wv_ref[...].astype(BF16)

    @pl.when(i % tiles_per_seq == 0)
    def _():
        tail_scr[...] = st_ref[...]

    wg = wg_scr[...]
    wv = wv_scr[...]
    cw = cw_ref[...]
    cb = cb_ref[...]
    tm = x_ref.shape[0]
    tail = tail_scr[...]

    def project(r):
        x = x_ref[r * sub:(r + 1) * sub, :]
        return jnp.dot(x, wg, preferred_element_type=F32), jnp.dot(x, wv, preferred_element_type=F32)

    nxt = project(0)
    for r in range(tm // sub):
        g, v = nxt
        if r + 1 < tm // sub:
            nxt = project(r + 1)
        taps = _shifted_taps(tail, g, FFN_TAPS)
        y = taps[0] * cw[0:1]
        for t in range(1, FFN_TAPS):
            y = y + taps[t] * cw[t:t + 1]
        act_ref[r * sub:(r + 1) * sub, :] = (_silu(y + cb) * v).astype(act_ref.dtype)
        tail = g[sub - SUBLANE:]
    tail_scr[...] = tail
    cst_ref[...] = tail


def ffn_up_act(h2, w_up_all, li, state8, conv_w, conv_b, l, tm, tn, sub):
    m, kd = h2.shape
    b = m // l
    nj = D_FF // tn
    tiles_per_seq = l // tm
    act, tails = pl.pallas_call(
        functools.partial(_ffn_up_act_kernel, tiles_per_seq=tiles_per_seq, sub=sub),
        grid=(nj, m // tm),
        in_specs=[
            pl.BlockSpec((tm, kd), lambda j, i: (i, 0)),
            pl.BlockSpec((None, kd, tn), lambda j, i: (li, 0, j)),
            pl.BlockSpec((None, kd, tn), lambda j, i: (li, 0, j + nj)),
            pl.BlockSpec((None, SUBLANE, tn), lambda j, i: (i // tiles_per_seq, 0, j)),
            pl.BlockSpec((FFN_TAPS, tn), lambda j, i: (0, j)),
            pl.BlockSpec((1, tn), lambda j, i: (0, j)),
        ],
        out_specs=[
            pl.BlockSpec((tm, tn), lambda j, i: (i, j)),
            pl.BlockSpec((None, SUBLANE, tn), lambda j, i: (i, 0, j)),
        ],
        out_shape=[jax.ShapeDtypeStruct((m, D_FF), BF16), jax.ShapeDtypeStruct((m // tm, SUBLANE, D_FF), F32)],
        scratch_shapes=[pltpu.VMEM((kd, tn), BF16), pltpu.VMEM((kd, tn), BF16), pltpu.VMEM((SUBLANE, tn), F32)],
        compiler_params=_cparams(("arbitrary", "arbitrary")),
        name="ffn_up_act",
    )(h2, w_up_all, w_up_all, state8, conv_w, conv_b.reshape(1, D_FF))
    return act, tails.reshape(b, tiles_per_seq, SUBLANE, D_FF)[:, tiles_per_seq - 1]


def ffn_act(up3, state8, conv_w, conv_b, ts, tn):
    b, l, _ = up3.shape
    nj = D_FF // tn
    hb = ts // SUBLANE
    return pl.pallas_call(
        _ffn_act_kernel,
        grid=(b, l // ts, nj),
        in_specs=[
            pl.BlockSpec((None, ts, tn), lambda bi, i, j: (bi, i, j)),
            pl.BlockSpec((None, ts, tn), lambda bi, i, j: (bi, i, j + nj)),
            pl.BlockSpec((None, SUBLANE, tn), lambda bi, i, j: (bi, jnp.maximum(i * hb - 1, 0), j)),
            pl.BlockSpec((None, SUBLANE, tn), lambda bi, i, j: (bi, 0, j)),
            pl.BlockSpec((FFN_TAPS, tn), lambda bi, i, j: (0, j)),
            pl.BlockSpec((1, tn), lambda bi, i, j: (0, j)),
        ],
        out_specs=pl.BlockSpec((None, ts, tn), lambda bi, i, j: (bi, i, j)),
        out_shape=jax.ShapeDtypeStruct((b, l, D_FF), BF16),
        compiler_params=_cparams(("parallel", "parallel", "parallel")),
        name="ffn_act",
    )(up3, up3, up3, state8, conv_w, conv_b.reshape(1, D_FF))


def _gdn_kernel(qkv_ref, z_ref, sm_ref, cbuf_ref, s0_ref, cw_ref, alog_ref, dtb_ref, nw_ref,
                o_ref, sout_ref, s_scr, tail_scr, *, c):
    ci = pl.program_id(1)

    @pl.when(ci == 0)
    def _():
        s_scr[...] = s0_ref[...]
        tail_scr[...] = cbuf_ref[...]

    x = qkv_ref[...]
    taps = _shifted_taps(tail_scr[...], x, GDN_TAPS)
    tail_scr[...] = x[c - SUBLANE:]
    cw = cw_ref[...]
    y = taps[0] * cw[0:1]
    for i in range(1, GDN_TAPS):
        y = y + taps[i] * cw[i:i + 1]
    y = _silu(y)

    sm = sm_ref[...]
    beta_all = jax.nn.sigmoid(sm)
    g_all = -jnp.exp(alog_ref[...]) * jax.nn.softplus(sm + dtb_ref[...])
    gcum_all = _cumsum_rows(g_all, c)
    rows_of = _row_getter(gcum_all, c)

    row = _iota2((c, c), 0)
    col = _iota2((c, c), 1)
    tri = row >= col
    strict = row > col
    z = z_ref[...]
    nw = nw_ref[...]
    heads = range(GDN_HEADS)
    ms, aqk, rhs, qd, kd, gls = [], [], [], [], [], []
    for h in heads:
        lo = h * GDN_D
        q = y[:, lo:lo + GDN_D]
        k = y[:, GROUP_W + lo:GROUP_W + lo + GDN_D]
        v = y[:, 2 * GROUP_W + lo:2 * GROUP_W + lo + GDN_D]
        q = q * lax.rsqrt(jnp.sum(q * q, axis=-1, keepdims=True) + 1e-6) * (GDN_D ** -0.5)
        k = k * lax.rsqrt(jnp.sum(k * k, axis=-1, keepdims=True) + 1e-6)
        beta = beta_all[:, SM_GDN_B + h:SM_GDN_B + h + 1]
        gc = gcum_all[:, SM_GDN_A + h:SM_GDN_A + h + 1]
        gam = jnp.exp(jnp.where(tri, gc - rows_of(SM_GDN_A + h), -jnp.inf))
        kbeta = k * beta
        mq = dot3s(jnp.concatenate([kbeta, q], axis=0), _split(k), NT)
        ms.append(jnp.where(strict, mq[:c] * gam, 0.0))
        aqk.append(mq[c:] * gam)
        eg = jnp.exp(gc)
        gl = gc[c - 1:c]
        rhs.append(jnp.concatenate([kbeta * eg, v * beta], axis=1))
        qd.append(q * eg)
        kd.append(k * jnp.exp(gl - gc))
        gls.append(gl)
    ts = _inv_unit_lower_multi(ms, c)
    wu = [dot3s(t, _split(x), NN) for t, x in zip(ts, rhs)]
    wqs = [dot3s(jnp.concatenate([x[:, :GDN_D], qd[h]], axis=0), _split(s_scr[h]), NN)
           for h, x in enumerate(wu)]
    for h in heads:
        lo = h * GDN_D
        v_new = wu[h][:, GDN_D:] - wqs[h][:c]
        o = wqs[h][c:] + dot1(aqk[h], v_new, NN)
        s_scr[h] = s_scr[h] * jnp.exp(gls[h]) + dot3(_split(kd[h]), _split(v_new), TN)
        o = o * lax.rsqrt(jnp.mean(o * o, axis=-1, keepdims=True) + NORM_EPS) * nw
        o = o * _silu(z[:, lo:lo + GDN_D])
        o_ref[:, lo:lo + GDN_D] = o.astype(o_ref.dtype)

    @pl.when(ci == pl.num_programs(1) - 1)
    def _():
        sout_ref[...] = s_scr[...]


def gdn_mixer(proj3, cbuf8, s0, conv_w, a_log, dt_bias, norm_w, c):
    b, l, _ = proj3.shape
    alog_row = jnp.zeros((1, LANE), F32).at[0, SM_GDN_A:SM_GDN_A + GDN_HEADS].set(a_log)
    dtb_row = jnp.zeros((1, LANE), F32).at[0, SM_GDN_A:SM_GDN_A + GDN_HEADS].set(dt_bias)
    w3 = 3 * GROUP_W
    const2 = lambda bi, ci: (0, 0)
    return pl.pallas_call(
        functools.partial(_gdn_kernel, c=c),
        grid=(b, l // c),
        in_specs=[
            pl.BlockSpec((None, c, w3), lambda bi, ci: (bi, ci, COL_A_QKV // w3)),
            pl.BlockSpec((None, c, GROUP_W), lambda bi, ci: (bi, ci, COL_A_Z // GROUP_W)),
            pl.BlockSpec((None, c, LANE), lambda bi, ci: (bi, ci, COL_SMALL // LANE)),
            pl.BlockSpec((None, SUBLANE, w3), lambda bi, ci: (bi, 0, 0)),
            pl.BlockSpec((None, GDN_HEADS, GDN_D, GDN_D), lambda bi, ci: (bi, 0, 0, 0)),
            pl.BlockSpec((GDN_TAPS, w3), const2),
            pl.BlockSpec((1, LANE), const2),
            pl.BlockSpec((1, LANE), const2),
            pl.BlockSpec((1, GDN_D), const2),
        ],
        out_specs=[
            pl.BlockSpec((None, c, GROUP_W), lambda bi, ci: (bi, ci, 0)),
            pl.BlockSpec((None, GDN_HEADS, GDN_D, GDN_D), lambda bi, ci: (bi, 0, 0, 0)),
        ],
        out_shape=[jax.ShapeDtypeStruct((b, l, GROUP_W), BF16),
                   jax.ShapeDtypeStruct((b, GDN_HEADS, GDN_D, GDN_D), F32)],
        scratch_shapes=[pltpu.VMEM((GDN_HEADS, GDN_D, GDN_D), F32), pltpu.VMEM((SUBLANE, w3), F32)],
        compiler_params=_cparams(("parallel", "arbitrary")),
        name="gdn_mixer",
    )(proj3, proj3, proj3, cbuf8, s0, conv_w, alog_row, dtb_row, norm_w.reshape(1, GDN_D))


def _rwkv_kernel(rkv_ref, lora_ref, sh_rkv_ref, sh_lora_ref, s0_ref, mu_rkv_ref, mu_lora_ref,
                 w0_ref, w2_ref, a0_ref, a2_ref, g2_ref, kk_ref, ka_ref, rk_ref, lnw_ref, lnb_ref,
                 o_ref, sout_ref, s_scr, tail_rkv, tail_lora, *, c):
    ci = pl.program_id(1)

    @pl.when(ci == 0)
    def _():
        s_scr[...] = s0_ref[...]
        tail_rkv[...] = sh_rkv_ref[...]
        tail_lora[...] = sh_lora_ref[...]

    x = rkv_ref[...]
    xl = lora_ref[...]
    prev = _shifted_taps(tail_rkv[...], x, 2)[0]
    prev_l = _shifted_taps(tail_lora[...], xl, 2)[0]
    tail_rkv[...] = x[c - SUBLANE:]
    tail_lora[...] = xl[c - SUBLANE:]
    zm = x + (prev - x) * mu_rkv_ref[...]
    zl = xl + (prev_l - xl) * mu_lora_ref[...]
    r = zm[:, 0:GROUP_W]
    k = zm[:, GROUP_W:2 * GROUP_W]
    v = zm[:, 2 * GROUP_W:3 * GROUP_W]
    wd = zl[:, 0:RWKV_W_LORA]
    ad = zl[:, RWKV_W_LORA:RWKV_W_LORA + RWKV_A_LORA]
    gd = zl[:, RWKV_W_LORA + RWKV_A_LORA:LORA_PAD]

    w_log = -jax.nn.softplus(-(w0_ref[...] + dot3s(jnp.tanh(wd), _split(w2_ref[...]), NN))) - 0.5
    logw = -jnp.exp(w_log)
    a = jax.nn.sigmoid(a0_ref[...] + dot3s(ad, _split(a2_ref[...]), NN))
    gate = dot1(jax.nn.sigmoid(gd), g2_ref[...], NN)
    kkv = k * kk_ref[...]
    k2 = k * (1.0 + (a - 1.0) * ka_ref[...])
    lcum = _cumsum_rows(logw, c)

    row = _iota2((c, c), 0)
    col = _iota2((c, c), 1)
    strict = row > col
    tri2 = _iota2((c, 2 * c), 0) >= (_iota2((c, 2 * c), 1) & (c - 1))
    rk = rk_ref[...]
    lnw = lnw_ref[...]
    lnb = lnb_ref[...]
    heads = range(RWKV_HEADS)
    sl = [slice(h * RWKV_HS, (h + 1) * RWKV_HS) for h in heads]
    acat, ars, rcat, av, bk_end, l_last, vs = [], [], [], [], [], [], []
    for h in heads:
        r_h, k_h, v_h, a_h = r[:, sl[h]], k2[:, sl[h]], v[:, sl[h]], a[:, sl[h]]
        kk = kkv[:, sl[h]]
        kk = kk * lax.rsqrt(jnp.sum(kk * kk, axis=-1, keepdims=True) + 1e-6)
        lc = lcum[:, sl[h]]
        ll = lc[c - 1:c]
        p_inv = jnp.exp(-lc)
        a_t = -kk * jnp.exp(lc - logw[:, sl[h]])
        b_vec = kk * a_h
        r_t = r_h * jnp.exp(lc)
        p_end = jnp.exp(ll - lc)
        bk_s = _split(jnp.concatenate([b_vec * p_inv, k_h * p_inv], axis=0))
        acat.append(dot3s(a_t, bk_s, NT))
        ars.append(dot3s(jnp.concatenate([a_t, r_t], axis=0), _split(s_scr[h]), NT))
        rcat.append(jnp.where(tri2, _dot(r_t.astype(BF16), bk_s[0], NT, None), 0.0))
        bk_end.append(jnp.concatenate([b_vec * p_end, k_h * p_end], axis=0))
        l_last.append(ll)
        vs.append(v_h)
    a_ab = [jnp.where(strict, x[:, :c], 0.0) for x in acat]
    av = [dot3s(jnp.where(strict, x[:, c:], 0.0), _split(v_h), NN) for x, v_h in zip(acat, vs)]
    ts = _inv_unit_lower_multi([-x for x in a_ab], c)
    us = [dot3s(t, _split(x[:c] + y), NN) for t, x, y in zip(ts, ars, av)]
    for h in heads:
        uv_s = _split(jnp.concatenate([us[h], vs[h]], axis=0))
        y = ars[h][c:] + _dot(rcat[h].astype(BF16), uv_s[0], NN, None)
        s_scr[h] = s_scr[h] * jnp.exp(l_last[h]) + dot3(uv_s, _split(bk_end[h]), TN)
        mean = jnp.mean(y, axis=-1, keepdims=True)
        var = jnp.mean(jnp.square(y - mean), axis=-1, keepdims=True)
        yn = (y - mean) * lax.rsqrt(var + RWKV_GN_EPS) * lnw[:, sl[h]] + lnb[:, sl[h]]
        bonus = jnp.sum(r[:, sl[h]] * k2[:, sl[h]] * rk[:, sl[h]], axis=-1, keepdims=True) * vs[h]
        o_ref[:, sl[h]] = ((yn + bonus) * gate[:, sl[h]]).astype(o_ref.dtype)

    @pl.when(ci == pl.num_programs(1) - 1)
    def _():
        sout_ref[...] = s_scr[...]


def rwkv_mixer(proj3, sh_rkv8, sh_lora8, s0, prm, c):
    b, l, _ = proj3.shape
    w3 = 3 * GROUP_W
    mu = prm['rwkv_mu']
    mu_rkv = mu[:w3].reshape(1, w3)
    mu_lora = jnp.pad(mu[w3:], (0, LORA_PAD - RWKV_LORA)).reshape(1, LORA_PAD)
    g2 = jnp.pad(prm['rwkv_g2'], ((0, G_LORA_PAD - RWKV_G_LORA), (0, 0)))
    row = lambda t: t.reshape(1, GROUP_W)
    const2 = lambda bi, ci: (0, 0)
    vec = pl.BlockSpec((1, GROUP_W), const2)
    return pl.pallas_call(
        functools.partial(_rwkv_kernel, c=c),
        grid=(b, l // c),
        in_specs=[
            pl.BlockSpec((None, c, w3), lambda bi, ci: (bi, ci, COL_B_RKV // w3)),
            pl.BlockSpec((None, c, LORA_PAD), lambda bi, ci: (bi, ci, COL_B_LORA // LORA_PAD)),
            pl.BlockSpec((None, SUBLANE, w3), lambda bi, ci: (bi, 0, 0)),
            pl.BlockSpec((None, SUBLANE, LORA_PAD), lambda bi, ci: (bi, 0, 0)),
            pl.BlockSpec((None, RWKV_HEADS, RWKV_HS, RWKV_HS), lambda bi, ci: (bi, 0, 0, 0)),
            pl.BlockSpec((1, w3), const2),
            pl.BlockSpec((1, LORA_PAD), const2),
            vec,
            pl.BlockSpec((RWKV_W_LORA, GROUP_W), const2),
            vec,
            pl.BlockSpec((RWKV_A_LORA, GROUP_W), const2),
            pl.BlockSpec((G_LORA_PAD, GROUP_W), const2),
            vec, vec, vec, vec, vec,
        ],
        out_specs=[
            pl.BlockSpec((None, c, GROUP_W), lambda bi, ci: (bi, ci, 0)),
            pl.BlockSpec((None, RWKV_HEADS, RWKV_HS, RWKV_HS), lambda bi, ci: (bi, 0, 0, 0)),
        ],
        out_shape=[jax.ShapeDtypeStruct((b, l, GROUP_W), BF16),
                   jax.ShapeDtypeStruct((b, RWKV_HEADS, RWKV_HS, RWKV_HS), F32)],
        scratch_shapes=[pltpu.VMEM((RWKV_HEADS, RWKV_HS, RWKV_HS), F32),
                        pltpu.VMEM((SUBLANE, w3), F32), pltpu.VMEM((SUBLANE, LORA_PAD), F32)],
        compiler_params=_cparams(("parallel", "arbitrary")),
        name="rwkv_mixer",
    )(proj3, proj3, sh_rkv8, sh_lora8, s0, mu_rkv, mu_lora,
      row(prm['rwkv_w0']), prm['rwkv_w2'], row(prm['rwkv_a0']), prm['rwkv_a2'], g2,
      row(prm['rwkv_k_k']), row(prm['rwkv_k_a']), row(prm['rwkv_r_k']),
      row(prm['rwkv_ln_w']), row(prm['rwkv_ln_b']))


def _ssd_kernel(z_ref, xbc_ref, sm_ref, cbuf_ref, s0_ref, cw_ref, cb_ref, dtb_ref, alog_ref, dsk_ref, nw_ref,
                o_ref, sout_ref, s_scr, tail_scr, y_scr, *, c):
    ci = pl.program_id(1)

    @pl.when(ci == 0)
    def _():
        s_scr[...] = s0_ref[...]
        tail_scr[...] = cbuf_ref[...]

    x = xbc_ref[...]
    taps = _shifted_taps(tail_scr[...], x, SSM_TAPS)
    tail_scr[...] = x[c - SUBLANE:]
    cw = cw_ref[...]
    y = taps[0] * cw[0:1]
    for i in range(1, SSM_TAPS):
        y = y + taps[i] * cw[i:i + 1]
    y = _silu(y + cb_ref[...])
    xs = y[:, 0:GROUP_W]
    gn = SSM_GROUPS * SSM_N

    sm = sm_ref[...]
    dt_all = jax.nn.softplus(sm + dtb_ref[...])
    da_all = dt_all * (-jnp.exp(alog_ref[...]))
    acs_all = _cumsum_rows(da_all, c)

    row = _iota2((c, c), 0)
    col = _iota2((c, c), 1)
    tri = row >= col
    z = z_ref[...]
    dsk = dsk_ref[...]
    hpg = SSM_HEADS // SSM_GROUPS
    gp = hpg * SSM_P
    rows_of = _row_getter(acs_all, c)
    for g in range(SSM_GROUPS):
        bm = y[:, GROUP_W + g * SSM_N:GROUP_W + (g + 1) * SSM_N]
        cm = y[:, GROUP_W + gn + g * SSM_N:GROUP_W + gn + (g + 1) * SSM_N]
        cb = dot1(cm, bm, NT)
        sg = s_scr[g]
        y_off = dot1(cm, sg, NT)
        xdec = []
        for rr in range(hpg):
            h = g * hpg + rr
            lo = h * SSM_P
            lane = SM_SSM_DT + h
            xs_h = xs[:, lo:lo + SSM_P]
            dt = dt_all[:, lane:lane + 1]
            acs = acs_all[:, lane:lane + 1]
            lmat = jnp.exp(jnp.where(tri, acs - rows_of(lane), -jnp.inf))
            xd = xs_h * dt
            a_last = acs[c - 1:c]
            xdec.append(xd * jnp.exp(a_last - acs))
            yh = dot1(cb * lmat, xd, NN) + y_off[:, rr * SSM_P:(rr + 1) * SSM_P] * jnp.exp(acs)
            yh = yh + xs_h * dsk[:, lane:lane + 1]
            y_scr[:, lo:lo + SSM_P] = yh * _silu(z[:, lo:lo + SSM_P])
        upd = dot3(_split(jnp.concatenate(xdec, axis=1)), _split(bm), TN)
        for rr in range(hpg):
            lane = SM_SSM_DT + g * hpg + rr
            dec = jnp.exp(acs_all[c - 1:c, lane:lane + 1])
            s_scr[g, rr * SSM_P:(rr + 1) * SSM_P, :] = (sg[rr * SSM_P:(rr + 1) * SSM_P] * dec
                                                         + upd[rr * SSM_P:(rr + 1) * SSM_P])

    gw = GROUP_W // SSM_GROUPS
    nw = nw_ref[...]
    for g in range(SSM_GROUPS):
        yg = y_scr[:, g * gw:(g + 1) * gw]
        yg = yg * lax.rsqrt(jnp.mean(yg * yg, axis=-1, keepdims=True) + NORM_EPS)
        o_ref[:, g * gw:(g + 1) * gw] = (yg * nw[:, g * gw:(g + 1) * gw]).astype(o_ref.dtype)

    @pl.when(ci == pl.num_programs(1) - 1)
    def _():
        sout_ref[...] = s_scr[...]


def ssd_mixer(proj3, cbuf8, s0, prm, c):
    b, l, _ = proj3.shape

    def lanes(t):
        return jnp.zeros((1, LANE), F32).at[0, SM_SSM_DT:SM_SSM_DT + SSM_HEADS].set(t)

    const2 = lambda bi, ci: (0, 0)
    small = pl.BlockSpec((1, LANE), const2)
    gshape = (SSM_GROUPS, SSM_HEADS // SSM_GROUPS * SSM_P, SSM_N)
    state_spec = pl.BlockSpec((None,) + gshape, lambda bi, ci: (bi, 0, 0, 0))
    o, s_new = pl.pallas_call(
        functools.partial(_ssd_kernel, c=c),
        grid=(b, l // c),
        in_specs=[
            pl.BlockSpec((None, c, GROUP_W), lambda bi, ci: (bi, ci, COL_C_Z // GROUP_W)),
            pl.BlockSpec((None, c, SSM_XBC), lambda bi, ci: (bi, ci, COL_C_XBC // SSM_XBC)),
            pl.BlockSpec((None, c, LANE), lambda bi, ci: (bi, ci, COL_SMALL // LANE)),
            pl.BlockSpec((None, SUBLANE, SSM_XBC), lambda bi, ci: (bi, 0, 0)),
            state_spec,
            pl.BlockSpec((SSM_TAPS, SSM_XBC), const2),
            pl.BlockSpec((1, SSM_XBC), const2),
            small, small, small,
            pl.BlockSpec((1, GROUP_W), const2),
        ],
        out_specs=[
            pl.BlockSpec((None, c, GROUP_W), lambda bi, ci: (bi, ci, 0)),
            state_spec,
        ],
        out_shape=[jax.ShapeDtypeStruct((b, l, GROUP_W), BF16),
                   jax.ShapeDtypeStruct((b,) + gshape, F32)],
        scratch_shapes=[pltpu.VMEM(gshape, F32), pltpu.VMEM((SUBLANE, SSM_XBC), F32),
                        pltpu.VMEM((c, GROUP_W), F32)],
        compiler_params=_cparams(("parallel", "arbitrary")),
        name="ssd_mixer",
    )(proj3, proj3, proj3, cbuf8, s0.reshape((b,) + gshape), prm['ssm_conv_w'],
      prm['ssm_conv_b'].reshape(1, SSM_XBC),
      lanes(prm['ssm_dt_bias']), lanes(prm['ssm_A_log']), lanes(prm['ssm_D']),
      prm['ssm_norm_w'].reshape(1, GROUP_W))
    return o, s_new.reshape(b, SSM_HEADS, SSM_P, SSM_N)


def _swa_weight(d):
    mult = jnp.zeros(d.shape, F32)
    for window, dil in SWA_PATTERNS:
        ok = (d >= 0) & (d <= window) & ((d & (dil - 1)) == 0)
        mult = mult + jnp.where(ok, 1.0, 0.0)
    return mult


def _swa_scores(q, k, d, slope):
    s = dot_nt(q.astype(BF16), k.astype(BF16), None) * (SWA_HD ** -0.5)
    mult = _swa_weight(d)
    s = s - slope * d.astype(F32)
    return jnp.where(mult > 0.0, s, NEG_INF), mult


def _swa_prompt_kernel(slopes_ref, q_ref, k_ref, v_ref, lw_ref, o_ref, m_scr, l_scr, acc_scr, *, t, sub):
    h = pl.program_id(1)
    qi = pl.program_id(2)
    ki = pl.program_id(3)

    @pl.when(ki == 0)
    def _():
        m_scr[...] = jnp.full(m_scr.shape, NEG_INF, F32)
        l_scr[...] = jnp.zeros(l_scr.shape, F32)
        acc_scr[...] = jnp.zeros(acc_scr.shape, F32)

    @pl.when(ki <= qi)
    def _():
        col = (ki * t + _iota2((1, t), 1)).astype(F32) * slopes_ref[h]
        kb = k_ref[...].astype(BF16)
        vb = v_ref[...].astype(BF16)
        def qk(r):
            q = (q_ref[r:r + sub, :] * (SWA_HD ** -0.5)).astype(BF16)
            return _dot(q, kb, NT, None)

        nxt = qk(0)
        for r in range(0, t, sub):
            rs = slice(r, r + sub)
            s = nxt + (lw_ref[qi - ki, rs, :] + col)
            if r + sub < t:
                nxt = qk(r + sub)
            m_old = m_scr[rs, :]
            m_new = jnp.maximum(m_old, jnp.max(s, axis=-1, keepdims=True))
            alpha = jnp.exp(m_old - m_new)
            p = jnp.exp(s - m_new)
            l_scr[rs, :] = alpha * l_scr[rs, :] + jnp.sum(p, axis=-1, keepdims=True)
            acc_scr[rs, :] = alpha * acc_scr[rs, :] + _dot(p.astype(BF16), vb, NN, None)
            m_scr[rs, :] = m_new

    @pl.when(ki == pl.num_programs(3) - 1)
    def _():
        o_ref[...] = (acc_scr[...] / l_scr[...]).astype(o_ref.dtype)


def _alibi_slopes():
    return jnp.asarray([2.0 ** (-8.0 * (i + 1) / SWA_HEADS) for i in range(SWA_HEADS)], F32)


def _swa_log_weight_tiles(n, t):
    d = (jnp.arange(n)[:, None, None] * t + jnp.arange(t)[None, :, None]) - jnp.arange(t)[None, None, :]
    mult = _swa_weight(d.astype(jnp.int32))
    return jnp.where(mult > 0.0, jnp.log(jnp.maximum(mult, 1.0)), NEG_INF)


def swa_prompt(proj3, k4, v4, li, t):
    b, l, _ = proj3.shape
    qc = COL_D_Q // SWA_HD
    n = l // t
    kv_spec = pl.BlockSpec((None, None, t, SWA_HD), lambda bi, h, qi, ki: (li, bi, jnp.minimum(ki, qi), h))

    return pl.pallas_call(
        functools.partial(_swa_prompt_kernel, t=t, sub=min(t, 256)),
        grid=(b, SWA_HEADS, n, n),
        in_specs=[
            pl.BlockSpec(memory_space=pltpu.SMEM),
            pl.BlockSpec((None, t, SWA_HD), lambda bi, h, qi, ki: (bi, qi, qc + h)),
            kv_spec,
            kv_spec,
            pl.BlockSpec((n, t, t), lambda bi, h, qi, ki: (0, 0, 0)),
        ],
        out_specs=pl.BlockSpec((None, t, SWA_HD), lambda bi, h, qi, ki: (bi, qi, h)),
        out_shape=jax.ShapeDtypeStruct((b, l, GROUP_W), BF16),
        scratch_shapes=[pltpu.VMEM((t, 1), F32), pltpu.VMEM((t, 1), F32), pltpu.VMEM((t, SWA_HD), F32)],
        compiler_params=_cparams(("parallel", "parallel", "parallel", "arbitrary")),
        name="swa_prompt",
    )(_alibi_slopes(), proj3, k4, v4, _swa_log_weight_tiles(n, t))


def _swa_sample_kernel(slopes_ref, q_ref, k_ref, v_ref, ck_ref, cv_ref, o_ref, *, t, wb):
    d_c = (wb + _iota2((t, wb), 0)) - _iota2((t, wb), 1)
    d_n = _iota2((t, t), 0) - _iota2((t, t), 1)
    for h in range(SWA_HEADS):
        hs = slice(h * SWA_HD, (h + 1) * SWA_HD)
        ck = ck_ref[pl.ds(h, wb, stride=SWA_HEADS), :]
        cv = cv_ref[pl.ds(h, wb, stride=SWA_HEADS), :]
        q = q_ref[:, hs]
        s_c, mult_c = _swa_scores(q, ck, d_c, slopes_ref[h])
        s_n, mult_n = _swa_scores(q, k_ref[:, hs], d_n, slopes_ref[h])
        m = jnp.maximum(jnp.max(s_c, axis=-1, keepdims=True), jnp.max(s_n, axis=-1, keepdims=True))
        p_c = jnp.exp(s_c - m) * mult_c
        p_n = jnp.exp(s_n - m) * mult_n
        den = jnp.sum(p_c, axis=-1, keepdims=True) + jnp.sum(p_n, axis=-1, keepdims=True)
        num = (dot_nn(p_c.astype(BF16), cv.astype(BF16), None)
               + dot_nn(p_n.astype(BF16), v_ref[:, hs].astype(BF16), None))
        o_ref[:, hs] = (num / den).astype(o_ref.dtype)


def swa_sample(proj3, k4, v4, li, cache_k_all, cache_v_all):
    b, t, _ = proj3.shape
    depth, _, wb = cache_k_all.shape[:3]
    ck = cache_k_all.reshape(depth, b, wb * SWA_HEADS, SWA_HD)
    cv = cache_v_all.reshape(depth, b, wb * SWA_HEADS, SWA_HD)
    new_spec = pl.BlockSpec((None, None, t, GROUP_W), lambda bi: (li, bi, 0, 0))
    cache_spec = pl.BlockSpec((None, None, wb * SWA_HEADS, SWA_HD), lambda bi: (li, bi, 0, 0))
    return pl.pallas_call(
        functools.partial(_swa_sample_kernel, t=t, wb=wb),
        grid=(b,),
        in_specs=[
            pl.BlockSpec(memory_space=pltpu.SMEM),
            pl.BlockSpec((None, t, GROUP_W), lambda bi: (bi, 0, COL_D_Q // GROUP_W)),
            new_spec,
            new_spec,
            cache_spec,
            cache_spec,
        ],
        out_specs=pl.BlockSpec((None, t, GROUP_W), lambda bi: (bi, 0, 0)),
        out_shape=jax.ShapeDtypeStruct((b, t, GROUP_W), BF16),
        compiler_params=_cparams(("parallel",)),
        name="swa_sample",
    )(_alibi_slopes(), proj3, k4, v4, ck, cv)


def _tiles(m):
    return (256, 1024) if m >= 1024 else (m, m)


def _front_pad_rows(t, rows=SUBLANE):
    return jnp.pad(t, ((0, 0), (rows - t.shape[1], 0), (0, 0)))


def prep_weights(p):
    w = p['w_in']
    o = 0
    seg = {}
    for name, n in (('a_qkv', 3 * GROUP_W), ('a_z', GROUP_W), ('a_b', GDN_HEADS), ('a_a', GDN_HEADS),
                    ('b_rkv', 3 * GROUP_W), ('b_lora', RWKV_LORA), ('c_z', GROUP_W), ('c_xbc', SSM_XBC),
                    ('c_dt', SSM_HEADS), ('d_q', GROUP_W), ('d_k', GROUP_W), ('d_v', GROUP_W)):
        seg[name] = w[:, o:o + n].astype(BF16)
        o += n
    zeros = lambda n: jnp.zeros((D_MODEL, n), BF16)
    small_used = 2 * GDN_HEADS + SSM_HEADS
    w_in = jnp.concatenate([
        seg['a_qkv'], seg['b_rkv'], seg['d_q'], seg['a_z'], seg['c_z'], seg['c_xbc'],
        seg['b_lora'], zeros(LORA_PAD - RWKV_LORA),
        seg['a_b'], seg['a_a'], seg['c_dt'], zeros(LANE - small_used)], axis=1)
    return {'w_in': w_in, 'w_k': seg['d_k'], 'w_v': seg['d_v']}


def decoder_layer(x, prm, wts, past, swa_cache, chunks, li, depth, kv=(None, None), h=None, next_pre_w=None):
    b, l, _ = x.shape
    m = b * l
    tr, tm = _tiles(m)
    x2 = x.reshape(m, D_MODEL)
    c_gdn, c_rwkv, c_ssd = chunks

    if h is None:
        h = rms_cast(x2, prm['norm_mix_pre'], tr)
    proj = matmul(h, wts['w_in'], tm, 512, D_MODEL, "mm_in")
    proj3 = proj.reshape(b, l, N_PROJ)
    kbuf = matmul_into(h, wts['w_k'], kv[0], li, depth, tm, 512, "mm_k")
    vbuf = matmul_into(h, wts['w_v'], kv[1], li, depth, tm, 512, "mm_v")
    k4 = kbuf.reshape(depth, b, l, GROUP_W)
    v4 = vbuf.reshape(depth, b, l, GROUP_W)

    o_a, gdn_s = gdn_mixer(proj3, _front_pad_rows(past['gdn_conv']), past['gdn'], prm['gdn_conv_w'],
                           prm['gdn_A_log'], prm['gdn_dt_bias'], prm['gdn_norm_w'], c_gdn)
    shift = past['rwkv_shift'][:, None, :]
    sh_rkv8 = _front_pad_rows(shift[:, :, :3 * GROUP_W])
    sh_lora8 = _front_pad_rows(jnp.pad(shift[:, :, 3 * GROUP_W:], ((0, 0), (0, 0), (0, LORA_PAD - RWKV_LORA))))
    o_b, rwkv_s = rwkv_mixer(proj3, sh_rkv8, sh_lora8, past['rwkv'], prm, c_rwkv)
    o_c, ssm_s = ssd_mixer(proj3, _front_pad_rows(past['ssm_conv']), past['ssm'], prm, c_ssd)
    if swa_cache is None:
        o_d = swa_prompt(proj3, k4, v4, li, min(l, SWA_TILE))
    else:
        o_d = swa_sample(proj3, k4, v4, li, swa_cache[0], swa_cache[1])

    gdn_conv = proj3[:, l - (GDN_TAPS - 1):, COL_A_QKV:COL_A_QKV + 3 * GROUP_W]
    ssm_conv = proj3[:, l - (SSM_TAPS - 1):, COL_C_XBC:COL_C_XBC + SSM_XBC]
    rwkv_shift = jnp.concatenate([proj3[:, l - 1, COL_B_RKV:COL_B_RKV + 3 * GROUP_W],
                                  proj3[:, l - 1, COL_B_LORA:COL_B_LORA + RWKV_LORA]], axis=-1)

    y = matmul_groups([o.reshape(m, GROUP_W) for o in (o_a, o_b, o_c, o_d)], wts['w_out_all'], li, tm, 512,
                      "mm_out")
    x2, h2 = add_rms_cast(x2, y, prm['norm_mix_post'], prm['norm_ffn_pre'], tr)
    state8 = _front_pad_rows(past['ffn_conv'])
    if l % tm == 0:
        act, cst = ffn_up_act(h2, wts['w_up_all'], li, state8, prm['ffn_conv_w'], prm['ffn_conv_b'], l, tm, 256,
                              256)
        ffn_conv = cst[:, SUBLANE - (FFN_TAPS - 1):, :]
    else:
        up3 = matmul(h2, wts['w_up_all'], tm, 512, D_MODEL, "mm_up", li).reshape(b, l, 2 * D_FF)
        act = ffn_act(up3, state8, prm['ffn_conv_w'], prm['ffn_conv_b'], l, D_FF // 2).reshape(m, D_FF)
        ffn_conv = up3[:, l - (FFN_TAPS - 1):, :D_FF]
    y2 = matmul(act, wts['w_down_all'], tm, 512, D_FF // 2, "mm_down", li)
    if next_pre_w is None:
        x2, h_next = add_rms(x2, y2, prm['norm_ffn_post'], tr), None
    else:
        x2, h_next = add_rms_cast(x2, y2, prm['norm_ffn_post'], next_pre_w, tr)
    return x2.reshape(b, l, D_MODEL), h_next, (kbuf, vbuf), (gdn_s, gdn_conv, rwkv_s, rwkv_shift, ssm_s, ssm_conv,
                                                             ffn_conv)


def _zero_past(bsz):
    return {
        'gdn': jnp.zeros((bsz, GDN_HEADS, GDN_D, GDN_D), F32),
        'gdn_conv': jnp.zeros((bsz, GDN_TAPS - 1, 3 * GROUP_W), F32),
        'rwkv': jnp.zeros((bsz, RWKV_HEADS, RWKV_HS, RWKV_HS), F32),
        'rwkv_shift': jnp.zeros((bsz, 3 * GROUP_W + RWKV_LORA), F32),
        'ssm': jnp.zeros((bsz, SSM_HEADS, SSM_P, SSM_N), F32),
        'ssm_conv': jnp.zeros((bsz, SSM_TAPS - 1, SSM_XBC), F32),
        'ffn_conv': jnp.zeros((bsz, FFN_TAPS - 1, D_FF), F32),
    }


PARAM_NAMES = ('norm_mix_pre', 'norm_mix_post', 'norm_ffn_pre', 'norm_ffn_post', 'w_in', 'w_out', 'gdn_conv_w',
               'gdn_A_log', 'gdn_dt_bias', 'gdn_norm_w', 'rwkv_mu', 'rwkv_w0', 'rwkv_w2', 'rwkv_a0', 'rwkv_a2',
               'rwkv_g2', 'rwkv_k_k', 'rwkv_k_a', 'rwkv_r_k', 'rwkv_ln_w', 'rwkv_ln_b', 'ssm_conv_w', 'ssm_conv_b',
               'ssm_dt_bias', 'ssm_A_log', 'ssm_D', 'ssm_norm_w', 'ffn_w_up', 'ffn_conv_w', 'ffn_conv_b',
               'ffn_w_down')


def kernel(x_prompt, x_sample, state_gdn, state_gdn_conv, state_rwkv, state_rwkv_shift, state_ssm, state_ssm_conv, cache_swa_k, cache_swa_v, state_ffn_conv, norm_mix_pre, norm_mix_post, norm_ffn_pre, norm_ffn_post, w_in, w_out, gdn_conv_w, gdn_A_log, gdn_dt_bias, gdn_norm_w, rwkv_mu, rwkv_w0, rwkv_w2, rwkv_a0, rwkv_a2, rwkv_g2, rwkv_k_k, rwkv_k_a, rwkv_r_k, rwkv_ln_w, rwkv_ln_b, ssm_conv_w, ssm_conv_b, ssm_dt_bias, ssm_A_log, ssm_D, ssm_norm_w, ffn_w_up, ffn_conv_w, ffn_conv_b, ffn_w_down):
    params = dict(zip(PARAM_NAMES, (norm_mix_pre, norm_mix_post, norm_ffn_pre, norm_ffn_post, w_in, w_out,
                                    gdn_conv_w, gdn_A_log, gdn_dt_bias, gdn_norm_w, rwkv_mu, rwkv_w0, rwkv_w2,
                                    rwkv_a0, rwkv_a2, rwkv_g2, rwkv_k_k, rwkv_k_a, rwkv_r_k, rwkv_ln_w, rwkv_ln_b,
                                    ssm_conv_w, ssm_conv_b, ssm_dt_bias, ssm_A_log, ssm_D, ssm_norm_w, ffn_w_up,
                                    ffn_conv_w, ffn_conv_b, ffn_w_down)))
    depth = w_in.shape[0]
    xp, xs = x_prompt, x_sample
    t_dec = x_sample.shape[1]
    prompt_states, sample_states = [], []
    hp = hs = None
    kvp = kvs = (None, None)
    stacked = {'w_out_all': w_out, 'w_up_all': ffn_w_up, 'w_down_all': ffn_w_down.astype(BF16)}
    for li in range(depth):
        prm = {k: v[li] for k, v in params.items()}
        wts = {**prep_weights(prm), **stacked}
        nxt = norm_mix_pre[li + 1] if li + 1 < depth else None
        xp, hp, kvp, stp = decoder_layer(xp, prm, wts, _zero_past(xp.shape[0]), None, (64, 64, 128),
                                         li, depth, kvp, hp, nxt)
        past = {'gdn': state_gdn[li], 'gdn_conv': state_gdn_conv[li], 'rwkv': state_rwkv[li],
                'rwkv_shift': state_rwkv_shift[li], 'ssm': state_ssm[li], 'ssm_conv': state_ssm_conv[li],
                'ffn_conv': state_ffn_conv[li]}
        xs, hs, kvs, sts = decoder_layer(xs, prm, wts, past, (cache_swa_k, cache_swa_v),
                                         (t_dec, t_dec, t_dec), li, depth, kvs, hs, nxt)
        prompt_states.append(stp)
        sample_states.append(sts)

    def window_rows(buf, x):
        bsz, l = x.shape[0], x.shape[1]
        rows = buf.reshape(depth, bsz, l, SWA_HEADS, SWA_HD)
        return rows[:, :, max(l - SWA_MAX_WINDOW, 0):]

    def outputs(states, kv, x):
        st = [jnp.stack(t) for t in zip(*states)]
        return (*st[:6], window_rows(kv[0], x), window_rows(kv[1], x), st[6])

    return (xp, xs, *outputs(prompt_states, kvp, x_prompt), *outputs(sample_states, kvs, x_sample))
```

```python
import functools

import jax
import jax.numpy as jnp
from jax import lax
from jax.experimental import pallas as pl
from jax.experimental.pallas import tpu as pltpu

F32 = jnp.float32
BF16 = jnp.bfloat16
HI = lax.Precision.HIGHEST

D_MODEL = 4096
GROUP_W = D_MODEL // 4
GDN_HEADS = 8
GDN_D = GROUP_W // GDN_HEADS
GDN_TAPS = 4
RWKV_HS = 64
RWKV_HEADS = GROUP_W // RWKV_HS
RWKV_W_LORA = 64
RWKV_A_LORA = 64
RWKV_G_LORA = 160
RWKV_LORA = RWKV_W_LORA + RWKV_A_LORA + RWKV_G_LORA
RWKV_GN_EPS = 64e-5
SSM_P = 64
SSM_HEADS = GROUP_W // SSM_P
SSM_GROUPS = 2
SSM_N = 128
SSM_TAPS = 4
SSM_XBC = GROUP_W + 2 * SSM_GROUPS * SSM_N
SWA_HEADS = 8
SWA_HD = GROUP_W // SWA_HEADS
SWA_PATTERNS = ((128, 1), (512, 4), (2048, 16))
SWA_MAX_WINDOW = 2048
D_FF = 256 * ((8 * D_MODEL // 3 + 255) // 256)
FFN_TAPS = 3
NORM_EPS = 1e-6
NEG_INF = -1e30

LANE = 128
SUBLANE = 8
LORA_PAD = 384
G_LORA_PAD = LORA_PAD - RWKV_W_LORA - RWKV_A_LORA

COL_A_QKV = 0
COL_B_RKV = 3 * GROUP_W
COL_D_Q = 6 * GROUP_W
COL_A_Z = 7 * GROUP_W
COL_C_Z = 8 * GROUP_W
COL_C_XBC = 9 * GROUP_W
COL_B_LORA = COL_C_XBC + SSM_XBC
COL_SMALL = COL_B_LORA + LORA_PAD
N_PROJ = COL_SMALL + LANE
SM_GDN_B = 0
SM_GDN_A = GDN_HEADS
SM_SSM_DT = 2 * GDN_HEADS

VMEM_LIMIT = 56 * 1024 * 1024
SWA_TILE = 1024


def _cparams(sem):
    return pltpu.CompilerParams(dimension_semantics=sem, vmem_limit_bytes=VMEM_LIMIT)


def _dot(a, b, dims, prec):
    return lax.dot_general(a, b, (dims, ((), ())), precision=prec, preferred_element_type=F32)


def dot_nn(a, b, prec=HI):
    return _dot(a, b, ((1,), (0,)), prec)


def dot_nt(a, b, prec=HI):
    return _dot(a, b, ((1,), (1,)), prec)


def dot_tn(a, b, prec=HI):
    return _dot(a, b, ((0,), (0,)), prec)


def _silu(x):
    return x * jax.nn.sigmoid(x)


def _iota2(shape, axis):
    return lax.broadcasted_iota(jnp.int32, shape, axis)


def _log2(n):
    s = n.bit_length() - 1
    assert 1 << s == n
    return s


NN = ((1,), (0,))
NT = ((1,), (1,))
TN = ((0,), (0,))


def _split(x):
    hi = x.astype(BF16)
    return hi, (x - hi.astype(F32)).astype(BF16)


def _split_rows(x):
    hi = x.astype(BF16)
    hif = hi.astype(F32)
    return jnp.concatenate([hif, x - hif], axis=0).astype(BF16), hi


def dot3(ap, bp, dims):
    return (_dot(ap[0], bp[0], dims, None) + _dot(ap[0], bp[1], dims, None)
            + _dot(ap[1], bp[0], dims, None))


def dot3s(a, bp, dims):
    r = a.shape[0]
    stacked, hi = _split_rows(a)
    both = _dot(stacked, bp[0], dims, None)
    return both[:r] + both[r:] + _dot(hi, bp[1], dims, None)


def dot1(a, b, dims):
    return _dot(a.astype(BF16), b.astype(BF16), dims, None)


def _inv_unit_lower_multi(ms, c):
    row = _iota2((c, c), 0)
    col = _iota2((c, c), 1)
    eye = jnp.where(row == col, 1.0, 0.0).astype(F32)
    base = min(SUBLANE, c)
    sb = _log2(base)
    blk = (row >> sb) == (col >> sb)
    ps = [jnp.where(blk, -m, 0.0) for m in ms]
    ts = [eye + p for p in ps]
    if sb > 1:
        ps = [_dot(p.astype(BF16), p.astype(BF16), NN, None) for p in ps]
        for _ in range(sb - 2):
            both = [_dot(jnp.concatenate([t, p], axis=0).astype(BF16), p.astype(BF16), NN, None)
                    for t, p in zip(ts, ps)]
            ts = [t + x[:c] for t, x in zip(ts, both)]
            ps = [x[c:] for x in both]
        ts = [t + _dot(t.astype(BF16), p.astype(BF16), NN, None) for t, p in zip(ts, ps)]
    s = base
    while s < c:
        ls = _log2(s)
        off = ((row >> (ls + 1)) == (col >> (ls + 1))) & ((row >> ls) > (col >> ls))
        tbs = [t.astype(BF16) for t in ts]
        inner = [_dot(jnp.where(off, m, 0.0).astype(BF16), tb, NN, None) for m, tb in zip(ms, tbs)]
        ts = [t - _dot(tb, x.astype(BF16), NN, None) for t, tb, x in zip(ts, tbs, inner)]
        s *= 2
    res = [eye - t - dot3s(m, _split(t), NN) for m, t in zip(ms, ts)]
    return [t + _dot(t.astype(BF16), r.astype(BF16), NN, None) for t, r in zip(ts, res)]


def _blockdiag_rows(x, half):
    left = _iota2(x.shape, 1) < half
    return jnp.concatenate([jnp.where(left, x, 0.0), jnp.where(left, 0.0, x)], axis=0)


def _head_sums(x, ones_bd):
    r = x.shape[0]
    stacked, _ = _split_rows(x)
    both = _dot(stacked, ones_bd, NN, None)
    return both[:r] + both[r:]


def _inv_unit_lower_pairs(ms, c):
    row = _iota2((c, 2 * c), 0)
    col = _iota2((c, 2 * c), 1) & (c - 1)
    eye = jnp.where(row == col, 1.0, 0.0).astype(F32)
    base = min(SUBLANE, c)
    sb = _log2(base)
    blk = (row >> sb) == (col >> sb)

    def mm(a, b):
        return _dot(a.astype(BF16), _blockdiag_rows(b, c).astype(BF16), NN, None)

    ps = [jnp.where(blk, -m, 0.0) for m in ms]
    ts = [eye + p for p in ps]
    if sb > 1:
        ps = [mm(p, p) for p in ps]
        for _ in range(sb - 2):
            both = [mm(jnp.concatenate([t, p], axis=0), p) for t, p in zip(ts, ps)]
            ts = [t + x[:c] for t, x in zip(ts, both)]
            ps = [x[c:] for x in both]
        ts = [t + mm(t, p) for t, p in zip(ts, ps)]
    s = base
    while s < c:
        ls = _log2(s)
        off = ((row >> (ls + 1)) == (col >> (ls + 1))) & ((row >> ls) > (col >> ls))
        inner = [mm(jnp.where(off, m, 0.0), t) for m, t in zip(ms, ts)]
        ts = [t - mm(t, x) for t, x in zip(ts, inner)]
        s *= 2
    res = [eye - t - dot3s(m, _split(_blockdiag_rows(t, c)), NN) for m, t in zip(ms, ts)]
    return [t + mm(t, r) for t, r in zip(ts, res)]


def _row_getter(x, c):
    if c % LANE:
        x = jnp.concatenate([x, jnp.zeros((LANE - c % LANE, LANE), F32)], axis=0)
    xt = x.T
    return lambda lane: xt[lane:lane + 1, :c]


def _cumsum_rows(x, c):
    tri = jnp.where(_iota2((c, c), 0) >= _iota2((c, c), 1), 1.0, 0.0).astype(F32)
    return dot_nn(tri, x)


def _shifted_taps(tail, x, taps):
    c = x.shape[0]
    xp = jnp.concatenate([tail, x], axis=0)
    out = []
    for s in range(taps - 1, 0, -1):
        out.append(pltpu.roll(xp, s, 0)[SUBLANE:SUBLANE + c])
    out.append(x)
    return out


def _rms_cast_kernel(x_ref, w_ref, o_ref):
    x = x_ref[...]
    y = x * lax.rsqrt(jnp.mean(x * x, axis=-1, keepdims=True) + NORM_EPS)
    o_ref[...] = (y * w_ref[...]).astype(o_ref.dtype)


def rms_cast(x, w, tr):
    m, d = x.shape
    return pl.pallas_call(
        _rms_cast_kernel,
        grid=(m // tr,),
        in_specs=[pl.BlockSpec((tr, d), lambda i: (i, 0)), pl.BlockSpec((1, d), lambda i: (0, 0))],
        out_specs=pl.BlockSpec((tr, d), lambda i: (i, 0)),
        out_shape=jax.ShapeDtypeStruct((m, d), BF16),
        compiler_params=_cparams(("parallel",)),
        name="rms_cast",
    )(x, w.reshape(1, d))


def _add_rms_kernel(x_ref, y_ref, w_ref, o_ref):
    y = y_ref[...]
    yn = y * lax.rsqrt(jnp.mean(y * y, axis=-1, keepdims=True) + NORM_EPS)
    o_ref[...] = x_ref[...] + yn * w_ref[...]


def add_rms(x, y, w, tr):
    m, d = x.shape
    return pl.pallas_call(
        _add_rms_kernel,
        grid=(m // tr,),
        in_specs=[pl.BlockSpec((tr, d), lambda i: (i, 0)), pl.BlockSpec((tr, d), lambda i: (i, 0)),
                  pl.BlockSpec((1, d), lambda i: (0, 0))],
        out_specs=pl.BlockSpec((tr, d), lambda i: (i, 0)),
        out_shape=jax.ShapeDtypeStruct((m, d), F32),
        compiler_params=_cparams(("parallel",)),
        name="add_rms",
    )(x, y, w.reshape(1, d))


def _add_rms_cast_kernel(x_ref, y_ref, w_ref, wn_ref, o_ref, h_ref):
    y = y_ref[...]
    yn = y * lax.rsqrt(jnp.mean(y * y, axis=-1, keepdims=True) + NORM_EPS)
    x = x_ref[...] + yn * w_ref[...]
    o_ref[...] = x
    xn = x * lax.rsqrt(jnp.mean(x * x, axis=-1, keepdims=True) + NORM_EPS)
    h_ref[...] = (xn * wn_ref[...]).astype(h_ref.dtype)


def add_rms_cast(x, y, w, w_next, tr):
    m, d = x.shape
    row = pl.BlockSpec((tr, d), lambda i: (i, 0))
    vec = pl.BlockSpec((1, d), lambda i: (0, 0))
    return pl.pallas_call(
        _add_rms_cast_kernel,
        grid=(m // tr,),
        in_specs=[row, row, vec, vec],
        out_specs=[row, row],
        out_shape=[jax.ShapeDtypeStruct((m, d), F32), jax.ShapeDtypeStruct((m, d), BF16)],
        compiler_params=_cparams(("parallel",)),
        name="add_rms_cast",
    )(x, y, w.reshape(1, d), w_next.reshape(1, d))


def _mm_kernel(a_ref, w_ref, o_ref, *, nk):
    p = jnp.dot(a_ref[...], w_ref[...].astype(BF16), preferred_element_type=F32)
    if nk == 1:
        o_ref[...] = p
    else:
        k = pl.program_id(2)

        @pl.when(k == 0)
        def _():
            o_ref[...] = p

        @pl.when(k > 0)
        def _():
            o_ref[...] += p


def matmul(a, w, tm, tn, tk, name, li=None):
    m, kd = a.shape
    n = w.shape[-1]
    nk = kd // tk
    if li is None:
        w_spec = pl.BlockSpec((tk, tn), lambda i, j, k: (k, j))
    else:
        w_spec = pl.BlockSpec((None, tk, tn), lambda i, j, k: (li, k, j))
    return pl.pallas_call(
        functools.partial(_mm_kernel, nk=nk),
        grid=(m // tm, n // tn, nk),
        in_specs=[pl.BlockSpec((tm, tk), lambda i, j, k: (i, k)), w_spec],
        out_specs=pl.BlockSpec((tm, tn), lambda i, j, k: (i, j)),
        out_shape=jax.ShapeDtypeStruct((m, n), F32),
        compiler_params=_cparams(("parallel", "parallel", "arbitrary")),
        name=name,
    )(a, w)


def _mm_into_kernel(a_ref, w_ref, *rest):
    rest[-1][...] = jnp.dot(a_ref[...], w_ref[...], preferred_element_type=F32)


def matmul_into(a, w, buf, li, depth, tm, tn, name):
    m, kd = a.shape
    n = w.shape[1]
    in_specs = [pl.BlockSpec((tm, kd), lambda i, j: (i, 0)), pl.BlockSpec((kd, tn), lambda i, j: (0, j))]
    args = [a, w]
    aliases = {}
    if buf is not None:
        in_specs.append(pl.BlockSpec(memory_space=pl.ANY))
        args.append(buf)
        aliases = {2: 0}
    return pl.pallas_call(
        _mm_into_kernel,
        grid=(m // tm, n // tn),
        in_specs=in_specs,
        out_specs=pl.BlockSpec((None, tm, tn), lambda i, j: (li, i, j)),
        out_shape=jax.ShapeDtypeStruct((depth, m, n), F32),
        input_output_aliases=aliases,
        compiler_params=_cparams(("parallel", "parallel")),
        name=name,
    )(*args)


def _mm_groups_kernel(*refs):
    *a_refs, w_ref, o_ref, w_scr = refs
    kg = a_refs[0].shape[1]

    @pl.when(pl.program_id(1) == 0)
    def _():
        w_scr[...] = w_ref[...].astype(BF16)

    acc = jnp.dot(a_refs[0][...], w_scr[0:kg, :], preferred_element_type=F32)
    for g in range(1, len(a_refs)):
        acc = acc + jnp.dot(a_refs[g][...], w_scr[g * kg:(g + 1) * kg, :], preferred_element_type=F32)
    o_ref[...] = acc


def matmul_groups(parts, w_all, li, tm, tn, name):
    m, kg = parts[0].shape
    _, kd, n = w_all.shape
    return pl.pallas_call(
        _mm_groups_kernel,
        grid=(n // tn, m // tm),
        in_specs=[pl.BlockSpec((tm, kg), lambda j, i: (i, 0)) for _ in parts]
        + [pl.BlockSpec((None, kd, tn), lambda j, i: (li, 0, j))],
        out_specs=pl.BlockSpec((tm, tn), lambda j, i: (i, j)),
        out_shape=jax.ShapeDtypeStruct((m, n), F32),
        scratch_shapes=[pltpu.VMEM((kd, tn), BF16)],
        compiler_params=_cparams(("arbitrary", "arbitrary")),
        name=name,
    )(*parts, w_all)


def _ffn_act_kernel(g_ref, v_ref, halo_ref, st_ref, cw_ref, cb_ref, o_ref):
    g = g_ref[...]
    tail = jnp.where(pl.program_id(1) == 0, st_ref[...], halo_ref[...])
    taps = _shifted_taps(tail, g, FFN_TAPS)
    cw = cw_ref[...]
    y = taps[0] * cw[0:1]
    for i in range(1, FFN_TAPS):
        y = y + taps[i] * cw[i:i + 1]
    y = y + cb_ref[...]
    o_ref[...] = (_silu(y) * v_ref[...]).astype(o_ref.dtype)


def _ffn_up_act_kernel(x_ref, wg_ref, wv_ref, st_ref, cw_ref, cb_ref, act_ref, cst_ref, wg_scr, wv_scr, tail_scr,
                       *, tiles_per_seq, sub):
    i = pl.program_id(1)

    @pl.when(i == 0)
    def _():
        wg_scr[...] = wg_ref[...].astype(BF16)
        wv_scr[...] = wv_ref[...].astype(BF16)

    @pl.when(i % tiles_per_seq == 0)
    def _():
        tail_scr[...] = st_ref[...]

    wg = wg_scr[...]
    wv = wv_scr[...]
    cw = cw_ref[...]
    cb = cb_ref[...]
    tm = x_ref.shape[0]
    tail = tail_scr[...]

    def project(r):
        x = x_ref[r * sub:(r + 1) * sub, :]
        return jnp.dot(x, wg, preferred_element_type=F32), jnp.dot(x, wv, preferred_element_type=F32)

    nxt = project(0)
    for r in range(tm // sub):
        g, v = nxt
        if r + 1 < tm // sub:
            nxt = project(r + 1)
        taps = _shifted_taps(tail, g, FFN_TAPS)
        y = taps[0] * cw[0:1]
        for t in range(1, FFN_TAPS):
            y = y + taps[t] * cw[t:t + 1]
        act_ref[r * sub:(r + 1) * sub, :] = (_silu(y + cb) * v).astype(act_ref.dtype)
        tail = g[sub - SUBLANE:]
    tail_scr[...] = tail
    cst_ref[...] = tail


def ffn_up_act(h2, w_up_all, li, state8, conv_w, conv_b, l, tm, tn, sub):
    m, kd = h2.shape
    b = m // l
    nj = D_FF // tn
    tiles_per_seq = l // tm
    act, tails = pl.pallas_call(
        functools.partial(_ffn_up_act_kernel, tiles_per_seq=tiles_per_seq, sub=sub),
        grid=(nj, m // tm),
        in_specs=[
            pl.BlockSpec((tm, kd), lambda j, i: (i, 0)),
            pl.BlockSpec((None, kd, tn), lambda j, i: (li, 0, j)),
            pl.BlockSpec((None, kd, tn), lambda j, i: (li, 0, j + nj)),
            pl.BlockSpec((None, SUBLANE, tn), lambda j, i: (i // tiles_per_seq, 0, j)),
            pl.BlockSpec((FFN_TAPS, tn), lambda j, i: (0, j)),
            pl.BlockSpec((1, tn), lambda j, i: (0, j)),
        ],
        out_specs=[
            pl.BlockSpec((tm, tn), lambda j, i: (i, j)),
            pl.BlockSpec((None, SUBLANE, tn), lambda j, i: (i, 0, j)),
        ],
        out_shape=[jax.ShapeDtypeStruct((m, D_FF), BF16), jax.ShapeDtypeStruct((m // tm, SUBLANE, D_FF), F32)],
        scratch_shapes=[pltpu.VMEM((kd, tn), BF16), pltpu.VMEM((kd, tn), BF16), pltpu.VMEM((SUBLANE, tn), F32)],
        compiler_params=_cparams(("arbitrary", "arbitrary")),
        name="ffn_up_act",
    )(h2, w_up_all, w_up_all, state8, conv_w, conv_b.reshape(1, D_FF))
    return act, tails.reshape(b, tiles_per_seq, SUBLANE, D_FF)[:, tiles_per_seq - 1]


def ffn_act(up3, state8, conv_w, conv_b, ts, tn):
    b, l, _ = up3.shape
    nj = D_FF // tn
    hb = ts // SUBLANE
    return pl.pallas_call(
        _ffn_act_kernel,
        grid=(b, l // ts, nj),
        in_specs=[
            pl.BlockSpec((None, ts, tn), lambda bi, i, j: (bi, i, j)),
            pl.BlockSpec((None, ts, tn), lambda bi, i, j: (bi, i, j + nj)),
            pl.BlockSpec((None, SUBLANE, tn), lambda bi, i, j: (bi, jnp.maximum(i * hb - 1, 0), j)),
            pl.BlockSpec((None, SUBLANE, tn), lambda bi, i, j: (bi, 0, j)),
            pl.BlockSpec((FFN_TAPS, tn), lambda bi, i, j: (0, j)),
            pl.BlockSpec((1, tn), lambda bi, i, j: (0, j)),
        ],
        out_specs=pl.BlockSpec((None, ts, tn), lambda bi, i, j: (bi, i, j)),
        out_shape=jax.ShapeDtypeStruct((b, l, D_FF), BF16),
        compiler_params=_cparams(("parallel", "parallel", "parallel")),
        name="ffn_act",
    )(up3, up3, up3, state8, conv_w, conv_b.reshape(1, D_FF))


def _gdn_kernel(qkv_ref, z_ref, sm_ref, cbuf_ref, s0_ref, cw_ref, alog_ref, dtb_ref, nw_ref,
                o_ref, sout_ref, s_scr, tail_scr, *, c, nsub):
    ci = pl.program_id(1)

    @pl.when(ci == 0)
    def _():
        s_scr[...] = s0_ref[...]
        tail_scr[...] = cbuf_ref[...]

    x = qkv_ref[...]
    taps = _shifted_taps(tail_scr[...], x, GDN_TAPS)
    rows = nsub * c
    tail_scr[...] = x[rows - SUBLANE:]
    cw = cw_ref[...]
    y = taps[0] * cw[0:1]
    for i in range(1, GDN_TAPS):
        y = y + taps[i] * cw[i:i + 1]
    y = _silu(y)

    sm = sm_ref[...]
    beta_all = jax.nn.sigmoid(sm)
    g_all = -jnp.exp(alog_ref[...]) * jax.nn.softplus(sm + dtb_ref[...])
    ri = _iota2((rows, rows), 0)
    rj = _iota2((rows, rows), 1)
    in_chunk_tri = jnp.where((ri >= rj) & ((ri >> _log2(c)) == (rj >> _log2(c))), 1.0, 0.0).astype(F32)
    gcum_all = dot_nn(in_chunk_tri, g_all)
    rows_of = _row_getter(gcum_all, rows)

    row = _iota2((c, c), 0)
    col = _iota2((c, c), 1)
    tri = row >= col
    strict = row > col
    z = z_ref[...]
    nw = nw_ref[...]
    heads = range(GDN_HEADS)
    units = [(q, h) for q in range(nsub) for h in heads]
    ms, aqk, rhs, qd, kd, gls = [], [], [], [], [], []
    for qi, h in units:
        rs = slice(qi * c, (qi + 1) * c)
        lo = h * GDN_D
        q = y[rs, lo:lo + GDN_D]
        k = y[rs, GROUP_W + lo:GROUP_W + lo + GDN_D]
        v = y[rs, 2 * GROUP_W + lo:2 * GROUP_W + lo + GDN_D]
        q = q * lax.rsqrt(jnp.sum(q * q, axis=-1, keepdims=True) + 1e-6) * (GDN_D ** -0.5)
        k = k * lax.rsqrt(jnp.sum(k * k, axis=-1, keepdims=True) + 1e-6)
        beta = beta_all[rs, SM_GDN_B + h:SM_GDN_B + h + 1]
        gc = gcum_all[rs, SM_GDN_A + h:SM_GDN_A + h + 1]
        gam = jnp.exp(jnp.where(tri, gc - rows_of(SM_GDN_A + h)[:, qi * c:(qi + 1) * c], -jnp.inf))
        kbeta = k * beta
        mq = dot3s(jnp.concatenate([kbeta, q], axis=0), _split(k), NT)
        ms.append(jnp.where(strict, mq[:c] * gam, 0.0))
        aqk.append(mq[c:] * gam)
        eg = jnp.exp(gc)
        gl = gc[c - 1:c]
        rhs.append(jnp.concatenate([kbeta * eg, v * beta], axis=1))
        qd.append(q * eg)
        kd.append(k * jnp.exp(gl - gc))
        gls.append(gl)
    packed = _inv_unit_lower_pairs([jnp.concatenate([ms[i], ms[i + 1]], axis=1) for i in range(0, len(ms), 2)], c)
    ts = [t[:, half * c:(half + 1) * c] for t in packed for half in range(2)]
    wu = [dot3s(t, _split(x), NN) for t, x in zip(ts, rhs)]
    state = [s_scr[h] for h in heads]
    for qi in range(nsub):
        rs = slice(qi * c, (qi + 1) * c)
        ix = [qi * GDN_HEADS + h for h in heads]
        wqs = [dot3s(jnp.concatenate([wu[i][:, :GDN_D], qd[i]], axis=0), _split(state[h]), NN)
               for h, i in zip(heads, ix)]
        v_new = [wu[i][:, GDN_D:] - wqs[h][:c] for h, i in zip(heads, ix)]
        outs = [wqs[h][c:] + dot1(aqk[i], v_new[h], NN) for h, i in zip(heads, ix)]
        state = [state[h] * jnp.exp(gls[i]) + dot3(_split(kd[i]), _split(v_new[h]), TN) for h, i in zip(heads, ix)]
        for h in heads:
            lo = h * GDN_D
            o = outs[h]
            o = o * lax.rsqrt(jnp.mean(o * o, axis=-1, keepdims=True) + NORM_EPS) * nw
            o = o * _silu(z[rs, lo:lo + GDN_D])
            o_ref[rs, lo:lo + GDN_D] = o.astype(o_ref.dtype)
    for h in heads:
        s_scr[h] = state[h]

    @pl.when(ci == pl.num_programs(1) - 1)
    def _():
        sout_ref[...] = s_scr[...]


def gdn_mixer(proj3, cbuf8, s0, conv_w, a_log, dt_bias, norm_w, c):
    b, l, _ = proj3.shape
    alog_row = jnp.zeros((1, LANE), F32).at[0, SM_GDN_A:SM_GDN_A + GDN_HEADS].set(a_log)
    dtb_row = jnp.zeros((1, LANE), F32).at[0, SM_GDN_A:SM_GDN_A + GDN_HEADS].set(dt_bias)
    w3 = 3 * GROUP_W
    const2 = lambda bi, ci: (0, 0)
    nsub = 2 if l % (2 * c) == 0 else 1
    rb = nsub * c
    return pl.pallas_call(
        functools.partial(_gdn_kernel, c=c, nsub=nsub),
        grid=(b, l // rb),
        in_specs=[
            pl.BlockSpec((None, rb, w3), lambda bi, ci: (bi, ci, COL_A_QKV // w3)),
            pl.BlockSpec((None, rb, GROUP_W), lambda bi, ci: (bi, ci, COL_A_Z // GROUP_W)),
            pl.BlockSpec((None, rb, LANE), lambda bi, ci: (bi, ci, COL_SMALL // LANE)),
            pl.BlockSpec((None, SUBLANE, w3), lambda bi, ci: (bi, 0, 0)),
            pl.BlockSpec((None, GDN_HEADS, GDN_D, GDN_D), lambda bi, ci: (bi, 0, 0, 0)),
            pl.BlockSpec((GDN_TAPS, w3), const2),
            pl.BlockSpec((1, LANE), const2),
            pl.BlockSpec((1, LANE), const2),
            pl.BlockSpec((1, GDN_D), const2),
        ],
        out_specs=[
            pl.BlockSpec((None, rb, GROUP_W), lambda bi, ci: (bi, ci, 0)),
            pl.BlockSpec((None, GDN_HEADS, GDN_D, GDN_D), lambda bi, ci: (bi, 0, 0, 0)),
        ],
        out_shape=[jax.ShapeDtypeStruct((b, l, GROUP_W), BF16),
                   jax.ShapeDtypeStruct((b, GDN_HEADS, GDN_D, GDN_D), F32)],
        scratch_shapes=[pltpu.VMEM((GDN_HEADS, GDN_D, GDN_D), F32), pltpu.VMEM((SUBLANE, w3), F32)],
        compiler_params=_cparams(("parallel", "arbitrary")),
        name="gdn_mixer",
    )(proj3, proj3, proj3, cbuf8, s0, conv_w, alog_row, dtb_row, norm_w.reshape(1, GDN_D))


def _rwkv_kernel(rkv_ref, lora_ref, sh_rkv_ref, sh_lora_ref, s0_ref, mu_rkv_ref, mu_lora_ref,
                 w0_ref, w2_ref, a0_ref, a2_ref, g2_ref, kk_ref, ka_ref, rk_ref, lnw_ref, lnb_ref,
                 o_ref, sout_ref, s_scr, tail_rkv, tail_lora, *, c, nsub):
    ci = pl.program_id(1)

    @pl.when(ci == 0)
    def _():
        s_scr[...] = s0_ref[...]
        tail_rkv[...] = sh_rkv_ref[...]
        tail_lora[...] = sh_lora_ref[...]

    x = rkv_ref[...]
    xl = lora_ref[...]
    prev = _shifted_taps(tail_rkv[...], x, 2)[0]
    prev_l = _shifted_taps(tail_lora[...], xl, 2)[0]
    tail_rkv[...] = x[nsub * c - SUBLANE:]
    tail_lora[...] = xl[nsub * c - SUBLANE:]
    zm = x + (prev - x) * mu_rkv_ref[...]
    zl = xl + (prev_l - xl) * mu_lora_ref[...]
    r = zm[:, 0:GROUP_W]
    k = zm[:, GROUP_W:2 * GROUP_W]
    v = zm[:, 2 * GROUP_W:3 * GROUP_W]
    wd = zl[:, 0:RWKV_W_LORA]
    ad = zl[:, RWKV_W_LORA:RWKV_W_LORA + RWKV_A_LORA]
    gd = zl[:, RWKV_W_LORA + RWKV_A_LORA:LORA_PAD]

    w_log = -jax.nn.softplus(-(w0_ref[...] + dot3s(jnp.tanh(wd), _split(w2_ref[...]), NN))) - 0.5
    logw = -jnp.exp(w_log)
    a = jax.nn.sigmoid(a0_ref[...] + dot3s(ad, _split(a2_ref[...]), NN))
    gate = dot1(jax.nn.sigmoid(gd), g2_ref[...], NN)
    kkv = k * kk_ref[...]
    k2 = k * (1.0 + (a - 1.0) * ka_ref[...])
    rows = nsub * c
    ri = _iota2((rows, rows), 0)
    rj = _iota2((rows, rows), 1)
    lc_sh = _log2(c)
    in_chunk_tri = jnp.where((ri >= rj) & ((ri >> lc_sh) == (rj >> lc_sh)), 1.0, 0.0).astype(F32)
    lcum = dot_nn(in_chunk_tri, logw)

    n = RWKV_HS
    pw = 2 * n
    rowc = _iota2((c, 2 * c), 0)
    colc = _iota2((c, 2 * c), 1) & (c - 1)
    strict2 = rowc > colc
    tri2 = rowc >= colc
    same_head = (_iota2((pw, pw), 0) < n) == (_iota2((pw, pw), 1) < n)
    ones_bd = jnp.where(same_head, 1.0, 0.0).astype(BF16)
    rk = rk_ref[...]
    lnw = lnw_ref[...]
    lnb = lnb_ref[...]
    pairs = range(RWKV_HEADS // 2)
    sl = [slice(p * pw, (p + 1) * pw) for p in pairs]
    units = [(slice(q * c, (q + 1) * c), sl[p]) for q in range(nsub) for p in pairs]
    kk_ss = [_head_sums(jnp.square(kkv[rs, ps]), ones_bd) for rs, ps in units]
    bonus_s = [_head_sums(r[rs, ps] * k2[rs, ps] * rk[:, ps], ones_bd) for rs, ps in units]
    lhs, x_bs, x_ks, bk_end, l_last = [], [], [], [], []
    for (rs, ps), ss in zip(units, kk_ss):
        kk = kkv[rs, ps] * lax.rsqrt(ss + 1e-6)
        lc = lcum[rs, ps]
        ll = lc[c - 1:c]
        p_inv = jnp.exp(-lc)
        a_t = -kk * jnp.exp(lc - logw[rs, ps])
        b_vec = kk * a[rs, ps]
        r_t = r[rs, ps] * jnp.exp(lc)
        p_end = jnp.exp(ll - lc)
        ar = jnp.concatenate([a_t, r_t], axis=0)
        x_bs.append(dot3s(ar, _split(_blockdiag_rows(b_vec * p_inv, n)), NT))
        x_ks.append(dot3s(ar, _split(_blockdiag_rows(k2[rs, ps] * p_inv, n)), NT))
        lhs.append(ar)
        bk_end.append(jnp.concatenate([b_vec * p_end, k2[rs, ps] * p_end], axis=0))
        l_last.append(ll)
    ms = [jnp.where(strict2, -x[:c], 0.0) for x in x_bs]
    av = [dot3s(jnp.where(strict2, x[:c], 0.0), _split(_blockdiag_rows(v[rs, ps], n)), NN)
          for x, (rs, ps) in zip(x_ks, units)]
    rbk = [jnp.concatenate([jnp.where(tri2, xb[c:], 0.0), jnp.where(tri2, xk[c:], 0.0)], axis=1)
           for xb, xk in zip(x_bs, x_ks)]
    ts = _inv_unit_lower_pairs(ms, c)
    state = [s_scr[p] for p in pairs]
    npair = len(pairs)
    for q in range(nsub):
        rs = slice(q * c, (q + 1) * c)
        ix = [q * npair + p for p in pairs]
        ars = [dot3s(lhs[i], _split(state[p]), NT) for p, i in zip(pairs, ix)]
        us = [dot3s(ts[i], _split(_blockdiag_rows(x[:c] + av[i], n)), NN) for x, i in zip(ars, ix)]
        vs = [v[rs, sl[p]] for p in pairs]
        ys = [ars[p][c:] + _dot(rbk[i].astype(BF16),
                                jnp.concatenate([_blockdiag_rows(us[p], n), _blockdiag_rows(vs[p], n)],
                                                axis=0).astype(BF16), NN, None)
              for p, i in zip(pairs, ix)]
        upd = [dot3(_split(jnp.concatenate([us[p], vs[p]], axis=0)), _split(bk_end[i]), TN)
               for p, i in zip(pairs, ix)]
        state = [jnp.where(same_head, state[p] * jnp.exp(l_last[i]) + upd[p], 0.0) for p, i in zip(pairs, ix)]
        devs = [y - _head_sums(y, ones_bd) * (1.0 / n) for y in ys]
        var = [_head_sums(jnp.square(d), ones_bd) * (1.0 / n) for d in devs]
        for p, i in zip(pairs, ix):
            yn = devs[p] * lax.rsqrt(var[p] + RWKV_GN_EPS) * lnw[:, sl[p]] + lnb[:, sl[p]]
            o_ref[rs, sl[p]] = ((yn + bonus_s[i] * vs[p]) * gate[rs, sl[p]]).astype(o_ref.dtype)
    for p in pairs:
        s_scr[p] = state[p]

    @pl.when(ci == pl.num_programs(1) - 1)
    def _():
        sout_ref[...] = s_scr[...]


def rwkv_mixer(proj3, sh_rkv8, sh_lora8, s0, prm, c):
    b, l, _ = proj3.shape
    nsub = 2 if l % (2 * c) == 0 else 1
    rb = nsub * c
    w3 = 3 * GROUP_W
    mu = prm['rwkv_mu']
    mu_rkv = mu[:w3].reshape(1, w3)
    mu_lora = jnp.pad(mu[w3:], (0, LORA_PAD - RWKV_LORA)).reshape(1, LORA_PAD)
    g2 = jnp.pad(prm['rwkv_g2'], ((0, G_LORA_PAD - RWKV_G_LORA), (0, 0)))
    row = lambda t: t.reshape(1, GROUP_W)
    const2 = lambda bi, ci: (0, 0)
    vec = pl.BlockSpec((1, GROUP_W), const2)
    n, npair = RWKV_HS, RWKV_HEADS // 2
    sp = s0.reshape(b, npair, 2, n, n)
    zero = jnp.zeros((b, npair, n, n), F32)
    s_pairs = jnp.concatenate([jnp.concatenate([sp[:, :, 0], zero], axis=-1),
                               jnp.concatenate([zero, sp[:, :, 1]], axis=-1)], axis=-2)
    state_spec = pl.BlockSpec((None, npair, 2 * n, 2 * n), lambda bi, ci: (bi, 0, 0, 0))
    o, s_new = pl.pallas_call(
        functools.partial(_rwkv_kernel, c=c, nsub=nsub),
        grid=(b, l // rb),
        in_specs=[
            pl.BlockSpec((None, rb, w3), lambda bi, ci: (bi, ci, COL_B_RKV // w3)),
            pl.BlockSpec((None, rb, LORA_PAD), lambda bi, ci: (bi, ci, COL_B_LORA // LORA_PAD)),
            pl.BlockSpec((None, SUBLANE, w3), lambda bi, ci: (bi, 0, 0)),
            pl.BlockSpec((None, SUBLANE, LORA_PAD), lambda bi, ci: (bi, 0, 0)),
            state_spec,
            pl.BlockSpec((1, w3), const2),
            pl.BlockSpec((1, LORA_PAD), const2),
            vec,
            pl.BlockSpec((RWKV_W_LORA, GROUP_W), const2),
            vec,
            pl.BlockSpec((RWKV_A_LORA, GROUP_W), const2),
            pl.BlockSpec((G_LORA_PAD, GROUP_W), const2),
            vec, vec, vec, vec, vec,
        ],
        out_specs=[
            pl.BlockSpec((None, rb, GROUP_W), lambda bi, ci: (bi, ci, 0)),
            state_spec,
        ],
        out_shape=[jax.ShapeDtypeStruct((b, l, GROUP_W), BF16),
                   jax.ShapeDtypeStruct((b, npair, 2 * n, 2 * n), F32)],
        scratch_shapes=[pltpu.VMEM((npair, 2 * n, 2 * n), F32),
                        pltpu.VMEM((SUBLANE, w3), F32), pltpu.VMEM((SUBLANE, LORA_PAD), F32)],
        compiler_params=_cparams(("parallel", "arbitrary")),
        name="rwkv_mixer",
    )(proj3, proj3, sh_rkv8, sh_lora8, s_pairs, mu_rkv, mu_lora,
      row(prm['rwkv_w0']), prm['rwkv_w2'], row(prm['rwkv_a0']), prm['rwkv_a2'], g2,
      row(prm['rwkv_k_k']), row(prm['rwkv_k_a']), row(prm['rwkv_r_k']),
      row(prm['rwkv_ln_w']), row(prm['rwkv_ln_b']))
    s_heads = jnp.stack([s_new[:, :, :n, :n], s_new[:, :, n:, n:]], axis=2)
    return o, s_heads.reshape(b, RWKV_HEADS, n, n)


def _ssd_kernel(z_ref, xbc_ref, sm_ref, cbuf_ref, s0_ref, cw_ref, cb_ref, dtb_ref, alog_ref, dsk_ref, nw_ref,
                o_ref, sout_ref, s_scr, tail_scr, y_scr, *, c):
    ci = pl.program_id(1)

    @pl.when(ci == 0)
    def _():
        s_scr[...] = s0_ref[...]
        tail_scr[...] = cbuf_ref[...]

    x = xbc_ref[...]
    taps = _shifted_taps(tail_scr[...], x, SSM_TAPS)
    tail_scr[...] = x[c - SUBLANE:]
    cw = cw_ref[...]
    y = taps[0] * cw[0:1]
    for i in range(1, SSM_TAPS):
        y = y + taps[i] * cw[i:i + 1]
    y = _silu(y + cb_ref[...])
    xs = y[:, 0:GROUP_W]
    gn = SSM_GROUPS * SSM_N

    sm = sm_ref[...]
    dt_all = jax.nn.softplus(sm + dtb_ref[...])
    da_all = dt_all * (-jnp.exp(alog_ref[...]))
    acs_all = _cumsum_rows(da_all, c)

    row = _iota2((c, c), 0)
    col = _iota2((c, c), 1)
    tri = row >= col
    z = z_ref[...]
    dsk = dsk_ref[...]
    hpg = SSM_HEADS // SSM_GROUPS
    gp = hpg * SSM_P
    rows_of = _row_getter(acs_all, c)
    for g in range(SSM_GROUPS):
        bm = y[:, GROUP_W + g * SSM_N:GROUP_W + (g + 1) * SSM_N]
        cm = y[:, GROUP_W + gn + g * SSM_N:GROUP_W + gn + (g + 1) * SSM_N]
        cb = dot1(cm, bm, NT)
        sg = s_scr[g]
        y_off = dot1(cm, sg, NT)
        xdec = []
        for rr in range(hpg):
            h = g * hpg + rr
            lo = h * SSM_P
            lane = SM_SSM_DT + h
            xs_h = xs[:, lo:lo + SSM_P]
            dt = dt_all[:, lane:lane + 1]
            acs = acs_all[:, lane:lane + 1]
            lmat = jnp.exp(jnp.where(tri, acs - rows_of(lane), -jnp.inf))
            xd = xs_h * dt
            a_last = acs[c - 1:c]
            xdec.append(xd * jnp.exp(a_last - acs))
            yh = dot1(cb * lmat, xd, NN) + y_off[:, rr * SSM_P:(rr + 1) * SSM_P] * jnp.exp(acs)
            yh = yh + xs_h * dsk[:, lane:lane + 1]
            y_scr[:, lo:lo + SSM_P] = yh * _silu(z[:, lo:lo + SSM_P])
        upd = dot3(_split(jnp.concatenate(xdec, axis=1)), _split(bm), TN)
        for rr in range(hpg):
            lane = SM_SSM_DT + g * hpg + rr
            dec = jnp.exp(acs_all[c - 1:c, lane:lane + 1])
            s_scr[g, rr * SSM_P:(rr + 1) * SSM_P, :] = (sg[rr * SSM_P:(rr + 1) * SSM_P] * dec
                                                         + upd[rr * SSM_P:(rr + 1) * SSM_P])

    gw = GROUP_W // SSM_GROUPS
    nw = nw_ref[...]
    for g in range(SSM_GROUPS):
        yg = y_scr[:, g * gw:(g + 1) * gw]
        yg = yg * lax.rsqrt(jnp.mean(yg * yg, axis=-1, keepdims=True) + NORM_EPS)
        o_ref[:, g * gw:(g + 1) * gw] = (yg * nw[:, g * gw:(g + 1) * gw]).astype(o_ref.dtype)

    @pl.when(ci == pl.num_programs(1) - 1)
    def _():
        sout_ref[...] = s_scr[...]


def ssd_mixer(proj3, cbuf8, s0, prm, c):
    b, l, _ = proj3.shape

    def lanes(t):
        return jnp.zeros((1, LANE), F32).at[0, SM_SSM_DT:SM_SSM_DT + SSM_HEADS].set(t)

    const2 = lambda bi, ci: (0, 0)
    small = pl.BlockSpec((1, LANE), const2)
    gshape = (SSM_GROUPS, SSM_HEADS // SSM_GROUPS * SSM_P, SSM_N)
    state_spec = pl.BlockSpec((None,) + gshape, lambda bi, ci: (bi, 0, 0, 0))
    o, s_new = pl.pallas_call(
        functools.partial(_ssd_kernel, c=c),
        grid=(b, l // c),
        in_specs=[
            pl.BlockSpec((None, c, GROUP_W), lambda bi, ci: (bi, ci, COL_C_Z // GROUP_W)),
            pl.BlockSpec((None, c, SSM_XBC), lambda bi, ci: (bi, ci, COL_C_XBC // SSM_XBC)),
            pl.BlockSpec((None, c, LANE), lambda bi, ci: (bi, ci, COL_SMALL // LANE)),
            pl.BlockSpec((None, SUBLANE, SSM_XBC), lambda bi, ci: (bi, 0, 0)),
            state_spec,
            pl.BlockSpec((SSM_TAPS, SSM_XBC), const2),
            pl.BlockSpec((1, SSM_XBC), const2),
            small, small, small,
            pl.BlockSpec((1, GROUP_W), const2),
        ],
        out_specs=[
            pl.BlockSpec((None, c, GROUP_W), lambda bi, ci: (bi, ci, 0)),
            state_spec,
        ],
        out_shape=[jax.ShapeDtypeStruct((b, l, GROUP_W), BF16),
                   jax.ShapeDtypeStruct((b,) + gshape, F32)],
        scratch_shapes=[pltpu.VMEM(gshape, F32), pltpu.VMEM((SUBLANE, SSM_XBC), F32),
                        pltpu.VMEM((c, GROUP_W), F32)],
        compiler_params=_cparams(("parallel", "arbitrary")),
        name="ssd_mixer",
    )(proj3, proj3, proj3, cbuf8, s0.reshape((b,) + gshape), prm['ssm_conv_w'],
      prm['ssm_conv_b'].reshape(1, SSM_XBC),
      lanes(prm['ssm_dt_bias']), lanes(prm['ssm_A_log']), lanes(prm['ssm_D']),
      prm['ssm_norm_w'].reshape(1, GROUP_W))
    return o, s_new.reshape(b, SSM_HEADS, SSM_P, SSM_N)


def _swa_weight(d):
    mult = jnp.zeros(d.shape, F32)
    for window, dil in SWA_PATTERNS:
        ok = (d >= 0) & (d <= window) & ((d & (dil - 1)) == 0)
        mult = mult + jnp.where(ok, 1.0, 0.0)
    return mult


def _swa_scores(q, k, d, slope):
    s = dot_nt(q.astype(BF16), k.astype(BF16), None) * (SWA_HD ** -0.5)
    mult = _swa_weight(d)
    s = s - slope * d.astype(F32)
    return jnp.where(mult > 0.0, s, NEG_INF), mult


def _swa_prompt_kernel(slopes_ref, q_ref, k_ref, v_ref, lw_ref, o_ref, m_scr, l_scr, acc_scr, *, t, sub):
    h = pl.program_id(1)
    qi = pl.program_id(2)
    ki = pl.program_id(3)

    @pl.when(ki == 0)
    def _():
        m_scr[...] = jnp.full(m_scr.shape, NEG_INF, F32)
        l_scr[...] = jnp.zeros(l_scr.shape, F32)
        acc_scr[...] = jnp.zeros(acc_scr.shape, F32)

    @pl.when(ki <= qi)
    def _():
        col = (ki * t + _iota2((1, t), 1)).astype(F32) * slopes_ref[h]
        kb = k_ref[...].astype(BF16)
        vb = v_ref[...].astype(BF16)
        def qk(r):
            q = (q_ref[r:r + sub, :] * (SWA_HD ** -0.5)).astype(BF16)
            return _dot(q, kb, NT, None)

        nxt = qk(0)
        for r in range(0, t, sub):
            rs = slice(r, r + sub)
            s = nxt + (lw_ref[qi - ki, rs, :] + col)
            if r + sub < t:
                nxt = qk(r + sub)
            m_old = m_scr[rs, :]
            m_new = jnp.maximum(m_old, jnp.max(s, axis=-1, keepdims=True))
            alpha = jnp.exp(m_old - m_new)
            p = jnp.exp(s - m_new)
            l_scr[rs, :] = alpha * l_scr[rs, :] + jnp.sum(p, axis=-1, keepdims=True)
            acc_scr[rs, :] = alpha * acc_scr[rs, :] + _dot(p.astype(BF16), vb, NN, None)
            m_scr[rs, :] = m_new

    @pl.when(ki == pl.num_programs(3) - 1)
    def _():
        o_ref[...] = (acc_scr[...] / l_scr[...]).astype(o_ref.dtype)


def _alibi_slopes():
    return jnp.asarray([2.0 ** (-8.0 * (i + 1) / SWA_HEADS) for i in range(SWA_HEADS)], F32)


def _swa_log_weight_tiles(n, t):
    d = (jnp.arange(n)[:, None, None] * t + jnp.arange(t)[None, :, None]) - jnp.arange(t)[None, None, :]
    mult = _swa_weight(d.astype(jnp.int32))
    return jnp.where(mult > 0.0, jnp.log(jnp.maximum(mult, 1.0)), NEG_INF)


def swa_prompt(proj3, k4, v4, li, t):
    b, l, _ = proj3.shape
    qc = COL_D_Q // SWA_HD
    n = l // t
    kv_spec = pl.BlockSpec((None, None, t, SWA_HD), lambda bi, h, qi, ki: (li, bi, jnp.minimum(ki, qi), h))

    return pl.pallas_call(
        functools.partial(_swa_prompt_kernel, t=t, sub=min(t, 256)),
        grid=(b, SWA_HEADS, n, n),
        in_specs=[
            pl.BlockSpec(memory_space=pltpu.SMEM),
            pl.BlockSpec((None, t, SWA_HD), lambda bi, h, qi, ki: (bi, qi, qc + h)),
            kv_spec,
            kv_spec,
            pl.BlockSpec((n, t, t), lambda bi, h, qi, ki: (0, 0, 0)),
        ],
        out_specs=pl.BlockSpec((None, t, SWA_HD), lambda bi, h, qi, ki: (bi, qi, h)),
        out_shape=jax.ShapeDtypeStruct((b, l, GROUP_W), BF16),
        scratch_shapes=[pltpu.VMEM((t, 1), F32), pltpu.VMEM((t, 1), F32), pltpu.VMEM((t, SWA_HD), F32)],
        compiler_params=_cparams(("parallel", "parallel", "parallel", "arbitrary")),
        name="swa_prompt",
    )(_alibi_slopes(), proj3, k4, v4, _swa_log_weight_tiles(n, t))


def _swa_sample_kernel(slopes_ref, q_ref, k_ref, v_ref, ck_ref, cv_ref, o_ref, *, t, wb):
    d_c = (wb + _iota2((t, wb), 0)) - _iota2((t, wb), 1)
    d_n = _iota2((t, t), 0) - _iota2((t, t), 1)
    for h in range(SWA_HEADS):
        hs = slice(h * SWA_HD, (h + 1) * SWA_HD)
        ck = ck_ref[pl.ds(h, wb, stride=SWA_HEADS), :]
        cv = cv_ref[pl.ds(h, wb, stride=SWA_HEADS), :]
        q = q_ref[:, hs]
        s_c, mult_c = _swa_scores(q, ck, d_c, slopes_ref[h])
        s_n, mult_n = _swa_scores(q, k_ref[:, hs], d_n, slopes_ref[h])
        m = jnp.maximum(jnp.max(s_c, axis=-1, keepdims=True), jnp.max(s_n, axis=-1, keepdims=True))
        p_c = jnp.exp(s_c - m) * mult_c
        p_n = jnp.exp(s_n - m) * mult_n
        den = jnp.sum(p_c, axis=-1, keepdims=True) + jnp.sum(p_n, axis=-1, keepdims=True)
        num = (dot_nn(p_c.astype(BF16), cv.astype(BF16), None)
               + dot_nn(p_n.astype(BF16), v_ref[:, hs].astype(BF16), None))
        o_ref[:, hs] = (num / den).astype(o_ref.dtype)


def swa_sample(proj3, k4, v4, li, cache_k_all, cache_v_all):
    b, t, _ = proj3.shape
    depth, _, wb = cache_k_all.shape[:3]
    ck = cache_k_all.reshape(depth, b, wb * SWA_HEADS, SWA_HD)
    cv = cache_v_all.reshape(depth, b, wb * SWA_HEADS, SWA_HD)
    new_spec = pl.BlockSpec((None, None, t, GROUP_W), lambda bi: (li, bi, 0, 0))
    cache_spec = pl.BlockSpec((None, None, wb * SWA_HEADS, SWA_HD), lambda bi: (li, bi, 0, 0))
    return pl.pallas_call(
        functools.partial(_swa_sample_kernel, t=t, wb=wb),
        grid=(b,),
        in_specs=[
            pl.BlockSpec(memory_space=pltpu.SMEM),
            pl.BlockSpec((None, t, GROUP_W), lambda bi: (bi, 0, COL_D_Q // GROUP_W)),
            new_spec,
            new_spec,
            cache_spec,
            cache_spec,
        ],
        out_specs=pl.BlockSpec((None, t, GROUP_W), lambda bi: (bi, 0, 0)),
        out_shape=jax.ShapeDtypeStruct((b, t, GROUP_W), BF16),
        compiler_params=_cparams(("parallel",)),
        name="swa_sample",
    )(_alibi_slopes(), proj3, k4, v4, ck, cv)


def _tiles(m):
    return (256, 1024) if m >= 1024 else (m, m)


def _front_pad_rows(t, rows=SUBLANE):
    return jnp.pad(t, ((0, 0), (rows - t.shape[1], 0), (0, 0)))


def prep_weights(p):
    w = p['w_in']
    o = 0
    seg = {}
    for name, n in (('a_qkv', 3 * GROUP_W), ('a_z', GROUP_W), ('a_b', GDN_HEADS), ('a_a', GDN_HEADS),
                    ('b_rkv', 3 * GROUP_W), ('b_lora', RWKV_LORA), ('c_z', GROUP_W), ('c_xbc', SSM_XBC),
                    ('c_dt', SSM_HEADS), ('d_q', GROUP_W), ('d_k', GROUP_W), ('d_v', GROUP_W)):
        seg[name] = w[:, o:o + n].astype(BF16)
        o += n
    zeros = lambda n: jnp.zeros((D_MODEL, n), BF16)
    small_used = 2 * GDN_HEADS + SSM_HEADS
    w_in = jnp.concatenate([
        seg['a_qkv'], seg['b_rkv'], seg['d_q'], seg['a_z'], seg['c_z'], seg['c_xbc'],
        seg['b_lora'], zeros(LORA_PAD - RWKV_LORA),
        seg['a_b'], seg['a_a'], seg['c_dt'], zeros(LANE - small_used)], axis=1)
    return {'w_in': w_in, 'w_k': seg['d_k'], 'w_v': seg['d_v']}


def decoder_layer(x, prm, wts, past, swa_cache, chunks, li, depth, kv=(None, None), h=None, next_pre_w=None):
    b, l, _ = x.shape
    m = b * l
    tr, tm = _tiles(m)
    x2 = x.reshape(m, D_MODEL)
    c_gdn, c_rwkv, c_ssd = chunks

    if h is None:
        h = rms_cast(x2, prm['norm_mix_pre'], tr)
    proj = matmul(h, wts['w_in'], tm, 512, D_MODEL, "mm_in")
    proj3 = proj.reshape(b, l, N_PROJ)
    kbuf = matmul_into(h, wts['w_k'], kv[0], li, depth, tm, 512, "mm_k")
    vbuf = matmul_into(h, wts['w_v'], kv[1], li, depth, tm, 512, "mm_v")
    k4 = kbuf.reshape(depth, b, l, GROUP_W)
    v4 = vbuf.reshape(depth, b, l, GROUP_W)

    o_a, gdn_s = gdn_mixer(proj3, _front_pad_rows(past['gdn_conv']), past['gdn'], prm['gdn_conv_w'],
                           prm['gdn_A_log'], prm['gdn_dt_bias'], prm['gdn_norm_w'], c_gdn)
    shift = past['rwkv_shift'][:, None, :]
    sh_rkv8 = _front_pad_rows(shift[:, :, :3 * GROUP_W])
    sh_lora8 = _front_pad_rows(jnp.pad(shift[:, :, 3 * GROUP_W:], ((0, 0), (0, 0), (0, LORA_PAD - RWKV_LORA))))
    o_b, rwkv_s = rwkv_mixer(proj3, sh_rkv8, sh_lora8, past['rwkv'], prm, c_rwkv)
    o_c, ssm_s = ssd_mixer(proj3, _front_pad_rows(past['ssm_conv']), past['ssm'], prm, c_ssd)
    if swa_cache is None:
        o_d = swa_prompt(proj3, k4, v4, li, min(l, SWA_TILE))
    else:
        o_d = swa_sample(proj3, k4, v4, li, swa_cache[0], swa_cache[1])

    gdn_conv = proj3[:, l - (GDN_TAPS - 1):, COL_A_QKV:COL_A_QKV + 3 * GROUP_W]
    ssm_conv = proj3[:, l - (SSM_TAPS - 1):, COL_C_XBC:COL_C_XBC + SSM_XBC]
    rwkv_shift = jnp.concatenate([proj3[:, l - 1, COL_B_RKV:COL_B_RKV + 3 * GROUP_W],
                                  proj3[:, l - 1, COL_B_LORA:COL_B_LORA + RWKV_LORA]], axis=-1)

    y = matmul_groups([o.reshape(m, GROUP_W) for o in (o_a, o_b, o_c, o_d)], wts['w_out_all'], li, tm, 512,
                      "mm_out")
    x2, h2 = add_rms_cast(x2, y, prm['norm_mix_post'], prm['norm_ffn_pre'], tr)
    state8 = _front_pad_rows(past['ffn_conv'])
    if l % tm == 0:
        act, cst = ffn_up_act(h2, wts['w_up_all'], li, state8, prm['ffn_conv_w'], prm['ffn_conv_b'], l, tm, 256,
                              256)
        ffn_conv = cst[:, SUBLANE - (FFN_TAPS - 1):, :]
    else:
        up3 = matmul(h2, wts['w_up_all'], tm, 512, D_MODEL, "mm_up", li).reshape(b, l, 2 * D_FF)
        act = ffn_act(up3, state8, prm['ffn_conv_w'], prm['ffn_conv_b'], l, D_FF // 2).reshape(m, D_FF)
        ffn_conv = up3[:, l - (FFN_TAPS - 1):, :D_FF]
    y2 = matmul(act, wts['w_down_all'], tm, 512, D_FF // 2, "mm_down", li)
    if next_pre_w is None:
        x2, h_next = add_rms(x2, y2, prm['norm_ffn_post'], tr), None
    else:
        x2, h_next = add_rms_cast(x2, y2, prm['norm_ffn_post'], next_pre_w, tr)
    return x2.reshape(b, l, D_MODEL), h_next, (kbuf, vbuf), (gdn_s, gdn_conv, rwkv_s, rwkv_shift, ssm_s, ssm_conv,
                                                             ffn_conv)


def _zero_past(bsz):
    return {
        'gdn': jnp.zeros((bsz, GDN_HEADS, GDN_D, GDN_D), F32),
        'gdn_conv': jnp.zeros((bsz, GDN_TAPS - 1, 3 * GROUP_W), F32),
        'rwkv': jnp.zeros((bsz, RWKV_HEADS, RWKV_HS, RWKV_HS), F32),
        'rwkv_shift': jnp.zeros((bsz, 3 * GROUP_W + RWKV_LORA), F32),
        'ssm': jnp.zeros((bsz, SSM_HEADS, SSM_P, SSM_N), F32),
        'ssm_conv': jnp.zeros((bsz, SSM_TAPS - 1, SSM_XBC), F32),
        'ffn_conv': jnp.zeros((bsz, FFN_TAPS - 1, D_FF), F32),
    }


PARAM_NAMES = ('norm_mix_pre', 'norm_mix_post', 'norm_ffn_pre', 'norm_ffn_post', 'w_in', 'w_out', 'gdn_conv_w',
               'gdn_A_log', 'gdn_dt_bias', 'gdn_norm_w', 'rwkv_mu', 'rwkv_w0', 'rwkv_w2', 'rwkv_a0', 'rwkv_a2',
               'rwkv_g2', 'rwkv_k_k', 'rwkv_k_a', 'rwkv_r_k', 'rwkv_ln_w', 'rwkv_ln_b', 'ssm_conv_w', 'ssm_conv_b',
               'ssm_dt_bias', 'ssm_A_log', 'ssm_D', 'ssm_norm_w', 'ffn_w_up', 'ffn_conv_w', 'ffn_conv_b',
               'ffn_w_down')


def kernel(x_prompt, x_sample, state_gdn, state_gdn_conv, state_rwkv, state_rwkv_shift, state_ssm, state_ssm_conv, cache_swa_k, cache_swa_v, state_ffn_conv, norm_mix_pre, norm_mix_post, norm_ffn_pre, norm_ffn_post, w_in, w_out, gdn_conv_w, gdn_A_log, gdn_dt_bias, gdn_norm_w, rwkv_mu, rwkv_w0, rwkv_w2, rwkv_a0, rwkv_a2, rwkv_g2, rwkv_k_k, rwkv_k_a, rwkv_r_k, rwkv_ln_w, rwkv_ln_b, ssm_conv_w, ssm_conv_b, ssm_dt_bias, ssm_A_log, ssm_D, ssm_norm_w, ffn_w_up, ffn_conv_w, ffn_conv_b, ffn_w_down):
    params = dict(zip(PARAM_NAMES, (norm_mix_pre, norm_mix_post, norm_ffn_pre, norm_ffn_post, w_in, w_out,
                                    gdn_conv_w, gdn_A_log, gdn_dt_bias, gdn_norm_w, rwkv_mu, rwkv_w0, rwkv_w2,
                                    rwkv_a0, rwkv_a2, rwkv_g2, rwkv_k_k, rwkv_k_a, rwkv_r_k, rwkv_ln_w, rwkv_ln_b,
                                    ssm_conv_w, ssm_conv_b, ssm_dt_bias, ssm_A_log, ssm_D, ssm_norm_w, ffn_w_up,
                                    ffn_conv_w, ffn_conv_b, ffn_w_down)))
    depth = w_in.shape[0]
    xp, xs = x_prompt, x_sample
    t_dec = x_sample.shape[1]
    prompt_states, sample_states = [], []
    hp = hs = None
    kvp = kvs = (None, None)
    stacked = {'w_out_all': w_out, 'w_up_all': ffn_w_up, 'w_down_all': ffn_w_down.astype(BF16)}
    for li in range(depth):
        prm = {k: v[li] for k, v in params.items()}
        wts = {**prep_weights(prm), **stacked}
        nxt = norm_mix_pre[li + 1] if li + 1 < depth else None
        xp, hp, kvp, stp = decoder_layer(xp, prm, wts, _zero_past(xp.shape[0]), None, (64, 64, 128),
                                         li, depth, kvp, hp, nxt)
        past = {'gdn': state_gdn[li], 'gdn_conv': state_gdn_conv[li], 'rwkv': state_rwkv[li],
                'rwkv_shift': state_rwkv_shift[li], 'ssm': state_ssm[li], 'ssm_conv': state_ssm_conv[li],
                'ffn_conv': state_ffn_conv[li]}
        xs, hs, kvs, sts = decoder_layer(xs, prm, wts, past, (cache_swa_k, cache_swa_v),
                                         (t_dec, t_dec, t_dec), li, depth, kvs, hs, nxt)
        prompt_states.append(stp)
        sample_states.append(sts)

    def window_rows(buf, x):
        bsz, l = x.shape[0], x.shape[1]
        rows = buf.reshape(depth, bsz, l, SWA_HEADS, SWA_HD)
        return rows[:, :, max(l - SWA_MAX_WINDOW, 0):]

    def outputs(states, kv, x):
        st = [jnp.stack(t) for t in zip(*states)]
        return (*st[:6], window_rows(kv[0], x), window_rows(kv[1], x), st[6])

    return (xp, xs, *outputs(prompt_states, kvp, x_prompt), *outputs(sample_states, kvs, x_sample))
```

```python
import functools

import jax
import jax.numpy as jnp
from jax import lax
from jax.experimental import pallas as pl
from jax.experimental.pallas import tpu as pltpu

F32 = jnp.float32
BF16 = jnp.bfloat16
HI = lax.Precision.HIGHEST

D_MODEL = 4096
GROUP_W = D_MODEL // 4
GDN_HEADS = 8
GDN_D = GROUP_W // GDN_HEADS
GDN_TAPS = 4
RWKV_HS = 64
RWKV_HEADS = GROUP_W // RWKV_HS
RWKV_W_LORA = 64
RWKV_A_LORA = 64
RWKV_G_LORA = 160
RWKV_LORA = RWKV_W_LORA + RWKV_A_LORA + RWKV_G_LORA
RWKV_GN_EPS = 64e-5
SSM_P = 64
SSM_HEADS = GROUP_W // SSM_P
SSM_GROUPS = 2
SSM_N = 128
SSM_TAPS = 4
SSM_XBC = GROUP_W + 2 * SSM_GROUPS * SSM_N
SWA_HEADS = 8
SWA_HD = GROUP_W // SWA_HEADS
SWA_PATTERNS = ((128, 1), (512, 4), (2048, 16))
SWA_MAX_WINDOW = 2048
D_FF = 256 * ((8 * D_MODEL // 3 + 255) // 256)
FFN_TAPS = 3
NORM_EPS = 1e-6
NEG_INF = -1e30

LANE = 128
SUBLANE = 8
LORA_PAD = 384
G_LORA_PAD = LORA_PAD - RWKV_W_LORA - RWKV_A_LORA

COL_A_QKV = 0
COL_B_RKV = 3 * GROUP_W
COL_D_Q = 6 * GROUP_W
COL_A_Z = 7 * GROUP_W
COL_C_Z = 8 * GROUP_W
COL_C_XBC = 9 * GROUP_W
COL_B_LORA = COL_C_XBC + SSM_XBC
COL_SMALL = COL_B_LORA + LORA_PAD
N_PROJ = COL_SMALL + LANE
SM_GDN_B = 0
SM_GDN_A = GDN_HEADS
SM_SSM_DT = 2 * GDN_HEADS

VMEM_LIMIT = 56 * 1024 * 1024
SWA_TILE = 1024


def _cparams(sem):
    return pltpu.CompilerParams(dimension_semantics=sem, vmem_limit_bytes=VMEM_LIMIT)


def _dot(a, b, dims, prec):
    return lax.dot_general(a, b, (dims, ((), ())), precision=prec, preferred_element_type=F32)


def dot_nn(a, b, prec=HI):
    return _dot(a, b, ((1,), (0,)), prec)


def dot_nt(a, b, prec=HI):
    return _dot(a, b, ((1,), (1,)), prec)


def dot_tn(a, b, prec=HI):
    return _dot(a, b, ((0,), (0,)), prec)


def _silu(x):
    return x * jax.nn.sigmoid(x)


def _iota2(shape, axis):
    return lax.broadcasted_iota(jnp.int32, shape, axis)


def _log2(n):
    s = n.bit_length() - 1
    assert 1 << s == n
    return s


NN = ((1,), (0,))
NT = ((1,), (1,))
TN = ((0,), (0,))


def _split(x):
    hi = x.astype(BF16)
    return hi, (x - hi.astype(F32)).astype(BF16)


def _split_rows(x):
    hi = x.astype(BF16)
    hif = hi.astype(F32)
    return jnp.concatenate([hif, x - hif], axis=0).astype(BF16), hi


def dot3(ap, bp, dims):
    return (_dot(ap[0], bp[0], dims, None) + _dot(ap[0], bp[1], dims, None)
            + _dot(ap[1], bp[0], dims, None))


def dot3s(a, bp, dims):
    r = a.shape[0]
    stacked, hi = _split_rows(a)
    both = _dot(stacked, bp[0], dims, None)
    return both[:r] + both[r:] + _dot(hi, bp[1], dims, None)


def dot1(a, b, dims):
    return _dot(a.astype(BF16), b.astype(BF16), dims, None)


def _inv_unit_lower_multi(ms, c):
    row = _iota2((c, c), 0)
    col = _iota2((c, c), 1)
    eye = jnp.where(row == col, 1.0, 0.0).astype(F32)
    base = min(SUBLANE, c)
    sb = _log2(base)
    blk = (row >> sb) == (col >> sb)
    ps = [jnp.where(blk, -m, 0.0) for m in ms]
    ts = [eye + p for p in ps]
    if sb > 1:
        ps = [_dot(p.astype(BF16), p.astype(BF16), NN, None) for p in ps]
        for _ in range(sb - 2):
            both = [_dot(jnp.concatenate([t, p], axis=0).astype(BF16), p.astype(BF16), NN, None)
                    for t, p in zip(ts, ps)]
            ts = [t + x[:c] for t, x in zip(ts, both)]
            ps = [x[c:] for x in both]
        ts = [t + _dot(t.astype(BF16), p.astype(BF16), NN, None) for t, p in zip(ts, ps)]
    s = base
    while s < c:
        ls = _log2(s)
        off = ((row >> (ls + 1)) == (col >> (ls + 1))) & ((row >> ls) > (col >> ls))
        tbs = [t.astype(BF16) for t in ts]
        inner = [_dot(jnp.where(off, m, 0.0).astype(BF16), tb, NN, None) for m, tb in zip(ms, tbs)]
        ts = [t - _dot(tb, x.astype(BF16), NN, None) for t, tb, x in zip(ts, tbs, inner)]
        s *= 2
    res = [eye - t - dot3s(m, _split(t), NN) for m, t in zip(ms, ts)]
    return [t + _dot(t.astype(BF16), r.astype(BF16), NN, None) for t, r in zip(ts, res)]


def _blockdiag_rows(x, half):
    left = _iota2(x.shape, 1) < half
    return jnp.concatenate([jnp.where(left, x, 0.0), jnp.where(left, 0.0, x)], axis=0)


def _head_sums(x, ones_bd):
    r = x.shape[0]
    stacked, _ = _split_rows(x)
    both = _dot(stacked, ones_bd, NN, None)
    return both[:r] + both[r:]


def _inv_unit_lower_pairs(ms, c):
    row = _iota2((c, 2 * c), 0)
    col = _iota2((c, 2 * c), 1) & (c - 1)
    eye = jnp.where(row == col, 1.0, 0.0).astype(F32)
    base = min(SUBLANE, c)
    sb = _log2(base)
    blk = (row >> sb) == (col >> sb)

    def mm(a, b):
        return _dot(a.astype(BF16), _blockdiag_rows(b, c).astype(BF16), NN, None)

    ps = [jnp.where(blk, -m, 0.0) for m in ms]
    ts = [eye + p for p in ps]
    if sb > 1:
        ps = [mm(p, p) for p in ps]
        for _ in range(sb - 2):
            both = [mm(jnp.concatenate([t, p], axis=0), p) for t, p in zip(ts, ps)]
            ts = [t + x[:c] for t, x in zip(ts, both)]
            ps = [x[c:] for x in both]
        ts = [t + mm(t, p) for t, p in zip(ts, ps)]
    s = base
    while s < c:
        ls = _log2(s)
        off = ((row >> (ls + 1)) == (col >> (ls + 1))) & ((row >> ls) > (col >> ls))
        inner = [mm(jnp.where(off, m, 0.0), t) for m, t in zip(ms, ts)]
        ts = [t - mm(t, x) for t, x in zip(ts, inner)]
        s *= 2
    res = [eye - t - dot3s(m, _split(_blockdiag_rows(t, c)), NN) for m, t in zip(ms, ts)]
    return [t + mm(t, r) for t, r in zip(ts, res)]


def _row_getter(x, c):
    if c % LANE:
        x = jnp.concatenate([x, jnp.zeros((LANE - c % LANE, LANE), F32)], axis=0)
    xt = x.T
    return lambda lane: xt[lane:lane + 1, :c]


def _cumsum_rows(x, c):
    tri = jnp.where(_iota2((c, c), 0) >= _iota2((c, c), 1), 1.0, 0.0).astype(F32)
    return dot_nn(tri, x)


def _shifted_taps(tail, x, taps):
    c = x.shape[0]
    xp = jnp.concatenate([tail, x], axis=0)
    out = []
    for s in range(taps - 1, 0, -1):
        out.append(pltpu.roll(xp, s, 0)[SUBLANE:SUBLANE + c])
    out.append(x)
    return out


def _rms_cast_kernel(x_ref, w_ref, o_ref):
    x = x_ref[...]
    y = x * lax.rsqrt(jnp.mean(x * x, axis=-1, keepdims=True) + NORM_EPS)
    o_ref[...] = (y * w_ref[...]).astype(o_ref.dtype)


def rms_cast(x, w, tr):
    m, d = x.shape
    return pl.pallas_call(
        _rms_cast_kernel,
        grid=(m // tr,),
        in_specs=[pl.BlockSpec((tr, d), lambda i: (i, 0)), pl.BlockSpec((1, d), lambda i: (0, 0))],
        out_specs=pl.BlockSpec((tr, d), lambda i: (i, 0)),
        out_shape=jax.ShapeDtypeStruct((m, d), BF16),
        compiler_params=_cparams(("parallel",)),
        name="rms_cast",
    )(x, w.reshape(1, d))


def _add_rms_kernel(x_ref, y_ref, w_ref, o_ref):
    y = y_ref[...]
    yn = y * lax.rsqrt(jnp.mean(y * y, axis=-1, keepdims=True) + NORM_EPS)
    o_ref[...] = x_ref[...] + yn * w_ref[...]


def add_rms(x, y, w, tr):
    m, d = x.shape
    return pl.pallas_call(
        _add_rms_kernel,
        grid=(m // tr,),
        in_specs=[pl.BlockSpec((tr, d), lambda i: (i, 0)), pl.BlockSpec((tr, d), lambda i: (i, 0)),
                  pl.BlockSpec((1, d), lambda i: (0, 0))],
        out_specs=pl.BlockSpec((tr, d), lambda i: (i, 0)),
        out_shape=jax.ShapeDtypeStruct((m, d), F32),
        compiler_params=_cparams(("parallel",)),
        name="add_rms",
    )(x, y, w.reshape(1, d))


def _add_rms_cast_kernel(x_ref, y_ref, w_ref, wn_ref, o_ref, h_ref):
    y = y_ref[...]
    yn = y * lax.rsqrt(jnp.mean(y * y, axis=-1, keepdims=True) + NORM_EPS)
    x = x_ref[...] + yn * w_ref[...]
    o_ref[...] = x
    xn = x * lax.rsqrt(jnp.mean(x * x, axis=-1, keepdims=True) + NORM_EPS)
    h_ref[...] = (xn * wn_ref[...]).astype(h_ref.dtype)


def add_rms_cast(x, y, w, w_next, tr):
    m, d = x.shape
    row = pl.BlockSpec((tr, d), lambda i: (i, 0))
    vec = pl.BlockSpec((1, d), lambda i: (0, 0))
    return pl.pallas_call(
        _add_rms_cast_kernel,
        grid=(m // tr,),
        in_specs=[row, row, vec, vec],
        out_specs=[row, row],
        out_shape=[jax.ShapeDtypeStruct((m, d), F32), jax.ShapeDtypeStruct((m, d), BF16)],
        compiler_params=_cparams(("parallel",)),
        name="add_rms_cast",
    )(x, y, w.reshape(1, d), w_next.reshape(1, d))


def _mm_kernel(a_ref, w_ref, o_ref, *, nk):
    p = jnp.dot(a_ref[...], w_ref[...].astype(BF16), preferred_element_type=F32)
    if nk == 1:
        o_ref[...] = p
    else:
        k = pl.program_id(2)

        @pl.when(k == 0)
        def _():
            o_ref[...] = p

        @pl.when(k > 0)
        def _():
            o_ref[...] += p


def matmul(a, w, tm, tn, tk, name, li=None):
    m, kd = a.shape
    n = w.shape[-1]
    nk = kd // tk
    if li is None:
        w_spec = pl.BlockSpec((tk, tn), lambda i, j, k: (k, j))
    else:
        w_spec = pl.BlockSpec((None, tk, tn), lambda i, j, k: (li, k, j))
    return pl.pallas_call(
        functools.partial(_mm_kernel, nk=nk),
        grid=(m // tm, n // tn, nk),
        in_specs=[pl.BlockSpec((tm, tk), lambda i, j, k: (i, k)), w_spec],
        out_specs=pl.BlockSpec((tm, tn), lambda i, j, k: (i, j)),
        out_shape=jax.ShapeDtypeStruct((m, n), F32),
        compiler_params=_cparams(("parallel", "parallel", "arbitrary")),
        name=name,
    )(a, w)


def _mm_into_kernel(a_ref, w_ref, *rest):
    rest[-1][...] = jnp.dot(a_ref[...], w_ref[...], preferred_element_type=F32)


def matmul_into(a, w, buf, li, depth, tm, tn, name):
    m, kd = a.shape
    n = w.shape[1]
    in_specs = [pl.BlockSpec((tm, kd), lambda i, j: (i, 0)), pl.BlockSpec((kd, tn), lambda i, j: (0, j))]
    args = [a, w]
    aliases = {}
    if buf is not None:
        in_specs.append(pl.BlockSpec(memory_space=pl.ANY))
        args.append(buf)
        aliases = {2: 0}
    return pl.pallas_call(
        _mm_into_kernel,
        grid=(m // tm, n // tn),
        in_specs=in_specs,
        out_specs=pl.BlockSpec((None, tm, tn), lambda i, j: (li, i, j)),
        out_shape=jax.ShapeDtypeStruct((depth, m, n), F32),
        input_output_aliases=aliases,
        compiler_params=_cparams(("parallel", "parallel")),
        name=name,
    )(*args)


def _mm_groups_kernel(*refs):
    *a_refs, w_ref, o_ref, w_scr = refs
    kg = a_refs[0].shape[1]

    @pl.when(pl.program_id(1) == 0)
    def _():
        w_scr[...] = w_ref[...].astype(BF16)

    acc = jnp.dot(a_refs[0][...], w_scr[0:kg, :], preferred_element_type=F32)
    for g in range(1, len(a_refs)):
        acc = acc + jnp.dot(a_refs[g][...], w_scr[g * kg:(g + 1) * kg, :], preferred_element_type=F32)
    o_ref[...] = acc


def matmul_groups(parts, w_all, li, tm, tn, name):
    m, kg = parts[0].shape
    _, kd, n = w_all.shape
    return pl.pallas_call(
        _mm_groups_kernel,
        grid=(n // tn, m // tm),
        in_specs=[pl.BlockSpec((tm, kg), lambda j, i: (i, 0)) for _ in parts]
        + [pl.BlockSpec((None, kd, tn), lambda j, i: (li, 0, j))],
        out_specs=pl.BlockSpec((tm, tn), lambda j, i: (i, j)),
        out_shape=jax.ShapeDtypeStruct((m, n), F32),
        scratch_shapes=[pltpu.VMEM((kd, tn), BF16)],
        compiler_params=_cparams(("arbitrary", "arbitrary")),
        name=name,
    )(*parts, w_all)


def _ffn_act_kernel(g_ref, v_ref, halo_ref, st_ref, cw_ref, cb_ref, o_ref):
    g = g_ref[...]
    tail = jnp.where(pl.program_id(1) == 0, st_ref[...], halo_ref[...])
    taps = _shifted_taps(tail, g, FFN_TAPS)
    cw = cw_ref[...]
    y = taps[0] * cw[0:1]
    for i in range(1, FFN_TAPS):
        y = y + taps[i] * cw[i:i + 1]
    y = y + cb_ref[...]
    o_ref[...] = (_silu(y) * v_ref[...]).astype(o_ref.dtype)


def _ffn_up_act_kernel(x_ref, wg_ref, wv_ref, st_ref, cw_ref, cb_ref, act_ref, cst_ref, wg_scr, wv_scr, tail_scr,
                       *, tiles_per_seq, sub):
    i = pl.program_id(1)

    @pl.when(i == 0)
    def _():
        wg_scr[...] = wg_ref[...].astype(BF16)
        wv_scr[...] = wv_ref[...].astype(BF16)

    @pl.when(i % tiles_per_seq == 0)
    def _():
        tail_scr[...] = st_ref[...]

    wg = wg_scr[...]
    wv = wv_scr[...]
    cw = cw_ref[...]
    cb = cb_ref[...]
    tm = x_ref.shape[0]
    tail = tail_scr[...]

    def project(r):
        x = x_ref[r * sub:(r + 1) * sub, :]
        return jnp.dot(x, wg, preferred_element_type=F32), jnp.dot(x, wv, preferred_element_type=F32)

    nxt = project(0)
    for r in range(tm // sub):
        g, v = nxt
        if r + 1 < tm // sub:
            nxt = project(r + 1)
        taps = _shifted_taps(tail, g, FFN_TAPS)
        y = taps[0] * cw[0:1]
        for t in range(1, FFN_TAPS):
            y = y + taps[t] * cw[t:t + 1]
        act_ref[r * sub:(r + 1) * sub, :] = (_silu(y + cb) * v).astype(act_ref.dtype)
        tail = g[sub - SUBLANE:]
    tail_scr[...] = tail
    cst_ref[...] = tail


def ffn_up_act(h2, w_up_all, li, state8, conv_w, conv_b, l, tm, tn, sub):
    m, kd = h2.shape
    b = m // l
    nj = D_FF // tn
    tiles_per_seq = l // tm
    act, tails = pl.pallas_call(
        functools.partial(_ffn_up_act_kernel, tiles_per_seq=tiles_per_seq, sub=sub),
        grid=(nj, m // tm),
        in_specs=[
            pl.BlockSpec((tm, kd), lambda j, i: (i, 0)),
            pl.BlockSpec((None, kd, tn), lambda j, i: (li, 0, j)),
            pl.BlockSpec((None, kd, tn), lambda j, i: (li, 0, j + nj)),
            pl.BlockSpec((None, SUBLANE, tn), lambda j, i: (i // tiles_per_seq, 0, j)),
            pl.BlockSpec((FFN_TAPS, tn), lambda j, i: (0, j)),
            pl.BlockSpec((1, tn), lambda j, i: (0, j)),
        ],
        out_specs=[
            pl.BlockSpec((tm, tn), lambda j, i: (i, j)),
            pl.BlockSpec((None, SUBLANE, tn), lambda j, i: (i, 0, j)),
        ],
        out_shape=[jax.ShapeDtypeStruct((m, D_FF), BF16), jax.ShapeDtypeStruct((m // tm, SUBLANE, D_FF), F32)],
        scratch_shapes=[pltpu.VMEM((kd, tn), BF16), pltpu.VMEM((kd, tn), BF16), pltpu.VMEM((SUBLANE, tn), F32)],
        compiler_params=_cparams(("arbitrary", "arbitrary")),
        name="ffn_up_act",
    )(h2, w_up_all, w_up_all, state8, conv_w, conv_b.reshape(1, D_FF))
    return act, tails.reshape(b, tiles_per_seq, SUBLANE, D_FF)[:, tiles_per_seq - 1]


def ffn_act(up3, state8, conv_w, conv_b, ts, tn):
    b, l, _ = up3.shape
    nj = D_FF // tn
    hb = ts // SUBLANE
    return pl.pallas_call(
        _ffn_act_kernel,
        grid=(b, l // ts, nj),
        in_specs=[
            pl.BlockSpec((None, ts, tn), lambda bi, i, j: (bi, i, j)),
            pl.BlockSpec((None, ts, tn), lambda bi, i, j: (bi, i, j + nj)),
            pl.BlockSpec((None, SUBLANE, tn), lambda bi, i, j: (bi, jnp.maximum(i * hb - 1, 0), j)),
            pl.BlockSpec((None, SUBLANE, tn), lambda bi, i, j: (bi, 0, j)),
            pl.BlockSpec((FFN_TAPS, tn), lambda bi, i, j: (0, j)),
            pl.BlockSpec((1, tn), lambda bi, i, j: (0, j)),
        ],
        out_specs=pl.BlockSpec((None, ts, tn), lambda bi, i, j: (bi, i, j)),
        out_shape=jax.ShapeDtypeStruct((b, l, D_FF), BF16),
        compiler_params=_cparams(("parallel", "parallel", "parallel")),
        name="ffn_act",
    )(up3, up3, up3, state8, conv_w, conv_b.reshape(1, D_FF))


def _gdn_kernel(qkv_ref, z_ref, sm_ref, cbuf_ref, s0_ref, cw_ref, alog_ref, dtb_ref, nw_ref,
                o_ref, sout_ref, s_scr, tail_scr, *, c, nsub):
    ci = pl.program_id(1)

    @pl.when(ci == 0)
    def _():
        s_scr[...] = s0_ref[...]
        tail_scr[...] = cbuf_ref[...]

    x = qkv_ref[...]
    taps = _shifted_taps(tail_scr[...], x, GDN_TAPS)
    rows = nsub * c
    tail_scr[...] = x[rows - SUBLANE:]
    cw = cw_ref[...]
    y = taps[0] * cw[0:1]
    for i in range(1, GDN_TAPS):
        y = y + taps[i] * cw[i:i + 1]
    y = _silu(y)

    sm = sm_ref[...]
    beta_all = jax.nn.sigmoid(sm)
    g_all = -jnp.exp(alog_ref[...]) * jax.nn.softplus(sm + dtb_ref[...])
    ri = _iota2((rows, rows), 0)
    rj = _iota2((rows, rows), 1)
    in_chunk_tri = jnp.where((ri >= rj) & ((ri >> _log2(c)) == (rj >> _log2(c))), 1.0, 0.0).astype(F32)
    gcum_all = dot_nn(in_chunk_tri, g_all)
    rows_of = _row_getter(gcum_all, rows)

    row = _iota2((c, c), 0)
    col = _iota2((c, c), 1)
    tri = row >= col
    strict = row > col
    z = z_ref[...]
    nw = nw_ref[...]
    heads = range(GDN_HEADS)
    units = [(q, h) for q in range(nsub) for h in heads]
    ms, aqk, rhs, qd, kd, gls = [], [], [], [], [], []
    for qi, h in units:
        rs = slice(qi * c, (qi + 1) * c)
        lo = h * GDN_D
        q = y[rs, lo:lo + GDN_D]
        k = y[rs, GROUP_W + lo:GROUP_W + lo + GDN_D]
        v = y[rs, 2 * GROUP_W + lo:2 * GROUP_W + lo + GDN_D]
        q = q * lax.rsqrt(jnp.sum(q * q, axis=-1, keepdims=True) + 1e-6) * (GDN_D ** -0.5)
        k = k * lax.rsqrt(jnp.sum(k * k, axis=-1, keepdims=True) + 1e-6)
        beta = beta_all[rs, SM_GDN_B + h:SM_GDN_B + h + 1]
        gc = gcum_all[rs, SM_GDN_A + h:SM_GDN_A + h + 1]
        gam = jnp.exp(jnp.where(tri, gc - rows_of(SM_GDN_A + h)[:, qi * c:(qi + 1) * c], -jnp.inf))
        kbeta = k * beta
        mq = dot3s(jnp.concatenate([kbeta, q], axis=0), _split(k), NT)
        ms.append(jnp.where(strict, mq[:c] * gam, 0.0))
        aqk.append(mq[c:] * gam)
        eg = jnp.exp(gc)
        gl = gc[c - 1:c]
        rhs.append(jnp.concatenate([kbeta * eg, v * beta], axis=1))
        qd.append(q * eg)
        kd.append(k * jnp.exp(gl - gc))
        gls.append(gl)
    packed = _inv_unit_lower_pairs([jnp.concatenate([ms[i], ms[i + 1]], axis=1) for i in range(0, len(ms), 2)], c)
    ts = [t[:, half * c:(half + 1) * c] for t in packed for half in range(2)]
    wu = [dot3s(t, _split(x), NN) for t, x in zip(ts, rhs)]
    state = [s_scr[h] for h in heads]
    for qi in range(nsub):
        rs = slice(qi * c, (qi + 1) * c)
        ix = [qi * GDN_HEADS + h for h in heads]
        wqs = [dot3s(jnp.concatenate([wu[i][:, :GDN_D], qd[i]], axis=0), _split(state[h]), NN)
               for h, i in zip(heads, ix)]
        v_new = [wu[i][:, GDN_D:] - wqs[h][:c] for h, i in zip(heads, ix)]
        outs = [wqs[h][c:] + dot1(aqk[i], v_new[h], NN) for h, i in zip(heads, ix)]
        state = [state[h] * jnp.exp(gls[i]) + dot3(_split(kd[i]), _split(v_new[h]), TN) for h, i in zip(heads, ix)]
        for h in heads:
            lo = h * GDN_D
            o = outs[h]
            o = o * lax.rsqrt(jnp.mean(o * o, axis=-1, keepdims=True) + NORM_EPS) * nw
            o = o * _silu(z[rs, lo:lo + GDN_D])
            o_ref[rs, lo:lo + GDN_D] = o.astype(o_ref.dtype)
    for h in heads:
        s_scr[h] = state[h]

    @pl.when(ci == pl.num_programs(1) - 1)
    def _():
        sout_ref[...] = s_scr[...]


def gdn_mixer(proj3, cbuf8, s0, conv_w, a_log, dt_bias, norm_w, c):
    b, l, _ = proj3.shape
    alog_row = jnp.zeros((1, LANE), F32).at[0, SM_GDN_A:SM_GDN_A + GDN_HEADS].set(a_log)
    dtb_row = jnp.zeros((1, LANE), F32).at[0, SM_GDN_A:SM_GDN_A + GDN_HEADS].set(dt_bias)
    w3 = 3 * GROUP_W
    const2 = lambda bi, ci: (0, 0)
    nsub = 2 if l % (2 * c) == 0 else 1
    rb = nsub * c
    return pl.pallas_call(
        functools.partial(_gdn_kernel, c=c, nsub=nsub),
        grid=(b, l // rb),
        in_specs=[
            pl.BlockSpec((None, rb, w3), lambda bi, ci: (bi, ci, COL_A_QKV // w3)),
            pl.BlockSpec((None, rb, GROUP_W), lambda bi, ci: (bi, ci, COL_A_Z // GROUP_W)),
            pl.BlockSpec((None, rb, LANE), lambda bi, ci: (bi, ci, COL_SMALL // LANE)),
            pl.BlockSpec((None, SUBLANE, w3), lambda bi, ci: (bi, 0, 0)),
            pl.BlockSpec((None, GDN_HEADS, GDN_D, GDN_D), lambda bi, ci: (bi, 0, 0, 0)),
            pl.BlockSpec((GDN_TAPS, w3), const2),
            pl.BlockSpec((1, LANE), const2),
            pl.BlockSpec((1, LANE), const2),
            pl.BlockSpec((1, GDN_D), const2),
        ],
        out_specs=[
            pl.BlockSpec((None, rb, GROUP_W), lambda bi, ci: (bi, ci, 0)),
            pl.BlockSpec((None, GDN_HEADS, GDN_D, GDN_D), lambda bi, ci: (bi, 0, 0, 0)),
        ],
        out_shape=[jax.ShapeDtypeStruct((b, l, GROUP_W), BF16),
                   jax.ShapeDtypeStruct((b, GDN_HEADS, GDN_D, GDN_D), F32)],
        scratch_shapes=[pltpu.VMEM((GDN_HEADS, GDN_D, GDN_D), F32), pltpu.VMEM((SUBLANE, w3), F32)],
        compiler_params=_cparams(("parallel", "arbitrary")),
        name="gdn_mixer",
    )(proj3, proj3, proj3, cbuf8, s0, conv_w, alog_row, dtb_row, norm_w.reshape(1, GDN_D))


def _rwkv_kernel(rkv_ref, lora_ref, sh_rkv_ref, sh_lora_ref, s0_ref, mu_rkv_ref, mu_lora_ref,
                 w0_ref, w2_ref, a0_ref, a2_ref, g2_ref, kk_ref, ka_ref, rk_ref, lnw_ref, lnb_ref,
                 o_ref, sout_ref, s_scr, tail_rkv, tail_lora, *, c, nsub):
    ci = pl.program_id(1)

    @pl.when(ci == 0)
    def _():
        s_scr[...] = s0_ref[...]
        tail_rkv[...] = sh_rkv_ref[...]
        tail_lora[...] = sh_lora_ref[...]

    x = rkv_ref[...]
    xl = lora_ref[...]
    prev = _shifted_taps(tail_rkv[...], x, 2)[0]
    prev_l = _shifted_taps(tail_lora[...], xl, 2)[0]
    tail_rkv[...] = x[nsub * c - SUBLANE:]
    tail_lora[...] = xl[nsub * c - SUBLANE:]
    zm = x + (prev - x) * mu_rkv_ref[...]
    zl = xl + (prev_l - xl) * mu_lora_ref[...]
    r = zm[:, 0:GROUP_W]
    k = zm[:, GROUP_W:2 * GROUP_W]
    v = zm[:, 2 * GROUP_W:3 * GROUP_W]
    wd = zl[:, 0:RWKV_W_LORA]
    ad = zl[:, RWKV_W_LORA:RWKV_W_LORA + RWKV_A_LORA]
    gd = zl[:, RWKV_W_LORA + RWKV_A_LORA:LORA_PAD]

    w_log = -jax.nn.softplus(-(w0_ref[...] + dot3s(jnp.tanh(wd), _split(w2_ref[...]), NN))) - 0.5
    logw = -jnp.exp(w_log)
    a = jax.nn.sigmoid(a0_ref[...] + dot3s(ad, _split(a2_ref[...]), NN))
    gate = dot1(jax.nn.sigmoid(gd), g2_ref[...], NN)
    kkv = k * kk_ref[...]
    k2 = k * (1.0 + (a - 1.0) * ka_ref[...])
    rows = nsub * c
    ri = _iota2((rows, rows), 0)
    rj = _iota2((rows, rows), 1)
    lc_sh = _log2(c)
    in_chunk_tri = jnp.where((ri >= rj) & ((ri >> lc_sh) == (rj >> lc_sh)), 1.0, 0.0).astype(F32)
    lcum = dot_nn(in_chunk_tri, logw)

    n = RWKV_HS
    pw = 2 * n
    rowc = _iota2((c, 2 * c), 0)
    colc = _iota2((c, 2 * c), 1) & (c - 1)
    strict2 = rowc > colc
    tri2 = rowc >= colc
    same_head = (_iota2((pw, pw), 0) < n) == (_iota2((pw, pw), 1) < n)
    ones_bd = jnp.where(same_head, 1.0, 0.0).astype(BF16)
    rk = rk_ref[...]
    lnw = lnw_ref[...]
    lnb = lnb_ref[...]
    pairs = range(RWKV_HEADS // 2)
    sl = [slice(p * pw, (p + 1) * pw) for p in pairs]
    units = [(slice(q * c, (q + 1) * c), sl[p]) for q in range(nsub) for p in pairs]
    kk_ss = [_head_sums(jnp.square(kkv[rs, ps]), ones_bd) for rs, ps in units]
    bonus_s = [_head_sums(r[rs, ps] * k2[rs, ps] * rk[:, ps], ones_bd) for rs, ps in units]
    lhs, x_bs, x_ks, bk_end, l_last = [], [], [], [], []
    for (rs, ps), ss in zip(units, kk_ss):
        kk = kkv[rs, ps] * lax.rsqrt(ss + 1e-6)
        lc = lcum[rs, ps]
        ll = lc[c - 1:c]
        p_inv = jnp.exp(-lc)
        a_t = -kk * jnp.exp(lc - logw[rs, ps])
        b_vec = kk * a[rs, ps]
        r_t = r[rs, ps] * jnp.exp(lc)
        p_end = jnp.exp(ll - lc)
        ar = jnp.concatenate([a_t, r_t], axis=0)
        x_bs.append(dot3s(ar, _split(_blockdiag_rows(b_vec * p_inv, n)), NT))
        x_ks.append(dot3s(ar, _split(_blockdiag_rows(k2[rs, ps] * p_inv, n)), NT))
        lhs.append(ar)
        bk_end.append(jnp.concatenate([b_vec * p_end, k2[rs, ps] * p_end], axis=0))
        l_last.append(ll)
    ms = [jnp.where(strict2, -x[:c], 0.0) for x in x_bs]
    av = [dot3s(jnp.where(strict2, x[:c], 0.0), _split(_blockdiag_rows(v[rs, ps], n)), NN)
          for x, (rs, ps) in zip(x_ks, units)]
    rbk = [jnp.concatenate([jnp.where(tri2, xb[c:], 0.0), jnp.where(tri2, xk[c:], 0.0)], axis=1)
           for xb, xk in zip(x_bs, x_ks)]
    ts = _inv_unit_lower_pairs(ms, c)
    state = [s_scr[p] for p in pairs]
    npair = len(pairs)
    for q in range(nsub):
        rs = slice(q * c, (q + 1) * c)
        ix = [q * npair + p for p in pairs]
        ars = [dot3s(lhs[i], _split(state[p]), NT) for p, i in zip(pairs, ix)]
        us = [dot3s(ts[i], _split(_blockdiag_rows(x[:c] + av[i], n)), NN) for x, i in zip(ars, ix)]
        vs = [v[rs, sl[p]] for p in pairs]
        ys = [ars[p][c:] + _dot(rbk[i].astype(BF16),
                                jnp.concatenate([_blockdiag_rows(us[p], n), _blockdiag_rows(vs[p], n)],
                                                axis=0).astype(BF16), NN, None)
              for p, i in zip(pairs, ix)]
        upd = [dot3(_split(jnp.concatenate([us[p], vs[p]], axis=0)), _split(bk_end[i]), TN)
               for p, i in zip(pairs, ix)]
        state = [jnp.where(same_head, state[p] * jnp.exp(l_last[i]) + upd[p], 0.0) for p, i in zip(pairs, ix)]
        devs = [y - _head_sums(y, ones_bd) * (1.0 / n) for y in ys]
        var = [_head_sums(jnp.square(d), ones_bd) * (1.0 / n) for d in devs]
        for p, i in zip(pairs, ix):
            yn = devs[p] * lax.rsqrt(var[p] + RWKV_GN_EPS) * lnw[:, sl[p]] + lnb[:, sl[p]]
            o_ref[rs, sl[p]] = ((yn + bonus_s[i] * vs[p]) * gate[rs, sl[p]]).astype(o_ref.dtype)
    for p in pairs:
        s_scr[p] = state[p]

    @pl.when(ci == pl.num_programs(1) - 1)
    def _():
        sout_ref[...] = s_scr[...]


def rwkv_mixer(proj3, sh_rkv8, sh_lora8, s0, prm, c):
    b, l, _ = proj3.shape
    nsub = 2 if l % (2 * c) == 0 else 1
    rb = nsub * c
    w3 = 3 * GROUP_W
    mu = prm['rwkv_mu']
    mu_rkv = mu[:w3].reshape(1, w3)
    mu_lora = jnp.pad(mu[w3:], (0, LORA_PAD - RWKV_LORA)).reshape(1, LORA_PAD)
    g2 = jnp.pad(prm['rwkv_g2'], ((0, G_LORA_PAD - RWKV_G_LORA), (0, 0)))
    row = lambda t: t.reshape(1, GROUP_W)
    const2 = lambda bi, ci: (0, 0)
    vec = pl.BlockSpec((1, GROUP_W), const2)
    n, npair = RWKV_HS, RWKV_HEADS // 2
    sp = s0.reshape(b, npair, 2, n, n)
    zero = jnp.zeros((b, npair, n, n), F32)
    s_pairs = jnp.concatenate([jnp.concatenate([sp[:, :, 0], zero], axis=-1),
                               jnp.concatenate([zero, sp[:, :, 1]], axis=-1)], axis=-2)
    state_spec = pl.BlockSpec((None, npair, 2 * n, 2 * n), lambda bi, ci: (bi, 0, 0, 0))
    o, s_new = pl.pallas_call(
        functools.partial(_rwkv_kernel, c=c, nsub=nsub),
        grid=(b, l // rb),
        in_specs=[
            pl.BlockSpec((None, rb, w3), lambda bi, ci: (bi, ci, COL_B_RKV // w3)),
            pl.BlockSpec((None, rb, LORA_PAD), lambda bi, ci: (bi, ci, COL_B_LORA // LORA_PAD)),
            pl.BlockSpec((None, SUBLANE, w3), lambda bi, ci: (bi, 0, 0)),
            pl.BlockSpec((None, SUBLANE, LORA_PAD), lambda bi, ci: (bi, 0, 0)),
            state_spec,
            pl.BlockSpec((1, w3), const2),
            pl.BlockSpec((1, LORA_PAD), const2),
            vec,
            pl.BlockSpec((RWKV_W_LORA, GROUP_W), const2),
            vec,
            pl.BlockSpec((RWKV_A_LORA, GROUP_W), const2),
            pl.BlockSpec((G_LORA_PAD, GROUP_W), const2),
            vec, vec, vec, vec, vec,
        ],
        out_specs=[
            pl.BlockSpec((None, rb, GROUP_W), lambda bi, ci: (bi, ci, 0)),
            state_spec,
        ],
        out_shape=[jax.ShapeDtypeStruct((b, l, GROUP_W), BF16),
                   jax.ShapeDtypeStruct((b, npair, 2 * n, 2 * n), F32)],
        scratch_shapes=[pltpu.VMEM((npair, 2 * n, 2 * n), F32),
                        pltpu.VMEM((SUBLANE, w3), F32), pltpu.VMEM((SUBLANE, LORA_PAD), F32)],
        compiler_params=_cparams(("parallel", "arbitrary")),
        name="rwkv_mixer",
    )(proj3, proj3, sh_rkv8, sh_lora8, s_pairs, mu_rkv, mu_lora,
      row(prm['rwkv_w0']), prm['rwkv_w2'], row(prm['rwkv_a0']), prm['rwkv_a2'], g2,
      row(prm['rwkv_k_k']), row(prm['rwkv_k_a']), row(prm['rwkv_r_k']),
      row(prm['rwkv_ln_w']), row(prm['rwkv_ln_b']))
    s_heads = jnp.stack([s_new[:, :, :n, :n], s_new[:, :, n:, n:]], axis=2)
    return o, s_heads.reshape(b, RWKV_HEADS, n, n)


def _ssd_kernel(z_ref, xbc_ref, sm_ref, cbuf_ref, s0_ref, cw_ref, cb_ref, dtb_ref, alog_ref, dsk_ref, nw_ref,
                o_ref, sout_ref, s_scr, tail_scr, y_scr, *, c):
    ci = pl.program_id(1)

    @pl.when(ci == 0)
    def _():
        s_scr[...] = s0_ref[...]
        tail_scr[...] = cbuf_ref[...]

    x = xbc_ref[...]
    taps = _shifted_taps(tail_scr[...], x, SSM_TAPS)
    tail_scr[...] = x[c - SUBLANE:]
    cw = cw_ref[...]
    y = taps[0] * cw[0:1]
    for i in range(1, SSM_TAPS):
        y = y + taps[i] * cw[i:i + 1]
    y = _silu(y + cb_ref[...])
    xs = y[:, 0:GROUP_W]
    gn = SSM_GROUPS * SSM_N

    sm = sm_ref[...]
    dt_all = jax.nn.softplus(sm + dtb_ref[...])
    da_all = dt_all * (-jnp.exp(alog_ref[...]))
    acs_all = _cumsum_rows(da_all, c)

    row = _iota2((c, c), 0)
    col = _iota2((c, c), 1)
    tri = row >= col
    z = z_ref[...]
    dsk = dsk_ref[...]
    hpg = SSM_HEADS // SSM_GROUPS
    gp = hpg * SSM_P
    rows_of = _row_getter(acs_all, c)
    for g in range(SSM_GROUPS):
        bm = y[:, GROUP_W + g * SSM_N:GROUP_W + (g + 1) * SSM_N]
        cm = y[:, GROUP_W + gn + g * SSM_N:GROUP_W + gn + (g + 1) * SSM_N]
        cb = dot1(cm, bm, NT)
        sg = s_scr[g]
        y_off = dot1(cm, sg, NT)
        xdec = []
        for rr in range(hpg):
            h = g * hpg + rr
            lo = h * SSM_P
            lane = SM_SSM_DT + h
            xs_h = xs[:, lo:lo + SSM_P]
            dt = dt_all[:, lane:lane + 1]
            acs = acs_all[:, lane:lane + 1]
            lmat = jnp.exp(jnp.where(tri, acs - rows_of(lane), -jnp.inf))
            xd = xs_h * dt
            a_last = acs[c - 1:c]
            xdec.append(xd * jnp.exp(a_last - acs))
            yh = dot1(cb * lmat, xd, NN) + y_off[:, rr * SSM_P:(rr + 1) * SSM_P] * jnp.exp(acs)
            yh = yh + xs_h * dsk[:, lane:lane + 1]
            y_scr[:, lo:lo + SSM_P] = yh * _silu(z[:, lo:lo + SSM_P])
        upd = dot3(_split(jnp.concatenate(xdec, axis=1)), _split(bm), TN)
        for rr in range(hpg):
            lane = SM_SSM_DT + g * hpg + rr
            dec = jnp.exp(acs_all[c - 1:c, lane:lane + 1])
            s_scr[g, rr * SSM_P:(rr + 1) * SSM_P, :] = (sg[rr * SSM_P:(rr + 1) * SSM_P] * dec
                                                         + upd[rr * SSM_P:(rr + 1) * SSM_P])

    gw = GROUP_W // SSM_GROUPS
    nw = nw_ref[...]
    for g in range(SSM_GROUPS):
        yg = y_scr[:, g * gw:(g + 1) * gw]
        yg = yg * lax.rsqrt(jnp.mean(yg * yg, axis=-1, keepdims=True) + NORM_EPS)
        o_ref[:, g * gw:(g + 1) * gw] = (yg * nw[:, g * gw:(g + 1) * gw]).astype(o_ref.dtype)

    @pl.when(ci == pl.num_programs(1) - 1)
    def _():
        sout_ref[...] = s_scr[...]


def ssd_mixer(proj3, cbuf8, s0, prm, c):
    b, l, _ = proj3.shape

    def lanes(t):
        return jnp.zeros((1, LANE), F32).at[0, SM_SSM_DT:SM_SSM_DT + SSM_HEADS].set(t)

    const2 = lambda bi, ci: (0, 0)
    small = pl.BlockSpec((1, LANE), const2)
    gshape = (SSM_GROUPS, SSM_HEADS // SSM_GROUPS * SSM_P, SSM_N)
    state_spec = pl.BlockSpec((None,) + gshape, lambda bi, ci: (bi, 0, 0, 0))
    o, s_new = pl.pallas_call(
        functools.partial(_ssd_kernel, c=c),
        grid=(b, l // c),
        in_specs=[
            pl.BlockSpec((None, c, GROUP_W), lambda bi, ci: (bi, ci, COL_C_Z // GROUP_W)),
            pl.BlockSpec((None, c, SSM_XBC), lambda bi, ci: (bi, ci, COL_C_XBC // SSM_XBC)),
            pl.BlockSpec((None, c, LANE), lambda bi, ci: (bi, ci, COL_SMALL // LANE)),
            pl.BlockSpec((None, SUBLANE, SSM_XBC), lambda bi, ci: (bi, 0, 0)),
            state_spec,
            pl.BlockSpec((SSM_TAPS, SSM_XBC), const2),
            pl.BlockSpec((1, SSM_XBC), const2),
            small, small, small,
            pl.BlockSpec((1, GROUP_W), const2),
        ],
        out_specs=[
            pl.BlockSpec((None, c, GROUP_W), lambda bi, ci: (bi, ci, 0)),
            state_spec,
        ],
        out_shape=[jax.ShapeDtypeStruct((b, l, GROUP_W), BF16),
                   jax.ShapeDtypeStruct((b,) + gshape, F32)],
        scratch_shapes=[pltpu.VMEM(gshape, F32), pltpu.VMEM((SUBLANE, SSM_XBC), F32),
                        pltpu.VMEM((c, GROUP_W), F32)],
        compiler_params=_cparams(("parallel", "arbitrary")),
        name="ssd_mixer",
    )(proj3, proj3, proj3, cbuf8, s0.reshape((b,) + gshape), prm['ssm_conv_w'],
      prm['ssm_conv_b'].reshape(1, SSM_XBC),
      lanes(prm['ssm_dt_bias']), lanes(prm['ssm_A_log']), lanes(prm['ssm_D']),
      prm['ssm_norm_w'].reshape(1, GROUP_W))
    return o, s_new.reshape(b, SSM_HEADS, SSM_P, SSM_N)


def _swa_weight(d):
    mult = jnp.zeros(d.shape, F32)
    for window, dil in SWA_PATTERNS:
        ok = (d >= 0) & (d <= window) & ((d & (dil - 1)) == 0)
        mult = mult + jnp.where(ok, 1.0, 0.0)
    return mult


def _swa_scores(q, k, d, slope):
    s = dot_nt(q.astype(BF16), k.astype(BF16), None) * (SWA_HD ** -0.5)
    mult = _swa_weight(d)
    s = s - slope * d.astype(F32)
    return jnp.where(mult > 0.0, s, NEG_INF), mult


def _swa_prompt_kernel(slopes_ref, q_ref, k_ref, v_ref, lw_ref, o_ref, m_scr, l_scr, acc_scr, *, t, sub):
    h = pl.program_id(1)
    qi = pl.program_id(2)
    ki = pl.program_id(3)

    @pl.when(ki == 0)
    def _():
        m_scr[...] = jnp.full(m_scr.shape, NEG_INF, F32)
        l_scr[...] = jnp.zeros(l_scr.shape, F32)
        acc_scr[...] = jnp.zeros(acc_scr.shape, F32)

    @pl.when(ki <= qi)
    def _():
        col = (ki * t + _iota2((1, t), 1)).astype(F32) * slopes_ref[h]
        kb = k_ref[...].astype(BF16)
        vb = v_ref[...].astype(BF16)
        def qk(r):
            q = (q_ref[r:r + sub, :] * (SWA_HD ** -0.5)).astype(BF16)
            return _dot(q, kb, NT, None)

        nxt = qk(0)
        for r in range(0, t, sub):
            rs = slice(r, r + sub)
            s = nxt + (lw_ref[qi - ki, rs, :] + col)
            if r + sub < t:
                nxt = qk(r + sub)
            m_old = m_scr[rs, :]
            m_new = jnp.maximum(m_old, jnp.max(s, axis=-1, keepdims=True))
            alpha = jnp.exp(m_old - m_new)
            p = jnp.exp(s - m_new)
            l_scr[rs, :] = alpha * l_scr[rs, :] + jnp.sum(p, axis=-1, keepdims=True)
            acc_scr[rs, :] = alpha * acc_scr[rs, :] + _dot(p.astype(BF16), vb, NN, None)
            m_scr[rs, :] = m_new

    @pl.when(ki == pl.num_programs(3) - 1)
    def _():
        o_ref[...] = (acc_scr[...] / l_scr[...]).astype(o_ref.dtype)


def _alibi_slopes():
    return jnp.asarray([2.0 ** (-8.0 * (i + 1) / SWA_HEADS) for i in range(SWA_HEADS)], F32)


def _swa_log_weight_tiles(n, t):
    d = (jnp.arange(n)[:, None, None] * t + jnp.arange(t)[None, :, None]) - jnp.arange(t)[None, None, :]
    mult = _swa_weight(d.astype(jnp.int32))
    return jnp.where(mult > 0.0, jnp.log(jnp.maximum(mult, 1.0)), NEG_INF)


def swa_prompt(proj3, k4, v4, li, t):
    b, l, _ = proj3.shape
    qc = COL_D_Q // SWA_HD
    n = l // t
    kv_spec = pl.BlockSpec((None, None, t, SWA_HD), lambda bi, h, qi, ki: (li, bi, jnp.minimum(ki, qi), h))

    return pl.pallas_call(
        functools.partial(_swa_prompt_kernel, t=t, sub=min(t, 256)),
        grid=(b, SWA_HEADS, n, n),
        in_specs=[
            pl.BlockSpec(memory_space=pltpu.SMEM),
            pl.BlockSpec((None, t, SWA_HD), lambda bi, h, qi, ki: (bi, qi, qc + h)),
            kv_spec,
            kv_spec,
            pl.BlockSpec((n, t, t), lambda bi, h, qi, ki: (0, 0, 0)),
        ],
        out_specs=pl.BlockSpec((None, t, SWA_HD), lambda bi, h, qi, ki: (bi, qi, h)),
        out_shape=jax.ShapeDtypeStruct((b, l, GROUP_W), BF16),
        scratch_shapes=[pltpu.VMEM((t, 1), F32), pltpu.VMEM((t, 1), F32), pltpu.VMEM((t, SWA_HD), F32)],
        compiler_params=_cparams(("parallel", "parallel", "parallel", "arbitrary")),
        name="swa_prompt",
    )(_alibi_slopes(), proj3, k4, v4, _swa_log_weight_tiles(n, t))


def _swa_sample_kernel(slopes_ref, q_ref, k_ref, v_ref, ck_ref, cv_ref, o_ref, *, t, wb):
    d_c = (wb + _iota2((t, wb), 0)) - _iota2((t, wb), 1)
    d_n = _iota2((t, t), 0) - _iota2((t, t), 1)
    for h in range(SWA_HEADS):
        hs = slice(h * SWA_HD, (h + 1) * SWA_HD)
        ck = ck_ref[pl.ds(h, wb, stride=SWA_HEADS), :]
        cv = cv_ref[pl.ds(h, wb, stride=SWA_HEADS), :]
        q = q_ref[:, hs]
        s_c, mult_c = _swa_scores(q, ck, d_c, slopes_ref[h])
        s_n, mult_n = _swa_scores(q, k_ref[:, hs], d_n, slopes_ref[h])
        m = jnp.maximum(jnp.max(s_c, axis=-1, keepdims=True), jnp.max(s_n, axis=-1, keepdims=True))
        p_c = jnp.exp(s_c - m) * mult_c
        p_n = jnp.exp(s_n - m) * mult_n
        den = jnp.sum(p_c, axis=-1, keepdims=True) + jnp.sum(p_n, axis=-1, keepdims=True)
        num = (dot_nn(p_c.astype(BF16), cv.astype(BF16), None)
               + dot_nn(p_n.astype(BF16), v_ref[:, hs].astype(BF16), None))
        o_ref[:, hs] = (num / den).astype(o_ref.dtype)


def swa_sample(proj3, k4, v4, li, cache_k_all, cache_v_all):
    b, t, _ = proj3.shape
    depth, _, wb = cache_k_all.shape[:3]
    ck = cache_k_all.reshape(depth, b, wb * SWA_HEADS, SWA_HD)
    cv = cache_v_all.reshape(depth, b, wb * SWA_HEADS, SWA_HD)
    new_spec = pl.BlockSpec((None, None, t, GROUP_W), lambda bi: (li, bi, 0, 0))
    cache_spec = pl.BlockSpec((None, None, wb * SWA_HEADS, SWA_HD), lambda bi: (li, bi, 0, 0))
    return pl.pallas_call(
        functools.partial(_swa_sample_kernel, t=t, wb=wb),
        grid=(b,),
        in_specs=[
            pl.BlockSpec(memory_space=pltpu.SMEM),
            pl.BlockSpec((None, t, GROUP_W), lambda bi: (bi, 0, COL_D_Q // GROUP_W)),
            new_spec,
            new_spec,
            cache_spec,
            cache_spec,
        ],
        out_specs=pl.BlockSpec((None, t, GROUP_W), lambda bi: (bi, 0, 0)),
        out_shape=jax.ShapeDtypeStruct((b, t, GROUP_W), BF16),
        compiler_params=_cparams(("parallel",)),
        name="swa_sample",
    )(_alibi_slopes(), proj3, k4, v4, ck, cv)


def _tiles(m):
    return (256, 1024) if m >= 1024 else (m, m)


def _front_pad_rows(t, rows=SUBLANE):
    return jnp.pad(t, ((0, 0), (rows - t.shape[1], 0), (0, 0)))


W_IN_SEGMENTS = (('a_qkv', 3 * GROUP_W), ('a_z', GROUP_W), ('a_b', GDN_HEADS), ('a_a', GDN_HEADS),
                 ('b_rkv', 3 * GROUP_W), ('b_lora', RWKV_LORA), ('c_z', GROUP_W), ('c_xbc', SSM_XBC),
                 ('c_dt', SSM_HEADS), ('d_q', GROUP_W), ('d_k', GROUP_W), ('d_v', GROUP_W))
W_IN_SRC = {}
_o = 0
for _name, _n in W_IN_SEGMENTS:
    W_IN_SRC[_name] = _o
    _o += _n
W_IN_WIDE = ((COL_A_QKV, 3 * GROUP_W, W_IN_SRC['a_qkv']), (COL_B_RKV, 3 * GROUP_W, W_IN_SRC['b_rkv']),
             (COL_D_Q, GROUP_W, W_IN_SRC['d_q']), (COL_A_Z, GROUP_W, W_IN_SRC['a_z']),
             (COL_C_Z, GROUP_W, W_IN_SRC['c_z']), (COL_C_XBC, SSM_XBC, W_IN_SRC['c_xbc']))
IN_TN = 512
W_IN_SHIFTS = tuple(sorted({src % LANE for _, _, src in W_IN_WIDE}))
assert COL_B_LORA % IN_TN == 0 and N_PROJ - COL_B_LORA == IN_TN


def _w_in_tables():
    src_blk, case = [], []
    for j in range(N_PROJ // IN_TN - 1):
        c0 = j * IN_TN
        dest, width, src = next(s for s in W_IN_WIDE if s[0] <= c0 < s[0] + s[1])
        assert c0 + IN_TN <= dest + width
        col = src + (c0 - dest)
        src_blk.append(col // LANE)
        case.append(W_IN_SHIFTS.index(col % LANE))
    src_blk.append(0)
    case.append(len(W_IN_SHIFTS))
    return jnp.asarray(src_blk, jnp.int32), jnp.asarray(case, jnp.int32)


def prep_weights(p):
    w = p['w_in']
    seg = {name: w[:, W_IN_SRC[name]:W_IN_SRC[name] + n].astype(BF16)
           for name, n in W_IN_SEGMENTS if name in ('a_b', 'a_a', 'b_lora', 'c_dt', 'd_k', 'd_v')}
    zeros = lambda n: jnp.zeros((D_MODEL, n), BF16)
    small_used = 2 * GDN_HEADS + SSM_HEADS
    tail = jnp.concatenate([seg['b_lora'], zeros(LORA_PAD - RWKV_LORA),
                            seg['a_b'], seg['a_a'], seg['c_dt'], zeros(LANE - small_used)], axis=1)
    return {'w_tail': tail, 'w_k': seg['d_k'], 'w_v': seg['d_v']}


def _mm_in_kernel(src_ref, case_ref, x_ref, *rest):
    del src_ref
    *w_refs, tail_ref, o_ref, w_scr = rest
    j = pl.program_id(0)

    @pl.when(pl.program_id(1) == 0)
    def _():
        case = case_ref[j]
        nb = len(w_refs) - 1
        for k, sh in enumerate(W_IN_SHIFTS):
            @pl.when(case == k)
            def _(sh=sh):
                for blk in range(nb):
                    if sh == 0:
                        piece = w_refs[blk][...]
                    else:
                        piece = jnp.concatenate([w_refs[blk][:, sh:], w_refs[blk + 1][:, :sh]], axis=1)
                    w_scr[:, blk * LANE:(blk + 1) * LANE] = piece.astype(BF16)

        @pl.when(case == len(W_IN_SHIFTS))
        def _():
            w_scr[...] = tail_ref[...]

    o_ref[...] = jnp.dot(x_ref[...], w_scr[...], preferred_element_type=F32)


def mm_in(h, w_in_all, li, w_tail, tm):
    m, kd = h.shape
    nb = IN_TN // LANE
    src_blk, case = _w_in_tables()

    def w_spec(k):
        return pl.BlockSpec((None, kd, LANE), lambda j, i, src, cs: (li, 0, src[j] + k))

    grid_spec = pltpu.PrefetchScalarGridSpec(
        num_scalar_prefetch=2,
        grid=(N_PROJ // IN_TN, m // tm),
        in_specs=[pl.BlockSpec((tm, kd), lambda j, i, src, cs: (i, 0))]
        + [w_spec(k) for k in range(nb + 1)]
        + [pl.BlockSpec((kd, IN_TN), lambda j, i, src, cs: (0, 0))],
        out_specs=pl.BlockSpec((tm, IN_TN), lambda j, i, src, cs: (i, j)),
        scratch_shapes=[pltpu.VMEM((kd, IN_TN), BF16)],
    )
    return pl.pallas_call(
        _mm_in_kernel,
        grid_spec=grid_spec,
        out_shape=jax.ShapeDtypeStruct((m, N_PROJ), F32),
        compiler_params=_cparams(("arbitrary", "arbitrary")),
        name="mm_in",
    )(src_blk, case, h, *([w_in_all] * (nb + 1)), w_tail)


def decoder_layer(x, prm, wts, past, swa_cache, chunks, li, depth, kv=(None, None), h=None, next_pre_w=None):
    b, l, _ = x.shape
    m = b * l
    tr, tm = _tiles(m)
    x2 = x.reshape(m, D_MODEL)
    c_gdn, c_rwkv, c_ssd = chunks

    if h is None:
        h = rms_cast(x2, prm['norm_mix_pre'], tr)
    proj = mm_in(h, wts['w_in_all'], li, wts['w_tail'], tm)
    proj3 = proj.reshape(b, l, N_PROJ)
    kbuf = matmul_into(h, wts['w_k'], kv[0], li, depth, tm, 512, "mm_k")
    vbuf = matmul_into(h, wts['w_v'], kv[1], li, depth, tm, 512, "mm_v")
    k4 = kbuf.reshape(depth, b, l, GROUP_W)
    v4 = vbuf.reshape(depth, b, l, GROUP_W)

    o_a, gdn_s = gdn_mixer(proj3, _front_pad_rows(past['gdn_conv']), past['gdn'], prm['gdn_conv_w'],
                           prm['gdn_A_log'], prm['gdn_dt_bias'], prm['gdn_norm_w'], c_gdn)
    shift = past['rwkv_shift'][:, None, :]
    sh_rkv8 = _front_pad_rows(shift[:, :, :3 * GROUP_W])
    sh_lora8 = _front_pad_rows(jnp.pad(shift[:, :, 3 * GROUP_W:], ((0, 0), (0, 0), (0, LORA_PAD - RWKV_LORA))))
    o_b, rwkv_s = rwkv_mixer(proj3, sh_rkv8, sh_lora8, past['rwkv'], prm, c_rwkv)
    o_c, ssm_s = ssd_mixer(proj3, _front_pad_rows(past['ssm_conv']), past['ssm'], prm, c_ssd)
    if swa_cache is None:
        o_d = swa_prompt(proj3, k4, v4, li, min(l, SWA_TILE))
    else:
        o_d = swa_sample(proj3, k4, v4, li, swa_cache[0], swa_cache[1])

    gdn_conv = proj3[:, l - (GDN_TAPS - 1):, COL_A_QKV:COL_A_QKV + 3 * GROUP_W]
    ssm_conv = proj3[:, l - (SSM_TAPS - 1):, COL_C_XBC:COL_C_XBC + SSM_XBC]
    rwkv_shift = jnp.concatenate([proj3[:, l - 1, COL_B_RKV:COL_B_RKV + 3 * GROUP_W],
                                  proj3[:, l - 1, COL_B_LORA:COL_B_LORA + RWKV_LORA]], axis=-1)

    y = matmul_groups([o.reshape(m, GROUP_W) for o in (o_a, o_b, o_c, o_d)], wts['w_out_all'], li, tm, 512,
                      "mm_out")
    x2, h2 = add_rms_cast(x2, y, prm['norm_mix_post'], prm['norm_ffn_pre'], tr)
    state8 = _front_pad_rows(past['ffn_conv'])
    if l % tm == 0:
        act, cst = ffn_up_act(h2, wts['w_up_all'], li, state8, prm['ffn_conv_w'], prm['ffn_conv_b'], l, tm, 256,
                              256)
        ffn_conv = cst[:, SUBLANE - (FFN_TAPS - 1):, :]
    else:
        up3 = matmul(h2, wts['w_up_all'], tm, 512, D_MODEL, "mm_up", li).reshape(b, l, 2 * D_FF)
        act = ffn_act(up3, state8, prm['ffn_conv_w'], prm['ffn_conv_b'], l, D_FF // 2).reshape(m, D_FF)
        ffn_conv = up3[:, l - (FFN_TAPS - 1):, :D_FF]
    y2 = matmul(act, wts['w_down_all'], tm, 512, D_FF // 2, "mm_down", li)
    if next_pre_w is None:
        x2, h_next = add_rms(x2, y2, prm['norm_ffn_post'], tr), None
    else:
        x2, h_next = add_rms_cast(x2, y2, prm['norm_ffn_post'], next_pre_w, tr)
    return x2.reshape(b, l, D_MODEL), h_next, (kbuf, vbuf), (gdn_s, gdn_conv, rwkv_s, rwkv_shift, ssm_s, ssm_conv,
                                                             ffn_conv)


def _zero_past(bsz):
    return {
        'gdn': jnp.zeros((bsz, GDN_HEADS, GDN_D, GDN_D), F32),
        'gdn_conv': jnp.zeros((bsz, GDN_TAPS - 1, 3 * GROUP_W), F32),
        'rwkv': jnp.zeros((bsz, RWKV_HEADS, RWKV_HS, RWKV_HS), F32),
        'rwkv_shift': jnp.zeros((bsz, 3 * GROUP_W + RWKV_LORA), F32),
        'ssm': jnp.zeros((bsz, SSM_HEADS, SSM_P, SSM_N), F32),
        'ssm_conv': jnp.zeros((bsz, SSM_TAPS - 1, SSM_XBC), F32),
        'ffn_conv': jnp.zeros((bsz, FFN_TAPS - 1, D_FF), F32),
    }


PARAM_NAMES = ('norm_mix_pre', 'norm_mix_post', 'norm_ffn_pre', 'norm_ffn_post', 'w_in', 'w_out', 'gdn_conv_w',
               'gdn_A_log', 'gdn_dt_bias', 'gdn_norm_w', 'rwkv_mu', 'rwkv_w0', 'rwkv_w2', 'rwkv_a0', 'rwkv_a2',
               'rwkv_g2', 'rwkv_k_k', 'rwkv_k_a', 'rwkv_r_k', 'rwkv_ln_w', 'rwkv_ln_b', 'ssm_conv_w', 'ssm_conv_b',
               'ssm_dt_bias', 'ssm_A_log', 'ssm_D', 'ssm_norm_w', 'ffn_w_up', 'ffn_conv_w', 'ffn_conv_b',
               'ffn_w_down')


def kernel(x_prompt, x_sample, state_gdn, state_gdn_conv, state_rwkv, state_rwkv_shift, state_ssm, state_ssm_conv, cache_swa_k, cache_swa_v, state_ffn_conv, norm_mix_pre, norm_mix_post, norm_ffn_pre, norm_ffn_post, w_in, w_out, gdn_conv_w, gdn_A_log, gdn_dt_bias, gdn_norm_w, rwkv_mu, rwkv_w0, rwkv_w2, rwkv_a0, rwkv_a2, rwkv_g2, rwkv_k_k, rwkv_k_a, rwkv_r_k, rwkv_ln_w, rwkv_ln_b, ssm_conv_w, ssm_conv_b, ssm_dt_bias, ssm_A_log, ssm_D, ssm_norm_w, ffn_w_up, ffn_conv_w, ffn_conv_b, ffn_w_down):
    params = dict(zip(PARAM_NAMES, (norm_mix_pre, norm_mix_post, norm_ffn_pre, norm_ffn_post, w_in, w_out,
                                    gdn_conv_w, gdn_A_log, gdn_dt_bias, gdn_norm_w, rwkv_mu, rwkv_w0, rwkv_w2,
                                    rwkv_a0, rwkv_a2, rwkv_g2, rwkv_k_k, rwkv_k_a, rwkv_r_k, rwkv_ln_w, rwkv_ln_b,
                                    ssm_conv_w, ssm_conv_b, ssm_dt_bias, ssm_A_log, ssm_D, ssm_norm_w, ffn_w_up,
                                    ffn_conv_w, ffn_conv_b, ffn_w_down)))
    depth = w_in.shape[0]
    xp, xs = x_prompt, x_sample
    t_dec = x_sample.shape[1]
    prompt_states, sample_states = [], []
    hp = hs = None
    kvp = kvs = (None, None)
    stacked = {'w_in_all': w_in, 'w_out_all': w_out, 'w_up_all': ffn_w_up, 'w_down_all': ffn_w_down.astype(BF16)}
    for li in range(depth):
        prm = {k: v[li] for k, v in params.items()}
        wts = {**prep_weights(prm), **stacked}
        nxt = norm_mix_pre[li + 1] if li + 1 < depth else None
        xp, hp, kvp, stp = decoder_layer(xp, prm, wts, _zero_past(xp.shape[0]), None, (64, 64, 128),
                                         li, depth, kvp, hp, nxt)
        past = {'gdn': state_gdn[li], 'gdn_conv': state_gdn_conv[li], 'rwkv': state_rwkv[li],
                'rwkv_shift': state_rwkv_shift[li], 'ssm': state_ssm[li], 'ssm_conv': state_ssm_conv[li],
                'ffn_conv': state_ffn_conv[li]}
        xs, hs, kvs, sts = decoder_layer(xs, prm, wts, past, (cache_swa_k, cache_swa_v),
                                         (t_dec, t_dec, t_dec), li, depth, kvs, hs, nxt)
        prompt_states.append(stp)
        sample_states.append(sts)

    def window_rows(buf, x):
        bsz, l = x.shape[0], x.shape[1]
        rows = buf.reshape(depth, bsz, l, SWA_HEADS, SWA_HD)
        return rows[:, :, max(l - SWA_MAX_WINDOW, 0):]

    def outputs(states, kv, x):
        st = [jnp.stack(t) for t in zip(*states)]
        return (*st[:6], window_rows(kv[0], x), window_rows(kv[1], x), st[6])

    return (xp, xs, *outputs(prompt_states, kvp, x_prompt), *outputs(sample_states, kvs, x_sample))
```

```python
import functools

import jax
import jax.numpy as jnp
from jax import lax
from jax.experimental import pallas as pl
from jax.experimental.pallas import tpu as pltpu

F32 = jnp.float32
BF16 = jnp.bfloat16
HI = lax.Precision.HIGHEST

D_MODEL = 4096
GROUP_W = D_MODEL // 4
GDN_HEADS = 8
GDN_D = GROUP_W // GDN_HEADS
GDN_TAPS = 4
RWKV_HS = 64
RWKV_HEADS = GROUP_W // RWKV_HS
RWKV_W_LORA = 64
RWKV_A_LORA = 64
RWKV_G_LORA = 160
RWKV_LORA = RWKV_W_LORA + RWKV_A_LORA + RWKV_G_LORA
RWKV_GN_EPS = 64e-5
SSM_P = 64
SSM_HEADS = GROUP_W // SSM_P
SSM_GROUPS = 2
SSM_N = 128
SSM_TAPS = 4
SSM_XBC = GROUP_W + 2 * SSM_GROUPS * SSM_N
SWA_HEADS = 8
SWA_HD = GROUP_W // SWA_HEADS
SWA_PATTERNS = ((128, 1), (512, 4), (2048, 16))
SWA_MAX_WINDOW = 2048
D_FF = 256 * ((8 * D_MODEL // 3 + 255) // 256)
FFN_TAPS = 3
NORM_EPS = 1e-6
NEG_INF = -1e30

LANE = 128
SUBLANE = 8
LORA_PAD = 384
G_LORA_PAD = LORA_PAD - RWKV_W_LORA - RWKV_A_LORA

COL_A_QKV = 0
COL_B_RKV = 3 * GROUP_W
COL_D_Q = 6 * GROUP_W
COL_A_Z = 7 * GROUP_W
COL_C_Z = 8 * GROUP_W
COL_C_XBC = 9 * GROUP_W
COL_B_LORA = COL_C_XBC + SSM_XBC
COL_SMALL = COL_B_LORA + LORA_PAD
N_PROJ = COL_SMALL + LANE
SM_GDN_B = 0
SM_GDN_A = GDN_HEADS
SM_SSM_DT = 2 * GDN_HEADS

VMEM_LIMIT = 56 * 1024 * 1024
SWA_TILE = 1024


def _cparams(sem):
    return pltpu.CompilerParams(dimension_semantics=sem, vmem_limit_bytes=VMEM_LIMIT)


def _dot(a, b, dims, prec):
    return lax.dot_general(a, b, (dims, ((), ())), precision=prec, preferred_element_type=F32)


def dot_nn(a, b, prec=HI):
    return _dot(a, b, ((1,), (0,)), prec)


def dot_nt(a, b, prec=HI):
    return _dot(a, b, ((1,), (1,)), prec)


def dot_tn(a, b, prec=HI):
    return _dot(a, b, ((0,), (0,)), prec)


def _silu(x):
    return x * jax.nn.sigmoid(x)


def _iota2(shape, axis):
    return lax.broadcasted_iota(jnp.int32, shape, axis)


def _log2(n):
    s = n.bit_length() - 1
    assert 1 << s == n
    return s


NN = ((1,), (0,))
NT = ((1,), (1,))
TN = ((0,), (0,))


def _split(x):
    hi = x.astype(BF16)
    return hi, (x - hi.astype(F32)).astype(BF16)


def _split_rows(x):
    hi = x.astype(BF16)
    hif = hi.astype(F32)
    return jnp.concatenate([hif, x - hif], axis=0).astype(BF16), hi


def dot3(ap, bp, dims):
    return (_dot(ap[0], bp[0], dims, None) + _dot(ap[0], bp[1], dims, None)
            + _dot(ap[1], bp[0], dims, None))


def dot3s(a, bp, dims):
    r = a.shape[0]
    stacked, hi = _split_rows(a)
    both = _dot(stacked, bp[0], dims, None)
    return both[:r] + both[r:] + _dot(hi, bp[1], dims, None)


def dot1(a, b, dims):
    return _dot(a.astype(BF16), b.astype(BF16), dims, None)


def _inv_unit_lower_multi(ms, c):
    row = _iota2((c, c), 0)
    col = _iota2((c, c), 1)
    eye = jnp.where(row == col, 1.0, 0.0).astype(F32)
    base = min(SUBLANE, c)
    sb = _log2(base)
    blk = (row >> sb) == (col >> sb)
    ps = [jnp.where(blk, -m, 0.0) for m in ms]
    ts = [eye + p for p in ps]
    if sb > 1:
        ps = [_dot(p.astype(BF16), p.astype(BF16), NN, None) for p in ps]
        for _ in range(sb - 2):
            both = [_dot(jnp.concatenate([t, p], axis=0).astype(BF16), p.astype(BF16), NN, None)
                    for t, p in zip(ts, ps)]
            ts = [t + x[:c] for t, x in zip(ts, both)]
            ps = [x[c:] for x in both]
        ts = [t + _dot(t.astype(BF16), p.astype(BF16), NN, None) for t, p in zip(ts, ps)]
    s = base
    while s < c:
        ls = _log2(s)
        off = ((row >> (ls + 1)) == (col >> (ls + 1))) & ((row >> ls) > (col >> ls))
        tbs = [t.astype(BF16) for t in ts]
        inner = [_dot(jnp.where(off, m, 0.0).astype(BF16), tb, NN, None) for m, tb in zip(ms, tbs)]
        ts = [t - _dot(tb, x.astype(BF16), NN, None) for t, tb, x in zip(ts, tbs, inner)]
        s *= 2
    res = [eye - t - dot3s(m, _split(t), NN) for m, t in zip(ms, ts)]
    return [t + _dot(t.astype(BF16), r.astype(BF16), NN, None) for t, r in zip(ts, res)]


def _blockdiag_rows(x, half):
    left = _iota2(x.shape, 1) < half
    return jnp.concatenate([jnp.where(left, x, 0.0), jnp.where(left, 0.0, x)], axis=0)


def _head_sums(x, ones_bd):
    r = x.shape[0]
    stacked, _ = _split_rows(x)
    both = _dot(stacked, ones_bd, NN, None)
    return both[:r] + both[r:]


def _inv_unit_lower_pairs(ms, c):
    row = _iota2((c, 2 * c), 0)
    col = _iota2((c, 2 * c), 1) & (c - 1)
    eye = jnp.where(row == col, 1.0, 0.0).astype(F32)
    base = min(SUBLANE, c)
    sb = _log2(base)
    blk = (row >> sb) == (col >> sb)

    def mm(a, b):
        return _dot(a.astype(BF16), _blockdiag_rows(b, c).astype(BF16), NN, None)

    ps = [jnp.where(blk, -m, 0.0) for m in ms]
    ts = [eye + p for p in ps]
    if sb > 1:
        ps = [mm(p, p) for p in ps]
        for _ in range(sb - 2):
            both = [mm(jnp.concatenate([t, p], axis=0), p) for t, p in zip(ts, ps)]
            ts = [t + x[:c] for t, x in zip(ts, both)]
            ps = [x[c:] for x in both]
        ts = [t + mm(t, p) for t, p in zip(ts, ps)]
    s = base
    while s < c:
        ls = _log2(s)
        off = ((row >> (ls + 1)) == (col >> (ls + 1))) & ((row >> ls) > (col >> ls))
        inner = [mm(jnp.where(off, m, 0.0), t) for m, t in zip(ms, ts)]
        ts = [t - mm(t, x) for t, x in zip(ts, inner)]
        s *= 2
    res = [eye - t - dot3s(m, _split(_blockdiag_rows(t, c)), NN) for m, t in zip(ms, ts)]
    return [t + mm(t, r) for t, r in zip(ts, res)]


def _row_getter(x, c):
    if c % LANE:
        x = jnp.concatenate([x, jnp.zeros((LANE - c % LANE, LANE), F32)], axis=0)
    xt = x.T
    return lambda lane: xt[lane:lane + 1, :c]


def _cumsum_rows(x, c):
    tri = jnp.where(_iota2((c, c), 0) >= _iota2((c, c), 1), 1.0, 0.0).astype(F32)
    return dot_nn(tri, x)


def _shifted_taps(tail, x, taps):
    c = x.shape[0]
    xp = jnp.concatenate([tail, x], axis=0)
    out = []
    for s in range(taps - 1, 0, -1):
        out.append(pltpu.roll(xp, s, 0)[SUBLANE:SUBLANE + c])
    out.append(x)
    return out


def _rms_cast_kernel(x_ref, w_ref, o_ref):
    x = x_ref[...]
    y = x * lax.rsqrt(jnp.mean(x * x, axis=-1, keepdims=True) + NORM_EPS)
    o_ref[...] = (y * w_ref[...]).astype(o_ref.dtype)


def rms_cast(x, w, tr):
    m, d = x.shape
    return pl.pallas_call(
        _rms_cast_kernel,
        grid=(m // tr,),
        in_specs=[pl.BlockSpec((tr, d), lambda i: (i, 0)), pl.BlockSpec((1, d), lambda i: (0, 0))],
        out_specs=pl.BlockSpec((tr, d), lambda i: (i, 0)),
        out_shape=jax.ShapeDtypeStruct((m, d), BF16),
        compiler_params=_cparams(("parallel",)),
        name="rms_cast",
    )(x, w.reshape(1, d))


def _add_rms_kernel(x_ref, y_ref, w_ref, o_ref):
    y = y_ref[...]
    yn = y * lax.rsqrt(jnp.mean(y * y, axis=-1, keepdims=True) + NORM_EPS)
    o_ref[...] = x_ref[...] + yn * w_ref[...]


def add_rms(x, y, w, tr):
    m, d = x.shape
    return pl.pallas_call(
        _add_rms_kernel,
        grid=(m // tr,),
        in_specs=[pl.BlockSpec((tr, d), lambda i: (i, 0)), pl.BlockSpec((tr, d), lambda i: (i, 0)),
                  pl.BlockSpec((1, d), lambda i: (0, 0))],
        out_specs=pl.BlockSpec((tr, d), lambda i: (i, 0)),
        out_shape=jax.ShapeDtypeStruct((m, d), F32),
        compiler_params=_cparams(("parallel",)),
        name="add_rms",
    )(x, y, w.reshape(1, d))


def _add_rms_cast_kernel(x_ref, y_ref, w_ref, wn_ref, o_ref, h_ref):
    y = y_ref[...]
    yn = y * lax.rsqrt(jnp.mean(y * y, axis=-1, keepdims=True) + NORM_EPS)
    x = x_ref[...] + yn * w_ref[...]
    o_ref[...] = x
    xn = x * lax.rsqrt(jnp.mean(x * x, axis=-1, keepdims=True) + NORM_EPS)
    h_ref[...] = (xn * wn_ref[...]).astype(h_ref.dtype)


def add_rms_cast(x, y, w, w_next, tr):
    m, d = x.shape
    row = pl.BlockSpec((tr, d), lambda i: (i, 0))
    vec = pl.BlockSpec((1, d), lambda i: (0, 0))
    return pl.pallas_call(
        _add_rms_cast_kernel,
        grid=(m // tr,),
        in_specs=[row, row, vec, vec],
        out_specs=[row, row],
        out_shape=[jax.ShapeDtypeStruct((m, d), F32), jax.ShapeDtypeStruct((m, d), BF16)],
        compiler_params=_cparams(("parallel",)),
        name="add_rms_cast",
    )(x, y, w.reshape(1, d), w_next.reshape(1, d))


def _mm_kernel(a_ref, w_ref, o_ref, *, nk):
    p = jnp.dot(a_ref[...], w_ref[...].astype(BF16), preferred_element_type=F32)
    if nk == 1:
        o_ref[...] = p
    else:
        k = pl.program_id(2)

        @pl.when(k == 0)
        def _():
            o_ref[...] = p

        @pl.when(k > 0)
        def _():
            o_ref[...] += p


def matmul(a, w, tm, tn, tk, name, li=None):
    m, kd = a.shape
    n = w.shape[-1]
    nk = kd // tk
    if li is None:
        w_spec = pl.BlockSpec((tk, tn), lambda i, j, k: (k, j))
    else:
        w_spec = pl.BlockSpec((None, tk, tn), lambda i, j, k: (li, k, j))
    return pl.pallas_call(
        functools.partial(_mm_kernel, nk=nk),
        grid=(m // tm, n // tn, nk),
        in_specs=[pl.BlockSpec((tm, tk), lambda i, j, k: (i, k)), w_spec],
        out_specs=pl.BlockSpec((tm, tn), lambda i, j, k: (i, j)),
        out_shape=jax.ShapeDtypeStruct((m, n), F32),
        compiler_params=_cparams(("parallel", "parallel", "arbitrary")),
        name=name,
    )(a, w)


def _mm_into_kernel(a_ref, w1_ref, w2_ref, *rest):
    o1_ref, o2_ref = rest[-2:]
    a = a_ref[...]
    o1_ref[...] = jnp.dot(a, w1_ref[...], preferred_element_type=F32)
    o2_ref[...] = jnp.dot(a, w2_ref[...], preferred_element_type=F32)


def matmul_into(a, w1, w2, bufs, li, depth, tm, tn, name):
    m, kd = a.shape
    n = w1.shape[1]
    w_spec = pl.BlockSpec((kd, tn), lambda i, j: (0, j))
    in_specs = [pl.BlockSpec((tm, kd), lambda i, j: (i, 0)), w_spec, w_spec]
    args = [a, w1, w2]
    aliases = {}
    if bufs[0] is not None:
        in_specs += [pl.BlockSpec(memory_space=pl.ANY)] * 2
        args += list(bufs)
        aliases = {3: 0, 4: 1}
    out_spec = pl.BlockSpec((None, tm, tn), lambda i, j: (li, i, j))
    return pl.pallas_call(
        _mm_into_kernel,
        grid=(m // tm, n // tn),
        in_specs=in_specs,
        out_specs=[out_spec, out_spec],
        out_shape=[jax.ShapeDtypeStruct((depth, m, n), F32)] * 2,
        input_output_aliases=aliases,
        compiler_params=_cparams(("parallel", "parallel")),
        name=name,
    )(*args)


def _mm_groups_kernel(*refs):
    *a_refs, w_ref, o_ref, w_scr = refs
    kg = a_refs[0].shape[1]

    @pl.when(pl.program_id(1) == 0)
    def _():
        w_scr[...] = w_ref[...].astype(BF16)

    acc = jnp.dot(a_refs[0][...], w_scr[0:kg, :], preferred_element_type=F32)
    for g in range(1, len(a_refs)):
        acc = acc + jnp.dot(a_refs[g][...], w_scr[g * kg:(g + 1) * kg, :], preferred_element_type=F32)
    o_ref[...] = acc


def matmul_groups(parts, w_all, li, tm, tn, name):
    m, kg = parts[0].shape
    _, kd, n = w_all.shape
    return pl.pallas_call(
        _mm_groups_kernel,
        grid=(n // tn, m // tm),
        in_specs=[pl.BlockSpec((tm, kg), lambda j, i: (i, 0)) for _ in parts]
        + [pl.BlockSpec((None, kd, tn), lambda j, i: (li, 0, j))],
        out_specs=pl.BlockSpec((tm, tn), lambda j, i: (i, j)),
        out_shape=jax.ShapeDtypeStruct((m, n), F32),
        scratch_shapes=[pltpu.VMEM((kd, tn), BF16)],
        compiler_params=_cparams(("arbitrary", "arbitrary")),
        name=name,
    )(*parts, w_all)


def _ffn_act_kernel(g_ref, v_ref, halo_ref, st_ref, cw_ref, cb_ref, o_ref):
    g = g_ref[...]
    tail = jnp.where(pl.program_id(1) == 0, st_ref[...], halo_ref[...])
    taps = _shifted_taps(tail, g, FFN_TAPS)
    cw = cw_ref[...]
    y = taps[0] * cw[0:1]
    for i in range(1, FFN_TAPS):
        y = y + taps[i] * cw[i:i + 1]
    y = y + cb_ref[...]
    o_ref[...] = (_silu(y) * v_ref[...]).astype(o_ref.dtype)


def _ffn_up_act_kernel(x_ref, wg_ref, wv_ref, st_ref, cw_ref, cb_ref, act_ref, cst_ref, wg_scr, wv_scr, tail_scr,
                       *, tiles_per_seq, sub):
    i = pl.program_id(1)

    @pl.when(i == 0)
    def _():
        wg_scr[...] = wg_ref[...].astype(BF16)
        wv_scr[...] = wv_ref[...].astype(BF16)

    @pl.when(i % tiles_per_seq == 0)
    def _():
        tail_scr[...] = st_ref[...]

    wg = wg_scr[...]
    wv = wv_scr[...]
    cw = cw_ref[...]
    cb = cb_ref[...]
    tm = x_ref.shape[0]
    tail = tail_scr[...]

    def project(r):
        x = x_ref[r * sub:(r + 1) * sub, :]
        return jnp.dot(x, wg, preferred_element_type=F32), jnp.dot(x, wv, preferred_element_type=F32)

    nxt = project(0)
    for r in range(tm // sub):
        g, v = nxt
        if r + 1 < tm // sub:
            nxt = project(r + 1)
        taps = _shifted_taps(tail, g, FFN_TAPS)
        y = taps[0] * cw[0:1]
        for t in range(1, FFN_TAPS):
            y = y + taps[t] * cw[t:t + 1]
        act_ref[r * sub:(r + 1) * sub, :] = (_silu(y + cb) * v).astype(act_ref.dtype)
        tail = g[sub - SUBLANE:]
    tail_scr[...] = tail
    cst_ref[...] = tail


def ffn_up_act(h2, w_up_all, li, state8, conv_w, conv_b, l, tm, tn, sub):
    m, kd = h2.shape
    b = m // l
    nj = D_FF // tn
    tiles_per_seq = l // tm
    act, tails = pl.pallas_call(
        functools.partial(_ffn_up_act_kernel, tiles_per_seq=tiles_per_seq, sub=sub),
        grid=(nj, m // tm),
        in_specs=[
            pl.BlockSpec((tm, kd), lambda j, i: (i, 0)),
            pl.BlockSpec((None, kd, tn), lambda j, i: (li, 0, j)),
            pl.BlockSpec((None, kd, tn), lambda j, i: (li, 0, j + nj)),
            pl.BlockSpec((None, SUBLANE, tn), lambda j, i: (i // tiles_per_seq, 0, j)),
            pl.BlockSpec((FFN_TAPS, tn), lambda j, i: (0, j)),
            pl.BlockSpec((1, tn), lambda j, i: (0, j)),
        ],
        out_specs=[
            pl.BlockSpec((tm, tn), lambda j, i: (i, j)),
            pl.BlockSpec((None, SUBLANE, tn), lambda j, i: (i, 0, j)),
        ],
        out_shape=[jax.ShapeDtypeStruct((m, D_FF), BF16), jax.ShapeDtypeStruct((m // tm, SUBLANE, D_FF), F32)],
        scratch_shapes=[pltpu.VMEM((kd, tn), BF16), pltpu.VMEM((kd, tn), BF16), pltpu.VMEM((SUBLANE, tn), F32)],
        compiler_params=_cparams(("arbitrary", "arbitrary")),
        name="ffn_up_act",
    )(h2, w_up_all, w_up_all, state8, conv_w, conv_b.reshape(1, D_FF))
    return act, tails.reshape(b, tiles_per_seq, SUBLANE, D_FF)[:, tiles_per_seq - 1]


def ffn_act(up3, state8, conv_w, conv_b, ts, tn):
    b, l, _ = up3.shape
    nj = D_FF // tn
    hb = ts // SUBLANE
    return pl.pallas_call(
        _ffn_act_kernel,
        grid=(b, l // ts, nj),
        in_specs=[
            pl.BlockSpec((None, ts, tn), lambda bi, i, j: (bi, i, j)),
            pl.BlockSpec((None, ts, tn), lambda bi, i, j: (bi, i, j + nj)),
            pl.BlockSpec((None, SUBLANE, tn), lambda bi, i, j: (bi, jnp.maximum(i * hb - 1, 0), j)),
            pl.BlockSpec((None, SUBLANE, tn), lambda bi, i, j: (bi, 0, j)),
            pl.BlockSpec((FFN_TAPS, tn), lambda bi, i, j: (0, j)),
            pl.BlockSpec((1, tn), lambda bi, i, j: (0, j)),
        ],
        out_specs=pl.BlockSpec((None, ts, tn), lambda bi, i, j: (bi, i, j)),
        out_shape=jax.ShapeDtypeStruct((b, l, D_FF), BF16),
        compiler_params=_cparams(("parallel", "parallel", "parallel")),
        name="ffn_act",
    )(up3, up3, up3, state8, conv_w, conv_b.reshape(1, D_FF))


def _gdn_kernel(qkv_ref, z_ref, sm_ref, cbuf_ref, s0_ref, cw_ref, alog_ref, dtb_ref, nw_ref,
                o_ref, sout_ref, s_scr, tail_scr, *, c, nsub):
    ci = pl.program_id(1)

    @pl.when(ci == 0)
    def _():
        s_scr[...] = s0_ref[...]
        tail_scr[...] = cbuf_ref[...]

    x = qkv_ref[...]
    taps = _shifted_taps(tail_scr[...], x, GDN_TAPS)
    rows = nsub * c
    tail_scr[...] = x[rows - SUBLANE:]
    cw = cw_ref[...]
    y = taps[0] * cw[0:1]
    for i in range(1, GDN_TAPS):
        y = y + taps[i] * cw[i:i + 1]
    y = _silu(y)

    sm = sm_ref[...]
    beta_all = jax.nn.sigmoid(sm)
    g_all = -jnp.exp(alog_ref[...]) * jax.nn.softplus(sm + dtb_ref[...])
    ri = _iota2((rows, rows), 0)
    rj = _iota2((rows, rows), 1)
    in_chunk_tri = jnp.where((ri >= rj) & ((ri >> _log2(c)) == (rj >> _log2(c))), 1.0, 0.0).astype(F32)
    gcum_all = dot_nn(in_chunk_tri, g_all)
    rows_of = _row_getter(gcum_all, rows)

    row = _iota2((c, c), 0)
    col = _iota2((c, c), 1)
    tri = row >= col
    strict = row > col
    z = z_ref[...]
    nw = nw_ref[...]
    heads = range(GDN_HEADS)
    units = [(q, h) for q in range(nsub) for h in heads]
    ms, aqk, rhs, qd, kd, gls = [], [], [], [], [], []
    for qi, h in units:
        rs = slice(qi * c, (qi + 1) * c)
        lo = h * GDN_D
        q = y[rs, lo:lo + GDN_D]
        k = y[rs, GROUP_W + lo:GROUP_W + lo + GDN_D]
        v = y[rs, 2 * GROUP_W + lo:2 * GROUP_W + lo + GDN_D]
        q = q * lax.rsqrt(jnp.sum(q * q, axis=-1, keepdims=True) + 1e-6) * (GDN_D ** -0.5)
        k = k * lax.rsqrt(jnp.sum(k * k, axis=-1, keepdims=True) + 1e-6)
        beta = beta_all[rs, SM_GDN_B + h:SM_GDN_B + h + 1]
        gc = gcum_all[rs, SM_GDN_A + h:SM_GDN_A + h + 1]
        gam = jnp.exp(jnp.where(tri, gc - rows_of(SM_GDN_A + h)[:, qi * c:(qi + 1) * c], -jnp.inf))
        kbeta = k * beta
        mq = dot3s(jnp.concatenate([kbeta, q], axis=0), _split(k), NT)
        ms.append(jnp.where(strict, mq[:c] * gam, 0.0))
        aqk.append(mq[c:] * gam)
        eg = jnp.exp(gc)
        gl = gc[c - 1:c]
        rhs.append(jnp.concatenate([kbeta * eg, v * beta], axis=1))
        qd.append(q * eg)
        kd.append(k * jnp.exp(gl - gc))
        gls.append(gl)
    packed = _inv_unit_lower_pairs([jnp.concatenate([ms[i], ms[i + 1]], axis=1) for i in range(0, len(ms), 2)], c)
    ts = [t[:, half * c:(half + 1) * c] for t in packed for half in range(2)]
    wu = [dot3s(t, _split(x), NN) for t, x in zip(ts, rhs)]
    state = [s_scr[h] for h in heads]
    for qi in range(nsub):
        rs = slice(qi * c, (qi + 1) * c)
        ix = [qi * GDN_HEADS + h for h in heads]
        wqs = [dot3s(jnp.concatenate([wu[i][:, :GDN_D], qd[i]], axis=0), _split(state[h]), NN)
               for h, i in zip(heads, ix)]
        v_new = [wu[i][:, GDN_D:] - wqs[h][:c] for h, i in zip(heads, ix)]
        outs = [wqs[h][c:] + dot1(aqk[i], v_new[h], NN) for h, i in zip(heads, ix)]
        state = [state[h] * jnp.exp(gls[i]) + dot3(_split(kd[i]), _split(v_new[h]), TN) for h, i in zip(heads, ix)]
        for h in heads:
            lo = h * GDN_D
            o = outs[h]
            o = o * lax.rsqrt(jnp.mean(o * o, axis=-1, keepdims=True) + NORM_EPS) * nw
            o = o * _silu(z[rs, lo:lo + GDN_D])
            o_ref[rs, lo:lo + GDN_D] = o.astype(o_ref.dtype)
    for h in heads:
        s_scr[h] = state[h]

    @pl.when(ci == pl.num_programs(1) - 1)
    def _():
        sout_ref[...] = s_scr[...]


def gdn_mixer(proj3, cbuf8, s0, conv_w, a_log, dt_bias, norm_w, c):
    b, l, _ = proj3.shape
    alog_row = jnp.zeros((1, LANE), F32).at[0, SM_GDN_A:SM_GDN_A + GDN_HEADS].set(a_log)
    dtb_row = jnp.zeros((1, LANE), F32).at[0, SM_GDN_A:SM_GDN_A + GDN_HEADS].set(dt_bias)
    w3 = 3 * GROUP_W
    const2 = lambda bi, ci: (0, 0)
    nsub = next(k for k in (4, 2, 1) if l % (k * c) == 0)
    rb = nsub * c
    return pl.pallas_call(
        functools.partial(_gdn_kernel, c=c, nsub=nsub),
        grid=(b, l // rb),
        in_specs=[
            pl.BlockSpec((None, rb, w3), lambda bi, ci: (bi, ci, COL_A_QKV // w3)),
            pl.BlockSpec((None, rb, GROUP_W), lambda bi, ci: (bi, ci, COL_A_Z // GROUP_W)),
            pl.BlockSpec((None, rb, LANE), lambda bi, ci: (bi, ci, COL_SMALL // LANE)),
            pl.BlockSpec((None, SUBLANE, w3), lambda bi, ci: (bi, 0, 0)),
            pl.BlockSpec((None, GDN_HEADS, GDN_D, GDN_D), lambda bi, ci: (bi, 0, 0, 0)),
            pl.BlockSpec((GDN_TAPS, w3), const2),
            pl.BlockSpec((1, LANE), const2),
            pl.BlockSpec((1, LANE), const2),
            pl.BlockSpec((1, GDN_D), const2),
        ],
        out_specs=[
            pl.BlockSpec((None, rb, GROUP_W), lambda bi, ci: (bi, ci, 0)),
            pl.BlockSpec((None, GDN_HEADS, GDN_D, GDN_D), lambda bi, ci: (bi, 0, 0, 0)),
        ],
        out_shape=[jax.ShapeDtypeStruct((b, l, GROUP_W), BF16),
                   jax.ShapeDtypeStruct((b, GDN_HEADS, GDN_D, GDN_D), F32)],
        scratch_shapes=[pltpu.VMEM((GDN_HEADS, GDN_D, GDN_D), F32), pltpu.VMEM((SUBLANE, w3), F32)],
        compiler_params=_cparams(("parallel", "arbitrary")),
        name="gdn_mixer",
    )(proj3, proj3, proj3, cbuf8, s0, conv_w, alog_row, dtb_row, norm_w.reshape(1, GDN_D))


def _rwkv_kernel(rkv_ref, lora_ref, sh_rkv_ref, sh_lora_ref, s0_ref, mu_rkv_ref, mu_lora_ref,
                 w0_ref, w2_ref, a0_ref, a2_ref, g2_ref, kk_ref, ka_ref, rk_ref, lnw_ref, lnb_ref,
                 o_ref, sout_ref, s_scr, tail_rkv, tail_lora, *, c, nsub):
    ci = pl.program_id(1)

    @pl.when(ci == 0)
    def _():
        s_scr[...] = s0_ref[...]
        tail_rkv[...] = sh_rkv_ref[...]
        tail_lora[...] = sh_lora_ref[...]

    x = rkv_ref[...]
    xl = lora_ref[...]
    prev = _shifted_taps(tail_rkv[...], x, 2)[0]
    prev_l = _shifted_taps(tail_lora[...], xl, 2)[0]
    tail_rkv[...] = x[nsub * c - SUBLANE:]
    tail_lora[...] = xl[nsub * c - SUBLANE:]
    zm = x + (prev - x) * mu_rkv_ref[...]
    zl = xl + (prev_l - xl) * mu_lora_ref[...]
    r = zm[:, 0:GROUP_W]
    k = zm[:, GROUP_W:2 * GROUP_W]
    v = zm[:, 2 * GROUP_W:3 * GROUP_W]
    wd = zl[:, 0:RWKV_W_LORA]
    ad = zl[:, RWKV_W_LORA:RWKV_W_LORA + RWKV_A_LORA]
    gd = zl[:, RWKV_W_LORA + RWKV_A_LORA:LORA_PAD]

    w_log = -jax.nn.softplus(-(w0_ref[...] + dot3s(jnp.tanh(wd), _split(w2_ref[...]), NN))) - 0.5
    logw = -jnp.exp(w_log)
    a = jax.nn.sigmoid(a0_ref[...] + dot3s(ad, _split(a2_ref[...]), NN))
    gate = dot1(jax.nn.sigmoid(gd), g2_ref[...], NN)
    kkv = k * kk_ref[...]
    k2 = k * (1.0 + (a - 1.0) * ka_ref[...])
    rows = nsub * c
    ri = _iota2((rows, rows), 0)
    rj = _iota2((rows, rows), 1)
    lc_sh = _log2(c)
    in_chunk_tri = jnp.where((ri >= rj) & ((ri >> lc_sh) == (rj >> lc_sh)), 1.0, 0.0).astype(F32)
    lcum = dot_nn(in_chunk_tri, logw)

    n = RWKV_HS
    pw = 2 * n
    rowc = _iota2((c, 2 * c), 0)
    colc = _iota2((c, 2 * c), 1) & (c - 1)
    strict2 = rowc > colc
    tri2 = rowc >= colc
    same_head = (_iota2((pw, pw), 0) < n) == (_iota2((pw, pw), 1) < n)
    ones_bd = jnp.where(same_head, 1.0, 0.0).astype(BF16)
    rk = rk_ref[...]
    lnw = lnw_ref[...]
    lnb = lnb_ref[...]
    pairs = range(RWKV_HEADS // 2)
    sl = [slice(p * pw, (p + 1) * pw) for p in pairs]
    units = [(slice(q * c, (q + 1) * c), sl[p]) for q in range(nsub) for p in pairs]
    kk_ss = [_head_sums(jnp.square(kkv[rs, ps]), ones_bd) for rs, ps in units]
    bonus_s = [_head_sums(r[rs, ps] * k2[rs, ps] * rk[:, ps], ones_bd) for rs, ps in units]
    lhs, x_bs, x_ks, bk_end, l_last = [], [], [], [], []
    for (rs, ps), ss in zip(units, kk_ss):
        kk = kkv[rs, ps] * lax.rsqrt(ss + 1e-6)
        lc = lcum[rs, ps]
        ll = lc[c - 1:c]
        p_inv = jnp.exp(-lc)
        a_t = -kk * jnp.exp(lc - logw[rs, ps])
        b_vec = kk * a[rs, ps]
        r_t = r[rs, ps] * jnp.exp(lc)
        p_end = jnp.exp(ll - lc)
        ar = jnp.concatenate([a_t, r_t], axis=0)
        x_bs.append(dot3s(ar, _split(_blockdiag_rows(b_vec * p_inv, n)), NT))
        x_ks.append(dot3s(ar, _split(_blockdiag_rows(k2[rs, ps] * p_inv, n)), NT))
        lhs.append(ar)
        bk_end.append(jnp.concatenate([b_vec * p_end, k2[rs, ps] * p_end], axis=0))
        l_last.append(ll)
    ms = [jnp.where(strict2, -x[:c], 0.0) for x in x_bs]
    av = [dot3s(jnp.where(strict2, x[:c], 0.0), _split(_blockdiag_rows(v[rs, ps], n)), NN)
          for x, (rs, ps) in zip(x_ks, units)]
    rbk = [jnp.concatenate([jnp.where(tri2, xb[c:], 0.0), jnp.where(tri2, xk[c:], 0.0)], axis=1)
           for xb, xk in zip(x_bs, x_ks)]
    ts = _inv_unit_lower_pairs(ms, c)
    state = [s_scr[p] for p in pairs]
    npair = len(pairs)
    for q in range(nsub):
        rs = slice(q * c, (q + 1) * c)
        ix = [q * npair + p for p in pairs]
        ars = [dot3s(lhs[i], _split(state[p]), NT) for p, i in zip(pairs, ix)]
        us = [dot3s(ts[i], _split(_blockdiag_rows(x[:c] + av[i], n)), NN) for x, i in zip(ars, ix)]
        vs = [v[rs, sl[p]] for p in pairs]
        ys = [ars[p][c:] + _dot(rbk[i].astype(BF16),
                                jnp.concatenate([_blockdiag_rows(us[p], n), _blockdiag_rows(vs[p], n)],
                                                axis=0).astype(BF16), NN, None)
              for p, i in zip(pairs, ix)]
        upd = [dot3(_split(jnp.concatenate([us[p], vs[p]], axis=0)), _split(bk_end[i]), TN)
               for p, i in zip(pairs, ix)]
        state = [jnp.where(same_head, state[p] * jnp.exp(l_last[i]) + upd[p], 0.0) for p, i in zip(pairs, ix)]
        devs = [y - _head_sums(y, ones_bd) * (1.0 / n) for y in ys]
        var = [_head_sums(jnp.square(d), ones_bd) * (1.0 / n) for d in devs]
        for p, i in zip(pairs, ix):
            yn = devs[p] * lax.rsqrt(var[p] + RWKV_GN_EPS) * lnw[:, sl[p]] + lnb[:, sl[p]]
            o_ref[rs, sl[p]] = ((yn + bonus_s[i] * vs[p]) * gate[rs, sl[p]]).astype(o_ref.dtype)
    for p in pairs:
        s_scr[p] = state[p]

    @pl.when(ci == pl.num_programs(1) - 1)
    def _():
        sout_ref[...] = s_scr[...]


def rwkv_mixer(proj3, sh_rkv8, sh_lora8, s0, prm, c):
    b, l, _ = proj3.shape
    nsub = next(k for k in (4, 2, 1) if l % (k * c) == 0)
    rb = nsub * c
    w3 = 3 * GROUP_W
    mu = prm['rwkv_mu']
    mu_rkv = mu[:w3].reshape(1, w3)
    mu_lora = jnp.pad(mu[w3:], (0, LORA_PAD - RWKV_LORA)).reshape(1, LORA_PAD)
    g2 = jnp.pad(prm['rwkv_g2'], ((0, G_LORA_PAD - RWKV_G_LORA), (0, 0)))
    row = lambda t: t.reshape(1, GROUP_W)
    const2 = lambda bi, ci: (0, 0)
    vec = pl.BlockSpec((1, GROUP_W), const2)
    n, npair = RWKV_HS, RWKV_HEADS // 2
    sp = s0.reshape(b, npair, 2, n, n)
    zero = jnp.zeros((b, npair, n, n), F32)
    s_pairs = jnp.concatenate([jnp.concatenate([sp[:, :, 0], zero], axis=-1),
                               jnp.concatenate([zero, sp[:, :, 1]], axis=-1)], axis=-2)
    state_spec = pl.BlockSpec((None, npair, 2 * n, 2 * n), lambda bi, ci: (bi, 0, 0, 0))
    o, s_new = pl.pallas_call(
        functools.partial(_rwkv_kernel, c=c, nsub=nsub),
        grid=(b, l // rb),
        in_specs=[
            pl.BlockSpec((None, rb, w3), lambda bi, ci: (bi, ci, COL_B_RKV // w3)),
            pl.BlockSpec((None, rb, LORA_PAD), lambda bi, ci: (bi, ci, COL_B_LORA // LORA_PAD)),
            pl.BlockSpec((None, SUBLANE, w3), lambda bi, ci: (bi, 0, 0)),
            pl.BlockSpec((None, SUBLANE, LORA_PAD), lambda bi, ci: (bi, 0, 0)),
            state_spec,
            pl.BlockSpec((1, w3), const2),
            pl.BlockSpec((1, LORA_PAD), const2),
            vec,
            pl.BlockSpec((RWKV_W_LORA, GROUP_W), const2),
            vec,
            pl.BlockSpec((RWKV_A_LORA, GROUP_W), const2),
            pl.BlockSpec((G_LORA_PAD, GROUP_W), const2),
            vec, vec, vec, vec, vec,
        ],
        out_specs=[
            pl.BlockSpec((None, rb, GROUP_W), lambda bi, ci: (bi, ci, 0)),
            state_spec,
        ],
        out_shape=[jax.ShapeDtypeStruct((b, l, GROUP_W), BF16),
                   jax.ShapeDtypeStruct((b, npair, 2 * n, 2 * n), F32)],
        scratch_shapes=[pltpu.VMEM((npair, 2 * n, 2 * n), F32),
                        pltpu.VMEM((SUBLANE, w3), F32), pltpu.VMEM((SUBLANE, LORA_PAD), F32)],
        compiler_params=_cparams(("parallel", "arbitrary")),
        name="rwkv_mixer",
    )(proj3, proj3, sh_rkv8, sh_lora8, s_pairs, mu_rkv, mu_lora,
      row(prm['rwkv_w0']), prm['rwkv_w2'], row(prm['rwkv_a0']), prm['rwkv_a2'], g2,
      row(prm['rwkv_k_k']), row(prm['rwkv_k_a']), row(prm['rwkv_r_k']),
      row(prm['rwkv_ln_w']), row(prm['rwkv_ln_b']))
    s_heads = jnp.stack([s_new[:, :, :n, :n], s_new[:, :, n:, n:]], axis=2)
    return o, s_heads.reshape(b, RWKV_HEADS, n, n)


def _ssd_kernel(z_ref, xbc_ref, sm_ref, cbuf_ref, s0_ref, cw_ref, cb_ref, dtb_ref, alog_ref, dsk_ref, nw_ref,
                o_ref, sout_ref, s_scr, tail_scr, y_scr, *, c):
    ci = pl.program_id(1)

    @pl.when(ci == 0)
    def _():
        s_scr[...] = s0_ref[...]
        tail_scr[...] = cbuf_ref[...]

    x = xbc_ref[...]
    taps = _shifted_taps(tail_scr[...], x, SSM_TAPS)
    tail_scr[...] = x[c - SUBLANE:]
    cw = cw_ref[...]
    y = taps[0] * cw[0:1]
    for i in range(1, SSM_TAPS):
        y = y + taps[i] * cw[i:i + 1]
    y = _silu(y + cb_ref[...])
    xs = y[:, 0:GROUP_W]
    gn = SSM_GROUPS * SSM_N

    sm = sm_ref[...]
    dt_all = jax.nn.softplus(sm + dtb_ref[...])
    da_all = dt_all * (-jnp.exp(alog_ref[...]))
    acs_all = _cumsum_rows(da_all, c)

    row = _iota2((c, c), 0)
    col = _iota2((c, c), 1)
    tri = row >= col
    z = z_ref[...]
    dsk = dsk_ref[...]
    hpg = SSM_HEADS // SSM_GROUPS
    gp = hpg * SSM_P
    rows_of = _row_getter(acs_all, c)
    for g in range(SSM_GROUPS):
        bm = y[:, GROUP_W + g * SSM_N:GROUP_W + (g + 1) * SSM_N]
        cm = y[:, GROUP_W + gn + g * SSM_N:GROUP_W + gn + (g + 1) * SSM_N]
        cb = dot1(cm, bm, NT)
        sg = s_scr[g]
        y_off = dot1(cm, sg, NT)
        xdec = []
        for rr in range(hpg):
            h = g * hpg + rr
            lo = h * SSM_P
            lane = SM_SSM_DT + h
            xs_h = xs[:, lo:lo + SSM_P]
            dt = dt_all[:, lane:lane + 1]
            acs = acs_all[:, lane:lane + 1]
            lmat = jnp.exp(jnp.where(tri, acs - rows_of(lane), -jnp.inf))
            xd = xs_h * dt
            a_last = acs[c - 1:c]
            xdec.append(xd * jnp.exp(a_last - acs))
            yh = dot1(cb * lmat, xd, NN) + y_off[:, rr * SSM_P:(rr + 1) * SSM_P] * jnp.exp(acs)
            yh = yh + xs_h * dsk[:, lane:lane + 1]
            y_scr[:, lo:lo + SSM_P] = yh * _silu(z[:, lo:lo + SSM_P])
        upd = dot3(_split(jnp.concatenate(xdec, axis=1)), _split(bm), TN)
        for rr in range(hpg):
            lane = SM_SSM_DT + g * hpg + rr
            dec = jnp.exp(acs_all[c - 1:c, lane:lane + 1])
            s_scr[g, rr * SSM_P:(rr + 1) * SSM_P, :] = (sg[rr * SSM_P:(rr + 1) * SSM_P] * dec
                                                         + upd[rr * SSM_P:(rr + 1) * SSM_P])

    gw = GROUP_W // SSM_GROUPS
    nw = nw_ref[...]
    for g in range(SSM_GROUPS):
        yg = y_scr[:, g * gw:(g + 1) * gw]
        yg = yg * lax.rsqrt(jnp.mean(yg * yg, axis=-1, keepdims=True) + NORM_EPS)
        o_ref[:, g * gw:(g + 1) * gw] = (yg * nw[:, g * gw:(g + 1) * gw]).astype(o_ref.dtype)

    @pl.when(ci == pl.num_programs(1) - 1)
    def _():
        sout_ref[...] = s_scr[...]


def ssd_mixer(proj3, cbuf8, s0, prm, c):
    b, l, _ = proj3.shape

    def lanes(t):
        return jnp.zeros((1, LANE), F32).at[0, SM_SSM_DT:SM_SSM_DT + SSM_HEADS].set(t)

    const2 = lambda bi, ci: (0, 0)
    small = pl.BlockSpec((1, LANE), const2)
    gshape = (SSM_GROUPS, SSM_HEADS // SSM_GROUPS * SSM_P, SSM_N)
    state_spec = pl.BlockSpec((None,) + gshape, lambda bi, ci: (bi, 0, 0, 0))
    o, s_new = pl.pallas_call(
        functools.partial(_ssd_kernel, c=c),
        grid=(b, l // c),
        in_specs=[
            pl.BlockSpec((None, c, GROUP_W), lambda bi, ci: (bi, ci, COL_C_Z // GROUP_W)),
            pl.BlockSpec((None, c, SSM_XBC), lambda bi, ci: (bi, ci, COL_C_XBC // SSM_XBC)),
            pl.BlockSpec((None, c, LANE), lambda bi, ci: (bi, ci, COL_SMALL // LANE)),
            pl.BlockSpec((None, SUBLANE, SSM_XBC), lambda bi, ci: (bi, 0, 0)),
            state_spec,
            pl.BlockSpec((SSM_TAPS, SSM_XBC), const2),
            pl.BlockSpec((1, SSM_XBC), const2),
            small, small, small,
            pl.BlockSpec((1, GROUP_W), const2),
        ],
        out_specs=[
            pl.BlockSpec((None, c, GROUP_W), lambda bi, ci: (bi, ci, 0)),
            state_spec,
        ],
        out_shape=[jax.ShapeDtypeStruct((b, l, GROUP_W), BF16),
                   jax.ShapeDtypeStruct((b,) + gshape, F32)],
        scratch_shapes=[pltpu.VMEM(gshape, F32), pltpu.VMEM((SUBLANE, SSM_XBC), F32),
                        pltpu.VMEM((c, GROUP_W), F32)],
        compiler_params=_cparams(("parallel", "arbitrary")),
        name="ssd_mixer",
    )(proj3, proj3, proj3, cbuf8, s0.reshape((b,) + gshape), prm['ssm_conv_w'],
      prm['ssm_conv_b'].reshape(1, SSM_XBC),
      lanes(prm['ssm_dt_bias']), lanes(prm['ssm_A_log']), lanes(prm['ssm_D']),
      prm['ssm_norm_w'].reshape(1, GROUP_W))
    return o, s_new.reshape(b, SSM_HEADS, SSM_P, SSM_N)


def _swa_weight(d):
    mult = jnp.zeros(d.shape, F32)
    for window, dil in SWA_PATTERNS:
        ok = (d >= 0) & (d <= window) & ((d & (dil - 1)) == 0)
        mult = mult + jnp.where(ok, 1.0, 0.0)
    return mult


def _swa_scores(q, k, d, slope):
    s = dot_nt(q.astype(BF16), k.astype(BF16), None) * (SWA_HD ** -0.5)
    mult = _swa_weight(d)
    s = s - slope * d.astype(F32)
    return jnp.where(mult > 0.0, s, NEG_INF), mult


def _swa_prompt_kernel(slopes_ref, q_ref, k_ref, v_ref, lw_ref, o_ref, m_scr, l_scr, acc_scr, *, t, sub):
    h = pl.program_id(1)
    qi = pl.program_id(2)
    ki = pl.program_id(3)

    @pl.when(ki == 0)
    def _():
        m_scr[...] = jnp.full(m_scr.shape, NEG_INF, F32)
        l_scr[...] = jnp.zeros(l_scr.shape, F32)
        acc_scr[...] = jnp.zeros(acc_scr.shape, F32)

    @pl.when(ki <= qi)
    def _():
        col = (ki * t + _iota2((1, t), 1)).astype(F32) * slopes_ref[h]
        kb = k_ref[...].astype(BF16)
        vb = v_ref[...].astype(BF16)
        def qk(r):
            q = (q_ref[r:r + sub, :] * (SWA_HD ** -0.5)).astype(BF16)
            return _dot(q, kb, NT, None)

        nxt = qk(0)
        for r in range(0, t, sub):
            rs = slice(r, r + sub)
            s = nxt + (lw_ref[qi - ki, rs, :] + col)
            if r + sub < t:
                nxt = qk(r + sub)
            m_old = m_scr[rs, :]
            m_new = jnp.maximum(m_old, jnp.max(s, axis=-1, keepdims=True))
            alpha = jnp.exp(m_old - m_new)
            p = jnp.exp(s - m_new)
            l_scr[rs, :] = alpha * l_scr[rs, :] + jnp.sum(p, axis=-1, keepdims=True)
            acc_scr[rs, :] = alpha * acc_scr[rs, :] + _dot(p.astype(BF16), vb, NN, None)
            m_scr[rs, :] = m_new

    @pl.when(ki == pl.num_programs(3) - 1)
    def _():
        o_ref[...] = (acc_scr[...] / l_scr[...]).astype(o_ref.dtype)


def _alibi_slopes():
    return jnp.asarray([2.0 ** (-8.0 * (i + 1) / SWA_HEADS) for i in range(SWA_HEADS)], F32)


def _swa_log_weight_tiles(n, t):
    d = (jnp.arange(n)[:, None, None] * t + jnp.arange(t)[None, :, None]) - jnp.arange(t)[None, None, :]
    mult = _swa_weight(d.astype(jnp.int32))
    return jnp.where(mult > 0.0, jnp.log(jnp.maximum(mult, 1.0)), NEG_INF)


def swa_prompt(proj3, k4, v4, li, t):
    b, l, _ = proj3.shape
    qc = COL_D_Q // SWA_HD
    n = l // t
    kv_spec = pl.BlockSpec((None, None, t, SWA_HD), lambda bi, h, qi, ki: (li, bi, jnp.minimum(ki, qi), h))

    return pl.pallas_call(
        functools.partial(_swa_prompt_kernel, t=t, sub=min(t, 256)),
        grid=(b, SWA_HEADS, n, n),
        in_specs=[
            pl.BlockSpec(memory_space=pltpu.SMEM),
            pl.BlockSpec((None, t, SWA_HD), lambda bi, h, qi, ki: (bi, qi, qc + h)),
            kv_spec,
            kv_spec,
            pl.BlockSpec((n, t, t), lambda bi, h, qi, ki: (0, 0, 0)),
        ],
        out_specs=pl.BlockSpec((None, t, SWA_HD), lambda bi, h, qi, ki: (bi, qi, h)),
        out_shape=jax.ShapeDtypeStruct((b, l, GROUP_W), BF16),
        scratch_shapes=[pltpu.VMEM((t, 1), F32), pltpu.VMEM((t, 1), F32), pltpu.VMEM((t, SWA_HD), F32)],
        compiler_params=_cparams(("parallel", "parallel", "parallel", "arbitrary")),
        name="swa_prompt",
    )(_alibi_slopes(), proj3, k4, v4, _swa_log_weight_tiles(n, t))


def _swa_sample_kernel(slopes_ref, q_ref, k_ref, v_ref, ck_ref, cv_ref, o_ref, *, t, wb):
    d_c = (wb + _iota2((t, wb), 0)) - _iota2((t, wb), 1)
    d_n = _iota2((t, t), 0) - _iota2((t, t), 1)
    for h in range(SWA_HEADS):
        hs = slice(h * SWA_HD, (h + 1) * SWA_HD)
        ck = ck_ref[pl.ds(h, wb, stride=SWA_HEADS), :]
        cv = cv_ref[pl.ds(h, wb, stride=SWA_HEADS), :]
        q = q_ref[:, hs]
        s_c, mult_c = _swa_scores(q, ck, d_c, slopes_ref[h])
        s_n, mult_n = _swa_scores(q, k_ref[:, hs], d_n, slopes_ref[h])
        m = jnp.maximum(jnp.max(s_c, axis=-1, keepdims=True), jnp.max(s_n, axis=-1, keepdims=True))
        p_c = jnp.exp(s_c - m) * mult_c
        p_n = jnp.exp(s_n - m) * mult_n
        den = jnp.sum(p_c, axis=-1, keepdims=True) + jnp.sum(p_n, axis=-1, keepdims=True)
        num = (dot_nn(p_c.astype(BF16), cv.astype(BF16), None)
               + dot_nn(p_n.astype(BF16), v_ref[:, hs].astype(BF16), None))
        o_ref[:, hs] = (num / den).astype(o_ref.dtype)


def swa_sample(proj3, k4, v4, li, cache_k_all, cache_v_all):
    b, t, _ = proj3.shape
    depth, _, wb = cache_k_all.shape[:3]
    ck = cache_k_all.reshape(depth, b, wb * SWA_HEADS, SWA_HD)
    cv = cache_v_all.reshape(depth, b, wb * SWA_HEADS, SWA_HD)
    new_spec = pl.BlockSpec((None, None, t, GROUP_W), lambda bi: (li, bi, 0, 0))
    cache_spec = pl.BlockSpec((None, None, wb * SWA_HEADS, SWA_HD), lambda bi: (li, bi, 0, 0))
    return pl.pallas_call(
        functools.partial(_swa_sample_kernel, t=t, wb=wb),
        grid=(b,),
        in_specs=[
            pl.BlockSpec(memory_space=pltpu.SMEM),
            pl.BlockSpec((None, t, GROUP_W), lambda bi: (bi, 0, COL_D_Q // GROUP_W)),
            new_spec,
            new_spec,
            cache_spec,
            cache_spec,
        ],
        out_specs=pl.BlockSpec((None, t, GROUP_W), lambda bi: (bi, 0, 0)),
        out_shape=jax.ShapeDtypeStruct((b, t, GROUP_W), BF16),
        compiler_params=_cparams(("parallel",)),
        name="swa_sample",
    )(_alibi_slopes(), proj3, k4, v4, ck, cv)


def _tiles(m):
    return (256, 1024) if m >= 1024 else (m, m)


def _front_pad_rows(t, rows=SUBLANE):
    return jnp.pad(t, ((0, 0), (rows - t.shape[1], 0), (0, 0)))


def prep_weights(p):
    w = p['w_in']
    o = 0
    seg = {}
    for name, n in (('a_qkv', 3 * GROUP_W), ('a_z', GROUP_W), ('a_b', GDN_HEADS), ('a_a', GDN_HEADS),
                    ('b_rkv', 3 * GROUP_W), ('b_lora', RWKV_LORA), ('c_z', GROUP_W), ('c_xbc', SSM_XBC),
                    ('c_dt', SSM_HEADS), ('d_q', GROUP_W), ('d_k', GROUP_W), ('d_v', GROUP_W)):
        seg[name] = w[:, o:o + n].astype(BF16)
        o += n
    zeros = lambda n: jnp.zeros((D_MODEL, n), BF16)
    small_used = 2 * GDN_HEADS + SSM_HEADS
    w_in = jnp.concatenate([
        seg['a_qkv'], seg['b_rkv'], seg['d_q'], seg['a_z'], seg['c_z'], seg['c_xbc'],
        seg['b_lora'], zeros(LORA_PAD - RWKV_LORA),
        seg['a_b'], seg['a_a'], seg['c_dt'], zeros(LANE - small_used)], axis=1)
    return {'w_in': w_in, 'w_k': seg['d_k'], 'w_v': seg['d_v']}


def decoder_layer(x, prm, wts, past, swa_cache, chunks, li, depth, kv=(None, None), h=None, next_pre_w=None):
    b, l, _ = x.shape
    m = b * l
    tr, tm = _tiles(m)
    x2 = x.reshape(m, D_MODEL)
    c_gdn, c_rwkv, c_ssd = chunks

    if h is None:
        h = rms_cast(x2, prm['norm_mix_pre'], tr)
    proj = matmul(h, wts['w_in'], tm, 512, D_MODEL, "mm_in")
    proj3 = proj.reshape(b, l, N_PROJ)
    kbuf, vbuf = matmul_into(h, wts['w_k'], wts['w_v'], kv, li, depth, tm, 512, "mm_kv")
    k4 = kbuf.reshape(depth, b, l, GROUP_W)
    v4 = vbuf.reshape(depth, b, l, GROUP_W)

    o_a, gdn_s = gdn_mixer(proj3, _front_pad_rows(past['gdn_conv']), past['gdn'], prm['gdn_conv_w'],
                           prm['gdn_A_log'], prm['gdn_dt_bias'], prm['gdn_norm_w'], c_gdn)
    shift = past['rwkv_shift'][:, None, :]
    sh_rkv8 = _front_pad_rows(shift[:, :, :3 * GROUP_W])
    sh_lora8 = _front_pad_rows(jnp.pad(shift[:, :, 3 * GROUP_W:], ((0, 0), (0, 0), (0, LORA_PAD - RWKV_LORA))))
    o_b, rwkv_s = rwkv_mixer(proj3, sh_rkv8, sh_lora8, past['rwkv'], prm, c_rwkv)
    o_c, ssm_s = ssd_mixer(proj3, _front_pad_rows(past['ssm_conv']), past['ssm'], prm, c_ssd)
    if swa_cache is None:
        o_d = swa_prompt(proj3, k4, v4, li, min(l, SWA_TILE))
    else:
        o_d = swa_sample(proj3, k4, v4, li, swa_cache[0], swa_cache[1])

    gdn_conv = proj3[:, l - (GDN_TAPS - 1):, COL_A_QKV:COL_A_QKV + 3 * GROUP_W]
    ssm_conv = proj3[:, l - (SSM_TAPS - 1):, COL_C_XBC:COL_C_XBC + SSM_XBC]
    rwkv_shift = jnp.concatenate([proj3[:, l - 1, COL_B_RKV:COL_B_RKV + 3 * GROUP_W],
                                  proj3[:, l - 1, COL_B_LORA:COL_B_LORA + RWKV_LORA]], axis=-1)

    y = matmul_groups([o.reshape(m, GROUP_W) for o in (o_a, o_b, o_c, o_d)], wts['w_out_all'], li, tm, 512,
                      "mm_out")
    x2, h2 = add_rms_cast(x2, y, prm['norm_mix_post'], prm['norm_ffn_pre'], tr)
    state8 = _front_pad_rows(past['ffn_conv'])
    if l % tm == 0:
        act, cst = ffn_up_act(h2, wts['w_up_all'], li, state8, prm['ffn_conv_w'], prm['ffn_conv_b'], l, tm, 256,
                              256)
        ffn_conv = cst[:, SUBLANE - (FFN_TAPS - 1):, :]
    else:
        up3 = matmul(h2, wts['w_up_all'], tm, 512, D_MODEL, "mm_up", li).reshape(b, l, 2 * D_FF)
        act = ffn_act(up3, state8, prm['ffn_conv_w'], prm['ffn_conv_b'], l, D_FF // 2).reshape(m, D_FF)
        ffn_conv = up3[:, l - (FFN_TAPS - 1):, :D_FF]
    y2 = matmul(act, wts['w_down_all'], tm, 512, D_FF // 2, "mm_down", li)
    if next_pre_w is None:
        x2, h_next = add_rms(x2, y2, prm['norm_ffn_post'], tr), None
    else:
        x2, h_next = add_rms_cast(x2, y2, prm['norm_ffn_post'], next_pre_w, tr)
    return x2.reshape(b, l, D_MODEL), h_next, (kbuf, vbuf), (gdn_s, gdn_conv, rwkv_s, rwkv_shift, ssm_s, ssm_conv,
                                                             ffn_conv)


def _zero_past(bsz):
    return {
        'gdn': jnp.zeros((bsz, GDN_HEADS, GDN_D, GDN_D), F32),
        'gdn_conv': jnp.zeros((bsz, GDN_TAPS - 1, 3 * GROUP_W), F32),
        'rwkv': jnp.zeros((bsz, RWKV_HEADS, RWKV_HS, RWKV_HS), F32),
        'rwkv_shift': jnp.zeros((bsz, 3 * GROUP_W + RWKV_LORA), F32),
        'ssm': jnp.zeros((bsz, SSM_HEADS, SSM_P, SSM_N), F32),
        'ssm_conv': jnp.zeros((bsz, SSM_TAPS - 1, SSM_XBC), F32),
        'ffn_conv': jnp.zeros((bsz, FFN_TAPS - 1, D_FF), F32),
    }


PARAM_NAMES = ('norm_mix_pre', 'norm_mix_post', 'norm_ffn_pre', 'norm_ffn_post', 'w_in', 'w_out', 'gdn_conv_w',
               'gdn_A_log', 'gdn_dt_bias', 'gdn_norm_w', 'rwkv_mu', 'rwkv_w0', 'rwkv_w2', 'rwkv_a0', 'rwkv_a2',
               'rwkv_g2', 'rwkv_k_k', 'rwkv_k_a', 'rwkv_r_k', 'rwkv_ln_w', 'rwkv_ln_b', 'ssm_conv_w', 'ssm_conv_b',
               'ssm_dt_bias', 'ssm_A_log', 'ssm_D', 'ssm_norm_w', 'ffn_w_up', 'ffn_conv_w', 'ffn_conv_b',
               'ffn_w_down')


def kernel(x_prompt, x_sample, state_gdn, state_gdn_conv, state_rwkv, state_rwkv_shift, state_ssm, state_ssm_conv, cache_swa_k, cache_swa_v, state_ffn_conv, norm_mix_pre, norm_mix_post, norm_ffn_pre, norm_ffn_post, w_in, w_out, gdn_conv_w, gdn_A_log, gdn_dt_bias, gdn_norm_w, rwkv_mu, rwkv_w0, rwkv_w2, rwkv_a0, rwkv_a2, rwkv_g2, rwkv_k_k, rwkv_k_a, rwkv_r_k, rwkv_ln_w, rwkv_ln_b, ssm_conv_w, ssm_conv_b, ssm_dt_bias, ssm_A_log, ssm_D, ssm_norm_w, ffn_w_up, ffn_conv_w, ffn_conv_b, ffn_w_down):
    params = dict(zip(PARAM_NAMES, (norm_mix_pre, norm_mix_post, norm_ffn_pre, norm_ffn_post, w_in, w_out,
                                    gdn_conv_w, gdn_A_log, gdn_dt_bias, gdn_norm_w, rwkv_mu, rwkv_w0, rwkv_w2,
                                    rwkv_a0, rwkv_a2, rwkv_g2, rwkv_k_k, rwkv_k_a, rwkv_r_k, rwkv_ln_w, rwkv_ln_b,
                                    ssm_conv_w, ssm_conv_b, ssm_dt_bias, ssm_A_log, ssm_D, ssm_norm_w, ffn_w_up,
                                    ffn_conv_w, ffn_conv_b, ffn_w_down)))
    depth = w_in.shape[0]
    xp, xs = x_prompt, x_sample
    t_dec = x_sample.shape[1]
    prompt_states, sample_states = [], []
    hp = hs = None
    kvp = kvs = (None, None)
    stacked = {'w_out_all': w_out, 'w_up_all': ffn_w_up, 'w_down_all': ffn_w_down.astype(BF16)}
    for li in range(depth):
        prm = {k: v[li] for k, v in params.items()}
        wts = {**prep_weights(prm), **stacked}
        nxt = norm_mix_pre[li + 1] if li + 1 < depth else None
        xp, hp, kvp, stp = decoder_layer(xp, prm, wts, _zero_past(xp.shape[0]), None, (64, 64, 128),
                                         li, depth, kvp, hp, nxt)
        past = {'gdn': state_gdn[li], 'gdn_conv': state_gdn_conv[li], 'rwkv': state_rwkv[li],
                'rwkv_shift': state_rwkv_shift[li], 'ssm': state_ssm[li], 'ssm_conv': state_ssm_conv[li],
                'ffn_conv': state_ffn_conv[li]}
        xs, hs, kvs, sts = decoder_layer(xs, prm, wts, past, (cache_swa_k, cache_swa_v),
                                         (t_dec, t_dec, t_dec), li, depth, kvs, hs, nxt)
        prompt_states.append(stp)
        sample_states.append(sts)

    def window_rows(buf, x):
        bsz, l = x.shape[0], x.shape[1]
        rows = buf.reshape(depth, bsz, l, SWA_HEADS, SWA_HD)
        return rows[:, :, max(l - SWA_MAX_WINDOW, 0):]

    def outputs(states, kv, x):
        st = [jnp.stack(t) for t in zip(*states)]
        return (*st[:6], window_rows(kv[0], x), window_rows(kv[1], x), st[6])

    return (xp, xs, *outputs(prompt_states, kvp, x_prompt), *outputs(sample_states, kvs, x_sample))
```

```python
import functools

import jax
import jax.numpy as jnp
from jax import lax
from jax.experimental import pallas as pl
from jax.experimental.pallas import tpu as pltpu

F32 = jnp.float32
BF16 = jnp.bfloat16
HI = lax.Precision.HIGHEST

D_MODEL = 4096
GROUP_W = D_MODEL // 4
GDN_HEADS = 8
GDN_D = GROUP_W // GDN_HEADS
GDN_TAPS = 4
RWKV_HS = 64
RWKV_HEADS = GROUP_W // RWKV_HS
RWKV_W_LORA = 64
RWKV_A_LORA = 64
RWKV_G_LORA = 160
RWKV_LORA = RWKV_W_LORA + RWKV_A_LORA + RWKV_G_LORA
RWKV_GN_EPS = 64e-5
SSM_P = 64
SSM_HEADS = GROUP_W // SSM_P
SSM_GROUPS = 2
SSM_N = 128
SSM_TAPS = 4
SSM_XBC = GROUP_W + 2 * SSM_GROUPS * SSM_N
SWA_HEADS = 8
SWA_HD = GROUP_W // SWA_HEADS
SWA_PATTERNS = ((128, 1), (512, 4), (2048, 16))
SWA_MAX_WINDOW = 2048
D_FF = 256 * ((8 * D_MODEL // 3 + 255) // 256)
FFN_TAPS = 3
NORM_EPS = 1e-6
NEG_INF = -1e30

LANE = 128
SUBLANE = 8
LORA_PAD = 384
G_LORA_PAD = LORA_PAD - RWKV_W_LORA - RWKV_A_LORA

COL_A_QKV = 0
COL_B_RKV = 3 * GROUP_W
COL_D_Q = 6 * GROUP_W
COL_A_Z = 7 * GROUP_W
COL_C_Z = 8 * GROUP_W
COL_C_XBC = 9 * GROUP_W
COL_B_LORA = COL_C_XBC + SSM_XBC
COL_SMALL = COL_B_LORA + LORA_PAD
N_PROJ = COL_SMALL + LANE
SM_GDN_B = 0
SM_GDN_A = GDN_HEADS
SM_SSM_DT = 2 * GDN_HEADS

VMEM_LIMIT = 56 * 1024 * 1024
SWA_TILE = 1024


def _cparams(sem):
    return pltpu.CompilerParams(dimension_semantics=sem, vmem_limit_bytes=VMEM_LIMIT)


def _dot(a, b, dims, prec):
    return lax.dot_general(a, b, (dims, ((), ())), precision=prec, preferred_element_type=F32)


def dot_nn(a, b, prec=HI):
    return _dot(a, b, ((1,), (0,)), prec)


def dot_nt(a, b, prec=HI):
    return _dot(a, b, ((1,), (1,)), prec)


def dot_tn(a, b, prec=HI):
    return _dot(a, b, ((0,), (0,)), prec)


def _silu(x):
    return x * jax.nn.sigmoid(x)


def _iota2(shape, axis):
    return lax.broadcasted_iota(jnp.int32, shape, axis)


def _log2(n):
    s = n.bit_length() - 1
    assert 1 << s == n
    return s


NN = ((1,), (0,))
NT = ((1,), (1,))
TN = ((0,), (0,))


def _split(x):
    hi = x.astype(BF16)
    return hi, (x - hi.astype(F32)).astype(BF16)


def _split_rows(x):
    hi = x.astype(BF16)
    hif = hi.astype(F32)
    return jnp.concatenate([hif, x - hif], axis=0).astype(BF16), hi


def dot3(ap, bp, dims):
    return (_dot(ap[0], bp[0], dims, None) + _dot(ap[0], bp[1], dims, None)
            + _dot(ap[1], bp[0], dims, None))


def dot3s(a, bp, dims):
    r = a.shape[0]
    stacked, hi = _split_rows(a)
    both = _dot(stacked, bp[0], dims, None)
    return both[:r] + both[r:] + _dot(hi, bp[1], dims, None)


def dot1(a, b, dims):
    return _dot(a.astype(BF16), b.astype(BF16), dims, None)


def _inv_unit_lower_multi(ms, c):
    row = _iota2((c, c), 0)
    col = _iota2((c, c), 1)
    eye = jnp.where(row == col, 1.0, 0.0).astype(F32)
    base = min(SUBLANE, c)
    sb = _log2(base)
    blk = (row >> sb) == (col >> sb)
    ps = [jnp.where(blk, -m, 0.0) for m in ms]
    ts = [eye + p for p in ps]
    if sb > 1:
        ps = [_dot(p.astype(BF16), p.astype(BF16), NN, None) for p in ps]
        for _ in range(sb - 2):
            both = [_dot(jnp.concatenate([t, p], axis=0).astype(BF16), p.astype(BF16), NN, None)
                    for t, p in zip(ts, ps)]
            ts = [t + x[:c] for t, x in zip(ts, both)]
            ps = [x[c:] for x in both]
        ts = [t + _dot(t.astype(BF16), p.astype(BF16), NN, None) for t, p in zip(ts, ps)]
    s = base
    while s < c:
        ls = _log2(s)
        off = ((row >> (ls + 1)) == (col >> (ls + 1))) & ((row >> ls) > (col >> ls))
        tbs = [t.astype(BF16) for t in ts]
        inner = [_dot(jnp.where(off, m, 0.0).astype(BF16), tb, NN, None) for m, tb in zip(ms, tbs)]
        ts = [t - _dot(tb, x.astype(BF16), NN, None) for t, tb, x in zip(ts, tbs, inner)]
        s *= 2
    res = [eye - t - dot3s(m, _split(t), NN) for m, t in zip(ms, ts)]
    return [t + _dot(t.astype(BF16), r.astype(BF16), NN, None) for t, r in zip(ts, res)]


def _blockdiag_rows(x, half):
    left = _iota2(x.shape, 1) < half
    return jnp.concatenate([jnp.where(left, x, 0.0), jnp.where(left, 0.0, x)], axis=0)


def _head_sums(x, ones_bd):
    r = x.shape[0]
    stacked, _ = _split_rows(x)
    both = _dot(stacked, ones_bd, NN, None)
    return both[:r] + both[r:]


def _inv_unit_lower_pairs(ms, c):
    row = _iota2((c, 2 * c), 0)
    col = _iota2((c, 2 * c), 1) & (c - 1)
    eye = jnp.where(row == col, 1.0, 0.0).astype(F32)
    base = min(SUBLANE, c)
    sb = _log2(base)
    blk = (row >> sb) == (col >> sb)

    def mm(a, b):
        return _dot(a.astype(BF16), _blockdiag_rows(b, c).astype(BF16), NN, None)

    ps = [jnp.where(blk, -m, 0.0) for m in ms]
    ts = [eye + p for p in ps]
    if sb > 1:
        ps = [mm(p, p) for p in ps]
        for _ in range(sb - 2):
            both = [mm(jnp.concatenate([t, p], axis=0), p) for t, p in zip(ts, ps)]
            ts = [t + x[:c] for t, x in zip(ts, both)]
            ps = [x[c:] for x in both]
        ts = [t + mm(t, p) for t, p in zip(ts, ps)]
    s = base
    while s < c:
        ls = _log2(s)
        off = ((row >> (ls + 1)) == (col >> (ls + 1))) & ((row >> ls) > (col >> ls))
        inner = [mm(jnp.where(off, m, 0.0), t) for m, t in zip(ms, ts)]
        ts = [t - mm(t, x) for t, x in zip(ts, inner)]
        s *= 2
    res = [eye - t - dot3s(m, _split(_blockdiag_rows(t, c)), NN) for m, t in zip(ms, ts)]
    return [t + mm(t, r) for t, r in zip(ts, res)]


def _row_getter(x, c):
    if c % LANE:
        x = jnp.concatenate([x, jnp.zeros((LANE - c % LANE, LANE), F32)], axis=0)
    xt = x.T
    return lambda lane: xt[lane:lane + 1, :c]


def _cumsum_rows(x, c):
    tri = jnp.where(_iota2((c, c), 0) >= _iota2((c, c), 1), 1.0, 0.0).astype(F32)
    return dot_nn(tri, x)


def _shifted_taps(tail, x, taps):
    c = x.shape[0]
    xp = jnp.concatenate([tail, x], axis=0)
    out = []
    for s in range(taps - 1, 0, -1):
        out.append(pltpu.roll(xp, s, 0)[SUBLANE:SUBLANE + c])
    out.append(x)
    return out


def _rms_cast_kernel(x_ref, w_ref, o_ref):
    x = x_ref[...]
    y = x * lax.rsqrt(jnp.mean(x * x, axis=-1, keepdims=True) + NORM_EPS)
    o_ref[...] = (y * w_ref[...]).astype(o_ref.dtype)


def rms_cast(x, w, tr):
    m, d = x.shape
    return pl.pallas_call(
        _rms_cast_kernel,
        grid=(m // tr,),
        in_specs=[pl.BlockSpec((tr, d), lambda i: (i, 0)), pl.BlockSpec((1, d), lambda i: (0, 0))],
        out_specs=pl.BlockSpec((tr, d), lambda i: (i, 0)),
        out_shape=jax.ShapeDtypeStruct((m, d), BF16),
        compiler_params=_cparams(("parallel",)),
        name="rms_cast",
    )(x, w.reshape(1, d))


def _add_rms_kernel(x_ref, y_ref, w_ref, o_ref):
    y = y_ref[...]
    yn = y * lax.rsqrt(jnp.mean(y * y, axis=-1, keepdims=True) + NORM_EPS)
    o_ref[...] = x_ref[...] + yn * w_ref[...]


def add_rms(x, y, w, tr):
    m, d = x.shape
    return pl.pallas_call(
        _add_rms_kernel,
        grid=(m // tr,),
        in_specs=[pl.BlockSpec((tr, d), lambda i: (i, 0)), pl.BlockSpec((tr, d), lambda i: (i, 0)),
                  pl.BlockSpec((1, d), lambda i: (0, 0))],
        out_specs=pl.BlockSpec((tr, d), lambda i: (i, 0)),
        out_shape=jax.ShapeDtypeStruct((m, d), F32),
        compiler_params=_cparams(("parallel",)),
        name="add_rms",
    )(x, y, w.reshape(1, d))


def _add_rms_cast_kernel(x_ref, y_ref, w_ref, wn_ref, o_ref, h_ref):
    y = y_ref[...]
    yn = y * lax.rsqrt(jnp.mean(y * y, axis=-1, keepdims=True) + NORM_EPS)
    x = x_ref[...] + yn * w_ref[...]
    o_ref[...] = x
    xn = x * lax.rsqrt(jnp.mean(x * x, axis=-1, keepdims=True) + NORM_EPS)
    h_ref[...] = (xn * wn_ref[...]).astype(h_ref.dtype)


def add_rms_cast(x, y, w, w_next, tr):
    m, d = x.shape
    row = pl.BlockSpec((tr, d), lambda i: (i, 0))
    vec = pl.BlockSpec((1, d), lambda i: (0, 0))
    return pl.pallas_call(
        _add_rms_cast_kernel,
        grid=(m // tr,),
        in_specs=[row, row, vec, vec],
        out_specs=[row, row],
        out_shape=[jax.ShapeDtypeStruct((m, d), F32), jax.ShapeDtypeStruct((m, d), BF16)],
        compiler_params=_cparams(("parallel",)),
        name="add_rms_cast",
    )(x, y, w.reshape(1, d), w_next.reshape(1, d))


def _mm_kernel(a_ref, w_ref, o_ref, *, nk):
    p = jnp.dot(a_ref[...], w_ref[...].astype(BF16), preferred_element_type=F32)
    if nk == 1:
        o_ref[...] = p
    else:
        k = pl.program_id(2)

        @pl.when(k == 0)
        def _():
            o_ref[...] = p

        @pl.when(k > 0)
        def _():
            o_ref[...] += p


def matmul(a, w, tm, tn, tk, name, li=None):
    m, kd = a.shape
    n = w.shape[-1]
    nk = kd // tk
    if li is None:
        w_spec = pl.BlockSpec((tk, tn), lambda i, j, k: (k, j))
    else:
        w_spec = pl.BlockSpec((None, tk, tn), lambda i, j, k: (li, k, j))
    return pl.pallas_call(
        functools.partial(_mm_kernel, nk=nk),
        grid=(m // tm, n // tn, nk),
        in_specs=[pl.BlockSpec((tm, tk), lambda i, j, k: (i, k)), w_spec],
        out_specs=pl.BlockSpec((tm, tn), lambda i, j, k: (i, j)),
        out_shape=jax.ShapeDtypeStruct((m, n), F32),
        compiler_params=_cparams(("parallel", "parallel", "arbitrary")),
        name=name,
    )(a, w)


def _mm_into_kernel(a_ref, w1_ref, w2_ref, *rest):
    o1_ref, o2_ref = rest[-2:]
    a = a_ref[...]
    o1_ref[...] = _dot(a, w1_ref[...], NT, None)
    o2_ref[...] = _dot(a, w2_ref[...], NT, None)


def matmul_into(a, w1, w2, bufs, li, depth, tm, tn, name):
    m, kd = a.shape
    n = w1.shape[0]
    w_spec = pl.BlockSpec((tn, kd), lambda i, j: (j, 0))
    in_specs = [pl.BlockSpec((tm, kd), lambda i, j: (i, 0)), w_spec, w_spec]
    args = [a, w1, w2]
    aliases = {}
    if bufs[0] is not None:
        in_specs += [pl.BlockSpec(memory_space=pl.ANY)] * 2
        args += list(bufs)
        aliases = {3: 0, 4: 1}
    out_spec = pl.BlockSpec((None, tm, tn), lambda i, j: (li, i, j))
    return pl.pallas_call(
        _mm_into_kernel,
        grid=(m // tm, n // tn),
        in_specs=in_specs,
        out_specs=[out_spec, out_spec],
        out_shape=[jax.ShapeDtypeStruct((depth, m, n), F32)] * 2,
        input_output_aliases=aliases,
        compiler_params=_cparams(("parallel", "parallel")),
        name=name,
    )(*args)


def _mm_groups_kernel(*refs):
    *a_refs, w_ref, o_ref, w_scr = refs
    kg = a_refs[0].shape[1]

    @pl.when(pl.program_id(1) == 0)
    def _():
        w_scr[...] = w_ref[...].astype(BF16)

    acc = jnp.dot(a_refs[0][...], w_scr[0:kg, :], preferred_element_type=F32)
    for g in range(1, len(a_refs)):
        acc = acc + jnp.dot(a_refs[g][...], w_scr[g * kg:(g + 1) * kg, :], preferred_element_type=F32)
    o_ref[...] = acc


def matmul_groups(parts, w_all, li, tm, tn, name):
    m, kg = parts[0].shape
    _, kd, n = w_all.shape
    return pl.pallas_call(
        _mm_groups_kernel,
        grid=(n // tn, m // tm),
        in_specs=[pl.BlockSpec((tm, kg), lambda j, i: (i, 0)) for _ in parts]
        + [pl.BlockSpec((None, kd, tn), lambda j, i: (li, 0, j))],
        out_specs=pl.BlockSpec((tm, tn), lambda j, i: (i, j)),
        out_shape=jax.ShapeDtypeStruct((m, n), F32),
        scratch_shapes=[pltpu.VMEM((kd, tn), BF16)],
        compiler_params=_cparams(("arbitrary", "arbitrary")),
        name=name,
    )(*parts, w_all)


def _ffn_act_kernel(g_ref, v_ref, halo_ref, st_ref, cw_ref, cb_ref, o_ref):
    g = g_ref[...]
    tail = jnp.where(pl.program_id(1) == 0, st_ref[...], halo_ref[...])
    taps = _shifted_taps(tail, g, FFN_TAPS)
    cw = cw_ref[...]
    y = taps[0] * cw[0:1]
    for i in range(1, FFN_TAPS):
        y = y + taps[i] * cw[i:i + 1]
    y = y + cb_ref[...]
    o_ref[...] = (_silu(y) * v_ref[...]).astype(o_ref.dtype)


def _ffn_up_act_kernel(x_ref, wg_ref, wv_ref, st_ref, cw_ref, cb_ref, act_ref, cst_ref, wg_scr, wv_scr, tail_scr,
                       *, tiles_per_seq, sub):
    i = pl.program_id(1)

    @pl.when(i == 0)
    def _():
        wg_scr[...] = wg_ref[...].astype(BF16)
        wv_scr[...] = wv_ref[...].astype(BF16)

    @pl.when(i % tiles_per_seq == 0)
    def _():
        tail_scr[...] = st_ref[...]

    wg = wg_scr[...]
    wv = wv_scr[...]
    cw = cw_ref[...]
    cb = cb_ref[...]
    tm = x_ref.shape[0]
    tail = tail_scr[...]

    def project(r):
        x = x_ref[r * sub:(r + 1) * sub, :]
        return jnp.dot(x, wg, preferred_element_type=F32), jnp.dot(x, wv, preferred_element_type=F32)

    nxt = project(0)
    for r in range(tm // sub):
        g, v = nxt
        if r + 1 < tm // sub:
            nxt = project(r + 1)
        taps = _shifted_taps(tail, g, FFN_TAPS)
        y = taps[0] * cw[0:1]
        for t in range(1, FFN_TAPS):
            y = y + taps[t] * cw[t:t + 1]
        act_ref[r * sub:(r + 1) * sub, :] = (_silu(y + cb) * v).astype(act_ref.dtype)
        tail = g[sub - SUBLANE:]
    tail_scr[...] = tail
    cst_ref[...] = tail


def ffn_up_act(h2, w_up_all, li, state8, conv_w, conv_b, l, tm, tn, sub):
    m, kd = h2.shape
    b = m // l
    nj = D_FF // tn
    tiles_per_seq = l // tm
    act, tails = pl.pallas_call(
        functools.partial(_ffn_up_act_kernel, tiles_per_seq=tiles_per_seq, sub=sub),
        grid=(nj, m // tm),
        in_specs=[
            pl.BlockSpec((tm, kd), lambda j, i: (i, 0)),
            pl.BlockSpec((None, kd, tn), lambda j, i: (li, 0, j)),
            pl.BlockSpec((None, kd, tn), lambda j, i: (li, 0, j + nj)),
            pl.BlockSpec((None, SUBLANE, tn), lambda j, i: (i // tiles_per_seq, 0, j)),
            pl.BlockSpec((FFN_TAPS, tn), lambda j, i: (0, j)),
            pl.BlockSpec((1, tn), lambda j, i: (0, j)),
        ],
        out_specs=[
            pl.BlockSpec((tm, tn), lambda j, i: (i, j)),
            pl.BlockSpec((None, SUBLANE, tn), lambda j, i: (i, 0, j)),
        ],
        out_shape=[jax.ShapeDtypeStruct((m, D_FF), BF16), jax.ShapeDtypeStruct((m // tm, SUBLANE, D_FF), F32)],
        scratch_shapes=[pltpu.VMEM((kd, tn), BF16), pltpu.VMEM((kd, tn), BF16), pltpu.VMEM((SUBLANE, tn), F32)],
        compiler_params=_cparams(("arbitrary", "arbitrary")),
        name="ffn_up_act",
    )(h2, w_up_all, w_up_all, state8, conv_w, conv_b.reshape(1, D_FF))
    return act, tails.reshape(b, tiles_per_seq, SUBLANE, D_FF)[:, tiles_per_seq - 1]


def ffn_act(up3, state8, conv_w, conv_b, ts, tn):
    b, l, _ = up3.shape
    nj = D_FF // tn
    hb = ts // SUBLANE
    return pl.pallas_call(
        _ffn_act_kernel,
        grid=(b, l // ts, nj),
        in_specs=[
            pl.BlockSpec((None, ts, tn), lambda bi, i, j: (bi, i, j)),
            pl.BlockSpec((None, ts, tn), lambda bi, i, j: (bi, i, j + nj)),
            pl.BlockSpec((None, SUBLANE, tn), lambda bi, i, j: (bi, jnp.maximum(i * hb - 1, 0), j)),
            pl.BlockSpec((None, SUBLANE, tn), lambda bi, i, j: (bi, 0, j)),
            pl.BlockSpec((FFN_TAPS, tn), lambda bi, i, j: (0, j)),
            pl.BlockSpec((1, tn), lambda bi, i, j: (0, j)),
        ],
        out_specs=pl.BlockSpec((None, ts, tn), lambda bi, i, j: (bi, i, j)),
        out_shape=jax.ShapeDtypeStruct((b, l, D_FF), BF16),
        compiler_params=_cparams(("parallel", "parallel", "parallel")),
        name="ffn_act",
    )(up3, up3, up3, state8, conv_w, conv_b.reshape(1, D_FF))


def _gdn_kernel(qkv_ref, z_ref, sm_ref, cbuf_ref, s0_ref, cw_ref, alog_ref, dtb_ref, nw_ref,
                o_ref, sout_ref, s_scr, tail_scr, *, c, nsub):
    ci = pl.program_id(1)

    @pl.when(ci == 0)
    def _():
        s_scr[...] = s0_ref[...]
        tail_scr[...] = cbuf_ref[...]

    x = qkv_ref[...]
    taps = _shifted_taps(tail_scr[...], x, GDN_TAPS)
    rows = nsub * c
    tail_scr[...] = x[rows - SUBLANE:]
    cw = cw_ref[...]
    y = taps[0] * cw[0:1]
    for i in range(1, GDN_TAPS):
        y = y + taps[i] * cw[i:i + 1]
    y = _silu(y)

    sm = sm_ref[...]
    beta_all = jax.nn.sigmoid(sm)
    g_all = -jnp.exp(alog_ref[...]) * jax.nn.softplus(sm + dtb_ref[...])
    ri = _iota2((rows, rows), 0)
    rj = _iota2((rows, rows), 1)
    in_chunk_tri = jnp.where((ri >= rj) & ((ri >> _log2(c)) == (rj >> _log2(c))), 1.0, 0.0).astype(F32)
    gcum_all = dot_nn(in_chunk_tri, g_all)
    rows_of = _row_getter(gcum_all, rows)

    row = _iota2((c, c), 0)
    col = _iota2((c, c), 1)
    tri = row >= col
    strict = row > col
    z = z_ref[...]
    nw = nw_ref[...]
    heads = range(GDN_HEADS)
    units = [(q, h) for q in range(nsub) for h in heads]
    ms, aqk, rhs, qd, kd, gls = [], [], [], [], [], []
    for qi, h in units:
        rs = slice(qi * c, (qi + 1) * c)
        lo = h * GDN_D
        q = y[rs, lo:lo + GDN_D]
        k = y[rs, GROUP_W + lo:GROUP_W + lo + GDN_D]
        v = y[rs, 2 * GROUP_W + lo:2 * GROUP_W + lo + GDN_D]
        q = q * lax.rsqrt(jnp.sum(q * q, axis=-1, keepdims=True) + 1e-6) * (GDN_D ** -0.5)
        k = k * lax.rsqrt(jnp.sum(k * k, axis=-1, keepdims=True) + 1e-6)
        beta = beta_all[rs, SM_GDN_B + h:SM_GDN_B + h + 1]
        gc = gcum_all[rs, SM_GDN_A + h:SM_GDN_A + h + 1]
        gam = jnp.exp(jnp.where(tri, gc - rows_of(SM_GDN_A + h)[:, qi * c:(qi + 1) * c], -jnp.inf))
        kbeta = k * beta
        mq = dot3s(jnp.concatenate([kbeta, q], axis=0), _split(k), NT)
        ms.append(jnp.where(strict, mq[:c] * gam, 0.0))
        aqk.append(mq[c:] * gam)
        eg = jnp.exp(gc)
        gl = gc[c - 1:c]
        rhs.append(jnp.concatenate([kbeta * eg, v * beta], axis=1))
        qd.append(q * eg)
        kd.append(k * jnp.exp(gl - gc))
        gls.append(gl)
    packed = _inv_unit_lower_pairs([jnp.concatenate([ms[i], ms[i + 1]], axis=1) for i in range(0, len(ms), 2)], c)
    ts = [t[:, half * c:(half + 1) * c] for t in packed for half in range(2)]
    wu = [dot3s(t, _split(x), NN) for t, x in zip(ts, rhs)]
    state = [s_scr[h] for h in heads]
    for qi in range(nsub):
        rs = slice(qi * c, (qi + 1) * c)
        ix = [qi * GDN_HEADS + h for h in heads]
        wqs = [dot3s(jnp.concatenate([wu[i][:, :GDN_D], qd[i]], axis=0), _split(state[h]), NN)
               for h, i in zip(heads, ix)]
        v_new = [wu[i][:, GDN_D:] - wqs[h][:c] for h, i in zip(heads, ix)]
        outs = [wqs[h][c:] + dot1(aqk[i], v_new[h], NN) for h, i in zip(heads, ix)]
        state = [state[h] * jnp.exp(gls[i]) + dot3(_split(kd[i]), _split(v_new[h]), TN) for h, i in zip(heads, ix)]
        for h in heads:
            lo = h * GDN_D
            o = outs[h]
            o = o * lax.rsqrt(jnp.mean(o * o, axis=-1, keepdims=True) + NORM_EPS) * nw
            o = o * _silu(z[rs, lo:lo + GDN_D])
            o_ref[rs, lo:lo + GDN_D] = o.astype(o_ref.dtype)
    for h in heads:
        s_scr[h] = state[h]

    @pl.when(ci == pl.num_programs(1) - 1)
    def _():
        sout_ref[...] = s_scr[...]


def gdn_mixer(proj3, cbuf8, s0, conv_w, a_log, dt_bias, norm_w, c):
    b, l, _ = proj3.shape
    alog_row = jnp.zeros((1, LANE), F32).at[0, SM_GDN_A:SM_GDN_A + GDN_HEADS].set(a_log)
    dtb_row = jnp.zeros((1, LANE), F32).at[0, SM_GDN_A:SM_GDN_A + GDN_HEADS].set(dt_bias)
    w3 = 3 * GROUP_W
    const2 = lambda bi, ci: (0, 0)
    nsub = next(k for k in (4, 2, 1) if l % (k * c) == 0)
    rb = nsub * c
    return pl.pallas_call(
        functools.partial(_gdn_kernel, c=c, nsub=nsub),
        grid=(b, l // rb),
        in_specs=[
            pl.BlockSpec((None, rb, w3), lambda bi, ci: (bi, ci, COL_A_QKV // w3)),
            pl.BlockSpec((None, rb, GROUP_W), lambda bi, ci: (bi, ci, COL_A_Z // GROUP_W)),
            pl.BlockSpec((None, rb, LANE), lambda bi, ci: (bi, ci, COL_SMALL // LANE)),
            pl.BlockSpec((None, SUBLANE, w3), lambda bi, ci: (bi, 0, 0)),
            pl.BlockSpec((None, GDN_HEADS, GDN_D, GDN_D), lambda bi, ci: (bi, 0, 0, 0)),
            pl.BlockSpec((GDN_TAPS, w3), const2),
            pl.BlockSpec((1, LANE), const2),
            pl.BlockSpec((1, LANE), const2),
            pl.BlockSpec((1, GDN_D), const2),
        ],
        out_specs=[
            pl.BlockSpec((None, rb, GROUP_W), lambda bi, ci: (bi, ci, 0)),
            pl.BlockSpec((None, GDN_HEADS, GDN_D, GDN_D), lambda bi, ci: (bi, 0, 0, 0)),
        ],
        out_shape=[jax.ShapeDtypeStruct((b, l, GROUP_W), BF16),
                   jax.ShapeDtypeStruct((b, GDN_HEADS, GDN_D, GDN_D), F32)],
        scratch_shapes=[pltpu.VMEM((GDN_HEADS, GDN_D, GDN_D), F32), pltpu.VMEM((SUBLANE, w3), F32)],
        compiler_params=_cparams(("parallel", "arbitrary")),
        name="gdn_mixer",
    )(proj3, proj3, proj3, cbuf8, s0, conv_w, alog_row, dtb_row, norm_w.reshape(1, GDN_D))


def _rwkv_kernel(rkv_ref, lora_ref, sh_rkv_ref, sh_lora_ref, s0_ref, mu_rkv_ref, mu_lora_ref,
                 w0_ref, w2_ref, a0_ref, a2_ref, g2_ref, kk_ref, ka_ref, rk_ref, lnw_ref, lnb_ref,
                 o_ref, sout_ref, s_scr, tail_rkv, tail_lora, *, c, nsub):
    ci = pl.program_id(1)

    @pl.when(ci == 0)
    def _():
        s_scr[...] = s0_ref[...]
        tail_rkv[...] = sh_rkv_ref[...]
        tail_lora[...] = sh_lora_ref[...]

    x = rkv_ref[...]
    xl = lora_ref[...]
    prev = _shifted_taps(tail_rkv[...], x, 2)[0]
    prev_l = _shifted_taps(tail_lora[...], xl, 2)[0]
    tail_rkv[...] = x[nsub * c - SUBLANE:]
    tail_lora[...] = xl[nsub * c - SUBLANE:]
    zm = x + (prev - x) * mu_rkv_ref[...]
    zl = xl + (prev_l - xl) * mu_lora_ref[...]
    r = zm[:, 0:GROUP_W]
    k = zm[:, GROUP_W:2 * GROUP_W]
    v = zm[:, 2 * GROUP_W:3 * GROUP_W]
    wd = zl[:, 0:RWKV_W_LORA]
    ad = zl[:, RWKV_W_LORA:RWKV_W_LORA + RWKV_A_LORA]
    gd = zl[:, RWKV_W_LORA + RWKV_A_LORA:LORA_PAD]

    w_log = -jax.nn.softplus(-(w0_ref[...] + dot3s(jnp.tanh(wd), _split(w2_ref[...]), NN))) - 0.5
    logw = -jnp.exp(w_log)
    a = jax.nn.sigmoid(a0_ref[...] + dot3s(ad, _split(a2_ref[...]), NN))
    gate = dot1(jax.nn.sigmoid(gd), g2_ref[...], NN)
    kkv = k * kk_ref[...]
    k2 = k * (1.0 + (a - 1.0) * ka_ref[...])
    rows = nsub * c
    ri = _iota2((rows, rows), 0)
    rj = _iota2((rows, rows), 1)
    lc_sh = _log2(c)
    in_chunk_tri = jnp.where((ri >= rj) & ((ri >> lc_sh) == (rj >> lc_sh)), 1.0, 0.0).astype(F32)
    lcum = dot_nn(in_chunk_tri, logw)

    n = RWKV_HS
    pw = 2 * n
    rowc = _iota2((c, 2 * c), 0)
    colc = _iota2((c, 2 * c), 1) & (c - 1)
    strict2 = rowc > colc
    tri2 = rowc >= colc
    same_head = (_iota2((pw, pw), 0) < n) == (_iota2((pw, pw), 1) < n)
    ones_bd = jnp.where(same_head, 1.0, 0.0).astype(BF16)
    rk = rk_ref[...]
    lnw = lnw_ref[...]
    lnb = lnb_ref[...]
    pairs = range(RWKV_HEADS // 2)
    sl = [slice(p * pw, (p + 1) * pw) for p in pairs]
    units = [(slice(q * c, (q + 1) * c), sl[p]) for q in range(nsub) for p in pairs]
    kk_ss = [_head_sums(jnp.square(kkv[rs, ps]), ones_bd) for rs, ps in units]
    bonus_s = [_head_sums(r[rs, ps] * k2[rs, ps] * rk[:, ps], ones_bd) for rs, ps in units]
    lhs, x_bs, x_ks, bk_end, l_last = [], [], [], [], []
    for (rs, ps), ss in zip(units, kk_ss):
        kk = kkv[rs, ps] * lax.rsqrt(ss + 1e-6)
        lc = lcum[rs, ps]
        ll = lc[c - 1:c]
        p_inv = jnp.exp(-lc)
        a_t = -kk * jnp.exp(lc - logw[rs, ps])
        b_vec = kk * a[rs, ps]
        r_t = r[rs, ps] * jnp.exp(lc)
        p_end = jnp.exp(ll - lc)
        ar = jnp.concatenate([a_t, r_t], axis=0)
        x_bs.append(dot3s(ar, _split(_blockdiag_rows(b_vec * p_inv, n)), NT))
        x_ks.append(dot3s(ar, _split(_blockdiag_rows(k2[rs, ps] * p_inv, n)), NT))
        lhs.append(ar)
        bk_end.append(jnp.concatenate([b_vec * p_end, k2[rs, ps] * p_end], axis=0))
        l_last.append(ll)
    ms = [jnp.where(strict2, -x[:c], 0.0) for x in x_bs]
    av = [dot3s(jnp.where(strict2, x[:c], 0.0), _split(_blockdiag_rows(v[rs, ps], n)), NN)
          for x, (rs, ps) in zip(x_ks, units)]
    rbk = [jnp.concatenate([jnp.where(tri2, xb[c:], 0.0), jnp.where(tri2, xk[c:], 0.0)], axis=1)
           for xb, xk in zip(x_bs, x_ks)]
    ts = _inv_unit_lower_pairs(ms, c)
    state = [s_scr[p] for p in pairs]
    npair = len(pairs)
    for q in range(nsub):
        rs = slice(q * c, (q + 1) * c)
        ix = [q * npair + p for p in pairs]
        ars = [dot3s(lhs[i], _split(state[p]), NT) for p, i in zip(pairs, ix)]
        us = [dot3s(ts[i], _split(_blockdiag_rows(x[:c] + av[i], n)), NN) for x, i in zip(ars, ix)]
        vs = [v[rs, sl[p]] for p in pairs]
        ys = [ars[p][c:] + _dot(rbk[i].astype(BF16),
                                jnp.concatenate([_blockdiag_rows(us[p], n), _blockdiag_rows(vs[p], n)],
                                                axis=0).astype(BF16), NN, None)
              for p, i in zip(pairs, ix)]
        upd = [dot3(_split(jnp.concatenate([us[p], vs[p]], axis=0)), _split(bk_end[i]), TN)
               for p, i in zip(pairs, ix)]
        state = [jnp.where(same_head, state[p] * jnp.exp(l_last[i]) + upd[p], 0.0) for p, i in zip(pairs, ix)]
        devs = [y - _head_sums(y, ones_bd) * (1.0 / n) for y in ys]
        var = [_head_sums(jnp.square(d), ones_bd) * (1.0 / n) for d in devs]
        for p, i in zip(pairs, ix):
            yn = devs[p] * lax.rsqrt(var[p] + RWKV_GN_EPS) * lnw[:, sl[p]] + lnb[:, sl[p]]
            o_ref[rs, sl[p]] = ((yn + bonus_s[i] * vs[p]) * gate[rs, sl[p]]).astype(o_ref.dtype)
    for p in pairs:
        s_scr[p] = state[p]

    @pl.when(ci == pl.num_programs(1) - 1)
    def _():
        sout_ref[...] = s_scr[...]


def rwkv_mixer(proj3, sh_rkv8, sh_lora8, s0, prm, c):
    b, l, _ = proj3.shape
    nsub = next(k for k in (4, 2, 1) if l % (k * c) == 0)
    rb = nsub * c
    w3 = 3 * GROUP_W
    mu = prm['rwkv_mu']
    mu_rkv = mu[:w3].reshape(1, w3)
    mu_lora = jnp.pad(mu[w3:], (0, LORA_PAD - RWKV_LORA)).reshape(1, LORA_PAD)
    g2 = jnp.pad(prm['rwkv_g2'], ((0, G_LORA_PAD - RWKV_G_LORA), (0, 0)))
    row = lambda t: t.reshape(1, GROUP_W)
    const2 = lambda bi, ci: (0, 0)
    vec = pl.BlockSpec((1, GROUP_W), const2)
    n, npair = RWKV_HS, RWKV_HEADS // 2
    sp = s0.reshape(b, npair, 2, n, n)
    zero = jnp.zeros((b, npair, n, n), F32)
    s_pairs = jnp.concatenate([jnp.concatenate([sp[:, :, 0], zero], axis=-1),
                               jnp.concatenate([zero, sp[:, :, 1]], axis=-1)], axis=-2)
    state_spec = pl.BlockSpec((None, npair, 2 * n, 2 * n), lambda bi, ci: (bi, 0, 0, 0))
    o, s_new = pl.pallas_call(
        functools.partial(_rwkv_kernel, c=c, nsub=nsub),
        grid=(b, l // rb),
        in_specs=[
            pl.BlockSpec((None, rb, w3), lambda bi, ci: (bi, ci, COL_B_RKV // w3)),
            pl.BlockSpec((None, rb, LORA_PAD), lambda bi, ci: (bi, ci, COL_B_LORA // LORA_PAD)),
            pl.BlockSpec((None, SUBLANE, w3), lambda bi, ci: (bi, 0, 0)),
            pl.BlockSpec((None, SUBLANE, LORA_PAD), lambda bi, ci: (bi, 0, 0)),
            state_spec,
            pl.BlockSpec((1, w3), const2),
            pl.BlockSpec((1, LORA_PAD), const2),
            vec,
            pl.BlockSpec((RWKV_W_LORA, GROUP_W), const2),
            vec,
            pl.BlockSpec((RWKV_A_LORA, GROUP_W), const2),
            pl.BlockSpec((G_LORA_PAD, GROUP_W), const2),
            vec, vec, vec, vec, vec,
        ],
        out_specs=[
            pl.BlockSpec((None, rb, GROUP_W), lambda bi, ci: (bi, ci, 0)),
            state_spec,
        ],
        out_shape=[jax.ShapeDtypeStruct((b, l, GROUP_W), BF16),
                   jax.ShapeDtypeStruct((b, npair, 2 * n, 2 * n), F32)],
        scratch_shapes=[pltpu.VMEM((npair, 2 * n, 2 * n), F32),
                        pltpu.VMEM((SUBLANE, w3), F32), pltpu.VMEM((SUBLANE, LORA_PAD), F32)],
        compiler_params=_cparams(("parallel", "arbitrary")),
        name="rwkv_mixer",
    )(proj3, proj3, sh_rkv8, sh_lora8, s_pairs, mu_rkv, mu_lora,
      row(prm['rwkv_w0']), prm['rwkv_w2'], row(prm['rwkv_a0']), prm['rwkv_a2'], g2,
      row(prm['rwkv_k_k']), row(prm['rwkv_k_a']), row(prm['rwkv_r_k']),
      row(prm['rwkv_ln_w']), row(prm['rwkv_ln_b']))
    s_heads = jnp.stack([s_new[:, :, :n, :n], s_new[:, :, n:, n:]], axis=2)
    return o, s_heads.reshape(b, RWKV_HEADS, n, n)


def _ssd_kernel(z_ref, xbc_ref, sm_ref, cbuf_ref, s0_ref, cw_ref, cb_ref, dtb_ref, alog_ref, dsk_ref, nw_ref,
                o_ref, sout_ref, s_scr, tail_scr, y_scr, *, c):
    ci = pl.program_id(1)

    @pl.when(ci == 0)
    def _():
        s_scr[...] = s0_ref[...]
        tail_scr[...] = cbuf_ref[...]

    x = xbc_ref[...]
    taps = _shifted_taps(tail_scr[...], x, SSM_TAPS)
    tail_scr[...] = x[c - SUBLANE:]
    cw = cw_ref[...]
    y = taps[0] * cw[0:1]
    for i in range(1, SSM_TAPS):
        y = y + taps[i] * cw[i:i + 1]
    y = _silu(y + cb_ref[...])
    xs = y[:, 0:GROUP_W]
    gn = SSM_GROUPS * SSM_N

    sm = sm_ref[...]
    dt_all = jax.nn.softplus(sm + dtb_ref[...])
    da_all = dt_all * (-jnp.exp(alog_ref[...]))
    acs_all = _cumsum_rows(da_all, c)

    row = _iota2((c, c), 0)
    col = _iota2((c, c), 1)
    tri = row >= col
    z = z_ref[...]
    dsk = dsk_ref[...]
    hpg = SSM_HEADS // SSM_GROUPS
    gp = hpg * SSM_P
    rows_of = _row_getter(acs_all, c)
    for g in range(SSM_GROUPS):
        bm = y[:, GROUP_W + g * SSM_N:GROUP_W + (g + 1) * SSM_N]
        cm = y[:, GROUP_W + gn + g * SSM_N:GROUP_W + gn + (g + 1) * SSM_N]
        cb = dot1(cm, bm, NT)
        sg = s_scr[g]
        y_off = dot1(cm, sg, NT)
        xdec = []
        for rr in range(hpg):
            h = g * hpg + rr
            lo = h * SSM_P
            lane = SM_SSM_DT + h
            xs_h = xs[:, lo:lo + SSM_P]
            dt = dt_all[:, lane:lane + 1]
            acs = acs_all[:, lane:lane + 1]
            lmat = jnp.exp(jnp.where(tri, acs - rows_of(lane), -jnp.inf))
            xd = xs_h * dt
            a_last = acs[c - 1:c]
            xdec.append(xd * jnp.exp(a_last - acs))
            yh = dot1(cb * lmat, xd, NN) + y_off[:, rr * SSM_P:(rr + 1) * SSM_P] * jnp.exp(acs)
            yh = yh + xs_h * dsk[:, lane:lane + 1]
            y_scr[:, lo:lo + SSM_P] = yh * _silu(z[:, lo:lo + SSM_P])
        upd = dot3(_split(jnp.concatenate(xdec, axis=1)), _split(bm), TN)
        for rr in range(hpg):
            lane = SM_SSM_DT + g * hpg + rr
            dec = jnp.exp(acs_all[c - 1:c, lane:lane + 1])
            s_scr[g, rr * SSM_P:(rr + 1) * SSM_P, :] = (sg[rr * SSM_P:(rr + 1) * SSM_P] * dec
                                                         + upd[rr * SSM_P:(rr + 1) * SSM_P])

    gw = GROUP_W // SSM_GROUPS
    nw = nw_ref[...]
    for g in range(SSM_GROUPS):
        yg = y_scr[:, g * gw:(g + 1) * gw]
        yg = yg * lax.rsqrt(jnp.mean(yg * yg, axis=-1, keepdims=True) + NORM_EPS)
        o_ref[:, g * gw:(g + 1) * gw] = (yg * nw[:, g * gw:(g + 1) * gw]).astype(o_ref.dtype)

    @pl.when(ci == pl.num_programs(1) - 1)
    def _():
        sout_ref[...] = s_scr[...]


def ssd_mixer(proj3, cbuf8, s0, prm, c):
    b, l, _ = proj3.shape

    def lanes(t):
        return jnp.zeros((1, LANE), F32).at[0, SM_SSM_DT:SM_SSM_DT + SSM_HEADS].set(t)

    const2 = lambda bi, ci: (0, 0)
    small = pl.BlockSpec((1, LANE), const2)
    gshape = (SSM_GROUPS, SSM_HEADS // SSM_GROUPS * SSM_P, SSM_N)
    state_spec = pl.BlockSpec((None,) + gshape, lambda bi, ci: (bi, 0, 0, 0))
    o, s_new = pl.pallas_call(
        functools.partial(_ssd_kernel, c=c),
        grid=(b, l // c),
        in_specs=[
            pl.BlockSpec((None, c, GROUP_W), lambda bi, ci: (bi, ci, COL_C_Z // GROUP_W)),
            pl.BlockSpec((None, c, SSM_XBC), lambda bi, ci: (bi, ci, COL_C_XBC // SSM_XBC)),
            pl.BlockSpec((None, c, LANE), lambda bi, ci: (bi, ci, COL_SMALL // LANE)),
            pl.BlockSpec((None, SUBLANE, SSM_XBC), lambda bi, ci: (bi, 0, 0)),
            state_spec,
            pl.BlockSpec((SSM_TAPS, SSM_XBC), const2),
            pl.BlockSpec((1, SSM_XBC), const2),
            small, small, small,
            pl.BlockSpec((1, GROUP_W), const2),
        ],
        out_specs=[
            pl.BlockSpec((None, c, GROUP_W), lambda bi, ci: (bi, ci, 0)),
            state_spec,
        ],
        out_shape=[jax.ShapeDtypeStruct((b, l, GROUP_W), BF16),
                   jax.ShapeDtypeStruct((b,) + gshape, F32)],
        scratch_shapes=[pltpu.VMEM(gshape, F32), pltpu.VMEM((SUBLANE, SSM_XBC), F32),
                        pltpu.VMEM((c, GROUP_W), F32)],
        compiler_params=_cparams(("parallel", "arbitrary")),
        name="ssd_mixer",
    )(proj3, proj3, proj3, cbuf8, s0.reshape((b,) + gshape), prm['ssm_conv_w'],
      prm['ssm_conv_b'].reshape(1, SSM_XBC),
      lanes(prm['ssm_dt_bias']), lanes(prm['ssm_A_log']), lanes(prm['ssm_D']),
      prm['ssm_norm_w'].reshape(1, GROUP_W))
    return o, s_new.reshape(b, SSM_HEADS, SSM_P, SSM_N)


def _swa_weight(d):
    mult = jnp.zeros(d.shape, F32)
    for window, dil in SWA_PATTERNS:
        ok = (d >= 0) & (d <= window) & ((d & (dil - 1)) == 0)
        mult = mult + jnp.where(ok, 1.0, 0.0)
    return mult


def _swa_scores(q, k, d, slope):
    s = dot_nt(q.astype(BF16), k.astype(BF16), None) * (SWA_HD ** -0.5)
    mult = _swa_weight(d)
    s = s - slope * d.astype(F32)
    return jnp.where(mult > 0.0, s, NEG_INF), mult


def _swa_prompt_kernel(slopes_ref, q_ref, k_ref, v_ref, lw_ref, o_ref, m_scr, l_scr, acc_scr, *, t, sub):
    h = pl.program_id(1)
    qi = pl.program_id(2)
    ki = pl.program_id(3)

    @pl.when(ki == 0)
    def _():
        m_scr[...] = jnp.full(m_scr.shape, NEG_INF, F32)
        l_scr[...] = jnp.zeros(l_scr.shape, F32)
        acc_scr[...] = jnp.zeros(acc_scr.shape, F32)

    @pl.when(ki <= qi)
    def _():
        col = (ki * t + _iota2((1, t), 1)).astype(F32) * slopes_ref[h]
        kb = k_ref[...].astype(BF16)
        vb = v_ref[...].astype(BF16)
        def qk(r):
            q = (q_ref[r:r + sub, :] * (SWA_HD ** -0.5)).astype(BF16)
            return _dot(q, kb, NT, None)

        nxt = qk(0)
        for r in range(0, t, sub):
            rs = slice(r, r + sub)
            s = nxt + (lw_ref[qi - ki, rs, :] + col)
            if r + sub < t:
                nxt = qk(r + sub)
            m_old = m_scr[rs, :]
            m_new = jnp.maximum(m_old, jnp.max(s, axis=-1, keepdims=True))
            alpha = jnp.exp(m_old - m_new)
            p = jnp.exp(s - m_new)
            l_scr[rs, :] = alpha * l_scr[rs, :] + jnp.sum(p, axis=-1, keepdims=True)
            acc_scr[rs, :] = alpha * acc_scr[rs, :] + _dot(p.astype(BF16), vb, NN, None)
            m_scr[rs, :] = m_new

    @pl.when(ki == pl.num_programs(3) - 1)
    def _():
        o_ref[...] = (acc_scr[...] / l_scr[...]).astype(o_ref.dtype)


def _alibi_slopes():
    return jnp.asarray([2.0 ** (-8.0 * (i + 1) / SWA_HEADS) for i in range(SWA_HEADS)], F32)


def _swa_log_weight_tiles(n, t):
    d = (jnp.arange(n)[:, None, None] * t + jnp.arange(t)[None, :, None]) - jnp.arange(t)[None, None, :]
    mult = _swa_weight(d.astype(jnp.int32))
    return jnp.where(mult > 0.0, jnp.log(jnp.maximum(mult, 1.0)), NEG_INF)


def swa_prompt(proj3, k4, v4, li, t):
    b, l, _ = proj3.shape
    qc = COL_D_Q // SWA_HD
    n = l // t
    kv_spec = pl.BlockSpec((None, None, t, SWA_HD), lambda bi, h, qi, ki: (li, bi, jnp.minimum(ki, qi), h))

    return pl.pallas_call(
        functools.partial(_swa_prompt_kernel, t=t, sub=min(t, 256)),
        grid=(b, SWA_HEADS, n, n),
        in_specs=[
            pl.BlockSpec(memory_space=pltpu.SMEM),
            pl.BlockSpec((None, t, SWA_HD), lambda bi, h, qi, ki: (bi, qi, qc + h)),
            kv_spec,
            kv_spec,
            pl.BlockSpec((n, t, t), lambda bi, h, qi, ki: (0, 0, 0)),
        ],
        out_specs=pl.BlockSpec((None, t, SWA_HD), lambda bi, h, qi, ki: (bi, qi, h)),
        out_shape=jax.ShapeDtypeStruct((b, l, GROUP_W), BF16),
        scratch_shapes=[pltpu.VMEM((t, 1), F32), pltpu.VMEM((t, 1), F32), pltpu.VMEM((t, SWA_HD), F32)],
        compiler_params=_cparams(("parallel", "parallel", "parallel", "arbitrary")),
        name="swa_prompt",
    )(_alibi_slopes(), proj3, k4, v4, _swa_log_weight_tiles(n, t))


def _swa_sample_kernel(slopes_ref, q_ref, k_ref, v_ref, ck_ref, cv_ref, o_ref, *, t, wb):
    d_c = (wb + _iota2((t, wb), 0)) - _iota2((t, wb), 1)
    d_n = _iota2((t, t), 0) - _iota2((t, t), 1)
    for h in range(SWA_HEADS):
        hs = slice(h * SWA_HD, (h + 1) * SWA_HD)
        ck = ck_ref[pl.ds(h, wb, stride=SWA_HEADS), :]
        cv = cv_ref[pl.ds(h, wb, stride=SWA_HEADS), :]
        q = q_ref[:, hs]
        s_c, mult_c = _swa_scores(q, ck, d_c, slopes_ref[h])
        s_n, mult_n = _swa_scores(q, k_ref[:, hs], d_n, slopes_ref[h])
        m = jnp.maximum(jnp.max(s_c, axis=-1, keepdims=True), jnp.max(s_n, axis=-1, keepdims=True))
        p_c = jnp.exp(s_c - m) * mult_c
        p_n = jnp.exp(s_n - m) * mult_n
        den = jnp.sum(p_c, axis=-1, keepdims=True) + jnp.sum(p_n, axis=-1, keepdims=True)
        num = (dot_nn(p_c.astype(BF16), cv.astype(BF16), None)
               + dot_nn(p_n.astype(BF16), v_ref[:, hs].astype(BF16), None))
        o_ref[:, hs] = (num / den).astype(o_ref.dtype)


def swa_sample(proj3, k4, v4, li, cache_k_all, cache_v_all):
    b, t, _ = proj3.shape
    depth, _, wb = cache_k_all.shape[:3]
    ck = cache_k_all.reshape(depth, b, wb * SWA_HEADS, SWA_HD)
    cv = cache_v_all.reshape(depth, b, wb * SWA_HEADS, SWA_HD)
    new_spec = pl.BlockSpec((None, None, t, GROUP_W), lambda bi: (li, bi, 0, 0))
    cache_spec = pl.BlockSpec((None, None, wb * SWA_HEADS, SWA_HD), lambda bi: (li, bi, 0, 0))
    return pl.pallas_call(
        functools.partial(_swa_sample_kernel, t=t, wb=wb),
        grid=(b,),
        in_specs=[
            pl.BlockSpec(memory_space=pltpu.SMEM),
            pl.BlockSpec((None, t, GROUP_W), lambda bi: (bi, 0, COL_D_Q // GROUP_W)),
            new_spec,
            new_spec,
            cache_spec,
            cache_spec,
        ],
        out_specs=pl.BlockSpec((None, t, GROUP_W), lambda bi: (bi, 0, 0)),
        out_shape=jax.ShapeDtypeStruct((b, t, GROUP_W), BF16),
        compiler_params=_cparams(("parallel",)),
        name="swa_sample",
    )(_alibi_slopes(), proj3, k4, v4, ck, cv)


def _tiles(m):
    return (256, 1024) if m >= 1024 else (m, m)


def _front_pad_rows(t, rows=SUBLANE):
    return jnp.pad(t, ((0, 0), (rows - t.shape[1], 0), (0, 0)))


W_IN_SEGMENTS = (('a_qkv', 3 * GROUP_W), ('a_z', GROUP_W), ('a_b', GDN_HEADS), ('a_a', GDN_HEADS),
                 ('b_rkv', 3 * GROUP_W), ('b_lora', RWKV_LORA), ('c_z', GROUP_W), ('c_xbc', SSM_XBC),
                 ('c_dt', SSM_HEADS), ('d_q', GROUP_W), ('d_k', GROUP_W), ('d_v', GROUP_W))
W_IN_SRC = {}
_o = 0
for _name, _n in W_IN_SEGMENTS:
    W_IN_SRC[_name] = _o
    _o += _n
W_IN_WIDE = ((COL_A_QKV, 3 * GROUP_W, W_IN_SRC['a_qkv']), (COL_B_RKV, 3 * GROUP_W, W_IN_SRC['b_rkv']),
             (COL_D_Q, GROUP_W, W_IN_SRC['d_q']), (COL_A_Z, GROUP_W, W_IN_SRC['a_z']),
             (COL_C_Z, GROUP_W, W_IN_SRC['c_z']), (COL_C_XBC, SSM_XBC, W_IN_SRC['c_xbc']))
IN_TN = 512
assert COL_B_LORA % IN_TN == 0 and N_PROJ - COL_B_LORA == IN_TN


ROW_UNIT = 2 * SUBLANE


def _w_in_tile_rows():
    rows = []
    for j in range(N_PROJ // IN_TN - 1):
        c0 = j * IN_TN
        dest, width, src = next(s for s in W_IN_WIDE if s[0] <= c0 < s[0] + s[1])
        assert c0 + IN_TN <= dest + width and (src + c0 - dest) % ROW_UNIT == 0
        rows.append((src + (c0 - dest)) // ROW_UNIT)
    rows.append(0)
    return jnp.asarray(rows, jnp.int32)


def prep_weights_t(w_t):
    def rows(name, n):
        return w_t[W_IN_SRC[name]:W_IN_SRC[name] + n].astype(BF16)

    zeros = lambda n: jnp.zeros((n, D_MODEL), BF16)
    small_used = 2 * GDN_HEADS + SSM_HEADS
    tail = jnp.concatenate([rows('b_lora', RWKV_LORA), zeros(LORA_PAD - RWKV_LORA), rows('a_b', GDN_HEADS),
                            rows('a_a', GDN_HEADS), rows('c_dt', SSM_HEADS), zeros(LANE - small_used)], axis=0)
    return {'w_tail_t': tail, 'w_k_t': rows('d_k', GROUP_W), 'w_v_t': rows('d_v', GROUP_W)}


def _mm_in_kernel(rows_ref, x_ref, wt_ref, tail_ref, o_ref, w_scr):
    del rows_ref

    @pl.when(pl.program_id(1) == 0)
    def _():
        @pl.when(pl.program_id(0) < pl.num_programs(0) - 1)
        def _():
            w_scr[...] = wt_ref[0].astype(BF16)

        @pl.when(pl.program_id(0) == pl.num_programs(0) - 1)
        def _():
            w_scr[...] = tail_ref[...]

    o_ref[...] = _dot(x_ref[...], w_scr[...], NT, None)


def mm_in(h, w_in_t_all, li, w_tail_t, tm):
    m, kd = h.shape
    grid_spec = pltpu.PrefetchScalarGridSpec(
        num_scalar_prefetch=1,
        grid=(N_PROJ // IN_TN, m // tm),
        in_specs=[
            pl.BlockSpec((tm, kd), lambda j, i, rows: (i, 0)),
            pl.BlockSpec((pl.Element(1), pl.Element(IN_TN), pl.Element(kd)),
                         lambda j, i, rows: (li, rows[j] * ROW_UNIT, 0)),
            pl.BlockSpec((IN_TN, kd), lambda j, i, rows: (0, 0)),
        ],
        out_specs=pl.BlockSpec((tm, IN_TN), lambda j, i, rows: (i, j)),
        scratch_shapes=[pltpu.VMEM((IN_TN, kd), BF16)],
    )
    return pl.pallas_call(
        _mm_in_kernel,
        grid_spec=grid_spec,
        out_shape=jax.ShapeDtypeStruct((m, N_PROJ), F32),
        compiler_params=_cparams(("arbitrary", "arbitrary")),
        name="mm_in",
    )(_w_in_tile_rows(), h, w_in_t_all, w_tail_t)


def decoder_layer(x, prm, wts, past, swa_cache, chunks, li, depth, kv=(None, None), h=None, next_pre_w=None):
    b, l, _ = x.shape
    m = b * l
    tr, tm = _tiles(m)
    x2 = x.reshape(m, D_MODEL)
    c_gdn, c_rwkv, c_ssd = chunks

    if h is None:
        h = rms_cast(x2, prm['norm_mix_pre'], tr)
    proj = mm_in(h, wts['w_in_t_all'], li, wts['w_tail_t'], tm)
    proj3 = proj.reshape(b, l, N_PROJ)
    kbuf, vbuf = matmul_into(h, wts['w_k_t'], wts['w_v_t'], kv, li, depth, tm, 512, "mm_kv")
    k4 = kbuf.reshape(depth, b, l, GROUP_W)
    v4 = vbuf.reshape(depth, b, l, GROUP_W)

    o_a, gdn_s = gdn_mixer(proj3, _front_pad_rows(past['gdn_conv']), past['gdn'], prm['gdn_conv_w'],
                           prm['gdn_A_log'], prm['gdn_dt_bias'], prm['gdn_norm_w'], c_gdn)
    shift = past['rwkv_shift'][:, None, :]
    sh_rkv8 = _front_pad_rows(shift[:, :, :3 * GROUP_W])
    sh_lora8 = _front_pad_rows(jnp.pad(shift[:, :, 3 * GROUP_W:], ((0, 0), (0, 0), (0, LORA_PAD - RWKV_LORA))))
    o_b, rwkv_s = rwkv_mixer(proj3, sh_rkv8, sh_lora8, past['rwkv'], prm, c_rwkv)
    o_c, ssm_s = ssd_mixer(proj3, _front_pad_rows(past['ssm_conv']), past['ssm'], prm, c_ssd)
    if swa_cache is None:
        o_d = swa_prompt(proj3, k4, v4, li, min(l, SWA_TILE))
    else:
        o_d = swa_sample(proj3, k4, v4, li, swa_cache[0], swa_cache[1])

    gdn_conv = proj3[:, l - (GDN_TAPS - 1):, COL_A_QKV:COL_A_QKV + 3 * GROUP_W]
    ssm_conv = proj3[:, l - (SSM_TAPS - 1):, COL_C_XBC:COL_C_XBC + SSM_XBC]
    rwkv_shift = jnp.concatenate([proj3[:, l - 1, COL_B_RKV:COL_B_RKV + 3 * GROUP_W],
                                  proj3[:, l - 1, COL_B_LORA:COL_B_LORA + RWKV_LORA]], axis=-1)

    y = matmul_groups([o.reshape(m, GROUP_W) for o in (o_a, o_b, o_c, o_d)], wts['w_out_all'], li, tm, 512,
                      "mm_out")
    x2, h2 = add_rms_cast(x2, y, prm['norm_mix_post'], prm['norm_ffn_pre'], tr)
    state8 = _front_pad_rows(past['ffn_conv'])
    if l % tm == 0:
        act, cst = ffn_up_act(h2, wts['w_up_all'], li, state8, prm['ffn_conv_w'], prm['ffn_conv_b'], l, tm, 256,
                              256)
        ffn_conv = cst[:, SUBLANE - (FFN_TAPS - 1):, :]
    else:
        up3 = matmul(h2, wts['w_up_all'], tm, 512, D_MODEL, "mm_up", li).reshape(b, l, 2 * D_FF)
        act = ffn_act(up3, state8, prm['ffn_conv_w'], prm['ffn_conv_b'], l, D_FF // 2).reshape(m, D_FF)
        ffn_conv = up3[:, l - (FFN_TAPS - 1):, :D_FF]
    y2 = matmul(act, wts['w_down_all'], tm, 512, D_FF // 2, "mm_down", li)
    if next_pre_w is None:
        x2, h_next = add_rms(x2, y2, prm['norm_ffn_post'], tr), None
    else:
        x2, h_next = add_rms_cast(x2, y2, prm['norm_ffn_post'], next_pre_w, tr)
    return x2.reshape(b, l, D_MODEL), h_next, (kbuf, vbuf), (gdn_s, gdn_conv, rwkv_s, rwkv_shift, ssm_s, ssm_conv,
                                                             ffn_conv)


def _zero_past(bsz):
    return {
        'gdn': jnp.zeros((bsz, GDN_HEADS, GDN_D, GDN_D), F32),
        'gdn_conv': jnp.zeros((bsz, GDN_TAPS - 1, 3 * GROUP_W), F32),
        'rwkv': jnp.zeros((bsz, RWKV_HEADS, RWKV_HS, RWKV_HS), F32),
        'rwkv_shift': jnp.zeros((bsz, 3 * GROUP_W + RWKV_LORA), F32),
        'ssm': jnp.zeros((bsz, SSM_HEADS, SSM_P, SSM_N), F32),
        'ssm_conv': jnp.zeros((bsz, SSM_TAPS - 1, SSM_XBC), F32),
        'ffn_conv': jnp.zeros((bsz, FFN_TAPS - 1, D_FF), F32),
    }


PARAM_NAMES = ('norm_mix_pre', 'norm_mix_post', 'norm_ffn_pre', 'norm_ffn_post', 'w_in', 'w_out', 'gdn_conv_w',
               'gdn_A_log', 'gdn_dt_bias', 'gdn_norm_w', 'rwkv_mu', 'rwkv_w0', 'rwkv_w2', 'rwkv_a0', 'rwkv_a2',
               'rwkv_g2', 'rwkv_k_k', 'rwkv_k_a', 'rwkv_r_k', 'rwkv_ln_w', 'rwkv_ln_b', 'ssm_conv_w', 'ssm_conv_b',
               'ssm_dt_bias', 'ssm_A_log', 'ssm_D', 'ssm_norm_w', 'ffn_w_up', 'ffn_conv_w', 'ffn_conv_b',
               'ffn_w_down')


def kernel(x_prompt, x_sample, state_gdn, state_gdn_conv, state_rwkv, state_rwkv_shift, state_ssm, state_ssm_conv, cache_swa_k, cache_swa_v, state_ffn_conv, norm_mix_pre, norm_mix_post, norm_ffn_pre, norm_ffn_post, w_in, w_out, gdn_conv_w, gdn_A_log, gdn_dt_bias, gdn_norm_w, rwkv_mu, rwkv_w0, rwkv_w2, rwkv_a0, rwkv_a2, rwkv_g2, rwkv_k_k, rwkv_k_a, rwkv_r_k, rwkv_ln_w, rwkv_ln_b, ssm_conv_w, ssm_conv_b, ssm_dt_bias, ssm_A_log, ssm_D, ssm_norm_w, ffn_w_up, ffn_conv_w, ffn_conv_b, ffn_w_down):
    params = dict(zip(PARAM_NAMES, (norm_mix_pre, norm_mix_post, norm_ffn_pre, norm_ffn_post, w_in, w_out,
                                    gdn_conv_w, gdn_A_log, gdn_dt_bias, gdn_norm_w, rwkv_mu, rwkv_w0, rwkv_w2,
                                    rwkv_a0, rwkv_a2, rwkv_g2, rwkv_k_k, rwkv_k_a, rwkv_r_k, rwkv_ln_w, rwkv_ln_b,
                                    ssm_conv_w, ssm_conv_b, ssm_dt_bias, ssm_A_log, ssm_D, ssm_norm_w, ffn_w_up,
                                    ffn_conv_w, ffn_conv_b, ffn_w_down)))
    depth = w_in.shape[0]
    xp, xs = x_prompt, x_sample
    t_dec = x_sample.shape[1]
    prompt_states, sample_states = [], []
    hp = hs = None
    kvp = kvs = (None, None)
    w_in_t = jnp.swapaxes(w_in, 1, 2)
    stacked = {'w_in_t_all': w_in_t, 'w_out_all': w_out, 'w_up_all': ffn_w_up,
               'w_down_all': ffn_w_down.astype(BF16)}
    for li in range(depth):
        prm = {k: v[li] for k, v in params.items()}
        wts = {**prep_weights_t(w_in_t[li]), **stacked}
        nxt = norm_mix_pre[li + 1] if li + 1 < depth else None
        xp, hp, kvp, stp = decoder_layer(xp, prm, wts, _zero_past(xp.shape[0]), None, (64, 64, 128),
                                         li, depth, kvp, hp, nxt)
        past = {'gdn': state_gdn[li], 'gdn_conv': state_gdn_conv[li], 'rwkv': state_rwkv[li],
                'rwkv_shift': state_rwkv_shift[li], 'ssm': state_ssm[li], 'ssm_conv': state_ssm_conv[li],
                'ffn_conv': state_ffn_conv[li]}
        xs, hs, kvs, sts = decoder_layer(xs, prm, wts, past, (cache_swa_k, cache_swa_v),
                                         (t_dec, t_dec, t_dec), li, depth, kvs, hs, nxt)
        prompt_states.append(stp)
        sample_states.append(sts)

    def window_rows(buf, x):
        bsz, l = x.shape[0], x.shape[1]
        rows = buf.reshape(depth, bsz, l, SWA_HEADS, SWA_HD)
        return rows[:, :, max(l - SWA_MAX_WINDOW, 0):]

    def outputs(states, kv, x):
        st = [jnp.stack(t) for t in zip(*states)]
        return (*st[:6], window_rows(kv[0], x), window_rows(kv[1], x), st[6])

    return (xp, xs, *outputs(prompt_states, kvp, x_prompt), *outputs(sample_states, kvs, x_sample))
```

```python
import functools

import jax
import jax.numpy as jnp
from jax import lax
from jax.experimental import pallas as pl
from jax.experimental.pallas import tpu as pltpu

F32 = jnp.float32
BF16 = jnp.bfloat16
HI = lax.Precision.HIGHEST

D_MODEL = 4096
GROUP_W = D_MODEL // 4
GDN_HEADS = 8
GDN_D = GROUP_W // GDN_HEADS
GDN_TAPS = 4
RWKV_HS = 64
RWKV_HEADS = GROUP_W // RWKV_HS
RWKV_W_LORA = 64
RWKV_A_LORA = 64
RWKV_G_LORA = 160
RWKV_LORA = RWKV_W_LORA + RWKV_A_LORA + RWKV_G_LORA
RWKV_GN_EPS = 64e-5
SSM_P = 64
SSM_HEADS = GROUP_W // SSM_P
SSM_GROUPS = 2
SSM_N = 128
SSM_TAPS = 4
SSM_XBC = GROUP_W + 2 * SSM_GROUPS * SSM_N
SWA_HEADS = 8
SWA_HD = GROUP_W // SWA_HEADS
SWA_PATTERNS = ((128, 1), (512, 4), (2048, 16))
SWA_MAX_WINDOW = 2048
D_FF = 256 * ((8 * D_MODEL // 3 + 255) // 256)
FFN_TAPS = 3
NORM_EPS = 1e-6
NEG_INF = -1e30

LANE = 128
SUBLANE = 8
LORA_PAD = 384
G_LORA_PAD = LORA_PAD - RWKV_W_LORA - RWKV_A_LORA

COL_A_QKV = 0
COL_B_RKV = 3 * GROUP_W
COL_D_Q = 6 * GROUP_W
COL_A_Z = 7 * GROUP_W
COL_C_Z = 8 * GROUP_W
COL_C_XBC = 9 * GROUP_W
COL_B_LORA = COL_C_XBC + SSM_XBC
COL_SMALL = COL_B_LORA + LORA_PAD
N_PROJ = COL_SMALL + LANE
SM_GDN_B = 0
SM_GDN_A = GDN_HEADS
SM_SSM_DT = 2 * GDN_HEADS

VMEM_LIMIT = 56 * 1024 * 1024
SWA_TILE = 1024


def _cparams(sem):
    return pltpu.CompilerParams(dimension_semantics=sem, vmem_limit_bytes=VMEM_LIMIT)


def _dot(a, b, dims, prec):
    return lax.dot_general(a, b, (dims, ((), ())), precision=prec, preferred_element_type=F32)


def dot_nn(a, b, prec=HI):
    return _dot(a, b, ((1,), (0,)), prec)


def dot_nt(a, b, prec=HI):
    return _dot(a, b, ((1,), (1,)), prec)


def dot_tn(a, b, prec=HI):
    return _dot(a, b, ((0,), (0,)), prec)


def _silu(x):
    return x * jax.nn.sigmoid(x)


def _iota2(shape, axis):
    return lax.broadcasted_iota(jnp.int32, shape, axis)


def _log2(n):
    s = n.bit_length() - 1
    assert 1 << s == n
    return s


NN = ((1,), (0,))
NT = ((1,), (1,))
TN = ((0,), (0,))


def _split(x):
    hi = x.astype(BF16)
    return hi, (x - hi.astype(F32)).astype(BF16)


def _split_rows(x):
    hi = x.astype(BF16)
    hif = hi.astype(F32)
    return jnp.concatenate([hif, x - hif], axis=0).astype(BF16), hi


def dot3(ap, bp, dims):
    return (_dot(ap[0], bp[0], dims, None) + _dot(ap[0], bp[1], dims, None)
            + _dot(ap[1], bp[0], dims, None))


def dot3s(a, bp, dims):
    r = a.shape[0]
    stacked, hi = _split_rows(a)
    both = _dot(stacked, bp[0], dims, None)
    return both[:r] + both[r:] + _dot(hi, bp[1], dims, None)


def dot1(a, b, dims):
    return _dot(a.astype(BF16), b.astype(BF16), dims, None)


def _inv_unit_lower_multi(ms, c):
    row = _iota2((c, c), 0)
    col = _iota2((c, c), 1)
    eye = jnp.where(row == col, 1.0, 0.0).astype(F32)
    base = min(SUBLANE, c)
    sb = _log2(base)
    blk = (row >> sb) == (col >> sb)
    ps = [jnp.where(blk, -m, 0.0) for m in ms]
    ts = [eye + p for p in ps]
    if sb > 1:
        ps = [_dot(p.astype(BF16), p.astype(BF16), NN, None) for p in ps]
        for _ in range(sb - 2):
            both = [_dot(jnp.concatenate([t, p], axis=0).astype(BF16), p.astype(BF16), NN, None)
                    for t, p in zip(ts, ps)]
            ts = [t + x[:c] for t, x in zip(ts, both)]
            ps = [x[c:] for x in both]
        ts = [t + _dot(t.astype(BF16), p.astype(BF16), NN, None) for t, p in zip(ts, ps)]
    s = base
    while s < c:
        ls = _log2(s)
        off = ((row >> (ls + 1)) == (col >> (ls + 1))) & ((row >> ls) > (col >> ls))
        tbs = [t.astype(BF16) for t in ts]
        inner = [_dot(jnp.where(off, m, 0.0).astype(BF16), tb, NN, None) for m, tb in zip(ms, tbs)]
        ts = [t - _dot(tb, x.astype(BF16), NN, None) for t, tb, x in zip(ts, tbs, inner)]
        s *= 2
    res = [eye - t - dot3s(m, _split(t), NN) for m, t in zip(ms, ts)]
    return [t + _dot(t.astype(BF16), r.astype(BF16), NN, None) for t, r in zip(ts, res)]


def _blockdiag_rows(x, half):
    left = _iota2(x.shape, 1) < half
    return jnp.concatenate([jnp.where(left, x, 0.0), jnp.where(left, 0.0, x)], axis=0)


def _head_sums(x, ones_bd):
    r = x.shape[0]
    stacked, _ = _split_rows(x)
    both = _dot(stacked, ones_bd, NN, None)
    return both[:r] + both[r:]


def _inv_unit_lower_pairs(ms, c):
    row = _iota2((c, 2 * c), 0)
    col = _iota2((c, 2 * c), 1) & (c - 1)
    eye = jnp.where(row == col, 1.0, 0.0).astype(F32)
    base = min(SUBLANE, c)
    sb = _log2(base)
    blk = (row >> sb) == (col >> sb)

    def mm(a, b):
        return _dot(a.astype(BF16), _blockdiag_rows(b, c).astype(BF16), NN, None)

    ps = [jnp.where(blk, -m, 0.0) for m in ms]
    ts = [eye + p for p in ps]
    if sb > 1:
        ps = [mm(p, p) for p in ps]
        for _ in range(sb - 2):
            both = [mm(jnp.concatenate([t, p], axis=0), p) for t, p in zip(ts, ps)]
            ts = [t + x[:c] for t, x in zip(ts, both)]
            ps = [x[c:] for x in both]
        ts = [t + mm(t, p) for t, p in zip(ts, ps)]
    s = base
    while s < c:
        ls = _log2(s)
        off = ((row >> (ls + 1)) == (col >> (ls + 1))) & ((row >> ls) > (col >> ls))
        inner = [mm(jnp.where(off, m, 0.0), t) for m, t in zip(ms, ts)]
        ts = [t - mm(t, x) for t, x in zip(ts, inner)]
        s *= 2
    res = [eye - t - dot3s(m, _split(_blockdiag_rows(t, c)), NN) for m, t in zip(ms, ts)]
    return [t + mm(t, r) for t, r in zip(ts, res)]


def _row_getter(x, c):
    if c % LANE:
        x = jnp.concatenate([x, jnp.zeros((LANE - c % LANE, LANE), F32)], axis=0)
    xt = x.T
    return lambda lane: xt[lane:lane + 1, :c]


def _cumsum_rows(x, c):
    tri = jnp.where(_iota2((c, c), 0) >= _iota2((c, c), 1), 1.0, 0.0).astype(F32)
    return dot_nn(tri, x)


def _shifted_taps(tail, x, taps):
    c = x.shape[0]
    xp = jnp.concatenate([tail, x], axis=0)
    out = []
    for s in range(taps - 1, 0, -1):
        out.append(pltpu.roll(xp, s, 0)[SUBLANE:SUBLANE + c])
    out.append(x)
    return out


def _rms_cast_kernel(x_ref, w_ref, o_ref):
    x = x_ref[...]
    y = x * lax.rsqrt(jnp.mean(x * x, axis=-1, keepdims=True) + NORM_EPS)
    o_ref[...] = (y * w_ref[...]).astype(o_ref.dtype)


def rms_cast(x, w, tr):
    m, d = x.shape
    return pl.pallas_call(
        _rms_cast_kernel,
        grid=(m // tr,),
        in_specs=[pl.BlockSpec((tr, d), lambda i: (i, 0)), pl.BlockSpec((1, d), lambda i: (0, 0))],
        out_specs=pl.BlockSpec((tr, d), lambda i: (i, 0)),
        out_shape=jax.ShapeDtypeStruct((m, d), BF16),
        compiler_params=_cparams(("parallel",)),
        name="rms_cast",
    )(x, w.reshape(1, d))


def _add_rms_kernel(x_ref, y_ref, w_ref, o_ref):
    y = y_ref[...]
    yn = y * lax.rsqrt(jnp.mean(y * y, axis=-1, keepdims=True) + NORM_EPS)
    o_ref[...] = x_ref[...] + yn * w_ref[...]


def add_rms(x, y, w, tr):
    m, d = x.shape
    return pl.pallas_call(
        _add_rms_kernel,
        grid=(m // tr,),
        in_specs=[pl.BlockSpec((tr, d), lambda i: (i, 0)), pl.BlockSpec((tr, d), lambda i: (i, 0)),
                  pl.BlockSpec((1, d), lambda i: (0, 0))],
        out_specs=pl.BlockSpec((tr, d), lambda i: (i, 0)),
        out_shape=jax.ShapeDtypeStruct((m, d), F32),
        compiler_params=_cparams(("parallel",)),
        name="add_rms",
    )(x, y, w.reshape(1, d))


def _add_rms_cast_kernel(x_ref, y_ref, w_ref, wn_ref, o_ref, h_ref):
    y = y_ref[...]
    yn = y * lax.rsqrt(jnp.mean(y * y, axis=-1, keepdims=True) + NORM_EPS)
    x = x_ref[...] + yn * w_ref[...]
    o_ref[...] = x
    xn = x * lax.rsqrt(jnp.mean(x * x, axis=-1, keepdims=True) + NORM_EPS)
    h_ref[...] = (xn * wn_ref[...]).astype(h_ref.dtype)


def add_rms_cast(x, y, w, w_next, tr):
    m, d = x.shape
    row = pl.BlockSpec((tr, d), lambda i: (i, 0))
    vec = pl.BlockSpec((1, d), lambda i: (0, 0))
    return pl.pallas_call(
        _add_rms_cast_kernel,
        grid=(m // tr,),
        in_specs=[row, row, vec, vec],
        out_specs=[row, row],
        out_shape=[jax.ShapeDtypeStruct((m, d), F32), jax.ShapeDtypeStruct((m, d), BF16)],
        compiler_params=_cparams(("parallel",)),
        name="add_rms_cast",
    )(x, y, w.reshape(1, d), w_next.reshape(1, d))


def _mm_kernel(a_ref, w_ref, o_ref, *, nk):
    p = jnp.dot(a_ref[...], w_ref[...].astype(BF16), preferred_element_type=F32)
    if nk == 1:
        o_ref[...] = p
    else:
        k = pl.program_id(2)

        @pl.when(k == 0)
        def _():
            o_ref[...] = p

        @pl.when(k > 0)
        def _():
            o_ref[...] += p


def matmul(a, w, tm, tn, tk, name, li=None):
    m, kd = a.shape
    n = w.shape[-1]
    nk = kd // tk
    if li is None:
        w_spec = pl.BlockSpec((tk, tn), lambda i, j, k: (k, j))
    else:
        w_spec = pl.BlockSpec((None, tk, tn), lambda i, j, k: (li, k, j))
    return pl.pallas_call(
        functools.partial(_mm_kernel, nk=nk),
        grid=(m // tm, n // tn, nk),
        in_specs=[pl.BlockSpec((tm, tk), lambda i, j, k: (i, k)), w_spec],
        out_specs=pl.BlockSpec((tm, tn), lambda i, j, k: (i, j)),
        out_shape=jax.ShapeDtypeStruct((m, n), F32),
        compiler_params=_cparams(("parallel", "parallel", "arbitrary")),
        name=name,
    )(a, w)


def _mm_into_kernel(a_ref, w1_ref, w2_ref, *rest):
    o1_ref, o2_ref, w1_scr, w2_scr = rest[-4:]

    @pl.when(pl.program_id(1) == 0)
    def _():
        w1_scr[...] = w1_ref[0].astype(BF16)
        w2_scr[...] = w2_ref[0].astype(BF16)

    a = a_ref[...]
    o1_ref[...] = _dot(a, w1_scr[...], NT, None)
    o2_ref[...] = _dot(a, w2_scr[...], NT, None)


def matmul_into(a, w_t_all, row1, row2, n, bufs, li, depth, tm, tn, name):
    m, kd = a.shape
    assert row1 % ROW_UNIT == 0 and row2 % ROW_UNIT == 0 and tn % ROW_UNIT == 0

    def w_spec(row):
        return pl.BlockSpec((pl.Element(1), pl.Element(tn), pl.Element(kd)),
                            lambda j, i: (li, (row // ROW_UNIT + j * (tn // ROW_UNIT)) * ROW_UNIT, 0))

    in_specs = [pl.BlockSpec((tm, kd), lambda j, i: (i, 0)), w_spec(row1), w_spec(row2)]
    args = [a, w_t_all, w_t_all]
    aliases = {}
    if bufs[0] is not None:
        in_specs += [pl.BlockSpec(memory_space=pl.ANY)] * 2
        args += list(bufs)
        aliases = {3: 0, 4: 1}
    out_spec = pl.BlockSpec((None, tm, tn), lambda j, i: (li, i, j))
    return pl.pallas_call(
        _mm_into_kernel,
        grid=(n // tn, m // tm),
        in_specs=in_specs,
        out_specs=[out_spec, out_spec],
        out_shape=[jax.ShapeDtypeStruct((depth, m, n), F32)] * 2,
        scratch_shapes=[pltpu.VMEM((tn, kd), BF16), pltpu.VMEM((tn, kd), BF16)],
        input_output_aliases=aliases,
        compiler_params=_cparams(("arbitrary", "arbitrary")),
        name=name,
    )(*args)


def _mm_groups_kernel(*refs):
    *a_refs, w_ref, o_ref, w_scr = refs
    kg = a_refs[0].shape[1]

    @pl.when(pl.program_id(1) == 0)
    def _():
        w_scr[...] = w_ref[...].astype(BF16)

    acc = jnp.dot(a_refs[0][...], w_scr[0:kg, :], preferred_element_type=F32)
    for g in range(1, len(a_refs)):
        acc = acc + jnp.dot(a_refs[g][...], w_scr[g * kg:(g + 1) * kg, :], preferred_element_type=F32)
    o_ref[...] = acc


def matmul_groups(parts, w_all, li, tm, tn, name):
    m, kg = parts[0].shape
    _, kd, n = w_all.shape
    return pl.pallas_call(
        _mm_groups_kernel,
        grid=(n // tn, m // tm),
        in_specs=[pl.BlockSpec((tm, kg), lambda j, i: (i, 0)) for _ in parts]
        + [pl.BlockSpec((None, kd, tn), lambda j, i: (li, 0, j))],
        out_specs=pl.BlockSpec((tm, tn), lambda j, i: (i, j)),
        out_shape=jax.ShapeDtypeStruct((m, n), F32),
        scratch_shapes=[pltpu.VMEM((kd, tn), BF16)],
        compiler_params=_cparams(("arbitrary", "arbitrary")),
        name=name,
    )(*parts, w_all)


def _ffn_act_kernel(g_ref, v_ref, halo_ref, st_ref, cw_ref, cb_ref, o_ref):
    g = g_ref[...]
    tail = jnp.where(pl.program_id(1) == 0, st_ref[...], halo_ref[...])
    taps = _shifted_taps(tail, g, FFN_TAPS)
    cw = cw_ref[...]
    y = taps[0] * cw[0:1]
    for i in range(1, FFN_TAPS):
        y = y + taps[i] * cw[i:i + 1]
    y = y + cb_ref[...]
    o_ref[...] = (_silu(y) * v_ref[...]).astype(o_ref.dtype)


def _ffn_up_act_kernel(x_ref, wg_ref, wv_ref, st_ref, cw_ref, cb_ref, act_ref, cst_ref, wg_scr, wv_scr, tail_scr,
                       *, tiles_per_seq, sub):
    i = pl.program_id(1)

    @pl.when(i == 0)
    def _():
        wg_scr[...] = wg_ref[...].astype(BF16)
        wv_scr[...] = wv_ref[...].astype(BF16)

    @pl.when(i % tiles_per_seq == 0)
    def _():
        tail_scr[...] = st_ref[...]

    wg = wg_scr[...]
    wv = wv_scr[...]
    cw = cw_ref[...]
    cb = cb_ref[...]
    tm = x_ref.shape[0]
    tail = tail_scr[...]

    def project(r):
        x = x_ref[r * sub:(r + 1) * sub, :]
        return jnp.dot(x, wg, preferred_element_type=F32), jnp.dot(x, wv, preferred_element_type=F32)

    nxt = project(0)
    for r in range(tm // sub):
        g, v = nxt
        if r + 1 < tm // sub:
            nxt = project(r + 1)
        taps = _shifted_taps(tail, g, FFN_TAPS)
        y = taps[0] * cw[0:1]
        for t in range(1, FFN_TAPS):
            y = y + taps[t] * cw[t:t + 1]
        act_ref[r * sub:(r + 1) * sub, :] = (_silu(y + cb) * v).astype(act_ref.dtype)
        tail = g[sub - SUBLANE:]
    tail_scr[...] = tail
    cst_ref[...] = tail


def ffn_up_act(h2, w_up_all, li, state8, conv_w, conv_b, l, tm, tn, sub):
    m, kd = h2.shape
    b = m // l
    nj = D_FF // tn
    tiles_per_seq = l // tm
    act, tails = pl.pallas_call(
        functools.partial(_ffn_up_act_kernel, tiles_per_seq=tiles_per_seq, sub=sub),
        grid=(nj, m // tm),
        in_specs=[
            pl.BlockSpec((tm, kd), lambda j, i: (i, 0)),
            pl.BlockSpec((None, kd, tn), lambda j, i: (li, 0, j)),
            pl.BlockSpec((None, kd, tn), lambda j, i: (li, 0, j + nj)),
            pl.BlockSpec((None, SUBLANE, tn), lambda j, i: (i // tiles_per_seq, 0, j)),
            pl.BlockSpec((FFN_TAPS, tn), lambda j, i: (0, j)),
            pl.BlockSpec((1, tn), lambda j, i: (0, j)),
        ],
        out_specs=[
            pl.BlockSpec((tm, tn), lambda j, i: (i, j)),
            pl.BlockSpec((None, SUBLANE, tn), lambda j, i: (i, 0, j)),
        ],
        out_shape=[jax.ShapeDtypeStruct((m, D_FF), BF16), jax.ShapeDtypeStruct((m // tm, SUBLANE, D_FF), F32)],
        scratch_shapes=[pltpu.VMEM((kd, tn), BF16), pltpu.VMEM((kd, tn), BF16), pltpu.VMEM((SUBLANE, tn), F32)],
        compiler_params=_cparams(("arbitrary", "arbitrary")),
        name="ffn_up_act",
    )(h2, w_up_all, w_up_all, state8, conv_w, conv_b.reshape(1, D_FF))
    return act, tails.reshape(b, tiles_per_seq, SUBLANE, D_FF)[:, tiles_per_seq - 1]


def ffn_act(up3, state8, conv_w, conv_b, ts, tn):
    b, l, _ = up3.shape
    nj = D_FF // tn
    hb = ts // SUBLANE
    return pl.pallas_call(
        _ffn_act_kernel,
        grid=(b, l // ts, nj),
        in_specs=[
            pl.BlockSpec((None, ts, tn), lambda bi, i, j: (bi, i, j)),
            pl.BlockSpec((None, ts, tn), lambda bi, i, j: (bi, i, j + nj)),
            pl.BlockSpec((None, SUBLANE, tn), lambda bi, i, j: (bi, jnp.maximum(i * hb - 1, 0), j)),
            pl.BlockSpec((None, SUBLANE, tn), lambda bi, i, j: (bi, 0, j)),
            pl.BlockSpec((FFN_TAPS, tn), lambda bi, i, j: (0, j)),
            pl.BlockSpec((1, tn), lambda bi, i, j: (0, j)),
        ],
        out_specs=pl.BlockSpec((None, ts, tn), lambda bi, i, j: (bi, i, j)),
        out_shape=jax.ShapeDtypeStruct((b, l, D_FF), BF16),
        compiler_params=_cparams(("parallel", "parallel", "parallel")),
        name="ffn_act",
    )(up3, up3, up3, state8, conv_w, conv_b.reshape(1, D_FF))


def _gdn_kernel(qkv_ref, z_ref, sm_ref, cbuf_ref, s0_ref, cw_ref, alog_ref, dtb_ref, nw_ref,
                o_ref, sout_ref, s_scr, tail_scr, *, c, nsub):
    ci = pl.program_id(1)

    @pl.when(ci == 0)
    def _():
        s_scr[...] = s0_ref[...]
        tail_scr[...] = cbuf_ref[...]

    x = qkv_ref[...]
    taps = _shifted_taps(tail_scr[...], x, GDN_TAPS)
    rows = nsub * c
    tail_scr[...] = x[rows - SUBLANE:]
    cw = cw_ref[...]
    y = taps[0] * cw[0:1]
    for i in range(1, GDN_TAPS):
        y = y + taps[i] * cw[i:i + 1]
    y = _silu(y)

    sm = sm_ref[...]
    beta_all = jax.nn.sigmoid(sm)
    g_all = -jnp.exp(alog_ref[...]) * jax.nn.softplus(sm + dtb_ref[...])
    ri = _iota2((rows, rows), 0)
    rj = _iota2((rows, rows), 1)
    in_chunk_tri = jnp.where((ri >= rj) & ((ri >> _log2(c)) == (rj >> _log2(c))), 1.0, 0.0).astype(F32)
    gcum_all = dot_nn(in_chunk_tri, g_all)
    rows_of = _row_getter(gcum_all, rows)

    row = _iota2((c, c), 0)
    col = _iota2((c, c), 1)
    tri = row >= col
    strict = row > col
    z = z_ref[...]
    nw = nw_ref[...]
    heads = range(GDN_HEADS)
    units = [(q, h) for q in range(nsub) for h in heads]
    ms, aqk, rhs, qd, kd, gls = [], [], [], [], [], []
    for qi, h in units:
        rs = slice(qi * c, (qi + 1) * c)
        lo = h * GDN_D
        q = y[rs, lo:lo + GDN_D]
        k = y[rs, GROUP_W + lo:GROUP_W + lo + GDN_D]
        v = y[rs, 2 * GROUP_W + lo:2 * GROUP_W + lo + GDN_D]
        q = q * lax.rsqrt(jnp.sum(q * q, axis=-1, keepdims=True) + 1e-6) * (GDN_D ** -0.5)
        k = k * lax.rsqrt(jnp.sum(k * k, axis=-1, keepdims=True) + 1e-6)
        beta = beta_all[rs, SM_GDN_B + h:SM_GDN_B + h + 1]
        gc = gcum_all[rs, SM_GDN_A + h:SM_GDN_A + h + 1]
        gam = jnp.exp(jnp.where(tri, gc - rows_of(SM_GDN_A + h)[:, qi * c:(qi + 1) * c], -jnp.inf))
        kbeta = k * beta
        mq = dot3s(jnp.concatenate([kbeta, q], axis=0), _split(k), NT)
        ms.append(jnp.where(strict, mq[:c] * gam, 0.0))
        aqk.append(mq[c:] * gam)
        eg = jnp.exp(gc)
        gl = gc[c - 1:c]
        rhs.append(jnp.concatenate([kbeta * eg, v * beta], axis=1))
        qd.append(q * eg)
        kd.append(k * jnp.exp(gl - gc))
        gls.append(gl)
    packed = _inv_unit_lower_pairs([jnp.concatenate([ms[i], ms[i + 1]], axis=1) for i in range(0, len(ms), 2)], c)
    ts = [t[:, half * c:(half + 1) * c] for t in packed for half in range(2)]
    wu = [dot3s(t, _split(x), NN) for t, x in zip(ts, rhs)]
    state = [s_scr[h] for h in heads]
    for qi in range(nsub):
        rs = slice(qi * c, (qi + 1) * c)
        ix = [qi * GDN_HEADS + h for h in heads]
        wqs = [dot3s(jnp.concatenate([wu[i][:, :GDN_D], qd[i]], axis=0), _split(state[h]), NN)
               for h, i in zip(heads, ix)]
        v_new = [wu[i][:, GDN_D:] - wqs[h][:c] for h, i in zip(heads, ix)]
        outs = [wqs[h][c:] + dot1(aqk[i], v_new[h], NN) for h, i in zip(heads, ix)]
        state = [state[h] * jnp.exp(gls[i]) + dot3(_split(kd[i]), _split(v_new[h]), TN) for h, i in zip(heads, ix)]
        for h in heads:
            lo = h * GDN_D
            o = outs[h]
            o = o * lax.rsqrt(jnp.mean(o * o, axis=-1, keepdims=True) + NORM_EPS) * nw
            o = o * _silu(z[rs, lo:lo + GDN_D])
            o_ref[rs, lo:lo + GDN_D] = o.astype(o_ref.dtype)
    for h in heads:
        s_scr[h] = state[h]

    @pl.when(ci == pl.num_programs(1) - 1)
    def _():
        sout_ref[...] = s_scr[...]


def gdn_mixer(proj3, cbuf8, s0, conv_w, a_log, dt_bias, norm_w, c):
    b, l, _ = proj3.shape
    alog_row = jnp.zeros((1, LANE), F32).at[0, SM_GDN_A:SM_GDN_A + GDN_HEADS].set(a_log)
    dtb_row = jnp.zeros((1, LANE), F32).at[0, SM_GDN_A:SM_GDN_A + GDN_HEADS].set(dt_bias)
    w3 = 3 * GROUP_W
    const2 = lambda bi, ci: (0, 0)
    nsub = next(k for k in (4, 2, 1) if l % (k * c) == 0)
    rb = nsub * c
    return pl.pallas_call(
        functools.partial(_gdn_kernel, c=c, nsub=nsub),
        grid=(b, l // rb),
        in_specs=[
            pl.BlockSpec((None, rb, w3), lambda bi, ci: (bi, ci, COL_A_QKV // w3)),
            pl.BlockSpec((None, rb, GROUP_W), lambda bi, ci: (bi, ci, COL_A_Z // GROUP_W)),
            pl.BlockSpec((None, rb, LANE), lambda bi, ci: (bi, ci, COL_SMALL // LANE)),
            pl.BlockSpec((None, SUBLANE, w3), lambda bi, ci: (bi, 0, 0)),
            pl.BlockSpec((None, GDN_HEADS, GDN_D, GDN_D), lambda bi, ci: (bi, 0, 0, 0)),
            pl.BlockSpec((GDN_TAPS, w3), const2),
            pl.BlockSpec((1, LANE), const2),
            pl.BlockSpec((1, LANE), const2),
            pl.BlockSpec((1, GDN_D), const2),
        ],
        out_specs=[
            pl.BlockSpec((None, rb, GROUP_W), lambda bi, ci: (bi, ci, 0)),
            pl.BlockSpec((None, GDN_HEADS, GDN_D, GDN_D), lambda bi, ci: (bi, 0, 0, 0)),
        ],
        out_shape=[jax.ShapeDtypeStruct((b, l, GROUP_W), BF16),
                   jax.ShapeDtypeStruct((b, GDN_HEADS, GDN_D, GDN_D), F32)],
        scratch_shapes=[pltpu.VMEM((GDN_HEADS, GDN_D, GDN_D), F32), pltpu.VMEM((SUBLANE, w3), F32)],
        compiler_params=_cparams(("parallel", "arbitrary")),
        name="gdn_mixer",
    )(proj3, proj3, proj3, cbuf8, s0, conv_w, alog_row, dtb_row, norm_w.reshape(1, GDN_D))


def _rwkv_kernel(rkv_ref, lora_ref, sh_rkv_ref, sh_lora_ref, s0_ref, mu_rkv_ref, mu_lora_ref,
                 w0_ref, w2_ref, a0_ref, a2_ref, g2_ref, kk_ref, ka_ref, rk_ref, lnw_ref, lnb_ref,
                 o_ref, sout_ref, s_scr, tail_rkv, tail_lora, *, c, nsub):
    ci = pl.program_id(1)

    @pl.when(ci == 0)
    def _():
        s_scr[...] = s0_ref[...]
        tail_rkv[...] = sh_rkv_ref[...]
        tail_lora[...] = sh_lora_ref[...]

    x = rkv_ref[...]
    xl = lora_ref[...]
    prev = _shifted_taps(tail_rkv[...], x, 2)[0]
    prev_l = _shifted_taps(tail_lora[...], xl, 2)[0]
    tail_rkv[...] = x[nsub * c - SUBLANE:]
    tail_lora[...] = xl[nsub * c - SUBLANE:]
    zm = x + (prev - x) * mu_rkv_ref[...]
    zl = xl + (prev_l - xl) * mu_lora_ref[...]
    r = zm[:, 0:GROUP_W]
    k = zm[:, GROUP_W:2 * GROUP_W]
    v = zm[:, 2 * GROUP_W:3 * GROUP_W]
    wd = zl[:, 0:RWKV_W_LORA]
    ad = zl[:, RWKV_W_LORA:RWKV_W_LORA + RWKV_A_LORA]
    gd = zl[:, RWKV_W_LORA + RWKV_A_LORA:LORA_PAD]

    w_log = -jax.nn.softplus(-(w0_ref[...] + dot3s(jnp.tanh(wd), _split(w2_ref[...]), NN))) - 0.5
    logw = -jnp.exp(w_log)
    a = jax.nn.sigmoid(a0_ref[...] + dot3s(ad, _split(a2_ref[...]), NN))
    gate = dot1(jax.nn.sigmoid(gd), g2_ref[...], NN)
    kkv = k * kk_ref[...]
    k2 = k * (1.0 + (a - 1.0) * ka_ref[...])
    rows = nsub * c
    ri = _iota2((rows, rows), 0)
    rj = _iota2((rows, rows), 1)
    lc_sh = _log2(c)
    in_chunk_tri = jnp.where((ri >= rj) & ((ri >> lc_sh) == (rj >> lc_sh)), 1.0, 0.0).astype(F32)
    lcum = dot_nn(in_chunk_tri, logw)

    n = RWKV_HS
    pw = 2 * n
    rowc = _iota2((c, 2 * c), 0)
    colc = _iota2((c, 2 * c), 1) & (c - 1)
    strict2 = rowc > colc
    tri2 = rowc >= colc
    same_head = (_iota2((pw, pw), 0) < n) == (_iota2((pw, pw), 1) < n)
    ones_bd = jnp.where(same_head, 1.0, 0.0).astype(BF16)
    rk = rk_ref[...]
    lnw = lnw_ref[...]
    lnb = lnb_ref[...]
    pairs = range(RWKV_HEADS // 2)
    sl = [slice(p * pw, (p + 1) * pw) for p in pairs]
    units = [(slice(q * c, (q + 1) * c), sl[p]) for q in range(nsub) for p in pairs]
    kk_ss = [_head_sums(jnp.square(kkv[rs, ps]), ones_bd) for rs, ps in units]
    bonus_s = [_head_sums(r[rs, ps] * k2[rs, ps] * rk[:, ps], ones_bd) for rs, ps in units]
    lhs, x_bs, x_ks, bk_end, l_last = [], [], [], [], []
    for (rs, ps), ss in zip(units, kk_ss):
        kk = kkv[rs, ps] * lax.rsqrt(ss + 1e-6)
        lc = lcum[rs, ps]
        ll = lc[c - 1:c]
        p_inv = jnp.exp(-lc)
        a_t = -kk * jnp.exp(lc - logw[rs, ps])
        b_vec = kk * a[rs, ps]
        r_t = r[rs, ps] * jnp.exp(lc)
        p_end = jnp.exp(ll - lc)
        ar = jnp.concatenate([a_t, r_t], axis=0)
        x_bs.append(dot3s(ar, _split(_blockdiag_rows(b_vec * p_inv, n)), NT))
        x_ks.append(dot3s(ar, _split(_blockdiag_rows(k2[rs, ps] * p_inv, n)), NT))
        lhs.append(ar)
        bk_end.append(jnp.concatenate([b_vec * p_end, k2[rs, ps] * p_end], axis=0))
        l_last.append(ll)
    ms = [jnp.where(strict2, -x[:c], 0.0) for x in x_bs]
    av = [dot3s(jnp.where(strict2, x[:c], 0.0), _split(_blockdiag_rows(v[rs, ps], n)), NN)
          for x, (rs, ps) in zip(x_ks, units)]
    rbk = [jnp.concatenate([jnp.where(tri2, xb[c:], 0.0), jnp.where(tri2, xk[c:], 0.0)], axis=1)
           for xb, xk in zip(x_bs, x_ks)]
    ts = _inv_unit_lower_pairs(ms, c)
    state = [s_scr[p] for p in pairs]
    npair = len(pairs)
    for q in range(nsub):
        rs = slice(q * c, (q + 1) * c)
        ix = [q * npair + p for p in pairs]
        ars = [dot3s(lhs[i], _split(state[p]), NT) for p, i in zip(pairs, ix)]
        us = [dot3s(ts[i], _split(_blockdiag_rows(x[:c] + av[i], n)), NN) for x, i in zip(ars, ix)]
        vs = [v[rs, sl[p]] for p in pairs]
        ys = [ars[p][c:] + _dot(rbk[i].astype(BF16),
                                jnp.concatenate([_blockdiag_rows(us[p], n), _blockdiag_rows(vs[p], n)],
                                                axis=0).astype(BF16), NN, None)
              for p, i in zip(pairs, ix)]
        upd = [dot3(_split(jnp.concatenate([us[p], vs[p]], axis=0)), _split(bk_end[i]), TN)
               for p, i in zip(pairs, ix)]
        state = [jnp.where(same_head, state[p] * jnp.exp(l_last[i]) + upd[p], 0.0) for p, i in zip(pairs, ix)]
        devs = [y - _head_sums(y, ones_bd) * (1.0 / n) for y in ys]
        var = [_head_sums(jnp.square(d), ones_bd) * (1.0 / n) for d in devs]
        for p, i in zip(pairs, ix):
            yn = devs[p] * lax.rsqrt(var[p] + RWKV_GN_EPS) * lnw[:, sl[p]] + lnb[:, sl[p]]
            o_ref[rs, sl[p]] = ((yn + bonus_s[i] * vs[p]) * gate[rs, sl[p]]).astype(o_ref.dtype)
    for p in pairs:
        s_scr[p] = state[p]

    @pl.when(ci == pl.num_programs(1) - 1)
    def _():
        sout_ref[...] = s_scr[...]


def rwkv_mixer(proj3, sh_rkv8, sh_lora8, s0, prm, c):
    b, l, _ = proj3.shape
    nsub = next(k for k in (4, 2, 1) if l % (k * c) == 0)
    rb = nsub * c
    w3 = 3 * GROUP_W
    mu = prm['rwkv_mu']
    mu_rkv = mu[:w3].reshape(1, w3)
    mu_lora = jnp.pad(mu[w3:], (0, LORA_PAD - RWKV_LORA)).reshape(1, LORA_PAD)
    g2 = jnp.pad(prm['rwkv_g2'], ((0, G_LORA_PAD - RWKV_G_LORA), (0, 0)))
    row = lambda t: t.reshape(1, GROUP_W)
    const2 = lambda bi, ci: (0, 0)
    vec = pl.BlockSpec((1, GROUP_W), const2)
    n, npair = RWKV_HS, RWKV_HEADS // 2
    sp = s0.reshape(b, npair, 2, n, n)
    zero = jnp.zeros((b, npair, n, n), F32)
    s_pairs = jnp.concatenate([jnp.concatenate([sp[:, :, 0], zero], axis=-1),
                               jnp.concatenate([zero, sp[:, :, 1]], axis=-1)], axis=-2)
    state_spec = pl.BlockSpec((None, npair, 2 * n, 2 * n), lambda bi, ci: (bi, 0, 0, 0))
    o, s_new = pl.pallas_call(
        functools.partial(_rwkv_kernel, c=c, nsub=nsub),
        grid=(b, l // rb),
        in_specs=[
            pl.BlockSpec((None, rb, w3), lambda bi, ci: (bi, ci, COL_B_RKV // w3)),
            pl.BlockSpec((None, rb, LORA_PAD), lambda bi, ci: (bi, ci, COL_B_LORA // LORA_PAD)),
            pl.BlockSpec((None, SUBLANE, w3), lambda bi, ci: (bi, 0, 0)),
            pl.BlockSpec((None, SUBLANE, LORA_PAD), lambda bi, ci: (bi, 0, 0)),
            state_spec,
            pl.BlockSpec((1, w3), const2),
            pl.BlockSpec((1, LORA_PAD), const2),
            vec,
            pl.BlockSpec((RWKV_W_LORA, GROUP_W), const2),
            vec,
            pl.BlockSpec((RWKV_A_LORA, GROUP_W), const2),
            pl.BlockSpec((G_LORA_PAD, GROUP_W), const2),
            vec, vec, vec, vec, vec,
        ],
        out_specs=[
            pl.BlockSpec((None, rb, GROUP_W), lambda bi, ci: (bi, ci, 0)),
            state_spec,
        ],
        out_shape=[jax.ShapeDtypeStruct((b, l, GROUP_W), BF16),
                   jax.ShapeDtypeStruct((b, npair, 2 * n, 2 * n), F32)],
        scratch_shapes=[pltpu.VMEM((npair, 2 * n, 2 * n), F32),
                        pltpu.VMEM((SUBLANE, w3), F32), pltpu.VMEM((SUBLANE, LORA_PAD), F32)],
        compiler_params=_cparams(("parallel", "arbitrary")),
        name="rwkv_mixer",
    )(proj3, proj3, sh_rkv8, sh_lora8, s_pairs, mu_rkv, mu_lora,
      row(prm['rwkv_w0']), prm['rwkv_w2'], row(prm['rwkv_a0']), prm['rwkv_a2'], g2,
      row(prm['rwkv_k_k']), row(prm['rwkv_k_a']), row(prm['rwkv_r_k']),
      row(prm['rwkv_ln_w']), row(prm['rwkv_ln_b']))
    s_heads = jnp.stack([s_new[:, :, :n, :n], s_new[:, :, n:, n:]], axis=2)
    return o, s_heads.reshape(b, RWKV_HEADS, n, n)


def _ssd_kernel(z_ref, xbc_ref, sm_ref, cbuf_ref, s0_ref, cw_ref, cb_ref, dtb_ref, alog_ref, dsk_ref, nw_ref,
                o_ref, sout_ref, s_scr, tail_scr, y_scr, *, c):
    ci = pl.program_id(1)

    @pl.when(ci == 0)
    def _():
        s_scr[...] = s0_ref[...]
        tail_scr[...] = cbuf_ref[...]

    x = xbc_ref[...]
    taps = _shifted_taps(tail_scr[...], x, SSM_TAPS)
    tail_scr[...] = x[c - SUBLANE:]
    cw = cw_ref[...]
    y = taps[0] * cw[0:1]
    for i in range(1, SSM_TAPS):
        y = y + taps[i] * cw[i:i + 1]
    y = _silu(y + cb_ref[...])
    xs = y[:, 0:GROUP_W]
    gn = SSM_GROUPS * SSM_N

    sm = sm_ref[...]
    dt_all = jax.nn.softplus(sm + dtb_ref[...])
    da_all = dt_all * (-jnp.exp(alog_ref[...]))
    acs_all = _cumsum_rows(da_all, c)

    row = _iota2((c, c), 0)
    col = _iota2((c, c), 1)
    tri = row >= col
    z = z_ref[...]
    dsk = dsk_ref[...]
    hpg = SSM_HEADS // SSM_GROUPS
    gp = hpg * SSM_P
    rows_of = _row_getter(acs_all, c)
    for g in range(SSM_GROUPS):
        bm = y[:, GROUP_W + g * SSM_N:GROUP_W + (g + 1) * SSM_N]
        cm = y[:, GROUP_W + gn + g * SSM_N:GROUP_W + gn + (g + 1) * SSM_N]
        cb = dot1(cm, bm, NT)
        sg = s_scr[g]
        y_off = dot1(cm, sg, NT)
        xdec = []
        for rr in range(hpg):
            h = g * hpg + rr
            lo = h * SSM_P
            lane = SM_SSM_DT + h
            xs_h = xs[:, lo:lo + SSM_P]
            dt = dt_all[:, lane:lane + 1]
            acs = acs_all[:, lane:lane + 1]
            lmat = jnp.exp(jnp.where(tri, acs - rows_of(lane), -jnp.inf))
            xd = xs_h * dt
            a_last = acs[c - 1:c]
            xdec.append(xd * jnp.exp(a_last - acs))
            yh = dot1(cb * lmat, xd, NN) + y_off[:, rr * SSM_P:(rr + 1) * SSM_P] * jnp.exp(acs)
            yh = yh + xs_h * dsk[:, lane:lane + 1]
            y_scr[:, lo:lo + SSM_P] = yh * _silu(z[:, lo:lo + SSM_P])
        upd = dot3(_split(jnp.concatenate(xdec, axis=1)), _split(bm), TN)
        for rr in range(hpg):
            lane = SM_SSM_DT + g * hpg + rr
            dec = jnp.exp(acs_all[c - 1:c, lane:lane + 1])
            s_scr[g, rr * SSM_P:(rr + 1) * SSM_P, :] = (sg[rr * SSM_P:(rr + 1) * SSM_P] * dec
                                                         + upd[rr * SSM_P:(rr + 1) * SSM_P])

    gw = GROUP_W // SSM_GROUPS
    nw = nw_ref[...]
    for g in range(SSM_GROUPS):
        yg = y_scr[:, g * gw:(g + 1) * gw]
        yg = yg * lax.rsqrt(jnp.mean(yg * yg, axis=-1, keepdims=True) + NORM_EPS)
        o_ref[:, g * gw:(g + 1) * gw] = (yg * nw[:, g * gw:(g + 1) * gw]).astype(o_ref.dtype)

    @pl.when(ci == pl.num_programs(1) - 1)
    def _():
        sout_ref[...] = s_scr[...]


def ssd_mixer(proj3, cbuf8, s0, prm, c):
    b, l, _ = proj3.shape

    def lanes(t):
        return jnp.zeros((1, LANE), F32).at[0, SM_SSM_DT:SM_SSM_DT + SSM_HEADS].set(t)

    const2 = lambda bi, ci: (0, 0)
    small = pl.BlockSpec((1, LANE), const2)
    gshape = (SSM_GROUPS, SSM_HEADS // SSM_GROUPS * SSM_P, SSM_N)
    state_spec = pl.BlockSpec((None,) + gshape, lambda bi, ci: (bi, 0, 0, 0))
    o, s_new = pl.pallas_call(
        functools.partial(_ssd_kernel, c=c),
        grid=(b, l // c),
        in_specs=[
            pl.BlockSpec((None, c, GROUP_W), lambda bi, ci: (bi, ci, COL_C_Z // GROUP_W)),
            pl.BlockSpec((None, c, SSM_XBC), lambda bi, ci: (bi, ci, COL_C_XBC // SSM_XBC)),
            pl.BlockSpec((None, c, LANE), lambda bi, ci: (bi, ci, COL_SMALL // LANE)),
            pl.BlockSpec((None, SUBLANE, SSM_XBC), lambda bi, ci: (bi, 0, 0)),
            state_spec,
            pl.BlockSpec((SSM_TAPS, SSM_XBC), const2),
            pl.BlockSpec((1, SSM_XBC), const2),
            small, small, small,
            pl.BlockSpec((1, GROUP_W), const2),
        ],
        out_specs=[
            pl.BlockSpec((None, c, GROUP_W), lambda bi, ci: (bi, ci, 0)),
            state_spec,
        ],
        out_shape=[jax.ShapeDtypeStruct((b, l, GROUP_W), BF16),
                   jax.ShapeDtypeStruct((b,) + gshape, F32)],
        scratch_shapes=[pltpu.VMEM(gshape, F32), pltpu.VMEM((SUBLANE, SSM_XBC), F32),
                        pltpu.VMEM((c, GROUP_W), F32)],
        compiler_params=_cparams(("parallel", "arbitrary")),
        name="ssd_mixer",
    )(proj3, proj3, proj3, cbuf8, s0.reshape((b,) + gshape), prm['ssm_conv_w'],
      prm['ssm_conv_b'].reshape(1, SSM_XBC),
      lanes(prm['ssm_dt_bias']), lanes(prm['ssm_A_log']), lanes(prm['ssm_D']),
      prm['ssm_norm_w'].reshape(1, GROUP_W))
    return o, s_new.reshape(b, SSM_HEADS, SSM_P, SSM_N)


def _swa_weight(d):
    mult = jnp.zeros(d.shape, F32)
    for window, dil in SWA_PATTERNS:
        ok = (d >= 0) & (d <= window) & ((d & (dil - 1)) == 0)
        mult = mult + jnp.where(ok, 1.0, 0.0)
    return mult


def _swa_scores(q, k, d, slope):
    s = dot_nt(q.astype(BF16), k.astype(BF16), None) * (SWA_HD ** -0.5)
    mult = _swa_weight(d)
    s = s - slope * d.astype(F32)
    return jnp.where(mult > 0.0, s, NEG_INF), mult


def _swa_prompt_kernel(slopes_ref, q_ref, k_ref, v_ref, lw_ref, o_ref, m_scr, l_scr, acc_scr, *, t, sub):
    h = pl.program_id(1)
    qi = pl.program_id(2)
    ki = pl.program_id(3)

    @pl.when(ki == 0)
    def _():
        m_scr[...] = jnp.full(m_scr.shape, NEG_INF, F32)
        l_scr[...] = jnp.zeros(l_scr.shape, F32)
        acc_scr[...] = jnp.zeros(acc_scr.shape, F32)

    @pl.when(ki <= qi)
    def _():
        col = (ki * t + _iota2((1, t), 1)).astype(F32) * slopes_ref[h]
        kb = k_ref[...].astype(BF16)
        vb = v_ref[...].astype(BF16)
        def qk(r):
            q = (q_ref[r:r + sub, :] * (SWA_HD ** -0.5)).astype(BF16)
            return _dot(q, kb, NT, None)

        nxt = qk(0)
        for r in range(0, t, sub):
            rs = slice(r, r + sub)
            s = nxt + (lw_ref[qi - ki, rs, :] + col)
            if r + sub < t:
                nxt = qk(r + sub)
            m_old = m_scr[rs, :]
            m_new = jnp.maximum(m_old, jnp.max(s, axis=-1, keepdims=True))
            alpha = jnp.exp(m_old - m_new)
            p = jnp.exp(s - m_new)
            l_scr[rs, :] = alpha * l_scr[rs, :] + jnp.sum(p, axis=-1, keepdims=True)
            acc_scr[rs, :] = alpha * acc_scr[rs, :] + _dot(p.astype(BF16), vb, NN, None)
            m_scr[rs, :] = m_new

    @pl.when(ki == pl.num_programs(3) - 1)
    def _():
        o_ref[...] = (acc_scr[...] / l_scr[...]).astype(o_ref.dtype)


def _alibi_slopes():
    return jnp.asarray([2.0 ** (-8.0 * (i + 1) / SWA_HEADS) for i in range(SWA_HEADS)], F32)


def _swa_log_weight_tiles(n, t):
    d = (jnp.arange(n)[:, None, None] * t + jnp.arange(t)[None, :, None]) - jnp.arange(t)[None, None, :]
    mult = _swa_weight(d.astype(jnp.int32))
    return jnp.where(mult > 0.0, jnp.log(jnp.maximum(mult, 1.0)), NEG_INF)


def swa_prompt(proj3, k4, v4, li, t):
    b, l, _ = proj3.shape
    qc = COL_D_Q // SWA_HD
    n = l // t
    kv_spec = pl.BlockSpec((None, None, t, SWA_HD), lambda bi, h, qi, ki: (li, bi, jnp.minimum(ki, qi), h))

    return pl.pallas_call(
        functools.partial(_swa_prompt_kernel, t=t, sub=min(t, 256)),
        grid=(b, SWA_HEADS, n, n),
        in_specs=[
            pl.BlockSpec(memory_space=pltpu.SMEM),
            pl.BlockSpec((None, t, SWA_HD), lambda bi, h, qi, ki: (bi, qi, qc + h)),
            kv_spec,
            kv_spec,
            pl.BlockSpec((n, t, t), lambda bi, h, qi, ki: (0, 0, 0)),
        ],
        out_specs=pl.BlockSpec((None, t, SWA_HD), lambda bi, h, qi, ki: (bi, qi, h)),
        out_shape=jax.ShapeDtypeStruct((b, l, GROUP_W), BF16),
        scratch_shapes=[pltpu.VMEM((t, 1), F32), pltpu.VMEM((t, 1), F32), pltpu.VMEM((t, SWA_HD), F32)],
        compiler_params=_cparams(("parallel", "parallel", "parallel", "arbitrary")),
        name="swa_prompt",
    )(_alibi_slopes(), proj3, k4, v4, _swa_log_weight_tiles(n, t))


def _swa_sample_kernel(slopes_ref, q_ref, k_ref, v_ref, ck_ref, cv_ref, o_ref, *, t, wb):
    d_c = (wb + _iota2((t, wb), 0)) - _iota2((t, wb), 1)
    d_n = _iota2((t, t), 0) - _iota2((t, t), 1)
    for h in range(SWA_HEADS):
        hs = slice(h * SWA_HD, (h + 1) * SWA_HD)
        ck = ck_ref[pl.ds(h, wb, stride=SWA_HEADS), :]
        cv = cv_ref[pl.ds(h, wb, stride=SWA_HEADS), :]
        q = q_ref[:, hs]
        s_c, mult_c = _swa_scores(q, ck, d_c, slopes_ref[h])
        s_n, mult_n = _swa_scores(q, k_ref[:, hs], d_n, slopes_ref[h])
        m = jnp.maximum(jnp.max(s_c, axis=-1, keepdims=True), jnp.max(s_n, axis=-1, keepdims=True))
        p_c = jnp.exp(s_c - m) * mult_c
        p_n = jnp.exp(s_n - m) * mult_n
        den = jnp.sum(p_c, axis=-1, keepdims=True) + jnp.sum(p_n, axis=-1, keepdims=True)
        num = (dot_nn(p_c.astype(BF16), cv.astype(BF16), None)
               + dot_nn(p_n.astype(BF16), v_ref[:, hs].astype(BF16), None))
        o_ref[:, hs] = (num / den).astype(o_ref.dtype)


def swa_sample(proj3, k4, v4, li, cache_k_all, cache_v_all):
    b, t, _ = proj3.shape
    depth, _, wb = cache_k_all.shape[:3]
    ck = cache_k_all.reshape(depth, b, wb * SWA_HEADS, SWA_HD)
    cv = cache_v_all.reshape(depth, b, wb * SWA_HEADS, SWA_HD)
    new_spec = pl.BlockSpec((None, None, t, GROUP_W), lambda bi: (li, bi, 0, 0))
    cache_spec = pl.BlockSpec((None, None, wb * SWA_HEADS, SWA_HD), lambda bi: (li, bi, 0, 0))
    return pl.pallas_call(
        functools.partial(_swa_sample_kernel, t=t, wb=wb),
        grid=(b,),
        in_specs=[
            pl.BlockSpec(memory_space=pltpu.SMEM),
            pl.BlockSpec((None, t, GROUP_W), lambda bi: (bi, 0, COL_D_Q // GROUP_W)),
            new_spec,
            new_spec,
            cache_spec,
            cache_spec,
        ],
        out_specs=pl.BlockSpec((None, t, GROUP_W), lambda bi: (bi, 0, 0)),
        out_shape=jax.ShapeDtypeStruct((b, t, GROUP_W), BF16),
        compiler_params=_cparams(("parallel",)),
        name="swa_sample",
    )(_alibi_slopes(), proj3, k4, v4, ck, cv)


def _tiles(m):
    return (256, 1024) if m >= 1024 else (m, m)


def _front_pad_rows(t, rows=SUBLANE):
    return jnp.pad(t, ((0, 0), (rows - t.shape[1], 0), (0, 0)))


W_IN_SEGMENTS = (('a_qkv', 3 * GROUP_W), ('a_z', GROUP_W), ('a_b', GDN_HEADS), ('a_a', GDN_HEADS),
                 ('b_rkv', 3 * GROUP_W), ('b_lora', RWKV_LORA), ('c_z', GROUP_W), ('c_xbc', SSM_XBC),
                 ('c_dt', SSM_HEADS), ('d_q', GROUP_W), ('d_k', GROUP_W), ('d_v', GROUP_W))
W_IN_SRC = {}
_o = 0
for _name, _n in W_IN_SEGMENTS:
    W_IN_SRC[_name] = _o
    _o += _n
W_IN_WIDE = ((COL_A_QKV, 3 * GROUP_W, W_IN_SRC['a_qkv']), (COL_B_RKV, 3 * GROUP_W, W_IN_SRC['b_rkv']),
             (COL_D_Q, GROUP_W, W_IN_SRC['d_q']), (COL_A_Z, GROUP_W, W_IN_SRC['a_z']),
             (COL_C_Z, GROUP_W, W_IN_SRC['c_z']), (COL_C_XBC, SSM_XBC, W_IN_SRC['c_xbc']))
IN_TN = 512
assert COL_B_LORA % IN_TN == 0 and N_PROJ - COL_B_LORA == IN_TN


ROW_UNIT = 2 * SUBLANE


def _w_in_tile_rows():
    rows = []
    for j in range(N_PROJ // IN_TN - 1):
        c0 = j * IN_TN
        dest, width, src = next(s for s in W_IN_WIDE if s[0] <= c0 < s[0] + s[1])
        assert c0 + IN_TN <= dest + width and (src + c0 - dest) % ROW_UNIT == 0
        rows.append((src + (c0 - dest)) // ROW_UNIT)
    rows.append(0)
    return jnp.asarray(rows, jnp.int32)


W_IN_TAIL = ((0, 'b_lora', RWKV_LORA), (LORA_PAD + SM_GDN_B, 'a_b', 2 * GDN_HEADS),
             (LORA_PAD + SM_SSM_DT, 'c_dt', SSM_HEADS))
assert W_IN_SRC['a_a'] == W_IN_SRC['a_b'] + GDN_HEADS and SM_GDN_A == SM_GDN_B + GDN_HEADS


def _mm_in_kernel(rows_ref, x_ref, wt_ref, *rest):
    del rows_ref
    *tail_refs, o_ref, w_scr = rest

    @pl.when(pl.program_id(1) == 0)
    def _():
        @pl.when(pl.program_id(0) < pl.num_programs(0) - 1)
        def _():
            w_scr[...] = wt_ref[0].astype(BF16)

        @pl.when(pl.program_id(0) == pl.num_programs(0) - 1)
        def _():
            w_scr[...] = jnp.zeros(w_scr.shape, BF16)
            for (row, _, n), ref in zip(W_IN_TAIL, tail_refs):
                w_scr[row:row + n, :] = ref[0].astype(BF16)

    o_ref[...] = _dot(x_ref[...], w_scr[...], NT, None)


def mm_in(h, w_in_t_all, li, tm):
    m, kd = h.shape

    def rows_spec(n, start):
        assert start % ROW_UNIT == 0 and n % ROW_UNIT == 0
        return pl.BlockSpec((pl.Element(1), pl.Element(n), pl.Element(kd)),
                            lambda j, i, rows: (li, (start // ROW_UNIT) * ROW_UNIT, 0))

    grid_spec = pltpu.PrefetchScalarGridSpec(
        num_scalar_prefetch=1,
        grid=(N_PROJ // IN_TN, m // tm),
        in_specs=[
            pl.BlockSpec((tm, kd), lambda j, i, rows: (i, 0)),
            pl.BlockSpec((pl.Element(1), pl.Element(IN_TN), pl.Element(kd)),
                         lambda j, i, rows: (li, rows[j] * ROW_UNIT, 0)),
        ] + [rows_spec(n, W_IN_SRC[name]) for _, name, n in W_IN_TAIL],
        out_specs=pl.BlockSpec((tm, IN_TN), lambda j, i, rows: (i, j)),
        scratch_shapes=[pltpu.VMEM((IN_TN, kd), BF16)],
    )
    return pl.pallas_call(
        _mm_in_kernel,
        grid_spec=grid_spec,
        out_shape=jax.ShapeDtypeStruct((m, N_PROJ), F32),
        compiler_params=_cparams(("arbitrary", "arbitrary")),
        name="mm_in",
    )(_w_in_tile_rows(), h, *([w_in_t_all] * (1 + len(W_IN_TAIL))))


def decoder_layer(x, prm, wts, past, swa_cache, chunks, li, depth, kv=(None, None), h=None, next_pre_w=None):
    b, l, _ = x.shape
    m = b * l
    tr, tm = _tiles(m)
    x2 = x.reshape(m, D_MODEL)
    c_gdn, c_rwkv, c_ssd = chunks

    if h is None:
        h = rms_cast(x2, prm['norm_mix_pre'], tr)
    proj = mm_in(h, wts['w_in_t_all'], li, tm)
    proj3 = proj.reshape(b, l, N_PROJ)
    kbuf, vbuf = matmul_into(h, wts['w_in_t_all'], W_IN_SRC['d_k'], W_IN_SRC['d_v'], GROUP_W, kv, li, depth, tm,
                             256, "mm_kv")
    k4 = kbuf.reshape(depth, b, l, GROUP_W)
    v4 = vbuf.reshape(depth, b, l, GROUP_W)

    o_a, gdn_s = gdn_mixer(proj3, _front_pad_rows(past['gdn_conv']), past['gdn'], prm['gdn_conv_w'],
                           prm['gdn_A_log'], prm['gdn_dt_bias'], prm['gdn_norm_w'], c_gdn)
    shift = past['rwkv_shift'][:, None, :]
    sh_rkv8 = _front_pad_rows(shift[:, :, :3 * GROUP_W])
    sh_lora8 = _front_pad_rows(jnp.pad(shift[:, :, 3 * GROUP_W:], ((0, 0), (0, 0), (0, LORA_PAD - RWKV_LORA))))
    o_b, rwkv_s = rwkv_mixer(proj3, sh_rkv8, sh_lora8, past['rwkv'], prm, c_rwkv)
    o_c, ssm_s = ssd_mixer(proj3, _front_pad_rows(past['ssm_conv']), past['ssm'], prm, c_ssd)
    if swa_cache is None:
        o_d = swa_prompt(proj3, k4, v4, li, min(l, SWA_TILE))
    else:
        o_d = swa_sample(proj3, k4, v4, li, swa_cache[0], swa_cache[1])

    gdn_conv = proj3[:, l - (GDN_TAPS - 1):, COL_A_QKV:COL_A_QKV + 3 * GROUP_W]
    ssm_conv = proj3[:, l - (SSM_TAPS - 1):, COL_C_XBC:COL_C_XBC + SSM_XBC]
    rwkv_shift = jnp.concatenate([proj3[:, l - 1, COL_B_RKV:COL_B_RKV + 3 * GROUP_W],
                                  proj3[:, l - 1, COL_B_LORA:COL_B_LORA + RWKV_LORA]], axis=-1)

    y = matmul_groups([o.reshape(m, GROUP_W) for o in (o_a, o_b, o_c, o_d)], wts['w_out_all'], li, tm, 512,
                      "mm_out")
    x2, h2 = add_rms_cast(x2, y, prm['norm_mix_post'], prm['norm_ffn_pre'], tr)
    state8 = _front_pad_rows(past['ffn_conv'])
    if l % tm == 0:
        act, cst = ffn_up_act(h2, wts['w_up_all'], li, state8, prm['ffn_conv_w'], prm['ffn_conv_b'], l, tm, 256,
                              256)
        ffn_conv = cst[:, SUBLANE - (FFN_TAPS - 1):, :]
    else:
        up3 = matmul(h2, wts['w_up_all'], tm, 512, D_MODEL, "mm_up", li).reshape(b, l, 2 * D_FF)
        act = ffn_act(up3, state8, prm['ffn_conv_w'], prm['ffn_conv_b'], l, D_FF // 2).reshape(m, D_FF)
        ffn_conv = up3[:, l - (FFN_TAPS - 1):, :D_FF]
    y2 = matmul(act, wts['w_down_all'], tm, 512, D_FF // 2, "mm_down", li)
    if next_pre_w is None:
        x2, h_next = add_rms(x2, y2, prm['norm_ffn_post'], tr), None
    else:
        x2, h_next = add_rms_cast(x2, y2, prm['norm_ffn_post'], next_pre_w, tr)
    return x2.reshape(b, l, D_MODEL), h_next, (kbuf, vbuf), (gdn_s, gdn_conv, rwkv_s, rwkv_shift, ssm_s, ssm_conv,
                                                             ffn_conv)


def _zero_past(bsz):
    return {
        'gdn': jnp.zeros((bsz, GDN_HEADS, GDN_D, GDN_D), F32),
        'gdn_conv': jnp.zeros((bsz, GDN_TAPS - 1, 3 * GROUP_W), F32),
        'rwkv': jnp.zeros((bsz, RWKV_HEADS, RWKV_HS, RWKV_HS), F32),
        'rwkv_shift': jnp.zeros((bsz, 3 * GROUP_W + RWKV_LORA), F32),
        'ssm': jnp.zeros((bsz, SSM_HEADS, SSM_P, SSM_N), F32),
        'ssm_conv': jnp.zeros((bsz, SSM_TAPS - 1, SSM_XBC), F32),
        'ffn_conv': jnp.zeros((bsz, FFN_TAPS - 1, D_FF), F32),
    }


PARAM_NAMES = ('norm_mix_pre', 'norm_mix_post', 'norm_ffn_pre', 'norm_ffn_post', 'w_in', 'w_out', 'gdn_conv_w',
               'gdn_A_log', 'gdn_dt_bias', 'gdn_norm_w', 'rwkv_mu', 'rwkv_w0', 'rwkv_w2', 'rwkv_a0', 'rwkv_a2',
               'rwkv_g2', 'rwkv_k_k', 'rwkv_k_a', 'rwkv_r_k', 'rwkv_ln_w', 'rwkv_ln_b', 'ssm_conv_w', 'ssm_conv_b',
               'ssm_dt_bias', 'ssm_A_log', 'ssm_D', 'ssm_norm_w', 'ffn_w_up', 'ffn_conv_w', 'ffn_conv_b',
               'ffn_w_down')


def kernel(x_prompt, x_sample, state_gdn, state_gdn_conv, state_rwkv, state_rwkv_shift, state_ssm, state_ssm_conv, cache_swa_k, cache_swa_v, state_ffn_conv, norm_mix_pre, norm_mix_post, norm_ffn_pre, norm_ffn_post, w_in, w_out, gdn_conv_w, gdn_A_log, gdn_dt_bias, gdn_norm_w, rwkv_mu, rwkv_w0, rwkv_w2, rwkv_a0, rwkv_a2, rwkv_g2, rwkv_k_k, rwkv_k_a, rwkv_r_k, rwkv_ln_w, rwkv_ln_b, ssm_conv_w, ssm_conv_b, ssm_dt_bias, ssm_A_log, ssm_D, ssm_norm_w, ffn_w_up, ffn_conv_w, ffn_conv_b, ffn_w_down):
    params = dict(zip(PARAM_NAMES, (norm_mix_pre, norm_mix_post, norm_ffn_pre, norm_ffn_post, w_in, w_out,
                                    gdn_conv_w, gdn_A_log, gdn_dt_bias, gdn_norm_w, rwkv_mu, rwkv_w0, rwkv_w2,
                                    rwkv_a0, rwkv_a2, rwkv_g2, rwkv_k_k, rwkv_k_a, rwkv_r_k, rwkv_ln_w, rwkv_ln_b,
                                    ssm_conv_w, ssm_conv_b, ssm_dt_bias, ssm_A_log, ssm_D, ssm_norm_w, ffn_w_up,
                                    ffn_conv_w, ffn_conv_b, ffn_w_down)))
    depth = w_in.shape[0]
    xp, xs = x_prompt, x_sample
    t_dec = x_sample.shape[1]
    prompt_states, sample_states = [], []
    hp = hs = None
    kvp = kvs = (None, None)
    wts = {'w_in_t_all': jnp.swapaxes(w_in, 1, 2), 'w_out_all': w_out, 'w_up_all': ffn_w_up,
           'w_down_all': ffn_w_down.astype(BF16)}
    for li in range(depth):
        prm = {k: v[li] for k, v in params.items()}
        nxt = norm_mix_pre[li + 1] if li + 1 < depth else None
        xp, hp, kvp, stp = decoder_layer(xp, prm, wts, _zero_past(xp.shape[0]), None, (64, 64, 128),
                                         li, depth, kvp, hp, nxt)
        past = {'gdn': state_gdn[li], 'gdn_conv': state_gdn_conv[li], 'rwkv': state_rwkv[li],
                'rwkv_shift': state_rwkv_shift[li], 'ssm': state_ssm[li], 'ssm_conv': state_ssm_conv[li],
                'ffn_conv': state_ffn_conv[li]}
        xs, hs, kvs, sts = decoder_layer(xs, prm, wts, past, (cache_swa_k, cache_swa_v),
                                         (t_dec, t_dec, t_dec), li, depth, kvs, hs, nxt)
        prompt_states.append(stp)
        sample_states.append(sts)

    def window_rows(buf, x):
        bsz, l = x.shape[0], x.shape[1]
        rows = buf.reshape(depth, bsz, l, SWA_HEADS, SWA_HD)
        return rows[:, :, max(l - SWA_MAX_WINDOW, 0):]

    def outputs(states, kv, x):
        st = [jnp.stack(t) for t in zip(*states)]
        return (*st[:6], window_rows(kv[0], x), window_rows(kv[1], x), st[6])

    return (xp, xs, *outputs(prompt_states, kvp, x_prompt), *outputs(sample_states, kvs, x_sample))
```

```python
import functools

import jax
import jax.numpy as jnp
from jax import lax
from jax.experimental import pallas as pl
from jax.experimental.pallas import tpu as pltpu

F32 = jnp.float32
BF16 = jnp.bfloat16
HI = lax.Precision.HIGHEST

D_MODEL = 4096
GROUP_W = D_MODEL // 4
GDN_HEADS = 8
GDN_D = GROUP_W // GDN_HEADS
GDN_TAPS = 4
RWKV_HS = 64
RWKV_HEADS = GROUP_W // RWKV_HS
RWKV_W_LORA = 64
RWKV_A_LORA = 64
RWKV_G_LORA = 160
RWKV_LORA = RWKV_W_LORA + RWKV_A_LORA + RWKV_G_LORA
RWKV_GN_EPS = 64e-5
SSM_P = 64
SSM_HEADS = GROUP_W // SSM_P
SSM_GROUPS = 2
SSM_N = 128
SSM_TAPS = 4
SSM_XBC = GROUP_W + 2 * SSM_GROUPS * SSM_N
SWA_HEADS = 8
SWA_HD = GROUP_W // SWA_HEADS
SWA_PATTERNS = ((128, 1), (512, 4), (2048, 16))
SWA_MAX_WINDOW = 2048
D_FF = 256 * ((8 * D_MODEL // 3 + 255) // 256)
FFN_TAPS = 3
NORM_EPS = 1e-6
NEG_INF = -1e30

LANE = 128
SUBLANE = 8
LORA_PAD = 384
G_LORA_PAD = LORA_PAD - RWKV_W_LORA - RWKV_A_LORA

COL_A_QKV = 0
COL_B_RKV = 3 * GROUP_W
COL_D_Q = 6 * GROUP_W
COL_A_Z = 7 * GROUP_W
COL_C_Z = 8 * GROUP_W
COL_C_XBC = 9 * GROUP_W
COL_B_LORA = COL_C_XBC + SSM_XBC
COL_SMALL = COL_B_LORA + LORA_PAD
N_PROJ = COL_SMALL + LANE
SM_GDN_B = 0
SM_GDN_A = GDN_HEADS
SM_SSM_DT = 2 * GDN_HEADS

VMEM_LIMIT = 56 * 1024 * 1024
SWA_TILE = 1024


def _cparams(sem):
    return pltpu.CompilerParams(dimension_semantics=sem, vmem_limit_bytes=VMEM_LIMIT)


def _dot(a, b, dims, prec):
    return lax.dot_general(a, b, (dims, ((), ())), precision=prec, preferred_element_type=F32)


def dot_nn(a, b, prec=HI):
    return _dot(a, b, ((1,), (0,)), prec)


def dot_nt(a, b, prec=HI):
    return _dot(a, b, ((1,), (1,)), prec)


def dot_tn(a, b, prec=HI):
    return _dot(a, b, ((0,), (0,)), prec)


def _silu(x):
    return x * jax.nn.sigmoid(x)


def _iota2(shape, axis):
    return lax.broadcasted_iota(jnp.int32, shape, axis)


def _log2(n):
    s = n.bit_length() - 1
    assert 1 << s == n
    return s


NN = ((1,), (0,))
NT = ((1,), (1,))
TN = ((0,), (0,))


def _split(x):
    hi = x.astype(BF16)
    return hi, (x - hi.astype(F32)).astype(BF16)


def _split_rows(x):
    hi = x.astype(BF16)
    hif = hi.astype(F32)
    return jnp.concatenate([hif, x - hif], axis=0).astype(BF16), hi


def dot3(ap, bp, dims):
    return (_dot(ap[0], bp[0], dims, None) + _dot(ap[0], bp[1], dims, None)
            + _dot(ap[1], bp[0], dims, None))


def dot3s(a, bp, dims):
    r = a.shape[0]
    stacked, hi = _split_rows(a)
    both = _dot(stacked, bp[0], dims, None)
    return both[:r] + both[r:] + _dot(hi, bp[1], dims, None)


def dot1(a, b, dims):
    return _dot(a.astype(BF16), b.astype(BF16), dims, None)


def _inv_unit_lower_multi(ms, c):
    row = _iota2((c, c), 0)
    col = _iota2((c, c), 1)
    eye = jnp.where(row == col, 1.0, 0.0).astype(F32)
    base = min(SUBLANE, c)
    sb = _log2(base)
    blk = (row >> sb) == (col >> sb)
    ps = [jnp.where(blk, -m, 0.0) for m in ms]
    ts = [eye + p for p in ps]
    if sb > 1:
        ps = [_dot(p.astype(BF16), p.astype(BF16), NN, None) for p in ps]
        for _ in range(sb - 2):
            both = [_dot(jnp.concatenate([t, p], axis=0).astype(BF16), p.astype(BF16), NN, None)
                    for t, p in zip(ts, ps)]
            ts = [t + x[:c] for t, x in zip(ts, both)]
            ps = [x[c:] for x in both]
        ts = [t + _dot(t.astype(BF16), p.astype(BF16), NN, None) for t, p in zip(ts, ps)]
    s = base
    while s < c:
        ls = _log2(s)
        off = ((row >> (ls + 1)) == (col >> (ls + 1))) & ((row >> ls) > (col >> ls))
        tbs = [t.astype(BF16) for t in ts]
        inner = [_dot(jnp.where(off, m, 0.0).astype(BF16), tb, NN, None) for m, tb in zip(ms, tbs)]
        ts = [t - _dot(tb, x.astype(BF16), NN, None) for t, tb, x in zip(ts, tbs, inner)]
        s *= 2
    res = [eye - t - dot3s(m, _split(t), NN) for m, t in zip(ms, ts)]
    return [t + _dot(t.astype(BF16), r.astype(BF16), NN, None) for t, r in zip(ts, res)]


def _blockdiag_rows(x, half):
    left = _iota2(x.shape, 1) < half
    return jnp.concatenate([jnp.where(left, x, 0.0), jnp.where(left, 0.0, x)], axis=0)


def _head_sums(x, ones_bd):
    r = x.shape[0]
    stacked, _ = _split_rows(x)
    both = _dot(stacked, ones_bd, NN, None)
    return both[:r] + both[r:]


def _inv_unit_lower_pairs(ms, c):
    row = _iota2((c, 2 * c), 0)
    col = _iota2((c, 2 * c), 1) & (c - 1)
    eye = jnp.where(row == col, 1.0, 0.0).astype(F32)
    base = min(SUBLANE, c)
    sb = _log2(base)
    blk = (row >> sb) == (col >> sb)

    def mm(a, b):
        return _dot(a.astype(BF16), _blockdiag_rows(b, c).astype(BF16), NN, None)

    ps = [jnp.where(blk, -m, 0.0) for m in ms]
    ts = [eye + p for p in ps]
    if sb > 1:
        ps = [mm(p, p) for p in ps]
        for _ in range(sb - 2):
            both = [mm(jnp.concatenate([t, p], axis=0), p) for t, p in zip(ts, ps)]
            ts = [t + x[:c] for t, x in zip(ts, both)]
            ps = [x[c:] for x in both]
        ts = [t + mm(t, p) for t, p in zip(ts, ps)]
    s = base
    while s < c:
        ls = _log2(s)
        off = ((row >> (ls + 1)) == (col >> (ls + 1))) & ((row >> ls) > (col >> ls))
        inner = [mm(jnp.where(off, m, 0.0), t) for m, t in zip(ms, ts)]
        ts = [t - mm(t, x) for t, x in zip(ts, inner)]
        s *= 2
    res = [eye - t - dot3s(m, _split(_blockdiag_rows(t, c)), NN) for m, t in zip(ms, ts)]
    return [t + mm(t, r) for t, r in zip(ts, res)]


def _row_getter(x, c):
    if c % LANE:
        x = jnp.concatenate([x, jnp.zeros((LANE - c % LANE, LANE), F32)], axis=0)
    xt = x.T
    return lambda lane: xt[lane:lane + 1, :c]


def _cumsum_rows(x, c):
    tri = jnp.where(_iota2((c, c), 0) >= _iota2((c, c), 1), 1.0, 0.0).astype(F32)
    return dot_nn(tri, x)


def _shifted_taps(tail, x, taps):
    c = x.shape[0]
    xp = jnp.concatenate([tail, x], axis=0)
    out = []
    for s in range(taps - 1, 0, -1):
        out.append(pltpu.roll(xp, s, 0)[SUBLANE:SUBLANE + c])
    out.append(x)
    return out


def _rms_cast_kernel(x_ref, w_ref, o_ref):
    x = x_ref[...]
    y = x * lax.rsqrt(jnp.mean(x * x, axis=-1, keepdims=True) + NORM_EPS)
    o_ref[...] = (y * w_ref[...]).astype(o_ref.dtype)


def rms_cast(x, w, tr):
    m, d = x.shape
    return pl.pallas_call(
        _rms_cast_kernel,
        grid=(m // tr,),
        in_specs=[pl.BlockSpec((tr, d), lambda i: (i, 0)), pl.BlockSpec((1, d), lambda i: (0, 0))],
        out_specs=pl.BlockSpec((tr, d), lambda i: (i, 0)),
        out_shape=jax.ShapeDtypeStruct((m, d), BF16),
        compiler_params=_cparams(("parallel",)),
        name="rms_cast",
    )(x, w.reshape(1, d))


def _add_rms_kernel(x_ref, y_ref, w_ref, o_ref):
    y = y_ref[...]
    yn = y * lax.rsqrt(jnp.mean(y * y, axis=-1, keepdims=True) + NORM_EPS)
    o_ref[...] = x_ref[...] + yn * w_ref[...]


def add_rms(x, y, w, tr):
    m, d = x.shape
    return pl.pallas_call(
        _add_rms_kernel,
        grid=(m // tr,),
        in_specs=[pl.BlockSpec((tr, d), lambda i: (i, 0)), pl.BlockSpec((tr, d), lambda i: (i, 0)),
                  pl.BlockSpec((1, d), lambda i: (0, 0))],
        out_specs=pl.BlockSpec((tr, d), lambda i: (i, 0)),
        out_shape=jax.ShapeDtypeStruct((m, d), F32),
        compiler_params=_cparams(("parallel",)),
        name="add_rms",
    )(x, y, w.reshape(1, d))


def _add_rms_cast_kernel(x_ref, y_ref, w_ref, wn_ref, o_ref, h_ref):
    y = y_ref[...]
    yn = y * lax.rsqrt(jnp.mean(y * y, axis=-1, keepdims=True) + NORM_EPS)
    x = x_ref[...] + yn * w_ref[...]
    o_ref[...] = x
    xn = x * lax.rsqrt(jnp.mean(x * x, axis=-1, keepdims=True) + NORM_EPS)
    h_ref[...] = (xn * wn_ref[...]).astype(h_ref.dtype)


def add_rms_cast(x, y, w, w_next, tr):
    m, d = x.shape
    row = pl.BlockSpec((tr, d), lambda i: (i, 0))
    vec = pl.BlockSpec((1, d), lambda i: (0, 0))
    return pl.pallas_call(
        _add_rms_cast_kernel,
        grid=(m // tr,),
        in_specs=[row, row, vec, vec],
        out_specs=[row, row],
        out_shape=[jax.ShapeDtypeStruct((m, d), F32), jax.ShapeDtypeStruct((m, d), BF16)],
        compiler_params=_cparams(("parallel",)),
        name="add_rms_cast",
    )(x, y, w.reshape(1, d), w_next.reshape(1, d))


def _mm_kernel(a_ref, w_ref, o_ref, *, nk):
    p = jnp.dot(a_ref[...], w_ref[...].astype(BF16), preferred_element_type=F32)
    if nk == 1:
        o_ref[...] = p
    else:
        k = pl.program_id(2)

        @pl.when(k == 0)
        def _():
            o_ref[...] = p

        @pl.when(k > 0)
        def _():
            o_ref[...] += p


def matmul(a, w, tm, tn, tk, name, li=None):
    m, kd = a.shape
    n = w.shape[-1]
    nk = kd // tk
    if li is None:
        w_spec = pl.BlockSpec((tk, tn), lambda i, j, k: (k, j))
    else:
        w_spec = pl.BlockSpec((None, tk, tn), lambda i, j, k: (li, k, j))
    return pl.pallas_call(
        functools.partial(_mm_kernel, nk=nk),
        grid=(m // tm, n // tn, nk),
        in_specs=[pl.BlockSpec((tm, tk), lambda i, j, k: (i, k)), w_spec],
        out_specs=pl.BlockSpec((tm, tn), lambda i, j, k: (i, j)),
        out_shape=jax.ShapeDtypeStruct((m, n), F32),
        compiler_params=_cparams(("parallel", "parallel", "arbitrary")),
        name=name,
    )(a, w)


def _mm_into_kernel(a_ref, w1_ref, w2_ref, *rest):
    o1_ref, o2_ref, w1_scr, w2_scr = rest[-4:]

    @pl.when(pl.program_id(1) == 0)
    def _():
        w1_scr[...] = w1_ref[0].astype(BF16)
        w2_scr[...] = w2_ref[0].astype(BF16)

    a = a_ref[...]
    o1_ref[...] = _dot(a, w1_scr[...], NT, None)
    o2_ref[...] = _dot(a, w2_scr[...], NT, None)


def matmul_into(a, w_t_all, row1, row2, n, bufs, li, depth, tm, tn, name):
    m, kd = a.shape
    assert row1 % ROW_UNIT == 0 and row2 % ROW_UNIT == 0 and tn % ROW_UNIT == 0

    def w_spec(row):
        return pl.BlockSpec((pl.Element(1), pl.Element(tn), pl.Element(kd)),
                            lambda j, i: (li, (row // ROW_UNIT + j * (tn // ROW_UNIT)) * ROW_UNIT, 0))

    in_specs = [pl.BlockSpec((tm, kd), lambda j, i: (i, 0)), w_spec(row1), w_spec(row2)]
    args = [a, w_t_all, w_t_all]
    aliases = {}
    if bufs[0] is not None:
        in_specs += [pl.BlockSpec(memory_space=pl.ANY)] * 2
        args += list(bufs)
        aliases = {3: 0, 4: 1}
    out_spec = pl.BlockSpec((None, tm, tn), lambda j, i: (li, i, j))
    return pl.pallas_call(
        _mm_into_kernel,
        grid=(n // tn, m // tm),
        in_specs=in_specs,
        out_specs=[out_spec, out_spec],
        out_shape=[jax.ShapeDtypeStruct((depth, m, n), F32)] * 2,
        scratch_shapes=[pltpu.VMEM((tn, kd), BF16), pltpu.VMEM((tn, kd), BF16)],
        input_output_aliases=aliases,
        compiler_params=_cparams(("arbitrary", "arbitrary")),
        name=name,
    )(*args)


def _mm_groups_kernel(*refs):
    *a_refs, w_ref, o_ref, w_scr = refs
    kg = a_refs[0].shape[1]

    @pl.when(pl.program_id(1) == 0)
    def _():
        w_scr[...] = w_ref[...].astype(BF16)

    acc = jnp.dot(a_refs[0][...], w_scr[0:kg, :], preferred_element_type=F32)
    for g in range(1, len(a_refs)):
        acc = acc + jnp.dot(a_refs[g][...], w_scr[g * kg:(g + 1) * kg, :], preferred_element_type=F32)
    o_ref[...] = acc


def matmul_groups(parts, w_all, li, tm, tn, name):
    m, kg = parts[0].shape
    _, kd, n = w_all.shape
    return pl.pallas_call(
        _mm_groups_kernel,
        grid=(n // tn, m // tm),
        in_specs=[pl.BlockSpec((tm, kg), lambda j, i: (i, 0)) for _ in parts]
        + [pl.BlockSpec((None, kd, tn), lambda j, i: (li, 0, j))],
        out_specs=pl.BlockSpec((tm, tn), lambda j, i: (i, j)),
        out_shape=jax.ShapeDtypeStruct((m, n), F32),
        scratch_shapes=[pltpu.VMEM((kd, tn), BF16)],
        compiler_params=_cparams(("arbitrary", "arbitrary")),
        name=name,
    )(*parts, w_all)


def _ffn_act_kernel(g_ref, v_ref, halo_ref, st_ref, cw_ref, cb_ref, o_ref):
    g = g_ref[...]
    tail = jnp.where(pl.program_id(1) == 0, st_ref[...], halo_ref[...])
    taps = _shifted_taps(tail, g, FFN_TAPS)
    cw = cw_ref[...]
    y = taps[0] * cw[0:1]
    for i in range(1, FFN_TAPS):
        y = y + taps[i] * cw[i:i + 1]
    y = y + cb_ref[...]
    o_ref[...] = (_silu(y) * v_ref[...]).astype(o_ref.dtype)


def _ffn_up_act_kernel(x_ref, wg_ref, wv_ref, st_ref, cw_ref, cb_ref, act_ref, cst_ref, wg_scr, wv_scr, tail_scr,
                       *, tiles_per_seq, sub):
    i = pl.program_id(1)

    @pl.when(i == 0)
    def _():
        wg_scr[...] = wg_ref[...].astype(BF16)
        wv_scr[...] = wv_ref[...].astype(BF16)

    @pl.when(i % tiles_per_seq == 0)
    def _():
        tail_scr[...] = st_ref[...]

    wg = wg_scr[...]
    wv = wv_scr[...]
    cw = cw_ref[...]
    cb = cb_ref[...]
    tm = x_ref.shape[0]
    tail = tail_scr[...]

    def project(r):
        x = x_ref[r * sub:(r + 1) * sub, :]
        return jnp.dot(x, wg, preferred_element_type=F32), jnp.dot(x, wv, preferred_element_type=F32)

    nxt = project(0)
    for r in range(tm // sub):
        g, v = nxt
        if r + 1 < tm // sub:
            nxt = project(r + 1)
        taps = _shifted_taps(tail, g, FFN_TAPS)
        y = taps[0] * cw[0:1]
        for t in range(1, FFN_TAPS):
            y = y + taps[t] * cw[t:t + 1]
        act_ref[r * sub:(r + 1) * sub, :] = (_silu(y + cb) * v).astype(act_ref.dtype)
        tail = g[sub - SUBLANE:]
    tail_scr[...] = tail
    cst_ref[...] = tail


def ffn_up_act(h2, w_up_all, li, state8, conv_w, conv_b, l, tm, tn, sub):
    m, kd = h2.shape
    b = m // l
    nj = D_FF // tn
    tiles_per_seq = l // tm
    act, tails = pl.pallas_call(
        functools.partial(_ffn_up_act_kernel, tiles_per_seq=tiles_per_seq, sub=sub),
        grid=(nj, m // tm),
        in_specs=[
            pl.BlockSpec((tm, kd), lambda j, i: (i, 0)),
            pl.BlockSpec((None, kd, tn), lambda j, i: (li, 0, j)),
            pl.BlockSpec((None, kd, tn), lambda j, i: (li, 0, j + nj)),
            pl.BlockSpec((None, SUBLANE, tn), lambda j, i: (i // tiles_per_seq, 0, j)),
            pl.BlockSpec((FFN_TAPS, tn), lambda j, i: (0, j)),
            pl.BlockSpec((1, tn), lambda j, i: (0, j)),
        ],
        out_specs=[
            pl.BlockSpec((tm, tn), lambda j, i: (i, j)),
            pl.BlockSpec((None, SUBLANE, tn), lambda j, i: (i, 0, j)),
        ],
        out_shape=[jax.ShapeDtypeStruct((m, D_FF), BF16), jax.ShapeDtypeStruct((m // tm, SUBLANE, D_FF), F32)],
        scratch_shapes=[pltpu.VMEM((kd, tn), BF16), pltpu.VMEM((kd, tn), BF16), pltpu.VMEM((SUBLANE, tn), F32)],
        compiler_params=_cparams(("arbitrary", "arbitrary")),
        name="ffn_up_act",
    )(h2, w_up_all, w_up_all, state8, conv_w, conv_b.reshape(1, D_FF))
    return act, tails.reshape(b, tiles_per_seq, SUBLANE, D_FF)[:, tiles_per_seq - 1]


def ffn_act(up3, state8, conv_w, conv_b, ts, tn):
    b, l, _ = up3.shape
    nj = D_FF // tn
    hb = ts // SUBLANE
    return pl.pallas_call(
        _ffn_act_kernel,
        grid=(b, l // ts, nj),
        in_specs=[
            pl.BlockSpec((None, ts, tn), lambda bi, i, j: (bi, i, j)),
            pl.BlockSpec((None, ts, tn), lambda bi, i, j: (bi, i, j + nj)),
            pl.BlockSpec((None, SUBLANE, tn), lambda bi, i, j: (bi, jnp.maximum(i * hb - 1, 0), j)),
            pl.BlockSpec((None, SUBLANE, tn), lambda bi, i, j: (bi, 0, j)),
            pl.BlockSpec((FFN_TAPS, tn), lambda bi, i, j: (0, j)),
            pl.BlockSpec((1, tn), lambda bi, i, j: (0, j)),
        ],
        out_specs=pl.BlockSpec((None, ts, tn), lambda bi, i, j: (bi, i, j)),
        out_shape=jax.ShapeDtypeStruct((b, l, D_FF), BF16),
        compiler_params=_cparams(("parallel", "parallel", "parallel")),
        name="ffn_act",
    )(up3, up3, up3, state8, conv_w, conv_b.reshape(1, D_FF))


def _gdn_kernel(qkv_ref, z_ref, sm_ref, cbuf_ref, s0_ref, cw_ref, alog_ref, dtb_ref, nw_ref,
                o_ref, sout_ref, s_scr, tail_scr, *, c, nsub):
    ci = pl.program_id(1)

    @pl.when(ci == 0)
    def _():
        s_scr[...] = s0_ref[...]
        tail_scr[...] = cbuf_ref[...]

    x = qkv_ref[...]
    taps = _shifted_taps(tail_scr[...], x, GDN_TAPS)
    rows = nsub * c
    tail_scr[...] = x[rows - SUBLANE:]
    cw = cw_ref[...]
    y = taps[0] * cw[0:1]
    for i in range(1, GDN_TAPS):
        y = y + taps[i] * cw[i:i + 1]
    y = _silu(y)

    sm = sm_ref[...]
    beta_all = jax.nn.sigmoid(sm)
    g_all = -jnp.exp(alog_ref[...]) * jax.nn.softplus(sm + dtb_ref[...])
    ri = _iota2((rows, rows), 0)
    rj = _iota2((rows, rows), 1)
    in_chunk_tri = jnp.where((ri >= rj) & ((ri >> _log2(c)) == (rj >> _log2(c))), 1.0, 0.0).astype(F32)
    gcum_all = dot_nn(in_chunk_tri, g_all)
    rows_of = _row_getter(gcum_all, rows)

    row = _iota2((c, c), 0)
    col = _iota2((c, c), 1)
    tri = row >= col
    strict = row > col
    z = z_ref[...]
    nw = nw_ref[...]
    heads = range(GDN_HEADS)
    units = [(q, h) for q in range(nsub) for h in heads]
    ms, aqk, rhs, qd, kd, gls = [], [], [], [], [], []
    for qi, h in units:
        rs = slice(qi * c, (qi + 1) * c)
        lo = h * GDN_D
        q = y[rs, lo:lo + GDN_D]
        k = y[rs, GROUP_W + lo:GROUP_W + lo + GDN_D]
        v = y[rs, 2 * GROUP_W + lo:2 * GROUP_W + lo + GDN_D]
        q = q * lax.rsqrt(jnp.sum(q * q, axis=-1, keepdims=True) + 1e-6) * (GDN_D ** -0.5)
        k = k * lax.rsqrt(jnp.sum(k * k, axis=-1, keepdims=True) + 1e-6)
        beta = beta_all[rs, SM_GDN_B + h:SM_GDN_B + h + 1]
        gc = gcum_all[rs, SM_GDN_A + h:SM_GDN_A + h + 1]
        gam = jnp.exp(jnp.where(tri, gc - rows_of(SM_GDN_A + h)[:, qi * c:(qi + 1) * c], -jnp.inf))
        kbeta = k * beta
        mq = dot3s(jnp.concatenate([kbeta, q], axis=0), _split(k), NT)
        ms.append(jnp.where(strict, mq[:c] * gam, 0.0))
        aqk.append(mq[c:] * gam)
        eg = jnp.exp(gc)
        gl = gc[c - 1:c]
        rhs.append(jnp.concatenate([kbeta * eg, v * beta], axis=1))
        qd.append(q * eg)
        kd.append(k * jnp.exp(gl - gc))
        gls.append(gl)
    packed = _inv_unit_lower_pairs([jnp.concatenate([ms[i], ms[i + 1]], axis=1) for i in range(0, len(ms), 2)], c)
    ts = [t[:, half * c:(half + 1) * c] for t in packed for half in range(2)]
    wu = [dot3s(t, _split(x), NN) for t, x in zip(ts, rhs)]
    state = [s_scr[h] for h in heads]
    for qi in range(nsub):
        rs = slice(qi * c, (qi + 1) * c)
        ix = [qi * GDN_HEADS + h for h in heads]
        wqs = [dot3s(jnp.concatenate([wu[i][:, :GDN_D], qd[i]], axis=0), _split(state[h]), NN)
               for h, i in zip(heads, ix)]
        v_new = [wu[i][:, GDN_D:] - wqs[h][:c] for h, i in zip(heads, ix)]
        outs = [wqs[h][c:] + dot1(aqk[i], v_new[h], NN) for h, i in zip(heads, ix)]
        state = [state[h] * jnp.exp(gls[i]) + dot3(_split(kd[i]), _split(v_new[h]), TN) for h, i in zip(heads, ix)]
        for h in heads:
            lo = h * GDN_D
            o = outs[h]
            o = o * lax.rsqrt(jnp.mean(o * o, axis=-1, keepdims=True) + NORM_EPS) * nw
            o = o * _silu(z[rs, lo:lo + GDN_D])
            o_ref[rs, lo:lo + GDN_D] = o.astype(o_ref.dtype)
    for h in heads:
        s_scr[h] = state[h]

    @pl.when(ci == pl.num_programs(1) - 1)
    def _():
        sout_ref[...] = s_scr[...]


def gdn_mixer(proj3, cbuf8, s0, conv_w, a_log, dt_bias, norm_w, c):
    b, l, _ = proj3.shape
    alog_row = jnp.zeros((1, LANE), F32).at[0, SM_GDN_A:SM_GDN_A + GDN_HEADS].set(a_log)
    dtb_row = jnp.zeros((1, LANE), F32).at[0, SM_GDN_A:SM_GDN_A + GDN_HEADS].set(dt_bias)
    w3 = 3 * GROUP_W
    const2 = lambda bi, ci: (0, 0)
    nsub = next(k for k in (4, 2, 1) if l % (k * c) == 0)
    rb = nsub * c
    return pl.pallas_call(
        functools.partial(_gdn_kernel, c=c, nsub=nsub),
        grid=(b, l // rb),
        in_specs=[
            pl.BlockSpec((None, rb, w3), lambda bi, ci: (bi, ci, COL_A_QKV // w3)),
            pl.BlockSpec((None, rb, GROUP_W), lambda bi, ci: (bi, ci, COL_A_Z // GROUP_W)),
            pl.BlockSpec((None, rb, LANE), lambda bi, ci: (bi, ci, COL_SMALL // LANE)),
            pl.BlockSpec((None, SUBLANE, w3), lambda bi, ci: (bi, 0, 0)),
            pl.BlockSpec((None, GDN_HEADS, GDN_D, GDN_D), lambda bi, ci: (bi, 0, 0, 0)),
            pl.BlockSpec((GDN_TAPS, w3), const2),
            pl.BlockSpec((1, LANE), const2),
            pl.BlockSpec((1, LANE), const2),
            pl.BlockSpec((1, GDN_D), const2),
        ],
        out_specs=[
            pl.BlockSpec((None, rb, GROUP_W), lambda bi, ci: (bi, ci, 0)),
            pl.BlockSpec((None, GDN_HEADS, GDN_D, GDN_D), lambda bi, ci: (bi, 0, 0, 0)),
        ],
        out_shape=[jax.ShapeDtypeStruct((b, l, GROUP_W), BF16),
                   jax.ShapeDtypeStruct((b, GDN_HEADS, GDN_D, GDN_D), F32)],
        scratch_shapes=[pltpu.VMEM((GDN_HEADS, GDN_D, GDN_D), F32), pltpu.VMEM((SUBLANE, w3), F32)],
        compiler_params=_cparams(("parallel", "arbitrary")),
        name="gdn_mixer",
    )(proj3, proj3, proj3, cbuf8, s0, conv_w, alog_row, dtb_row, norm_w.reshape(1, GDN_D))


def _rwkv_kernel(rkv_ref, lora_ref, sh_rkv_ref, sh_lora_ref, s0_ref, mu_rkv_ref, mu_lora_ref,
                 w0_ref, w2_ref, a0_ref, a2_ref, g2_ref, kk_ref, ka_ref, rk_ref, lnw_ref, lnb_ref,
                 o_ref, sout_ref, s_scr, tail_rkv, tail_lora, *, c, nsub):
    ci = pl.program_id(1)

    @pl.when(ci == 0)
    def _():
        s_scr[...] = s0_ref[...]
        tail_rkv[...] = sh_rkv_ref[...]
        tail_lora[...] = sh_lora_ref[...]

    x = rkv_ref[...]
    xl = lora_ref[...]
    prev = _shifted_taps(tail_rkv[...], x, 2)[0]
    prev_l = _shifted_taps(tail_lora[...], xl, 2)[0]
    tail_rkv[...] = x[nsub * c - SUBLANE:]
    tail_lora[...] = xl[nsub * c - SUBLANE:]
    zm = x + (prev - x) * mu_rkv_ref[...]
    zl = xl + (prev_l - xl) * mu_lora_ref[...]
    r = zm[:, 0:GROUP_W]
    k = zm[:, GROUP_W:2 * GROUP_W]
    v = zm[:, 2 * GROUP_W:3 * GROUP_W]
    wd = zl[:, 0:RWKV_W_LORA]
    ad = zl[:, RWKV_W_LORA:RWKV_W_LORA + RWKV_A_LORA]
    gd = zl[:, RWKV_W_LORA + RWKV_A_LORA:LORA_PAD]

    w_log = -jax.nn.softplus(-(w0_ref[...] + dot3s(jnp.tanh(wd), _split(w2_ref[...]), NN))) - 0.5
    logw = -jnp.exp(w_log)
    a = jax.nn.sigmoid(a0_ref[...] + dot3s(ad, _split(a2_ref[...]), NN))
    gate = dot1(jax.nn.sigmoid(gd), g2_ref[...], NN)
    kkv = k * kk_ref[...]
    k2 = k * (1.0 + (a - 1.0) * ka_ref[...])
    rows = nsub * c
    ri = _iota2((rows, rows), 0)
    rj = _iota2((rows, rows), 1)
    lc_sh = _log2(c)
    in_chunk_tri = jnp.where((ri >= rj) & ((ri >> lc_sh) == (rj >> lc_sh)), 1.0, 0.0).astype(F32)
    lcum = dot_nn(in_chunk_tri, logw)

    n = RWKV_HS
    pw = 2 * n
    rowc = _iota2((c, 2 * c), 0)
    colc = _iota2((c, 2 * c), 1) & (c - 1)
    strict2 = rowc > colc
    tri2 = rowc >= colc
    same_head = (_iota2((pw, pw), 0) < n) == (_iota2((pw, pw), 1) < n)
    ones_bd = jnp.where(same_head, 1.0, 0.0).astype(BF16)
    rk = rk_ref[...]
    lnw = lnw_ref[...]
    lnb = lnb_ref[...]
    pairs = range(RWKV_HEADS // 2)
    sl = [slice(p * pw, (p + 1) * pw) for p in pairs]
    units = [(slice(q * c, (q + 1) * c), sl[p]) for q in range(nsub) for p in pairs]
    kk_ss = [_head_sums(jnp.square(kkv[rs, ps]), ones_bd) for rs, ps in units]
    bonus_s = [_head_sums(r[rs, ps] * k2[rs, ps] * rk[:, ps], ones_bd) for rs, ps in units]
    lhs, x_bs, x_ks, bk_end, l_last = [], [], [], [], []
    for (rs, ps), ss in zip(units, kk_ss):
        kk = kkv[rs, ps] * lax.rsqrt(ss + 1e-6)
        lc = lcum[rs, ps]
        ll = lc[c - 1:c]
        p_inv = jnp.exp(-lc)
        a_t = -kk * jnp.exp(lc - logw[rs, ps])
        b_vec = kk * a[rs, ps]
        r_t = r[rs, ps] * jnp.exp(lc)
        p_end = jnp.exp(ll - lc)
        ar = jnp.concatenate([a_t, r_t], axis=0)
        x_bs.append(dot3s(ar, _split(_blockdiag_rows(b_vec * p_inv, n)), NT))
        x_ks.append(dot3s(ar, _split(_blockdiag_rows(k2[rs, ps] * p_inv, n)), NT))
        lhs.append(ar)
        bk_end.append(jnp.concatenate([b_vec * p_end, k2[rs, ps] * p_end], axis=0))
        l_last.append(ll)
    ms = [jnp.where(strict2, -x[:c], 0.0) for x in x_bs]
    av = [dot3s(jnp.where(strict2, x[:c], 0.0), _split(_blockdiag_rows(v[rs, ps], n)), NN)
          for x, (rs, ps) in zip(x_ks, units)]
    rbk = [jnp.concatenate([jnp.where(tri2, xb[c:], 0.0), jnp.where(tri2, xk[c:], 0.0)], axis=1)
           for xb, xk in zip(x_bs, x_ks)]
    ts = _inv_unit_lower_pairs(ms, c)
    state = [s_scr[p] for p in pairs]
    npair = len(pairs)
    for q in range(nsub):
        rs = slice(q * c, (q + 1) * c)
        ix = [q * npair + p for p in pairs]
        ars = [dot3s(lhs[i], _split(state[p]), NT) for p, i in zip(pairs, ix)]
        us = [dot3s(ts[i], _split(_blockdiag_rows(x[:c] + av[i], n)), NN) for x, i in zip(ars, ix)]
        vs = [v[rs, sl[p]] for p in pairs]
        ys = [ars[p][c:] + _dot(rbk[i].astype(BF16),
                                jnp.concatenate([_blockdiag_rows(us[p], n), _blockdiag_rows(vs[p], n)],
                                                axis=0).astype(BF16), NN, None)
              for p, i in zip(pairs, ix)]
        upd = [dot3(_split(jnp.concatenate([us[p], vs[p]], axis=0)), _split(bk_end[i]), TN)
               for p, i in zip(pairs, ix)]
        state = [jnp.where(same_head, state[p] * jnp.exp(l_last[i]) + upd[p], 0.0) for p, i in zip(pairs, ix)]
        devs = [y - _head_sums(y, ones_bd) * (1.0 / n) for y in ys]
        var = [_head_sums(jnp.square(d), ones_bd) * (1.0 / n) for d in devs]
        for p, i in zip(pairs, ix):
            yn = devs[p] * lax.rsqrt(var[p] + RWKV_GN_EPS) * lnw[:, sl[p]] + lnb[:, sl[p]]
            o_ref[rs, sl[p]] = ((yn + bonus_s[i] * vs[p]) * gate[rs, sl[p]]).astype(o_ref.dtype)
    for p in pairs:
        s_scr[p] = state[p]

    @pl.when(ci == pl.num_programs(1) - 1)
    def _():
        sout_ref[...] = s_scr[...]


def rwkv_mixer(proj3, sh_rkv8, sh_lora8, s0, prm, c):
    b, l, _ = proj3.shape
    nsub = next(k for k in (4, 2, 1) if l % (k * c) == 0)
    rb = nsub * c
    w3 = 3 * GROUP_W
    mu = prm['rwkv_mu']
    mu_rkv = mu[:w3].reshape(1, w3)
    mu_lora = jnp.pad(mu[w3:], (0, LORA_PAD - RWKV_LORA)).reshape(1, LORA_PAD)
    g2 = jnp.pad(prm['rwkv_g2'], ((0, G_LORA_PAD - RWKV_G_LORA), (0, 0)))
    row = lambda t: t.reshape(1, GROUP_W)
    const2 = lambda bi, ci: (0, 0)
    vec = pl.BlockSpec((1, GROUP_W), const2)
    n, npair = RWKV_HS, RWKV_HEADS // 2
    sp = s0.reshape(b, npair, 2, n, n)
    zero = jnp.zeros((b, npair, n, n), F32)
    s_pairs = jnp.concatenate([jnp.concatenate([sp[:, :, 0], zero], axis=-1),
                               jnp.concatenate([zero, sp[:, :, 1]], axis=-1)], axis=-2)
    state_spec = pl.BlockSpec((None, npair, 2 * n, 2 * n), lambda bi, ci: (bi, 0, 0, 0))
    o, s_new = pl.pallas_call(
        functools.partial(_rwkv_kernel, c=c, nsub=nsub),
        grid=(b, l // rb),
        in_specs=[
            pl.BlockSpec((None, rb, w3), lambda bi, ci: (bi, ci, COL_B_RKV // w3)),
            pl.BlockSpec((None, rb, LORA_PAD), lambda bi, ci: (bi, ci, COL_B_LORA // LORA_PAD)),
            pl.BlockSpec((None, SUBLANE, w3), lambda bi, ci: (bi, 0, 0)),
            pl.BlockSpec((None, SUBLANE, LORA_PAD), lambda bi, ci: (bi, 0, 0)),
            state_spec,
            pl.BlockSpec((1, w3), const2),
            pl.BlockSpec((1, LORA_PAD), const2),
            vec,
            pl.BlockSpec((RWKV_W_LORA, GROUP_W), const2),
            vec,
            pl.BlockSpec((RWKV_A_LORA, GROUP_W), const2),
            pl.BlockSpec((G_LORA_PAD, GROUP_W), const2),
            vec, vec, vec, vec, vec,
        ],
        out_specs=[
            pl.BlockSpec((None, rb, GROUP_W), lambda bi, ci: (bi, ci, 0)),
            state_spec,
        ],
        out_shape=[jax.ShapeDtypeStruct((b, l, GROUP_W), BF16),
                   jax.ShapeDtypeStruct((b, npair, 2 * n, 2 * n), F32)],
        scratch_shapes=[pltpu.VMEM((npair, 2 * n, 2 * n), F32),
                        pltpu.VMEM((SUBLANE, w3), F32), pltpu.VMEM((SUBLANE, LORA_PAD), F32)],
        compiler_params=_cparams(("parallel", "arbitrary")),
        name="rwkv_mixer",
    )(proj3, proj3, sh_rkv8, sh_lora8, s_pairs, mu_rkv, mu_lora,
      row(prm['rwkv_w0']), prm['rwkv_w2'], row(prm['rwkv_a0']), prm['rwkv_a2'], g2,
      row(prm['rwkv_k_k']), row(prm['rwkv_k_a']), row(prm['rwkv_r_k']),
      row(prm['rwkv_ln_w']), row(prm['rwkv_ln_b']))
    s_heads = jnp.stack([s_new[:, :, :n, :n], s_new[:, :, n:, n:]], axis=2)
    return o, s_heads.reshape(b, RWKV_HEADS, n, n)


def _ssd_kernel(z_ref, xbc_ref, sm_ref, cbuf_ref, s0_ref, cw_ref, cb_ref, dtb_ref, alog_ref, dsk_ref, nw_ref,
                o_ref, sout_ref, s_scr, tail_scr, y_scr, *, c):
    ci = pl.program_id(1)

    @pl.when(ci == 0)
    def _():
        s_scr[...] = s0_ref[...]
        tail_scr[...] = cbuf_ref[...]

    x = xbc_ref[...]
    taps = _shifted_taps(tail_scr[...], x, SSM_TAPS)
    tail_scr[...] = x[c - SUBLANE:]
    cw = cw_ref[...]
    y = taps[0] * cw[0:1]
    for i in range(1, SSM_TAPS):
        y = y + taps[i] * cw[i:i + 1]
    y = _silu(y + cb_ref[...])
    xs = y[:, 0:GROUP_W]
    gn = SSM_GROUPS * SSM_N

    sm = sm_ref[...]
    dt_all = jax.nn.softplus(sm + dtb_ref[...])
    da_all = dt_all * (-jnp.exp(alog_ref[...]))
    acs_all = _cumsum_rows(da_all, c)

    row = _iota2((c, c), 0)
    col = _iota2((c, c), 1)
    tri = row >= col
    z = z_ref[...]
    dsk = dsk_ref[...]
    hpg = SSM_HEADS // SSM_GROUPS
    gp = hpg * SSM_P
    rows_of = _row_getter(acs_all, c)
    for g in range(SSM_GROUPS):
        bm = y[:, GROUP_W + g * SSM_N:GROUP_W + (g + 1) * SSM_N]
        cm = y[:, GROUP_W + gn + g * SSM_N:GROUP_W + gn + (g + 1) * SSM_N]
        cb = dot1(cm, bm, NT)
        sg = s_scr[g]
        y_off = dot1(cm, sg, NT)
        xdec = []
        for rr in range(hpg):
            h = g * hpg + rr
            lo = h * SSM_P
            lane = SM_SSM_DT + h
            xs_h = xs[:, lo:lo + SSM_P]
            dt = dt_all[:, lane:lane + 1]
            acs = acs_all[:, lane:lane + 1]
            lmat = jnp.exp(jnp.where(tri, acs - rows_of(lane), -jnp.inf))
            xd = xs_h * dt
            a_last = acs[c - 1:c]
            xdec.append(xd * jnp.exp(a_last - acs))
            yh = dot1(cb * lmat, xd, NN) + y_off[:, rr * SSM_P:(rr + 1) * SSM_P] * jnp.exp(acs)
            yh = yh + xs_h * dsk[:, lane:lane + 1]
            y_scr[:, lo:lo + SSM_P] = yh * _silu(z[:, lo:lo + SSM_P])
        upd = dot3(_split(jnp.concatenate(xdec, axis=1)), _split(bm), TN)
        for rr in range(hpg):
            lane = SM_SSM_DT + g * hpg + rr
            dec = jnp.exp(acs_all[c - 1:c, lane:lane + 1])
            s_scr[g, rr * SSM_P:(rr + 1) * SSM_P, :] = (sg[rr * SSM_P:(rr + 1) * SSM_P] * dec
                                                         + upd[rr * SSM_P:(rr + 1) * SSM_P])

    gw = GROUP_W // SSM_GROUPS
    nw = nw_ref[...]
    for g in range(SSM_GROUPS):
        yg = y_scr[:, g * gw:(g + 1) * gw]
        yg = yg * lax.rsqrt(jnp.mean(yg * yg, axis=-1, keepdims=True) + NORM_EPS)
        o_ref[:, g * gw:(g + 1) * gw] = (yg * nw[:, g * gw:(g + 1) * gw]).astype(o_ref.dtype)

    @pl.when(ci == pl.num_programs(1) - 1)
    def _():
        sout_ref[...] = s_scr[...]


def ssd_mixer(proj3, cbuf8, s0, prm, c):
    b, l, _ = proj3.shape

    def lanes(t):
        return jnp.zeros((1, LANE), F32).at[0, SM_SSM_DT:SM_SSM_DT + SSM_HEADS].set(t)

    const2 = lambda bi, ci: (0, 0)
    small = pl.BlockSpec((1, LANE), const2)
    gshape = (SSM_GROUPS, SSM_HEADS // SSM_GROUPS * SSM_P, SSM_N)
    state_spec = pl.BlockSpec((None,) + gshape, lambda bi, ci: (bi, 0, 0, 0))
    o, s_new = pl.pallas_call(
        functools.partial(_ssd_kernel, c=c),
        grid=(b, l // c),
        in_specs=[
            pl.BlockSpec((None, c, GROUP_W), lambda bi, ci: (bi, ci, COL_C_Z // GROUP_W)),
            pl.BlockSpec((None, c, SSM_XBC), lambda bi, ci: (bi, ci, COL_C_XBC // SSM_XBC)),
            pl.BlockSpec((None, c, LANE), lambda bi, ci: (bi, ci, COL_SMALL // LANE)),
            pl.BlockSpec((None, SUBLANE, SSM_XBC), lambda bi, ci: (bi, 0, 0)),
            state_spec,
            pl.BlockSpec((SSM_TAPS, SSM_XBC), const2),
            pl.BlockSpec((1, SSM_XBC), const2),
            small, small, small,
            pl.BlockSpec((1, GROUP_W), const2),
        ],
        out_specs=[
            pl.BlockSpec((None, c, GROUP_W), lambda bi, ci: (bi, ci, 0)),
            state_spec,
        ],
        out_shape=[jax.ShapeDtypeStruct((b, l, GROUP_W), BF16),
                   jax.ShapeDtypeStruct((b,) + gshape, F32)],
        scratch_shapes=[pltpu.VMEM(gshape, F32), pltpu.VMEM((SUBLANE, SSM_XBC), F32),
                        pltpu.VMEM((c, GROUP_W), F32)],
        compiler_params=_cparams(("parallel", "arbitrary")),
        name="ssd_mixer",
    )(proj3, proj3, proj3, cbuf8, s0.reshape((b,) + gshape), prm['ssm_conv_w'],
      prm['ssm_conv_b'].reshape(1, SSM_XBC),
      lanes(prm['ssm_dt_bias']), lanes(prm['ssm_A_log']), lanes(prm['ssm_D']),
      prm['ssm_norm_w'].reshape(1, GROUP_W))
    return o, s_new.reshape(b, SSM_HEADS, SSM_P, SSM_N)


def _swa_weight(d):
    mult = jnp.zeros(d.shape, F32)
    for window, dil in SWA_PATTERNS:
        ok = (d >= 0) & (d <= window) & ((d & (dil - 1)) == 0)
        mult = mult + jnp.where(ok, 1.0, 0.0)
    return mult


def _swa_scores(q, k, d, slope):
    s = dot_nt(q.astype(BF16), k.astype(BF16), None) * (SWA_HD ** -0.5)
    mult = _swa_weight(d)
    s = s - slope * d.astype(F32)
    return jnp.where(mult > 0.0, s, NEG_INF), mult


def _swa_prompt_kernel(slopes_ref, q_ref, k_ref, v_ref, lw_ref, o_ref, m_scr, l_scr, acc_scr, *, t, sub):
    h = pl.program_id(1)
    qi = pl.program_id(2)
    ki = pl.program_id(3)

    @pl.when(ki == 0)
    def _():
        m_scr[...] = jnp.full(m_scr.shape, NEG_INF, F32)
        l_scr[...] = jnp.zeros(l_scr.shape, F32)
        acc_scr[...] = jnp.zeros(acc_scr.shape, F32)

    @pl.when(ki <= qi)
    def _():
        col = (ki * t + _iota2((1, t), 1)).astype(F32) * slopes_ref[h]
        kb = k_ref[...].astype(BF16)
        vb = v_ref[...].astype(BF16)
        def qk(r):
            q = (q_ref[r:r + sub, :] * (SWA_HD ** -0.5)).astype(BF16)
            return _dot(q, kb, NT, None)

        nxt = qk(0)
        for r in range(0, t, sub):
            rs = slice(r, r + sub)
            s = nxt + (lw_ref[qi - ki, rs, :] + col)
            if r + sub < t:
                nxt = qk(r + sub)
            m_old = m_scr[rs, :]
            m_new = jnp.maximum(m_old, jnp.max(s, axis=-1, keepdims=True))
            alpha = jnp.exp(m_old - m_new)
            p = jnp.exp(s - m_new)
            l_scr[rs, :] = alpha * l_scr[rs, :] + jnp.sum(p, axis=-1, keepdims=True)
            acc_scr[rs, :] = alpha * acc_scr[rs, :] + _dot(p.astype(BF16), vb, NN, None)
            m_scr[rs, :] = m_new

    @pl.when(ki == pl.num_programs(3) - 1)
    def _():
        o_ref[...] = (acc_scr[...] / l_scr[...]).astype(o_ref.dtype)


def _alibi_slopes():
    return jnp.asarray([2.0 ** (-8.0 * (i + 1) / SWA_HEADS) for i in range(SWA_HEADS)], F32)


def _swa_log_weight_tiles(n, t):
    d = (jnp.arange(n)[:, None, None] * t + jnp.arange(t)[None, :, None]) - jnp.arange(t)[None, None, :]
    mult = _swa_weight(d.astype(jnp.int32))
    return jnp.where(mult > 0.0, jnp.log(jnp.maximum(mult, 1.0)), NEG_INF)


def swa_prompt(proj3, k4, v4, li, t):
    b, l, _ = proj3.shape
    qc = COL_D_Q // SWA_HD
    n = l // t
    kv_spec = pl.BlockSpec((None, None, t, SWA_HD), lambda bi, h, qi, ki: (li, bi, jnp.minimum(ki, qi), h))

    return pl.pallas_call(
        functools.partial(_swa_prompt_kernel, t=t, sub=min(t, 256)),
        grid=(b, SWA_HEADS, n, n),
        in_specs=[
            pl.BlockSpec(memory_space=pltpu.SMEM),
            pl.BlockSpec((None, t, SWA_HD), lambda bi, h, qi, ki: (bi, qi, qc + h)),
            kv_spec,
            kv_spec,
            pl.BlockSpec((n, t, t), lambda bi, h, qi, ki: (0, 0, 0)),
        ],
        out_specs=pl.BlockSpec((None, t, SWA_HD), lambda bi, h, qi, ki: (bi, qi, h)),
        out_shape=jax.ShapeDtypeStruct((b, l, GROUP_W), BF16),
        scratch_shapes=[pltpu.VMEM((t, 1), F32), pltpu.VMEM((t, 1), F32), pltpu.VMEM((t, SWA_HD), F32)],
        compiler_params=_cparams(("parallel", "parallel", "parallel", "arbitrary")),
        name="swa_prompt",
    )(_alibi_slopes(), proj3, k4, v4, _swa_log_weight_tiles(n, t))


def _swa_sample_kernel(slopes_ref, q_ref, k_ref, v_ref, ck_ref, cv_ref, o_ref, *, t, wb):
    d_c = (wb + _iota2((t, wb), 0)) - _iota2((t, wb), 1)
    d_n = _iota2((t, t), 0) - _iota2((t, t), 1)
    for h in range(SWA_HEADS):
        hs = slice(h * SWA_HD, (h + 1) * SWA_HD)
        ck = ck_ref[pl.ds(h, wb, stride=SWA_HEADS), :]
        cv = cv_ref[pl.ds(h, wb, stride=SWA_HEADS), :]
        q = q_ref[:, hs]
        s_c, mult_c = _swa_scores(q, ck, d_c, slopes_ref[h])
        s_n, mult_n = _swa_scores(q, k_ref[:, hs], d_n, slopes_ref[h])
        m = jnp.maximum(jnp.max(s_c, axis=-1, keepdims=True), jnp.max(s_n, axis=-1, keepdims=True))
        p_c = jnp.exp(s_c - m) * mult_c
        p_n = jnp.exp(s_n - m) * mult_n
        den = jnp.sum(p_c, axis=-1, keepdims=True) + jnp.sum(p_n, axis=-1, keepdims=True)
        num = (dot_nn(p_c.astype(BF16), cv.astype(BF16), None)
               + dot_nn(p_n.astype(BF16), v_ref[:, hs].astype(BF16), None))
        o_ref[:, hs] = (num / den).astype(o_ref.dtype)


def swa_sample(proj3, k4, v4, li, cache_k_all, cache_v_all):
    b, t, _ = proj3.shape
    depth, _, wb = cache_k_all.shape[:3]
    ck = cache_k_all.reshape(depth, b, wb * SWA_HEADS, SWA_HD)
    cv = cache_v_all.reshape(depth, b, wb * SWA_HEADS, SWA_HD)
    new_spec = pl.BlockSpec((None, None, t, GROUP_W), lambda bi: (li, bi, 0, 0))
    cache_spec = pl.BlockSpec((None, None, wb * SWA_HEADS, SWA_HD), lambda bi: (li, bi, 0, 0))
    return pl.pallas_call(
        functools.partial(_swa_sample_kernel, t=t, wb=wb),
        grid=(b,),
        in_specs=[
            pl.BlockSpec(memory_space=pltpu.SMEM),
            pl.BlockSpec((None, t, GROUP_W), lambda bi: (bi, 0, COL_D_Q // GROUP_W)),
            new_spec,
            new_spec,
            cache_spec,
            cache_spec,
        ],
        out_specs=pl.BlockSpec((None, t, GROUP_W), lambda bi: (bi, 0, 0)),
        out_shape=jax.ShapeDtypeStruct((b, t, GROUP_W), BF16),
        compiler_params=_cparams(("parallel",)),
        name="swa_sample",
    )(_alibi_slopes(), proj3, k4, v4, ck, cv)


def _tiles(m):
    return (256, 1024) if m >= 1024 else (m, m)


def _front_pad_rows(t, rows=SUBLANE):
    return jnp.pad(t, ((0, 0), (rows - t.shape[1], 0), (0, 0)))


W_IN_SEGMENTS = (('a_qkv', 3 * GROUP_W), ('a_z', GROUP_W), ('a_b', GDN_HEADS), ('a_a', GDN_HEADS),
                 ('b_rkv', 3 * GROUP_W), ('b_lora', RWKV_LORA), ('c_z', GROUP_W), ('c_xbc', SSM_XBC),
                 ('c_dt', SSM_HEADS), ('d_q', GROUP_W), ('d_k', GROUP_W), ('d_v', GROUP_W))
W_IN_SRC = {}
_o = 0
for _name, _n in W_IN_SEGMENTS:
    W_IN_SRC[_name] = _o
    _o += _n
W_IN_WIDE = ((COL_A_QKV, 3 * GROUP_W, W_IN_SRC['a_qkv']), (COL_B_RKV, 3 * GROUP_W, W_IN_SRC['b_rkv']),
             (COL_D_Q, GROUP_W, W_IN_SRC['d_q']), (COL_A_Z, GROUP_W, W_IN_SRC['a_z']),
             (COL_C_Z, GROUP_W, W_IN_SRC['c_z']), (COL_C_XBC, SSM_XBC, W_IN_SRC['c_xbc']))
IN_TN = 512
assert COL_B_LORA % IN_TN == 0 and N_PROJ - COL_B_LORA == IN_TN


ROW_UNIT = 2 * SUBLANE


def _w_in_tile_rows():
    rows = []
    for j in range(N_PROJ // IN_TN - 1):
        c0 = j * IN_TN
        dest, width, src = next(s for s in W_IN_WIDE if s[0] <= c0 < s[0] + s[1])
        assert c0 + IN_TN <= dest + width and (src + c0 - dest) % ROW_UNIT == 0
        rows.append((src + (c0 - dest)) // ROW_UNIT)
    rows.append(0)
    return jnp.asarray(rows, jnp.int32)


W_IN_TAIL = ((0, 'b_lora', RWKV_LORA), (LORA_PAD + SM_GDN_B, 'a_b', 2 * GDN_HEADS),
             (LORA_PAD + SM_SSM_DT, 'c_dt', SSM_HEADS))
assert W_IN_SRC['a_a'] == W_IN_SRC['a_b'] + GDN_HEADS and SM_GDN_A == SM_GDN_B + GDN_HEADS


def _mm_in_kernel(rows_ref, x_ref, wt_ref, *rest, guest):
    del rows_ref
    if guest:
        *tail_refs, xg_ref, o_ref, og_ref, w_scr = rest
    else:
        *tail_refs, o_ref, w_scr = rest

    @pl.when(pl.program_id(1) == 0)
    def _():
        @pl.when(pl.program_id(0) < pl.num_programs(0) - 1)
        def _():
            w_scr[...] = wt_ref[0].astype(BF16)

        @pl.when(pl.program_id(0) == pl.num_programs(0) - 1)
        def _():
            w_scr[...] = jnp.zeros(w_scr.shape, BF16)
            for (row, _, n), ref in zip(W_IN_TAIL, tail_refs):
                w_scr[row:row + n, :] = ref[0].astype(BF16)

    if guest:
        @pl.when(pl.program_id(1) == 0)
        def _():
            og_ref[...] = _dot(xg_ref[...], w_scr[...], NT, None)

    o_ref[...] = _dot(x_ref[...], w_scr[...], NT, None)


def mm_in(h, w_in_t_all, li, tm, h_guest=None):
    m, kd = h.shape
    guest = h_guest is not None
    g_in, g_out, g_shape, g_args = [], [], [], []
    if guest:
        mg = h_guest.shape[0]
        g_in = [pl.BlockSpec((mg, kd), lambda j, i, rows: (0, 0))]
        g_out = [pl.BlockSpec((mg, IN_TN), lambda j, i, rows: (0, j))]
        g_shape = [jax.ShapeDtypeStruct((mg, N_PROJ), F32)]
        g_args = [h_guest]

    def rows_spec(n, start):
        assert start % ROW_UNIT == 0 and n % ROW_UNIT == 0
        return pl.BlockSpec((pl.Element(1), pl.Element(n), pl.Element(kd)),
                            lambda j, i, rows: (li, (start // ROW_UNIT) * ROW_UNIT, 0))

    grid_spec = pltpu.PrefetchScalarGridSpec(
        num_scalar_prefetch=1,
        grid=(N_PROJ // IN_TN, m // tm),
        in_specs=[
            pl.BlockSpec((tm, kd), lambda j, i, rows: (i, 0)),
            pl.BlockSpec((pl.Element(1), pl.Element(IN_TN), pl.Element(kd)),
                         lambda j, i, rows: (li, rows[j] * ROW_UNIT, 0)),
        ] + [rows_spec(n, W_IN_SRC[name]) for _, name, n in W_IN_TAIL] + g_in,
        out_specs=[pl.BlockSpec((tm, IN_TN), lambda j, i, rows: (i, j))] + g_out,
        scratch_shapes=[pltpu.VMEM((IN_TN, kd), BF16)],
    )
    out = pl.pallas_call(
        functools.partial(_mm_in_kernel, guest=guest),
        grid_spec=grid_spec,
        out_shape=[jax.ShapeDtypeStruct((m, N_PROJ), F32)] + g_shape,
        compiler_params=_cparams(("arbitrary", "arbitrary")),
        name="mm_in",
    )(_w_in_tile_rows(), h, *([w_in_t_all] * (1 + len(W_IN_TAIL))), *g_args)
    return (out[0], out[1]) if guest else (out[0], None)


def decoder_layer(x, prm, wts, past, swa_cache, chunks, li, depth, kv=(None, None), h=None, next_pre_w=None,
                  h_guest=None, proj=None):
    b, l, _ = x.shape
    m = b * l
    tr, tm = _tiles(m)
    x2 = x.reshape(m, D_MODEL)
    c_gdn, c_rwkv, c_ssd = chunks

    if h is None:
        h = rms_cast(x2, prm['norm_mix_pre'], tr)
    proj_guest = None
    if proj is None:
        proj, proj_guest = mm_in(h, wts['w_in_t_all'], li, tm, h_guest)
    proj3 = proj.reshape(b, l, N_PROJ)
    kbuf, vbuf = matmul_into(h, wts['w_in_t_all'], W_IN_SRC['d_k'], W_IN_SRC['d_v'], GROUP_W, kv, li, depth, tm,
                             256, "mm_kv")
    k4 = kbuf.reshape(depth, b, l, GROUP_W)
    v4 = vbuf.reshape(depth, b, l, GROUP_W)

    o_a, gdn_s = gdn_mixer(proj3, _front_pad_rows(past['gdn_conv']), past['gdn'], prm['gdn_conv_w'],
                           prm['gdn_A_log'], prm['gdn_dt_bias'], prm['gdn_norm_w'], c_gdn)
    shift = past['rwkv_shift'][:, None, :]
    sh_rkv8 = _front_pad_rows(shift[:, :, :3 * GROUP_W])
    sh_lora8 = _front_pad_rows(jnp.pad(shift[:, :, 3 * GROUP_W:], ((0, 0), (0, 0), (0, LORA_PAD - RWKV_LORA))))
    o_b, rwkv_s = rwkv_mixer(proj3, sh_rkv8, sh_lora8, past['rwkv'], prm, c_rwkv)
    o_c, ssm_s = ssd_mixer(proj3, _front_pad_rows(past['ssm_conv']), past['ssm'], prm, c_ssd)
    if swa_cache is None:
        o_d = swa_prompt(proj3, k4, v4, li, min(l, SWA_TILE))
    else:
        o_d = swa_sample(proj3, k4, v4, li, swa_cache[0], swa_cache[1])

    gdn_conv = proj3[:, l - (GDN_TAPS - 1):, COL_A_QKV:COL_A_QKV + 3 * GROUP_W]
    ssm_conv = proj3[:, l - (SSM_TAPS - 1):, COL_C_XBC:COL_C_XBC + SSM_XBC]
    rwkv_shift = jnp.concatenate([proj3[:, l - 1, COL_B_RKV:COL_B_RKV + 3 * GROUP_W],
                                  proj3[:, l - 1, COL_B_LORA:COL_B_LORA + RWKV_LORA]], axis=-1)

    y = matmul_groups([o.reshape(m, GROUP_W) for o in (o_a, o_b, o_c, o_d)], wts['w_out_all'], li, tm, 512,
                      "mm_out")
    x2, h2 = add_rms_cast(x2, y, prm['norm_mix_post'], prm['norm_ffn_pre'], tr)
    state8 = _front_pad_rows(past['ffn_conv'])
    if l % tm == 0:
        act, cst = ffn_up_act(h2, wts['w_up_all'], li, state8, prm['ffn_conv_w'], prm['ffn_conv_b'], l, tm, 256,
                              256)
        ffn_conv = cst[:, SUBLANE - (FFN_TAPS - 1):, :]
    else:
        up3 = matmul(h2, wts['w_up_all'], tm, 512, D_MODEL, "mm_up", li).reshape(b, l, 2 * D_FF)
        act = ffn_act(up3, state8, prm['ffn_conv_w'], prm['ffn_conv_b'], l, D_FF // 2).reshape(m, D_FF)
        ffn_conv = up3[:, l - (FFN_TAPS - 1):, :D_FF]
    y2 = matmul(act, wts['w_down_all'], tm, 512, D_FF // 2, "mm_down", li)
    if next_pre_w is None:
        x2, h_next = add_rms(x2, y2, prm['norm_ffn_post'], tr), None
    else:
        x2, h_next = add_rms_cast(x2, y2, prm['norm_ffn_post'], next_pre_w, tr)
    return x2.reshape(b, l, D_MODEL), h_next, (kbuf, vbuf), (gdn_s, gdn_conv, rwkv_s, rwkv_shift, ssm_s, ssm_conv,
                                                             ffn_conv), proj_guest


def _zero_past(bsz):
    return {
        'gdn': jnp.zeros((bsz, GDN_HEADS, GDN_D, GDN_D), F32),
        'gdn_conv': jnp.zeros((bsz, GDN_TAPS - 1, 3 * GROUP_W), F32),
        'rwkv': jnp.zeros((bsz, RWKV_HEADS, RWKV_HS, RWKV_HS), F32),
        'rwkv_shift': jnp.zeros((bsz, 3 * GROUP_W + RWKV_LORA), F32),
        'ssm': jnp.zeros((bsz, SSM_HEADS, SSM_P, SSM_N), F32),
        'ssm_conv': jnp.zeros((bsz, SSM_TAPS - 1, SSM_XBC), F32),
        'ffn_conv': jnp.zeros((bsz, FFN_TAPS - 1, D_FF), F32),
    }


PARAM_NAMES = ('norm_mix_pre', 'norm_mix_post', 'norm_ffn_pre', 'norm_ffn_post', 'w_in', 'w_out', 'gdn_conv_w',
               'gdn_A_log', 'gdn_dt_bias', 'gdn_norm_w', 'rwkv_mu', 'rwkv_w0', 'rwkv_w2', 'rwkv_a0', 'rwkv_a2',
               'rwkv_g2', 'rwkv_k_k', 'rwkv_k_a', 'rwkv_r_k', 'rwkv_ln_w', 'rwkv_ln_b', 'ssm_conv_w', 'ssm_conv_b',
               'ssm_dt_bias', 'ssm_A_log', 'ssm_D', 'ssm_norm_w', 'ffn_w_up', 'ffn_conv_w', 'ffn_conv_b',
               'ffn_w_down')


def kernel(x_prompt, x_sample, state_gdn, state_gdn_conv, state_rwkv, state_rwkv_shift, state_ssm, state_ssm_conv, cache_swa_k, cache_swa_v, state_ffn_conv, norm_mix_pre, norm_mix_post, norm_ffn_pre, norm_ffn_post, w_in, w_out, gdn_conv_w, gdn_A_log, gdn_dt_bias, gdn_norm_w, rwkv_mu, rwkv_w0, rwkv_w2, rwkv_a0, rwkv_a2, rwkv_g2, rwkv_k_k, rwkv_k_a, rwkv_r_k, rwkv_ln_w, rwkv_ln_b, ssm_conv_w, ssm_conv_b, ssm_dt_bias, ssm_A_log, ssm_D, ssm_norm_w, ffn_w_up, ffn_conv_w, ffn_conv_b, ffn_w_down):
    params = dict(zip(PARAM_NAMES, (norm_mix_pre, norm_mix_post, norm_ffn_pre, norm_ffn_post, w_in, w_out,
                                    gdn_conv_w, gdn_A_log, gdn_dt_bias, gdn_norm_w, rwkv_mu, rwkv_w0, rwkv_w2,
                                    rwkv_a0, rwkv_a2, rwkv_g2, rwkv_k_k, rwkv_k_a, rwkv_r_k, rwkv_ln_w, rwkv_ln_b,
                                    ssm_conv_w, ssm_conv_b, ssm_dt_bias, ssm_A_log, ssm_D, ssm_norm_w, ffn_w_up,
                                    ffn_conv_w, ffn_conv_b, ffn_w_down)))
    depth = w_in.shape[0]
    xp, xs = x_prompt, x_sample
    t_dec = x_sample.shape[1]
    prompt_states, sample_states = [], []
    hp = hs = None
    kvp = kvs = (None, None)
    wts = {'w_in_t_all': jnp.swapaxes(w_in, 1, 2), 'w_out_all': w_out, 'w_up_all': ffn_w_up,
           'w_down_all': ffn_w_down.astype(BF16)}
    for li in range(depth):
        prm = {k: v[li] for k, v in params.items()}
        nxt = norm_mix_pre[li + 1] if li + 1 < depth else None
        if hs is None:
            ms = xs.shape[0] * xs.shape[1]
            hs = rms_cast(xs.reshape(ms, D_MODEL), prm['norm_mix_pre'], ms)
        xp, hp, kvp, stp, proj_s = decoder_layer(xp, prm, wts, _zero_past(xp.shape[0]), None, (64, 64, 128),
                                                 li, depth, kvp, hp, nxt, h_guest=hs)
        past = {'gdn': state_gdn[li], 'gdn_conv': state_gdn_conv[li], 'rwkv': state_rwkv[li],
                'rwkv_shift': state_rwkv_shift[li], 'ssm': state_ssm[li], 'ssm_conv': state_ssm_conv[li],
                'ffn_conv': state_ffn_conv[li]}
        xs, hs, kvs, sts, _ = decoder_layer(xs, prm, wts, past, (cache_swa_k, cache_swa_v),
                                            (t_dec, t_dec, t_dec), li, depth, kvs, hs, nxt, proj=proj_s)
        prompt_states.append(stp)
        sample_states.append(sts)

    def window_rows(buf, x):
        bsz, l = x.shape[0], x.shape[1]
        rows = buf.reshape(depth, bsz, l, SWA_HEADS, SWA_HD)
        return rows[:, :, max(l - SWA_MAX_WINDOW, 0):]

    def outputs(states, kv, x):
        st = [jnp.stack(t) for t in zip(*states)]
        return (*st[:6], window_rows(kv[0], x), window_rows(kv[1], x), st[6])

    return (xp, xs, *outputs(prompt_states, kvp, x_prompt), *outputs(sample_states, kvs, x_sample))
```

```python
import functools

import jax
import jax.numpy as jnp
from jax import lax
from jax.experimental import pallas as pl
from jax.experimental.pallas import tpu as pltpu

F32 = jnp.float32
BF16 = jnp.bfloat16
HI = lax.Precision.HIGHEST

D_MODEL = 4096
GROUP_W = D_MODEL // 4
GDN_HEADS = 8
GDN_D = GROUP_W // GDN_HEADS
GDN_TAPS = 4
RWKV_HS = 64
RWKV_HEADS = GROUP_W // RWKV_HS
RWKV_W_LORA = 64
RWKV_A_LORA = 64
RWKV_G_LORA = 160
RWKV_LORA = RWKV_W_LORA + RWKV_A_LORA + RWKV_G_LORA
RWKV_GN_EPS = 64e-5
SSM_P = 64
SSM_HEADS = GROUP_W // SSM_P
SSM_GROUPS = 2
SSM_N = 128
SSM_TAPS = 4
SSM_XBC = GROUP_W + 2 * SSM_GROUPS * SSM_N
SWA_HEADS = 8
SWA_HD = GROUP_W // SWA_HEADS
SWA_PATTERNS = ((128, 1), (512, 4), (2048, 16))
SWA_MAX_WINDOW = 2048
D_FF = 256 * ((8 * D_MODEL // 3 + 255) // 256)
FFN_TAPS = 3
NORM_EPS = 1e-6
NEG_INF = -1e30

LANE = 128
SUBLANE = 8
LORA_PAD = 384
G_LORA_PAD = LORA_PAD - RWKV_W_LORA - RWKV_A_LORA

COL_A_QKV = 0
COL_B_RKV = 3 * GROUP_W
COL_D_Q = 6 * GROUP_W
COL_A_Z = 7 * GROUP_W
COL_C_Z = 8 * GROUP_W
COL_C_XBC = 9 * GROUP_W
COL_B_LORA = COL_C_XBC + SSM_XBC
COL_SMALL = COL_B_LORA + LORA_PAD
N_PROJ = COL_SMALL + LANE
SM_GDN_B = 0
SM_GDN_A = GDN_HEADS
SM_SSM_DT = 2 * GDN_HEADS

VMEM_LIMIT = 56 * 1024 * 1024
SWA_TILE = 1024


def _cparams(sem):
    return pltpu.CompilerParams(dimension_semantics=sem, vmem_limit_bytes=VMEM_LIMIT)


def _dot(a, b, dims, prec):
    return lax.dot_general(a, b, (dims, ((), ())), precision=prec, preferred_element_type=F32)


def dot_nn(a, b, prec=HI):
    return _dot(a, b, ((1,), (0,)), prec)


def dot_nt(a, b, prec=HI):
    return _dot(a, b, ((1,), (1,)), prec)


def dot_tn(a, b, prec=HI):
    return _dot(a, b, ((0,), (0,)), prec)


def _silu(x):
    return x * jax.nn.sigmoid(x)


def _iota2(shape, axis):
    return lax.broadcasted_iota(jnp.int32, shape, axis)


def _log2(n):
    s = n.bit_length() - 1
    assert 1 << s == n
    return s


NN = ((1,), (0,))
NT = ((1,), (1,))
TN = ((0,), (0,))


def _split(x):
    hi = x.astype(BF16)
    return hi, (x - hi.astype(F32)).astype(BF16)


def _split_rows(x):
    hi = x.astype(BF16)
    hif = hi.astype(F32)
    return jnp.concatenate([hif, x - hif], axis=0).astype(BF16), hi


def dot3(ap, bp, dims):
    return (_dot(ap[0], bp[0], dims, None) + _dot(ap[0], bp[1], dims, None)
            + _dot(ap[1], bp[0], dims, None))


def dot3s(a, bp, dims):
    r = a.shape[0]
    stacked, hi = _split_rows(a)
    both = _dot(stacked, bp[0], dims, None)
    return both[:r] + both[r:] + _dot(hi, bp[1], dims, None)


def dot1(a, b, dims):
    return _dot(a.astype(BF16), b.astype(BF16), dims, None)


def _inv_unit_lower_multi(ms, c):
    row = _iota2((c, c), 0)
    col = _iota2((c, c), 1)
    eye = jnp.where(row == col, 1.0, 0.0).astype(F32)
    base = min(SUBLANE, c)
    sb = _log2(base)
    blk = (row >> sb) == (col >> sb)
    ps = [jnp.where(blk, -m, 0.0) for m in ms]
    ts = [eye + p for p in ps]
    if sb > 1:
        ps = [_dot(p.astype(BF16), p.astype(BF16), NN, None) for p in ps]
        for _ in range(sb - 2):
            both = [_dot(jnp.concatenate([t, p], axis=0).astype(BF16), p.astype(BF16), NN, None)
                    for t, p in zip(ts, ps)]
            ts = [t + x[:c] for t, x in zip(ts, both)]
            ps = [x[c:] for x in both]
        ts = [t + _dot(t.astype(BF16), p.astype(BF16), NN, None) for t, p in zip(ts, ps)]
    s = base
    while s < c:
        ls = _log2(s)
        off = ((row >> (ls + 1)) == (col >> (ls + 1))) & ((row >> ls) > (col >> ls))
        tbs = [t.astype(BF16) for t in ts]
        inner = [_dot(jnp.where(off, m, 0.0).astype(BF16), tb, NN, None) for m, tb in zip(ms, tbs)]
        ts = [t - _dot(tb, x.astype(BF16), NN, None) for t, tb, x in zip(ts, tbs, inner)]
        s *= 2
    res = [eye - t - dot3s(m, _split(t), NN) for m, t in zip(ms, ts)]
    return [t + _dot(t.astype(BF16), r.astype(BF16), NN, None) for t, r in zip(ts, res)]


def _blockdiag_rows(x, half):
    left = _iota2(x.shape, 1) < half
    return jnp.concatenate([jnp.where(left, x, 0.0), jnp.where(left, 0.0, x)], axis=0)


def _head_sums(x, ones_bd):
    r = x.shape[0]
    stacked, _ = _split_rows(x)
    both = _dot(stacked, ones_bd, NN, None)
    return both[:r] + both[r:]


def _inv_unit_lower_pairs(ms, c):
    row = _iota2((c, 2 * c), 0)
    col = _iota2((c, 2 * c), 1) & (c - 1)
    eye = jnp.where(row == col, 1.0, 0.0).astype(F32)
    base = min(SUBLANE, c)
    sb = _log2(base)
    blk = (row >> sb) == (col >> sb)

    def mm(a, b):
        return _dot(a.astype(BF16), _blockdiag_rows(b, c).astype(BF16), NN, None)

    ps = [jnp.where(blk, -m, 0.0) for m in ms]
    ts = [eye + p for p in ps]
    if sb > 1:
        ps = [mm(p, p) for p in ps]
        for _ in range(sb - 2):
            both = [mm(jnp.concatenate([t, p], axis=0), p) for t, p in zip(ts, ps)]
            ts = [t + x[:c] for t, x in zip(ts, both)]
            ps = [x[c:] for x in both]
        ts = [t + mm(t, p) for t, p in zip(ts, ps)]
    s = base
    while s < c:
        ls = _log2(s)
        off = ((row >> (ls + 1)) == (col >> (ls + 1))) & ((row >> ls) > (col >> ls))
        inner = [mm(jnp.where(off, m, 0.0), t) for m, t in zip(ms, ts)]
        ts = [t - mm(t, x) for t, x in zip(ts, inner)]
        s *= 2
    res = [eye - t - dot3s(m, _split(_blockdiag_rows(t, c)), NN) for m, t in zip(ms, ts)]
    return [t + mm(t, r) for t, r in zip(ts, res)]


def _row_getter(x, c):
    if c % LANE:
        x = jnp.concatenate([x, jnp.zeros((LANE - c % LANE, LANE), F32)], axis=0)
    xt = x.T
    return lambda lane: xt[lane:lane + 1, :c]


def _cumsum_rows(x, c):
    tri = jnp.where(_iota2((c, c), 0) >= _iota2((c, c), 1), 1.0, 0.0).astype(F32)
    return dot_nn(tri, x)


def _shifted_taps(tail, x, taps):
    c = x.shape[0]
    xp = jnp.concatenate([tail, x], axis=0)
    out = []
    for s in range(taps - 1, 0, -1):
        out.append(pltpu.roll(xp, s, 0)[SUBLANE:SUBLANE + c])
    out.append(x)
    return out


def _rms_cast_kernel(x_ref, w_ref, o_ref):
    x = x_ref[...]
    y = x * lax.rsqrt(jnp.mean(x * x, axis=-1, keepdims=True) + NORM_EPS)
    o_ref[...] = (y * w_ref[...]).astype(o_ref.dtype)


def rms_cast(x, w, tr):
    m, d = x.shape
    return pl.pallas_call(
        _rms_cast_kernel,
        grid=(m // tr,),
        in_specs=[pl.BlockSpec((tr, d), lambda i: (i, 0)), pl.BlockSpec((1, d), lambda i: (0, 0))],
        out_specs=pl.BlockSpec((tr, d), lambda i: (i, 0)),
        out_shape=jax.ShapeDtypeStruct((m, d), BF16),
        compiler_params=_cparams(("parallel",)),
        name="rms_cast",
    )(x, w.reshape(1, d))


def _add_rms_kernel(x_ref, y_ref, w_ref, o_ref):
    y = y_ref[...]
    yn = y * lax.rsqrt(jnp.mean(y * y, axis=-1, keepdims=True) + NORM_EPS)
    o_ref[...] = x_ref[...] + yn * w_ref[...]


def add_rms(x, y, w, tr):
    m, d = x.shape
    return pl.pallas_call(
        _add_rms_kernel,
        grid=(m // tr,),
        in_specs=[pl.BlockSpec((tr, d), lambda i: (i, 0)), pl.BlockSpec((tr, d), lambda i: (i, 0)),
                  pl.BlockSpec((1, d), lambda i: (0, 0))],
        out_specs=pl.BlockSpec((tr, d), lambda i: (i, 0)),
        out_shape=jax.ShapeDtypeStruct((m, d), F32),
        compiler_params=_cparams(("parallel",)),
        name="add_rms",
    )(x, y, w.reshape(1, d))


def _add_rms_cast_kernel(x_ref, y_ref, w_ref, wn_ref, o_ref, h_ref):
    y = y_ref[...]
    yn = y * lax.rsqrt(jnp.mean(y * y, axis=-1, keepdims=True) + NORM_EPS)
    x = x_ref[...] + yn * w_ref[...]
    o_ref[...] = x
    xn = x * lax.rsqrt(jnp.mean(x * x, axis=-1, keepdims=True) + NORM_EPS)
    h_ref[...] = (xn * wn_ref[...]).astype(h_ref.dtype)


def add_rms_cast(x, y, w, w_next, tr):
    m, d = x.shape
    row = pl.BlockSpec((tr, d), lambda i: (i, 0))
    vec = pl.BlockSpec((1, d), lambda i: (0, 0))
    return pl.pallas_call(
        _add_rms_cast_kernel,
        grid=(m // tr,),
        in_specs=[row, row, vec, vec],
        out_specs=[row, row],
        out_shape=[jax.ShapeDtypeStruct((m, d), F32), jax.ShapeDtypeStruct((m, d), BF16)],
        compiler_params=_cparams(("parallel",)),
        name="add_rms_cast",
    )(x, y, w.reshape(1, d), w_next.reshape(1, d))


def _mm_kernel(a_ref, w_ref, o_ref, *, nk):
    p = jnp.dot(a_ref[...], w_ref[...].astype(BF16), preferred_element_type=F32)
    if nk == 1:
        o_ref[...] = p
    else:
        k = pl.program_id(2)

        @pl.when(k == 0)
        def _():
            o_ref[...] = p

        @pl.when(k > 0)
        def _():
            o_ref[...] += p


def matmul(a, w, tm, tn, tk, name, li=None):
    m, kd = a.shape
    n = w.shape[-1]
    nk = kd // tk
    if li is None:
        w_spec = pl.BlockSpec((tk, tn), lambda i, j, k: (k, j))
    else:
        w_spec = pl.BlockSpec((None, tk, tn), lambda i, j, k: (li, k, j))
    return pl.pallas_call(
        functools.partial(_mm_kernel, nk=nk),
        grid=(m // tm, n // tn, nk),
        in_specs=[pl.BlockSpec((tm, tk), lambda i, j, k: (i, k)), w_spec],
        out_specs=pl.BlockSpec((tm, tn), lambda i, j, k: (i, j)),
        out_shape=jax.ShapeDtypeStruct((m, n), F32),
        compiler_params=_cparams(("parallel", "parallel", "arbitrary")),
        name=name,
    )(a, w)


def _mm_into_kernel(a_ref, w1_ref, w2_ref, *rest):
    o1_ref, o2_ref, w1_scr, w2_scr = rest[-4:]

    @pl.when(pl.program_id(1) == 0)
    def _():
        w1_scr[...] = w1_ref[0].astype(BF16)
        w2_scr[...] = w2_ref[0].astype(BF16)

    a = a_ref[...]
    o1_ref[...] = _dot(a, w1_scr[...], NT, None)
    o2_ref[...] = _dot(a, w2_scr[...], NT, None)


def matmul_into(a, w_t_all, row1, row2, n, bufs, li, depth, tm, tn, name):
    m, kd = a.shape
    assert row1 % ROW_UNIT == 0 and row2 % ROW_UNIT == 0 and tn % ROW_UNIT == 0

    def w_spec(row):
        return pl.BlockSpec((pl.Element(1), pl.Element(tn), pl.Element(kd)),
                            lambda j, i: (li, (row // ROW_UNIT + j * (tn // ROW_UNIT)) * ROW_UNIT, 0))

    in_specs = [pl.BlockSpec((tm, kd), lambda j, i: (i, 0)), w_spec(row1), w_spec(row2)]
    args = [a, w_t_all, w_t_all]
    aliases = {}
    if bufs[0] is not None:
        in_specs += [pl.BlockSpec(memory_space=pl.ANY)] * 2
        args += list(bufs)
        aliases = {3: 0, 4: 1}
    out_spec = pl.BlockSpec((None, tm, tn), lambda j, i: (li, i, j))
    return pl.pallas_call(
        _mm_into_kernel,
        grid=(n // tn, m // tm),
        in_specs=in_specs,
        out_specs=[out_spec, out_spec],
        out_shape=[jax.ShapeDtypeStruct((depth, m, n), F32)] * 2,
        scratch_shapes=[pltpu.VMEM((tn, kd), BF16), pltpu.VMEM((tn, kd), BF16)],
        input_output_aliases=aliases,
        compiler_params=_cparams(("arbitrary", "arbitrary")),
        name=name,
    )(*args)


def _mm_groups_kernel(*refs):
    *a_refs, w_ref, o_ref, w_scr = refs
    kg = a_refs[0].shape[1]

    @pl.when(pl.program_id(1) == 0)
    def _():
        w_scr[...] = w_ref[...].astype(BF16)

    acc = jnp.dot(a_refs[0][...], w_scr[0:kg, :], preferred_element_type=F32)
    for g in range(1, len(a_refs)):
        acc = acc + jnp.dot(a_refs[g][...], w_scr[g * kg:(g + 1) * kg, :], preferred_element_type=F32)
    o_ref[...] = acc


def matmul_groups(parts, w_all, li, tm, tn, name):
    m, kg = parts[0].shape
    _, kd, n = w_all.shape
    return pl.pallas_call(
        _mm_groups_kernel,
        grid=(n // tn, m // tm),
        in_specs=[pl.BlockSpec((tm, kg), lambda j, i: (i, 0)) for _ in parts]
        + [pl.BlockSpec((None, kd, tn), lambda j, i: (li, 0, j))],
        out_specs=pl.BlockSpec((tm, tn), lambda j, i: (i, j)),
        out_shape=jax.ShapeDtypeStruct((m, n), F32),
        scratch_shapes=[pltpu.VMEM((kd, tn), BF16)],
        compiler_params=_cparams(("arbitrary", "arbitrary")),
        name=name,
    )(*parts, w_all)


def _ffn_act_kernel(g_ref, v_ref, halo_ref, st_ref, cw_ref, cb_ref, o_ref):
    g = g_ref[...]
    tail = jnp.where(pl.program_id(1) == 0, st_ref[...], halo_ref[...])
    taps = _shifted_taps(tail, g, FFN_TAPS)
    cw = cw_ref[...]
    y = taps[0] * cw[0:1]
    for i in range(1, FFN_TAPS):
        y = y + taps[i] * cw[i:i + 1]
    y = y + cb_ref[...]
    o_ref[...] = (_silu(y) * v_ref[...]).astype(o_ref.dtype)


def _ffn_up_act_kernel(x_ref, wg_ref, wv_ref, st_ref, cw_ref, cb_ref, act_ref, cst_ref, wg_scr, wv_scr, tail_scr,
                       *, tiles_per_seq, sub):
    i = pl.program_id(1)

    @pl.when(i == 0)
    def _():
        wg_scr[...] = wg_ref[...].astype(BF16)
        wv_scr[...] = wv_ref[...].astype(BF16)

    @pl.when(i % tiles_per_seq == 0)
    def _():
        tail_scr[...] = st_ref[...]

    wg = wg_scr[...]
    wv = wv_scr[...]
    cw = cw_ref[...]
    cb = cb_ref[...]
    tm = x_ref.shape[0]
    tail = tail_scr[...]

    def project(r):
        x = x_ref[r * sub:(r + 1) * sub, :]
        return jnp.dot(x, wg, preferred_element_type=F32), jnp.dot(x, wv, preferred_element_type=F32)

    nxt = project(0)
    for r in range(tm // sub):
        g, v = nxt
        if r + 1 < tm // sub:
            nxt = project(r + 1)
        taps = _shifted_taps(tail, g, FFN_TAPS)
        y = taps[0] * cw[0:1]
        for t in range(1, FFN_TAPS):
            y = y + taps[t] * cw[t:t + 1]
        t = 0.5 * (y + cb)
        act_ref[r * sub:(r + 1) * sub, :] = ((t + t * jnp.tanh(t)) * v).astype(act_ref.dtype)
        tail = g[sub - SUBLANE:]
    tail_scr[...] = tail
    cst_ref[...] = tail


def ffn_up_act(h2, w_up_all, li, state8, conv_w, conv_b, l, tm, tn, sub):
    m, kd = h2.shape
    b = m // l
    nj = D_FF // tn
    tiles_per_seq = l // tm
    act, tails = pl.pallas_call(
        functools.partial(_ffn_up_act_kernel, tiles_per_seq=tiles_per_seq, sub=sub),
        grid=(nj, m // tm),
        in_specs=[
            pl.BlockSpec((tm, kd), lambda j, i: (i, 0)),
            pl.BlockSpec((None, kd, tn), lambda j, i: (li, 0, j)),
            pl.BlockSpec((None, kd, tn), lambda j, i: (li, 0, j + nj)),
            pl.BlockSpec((None, SUBLANE, tn), lambda j, i: (i // tiles_per_seq, 0, j)),
            pl.BlockSpec((FFN_TAPS, tn), lambda j, i: (0, j)),
            pl.BlockSpec((1, tn), lambda j, i: (0, j)),
        ],
        out_specs=[
            pl.BlockSpec((tm, tn), lambda j, i: (i, j)),
            pl.BlockSpec((None, SUBLANE, tn), lambda j, i: (i, 0, j)),
        ],
        out_shape=[jax.ShapeDtypeStruct((m, D_FF), BF16), jax.ShapeDtypeStruct((m // tm, SUBLANE, D_FF), F32)],
        scratch_shapes=[pltpu.VMEM((kd, tn), BF16), pltpu.VMEM((kd, tn), BF16), pltpu.VMEM((SUBLANE, tn), F32)],
        compiler_params=_cparams(("arbitrary", "arbitrary")),
        name="ffn_up_act",
    )(h2, w_up_all, w_up_all, state8, conv_w, conv_b.reshape(1, D_FF))
    return act, tails.reshape(b, tiles_per_seq, SUBLANE, D_FF)[:, tiles_per_seq - 1]


def ffn_act(up3, state8, conv_w, conv_b, ts, tn):
    b, l, _ = up3.shape
    nj = D_FF // tn
    hb = ts // SUBLANE
    return pl.pallas_call(
        _ffn_act_kernel,
        grid=(b, l // ts, nj),
        in_specs=[
            pl.BlockSpec((None, ts, tn), lambda bi, i, j: (bi, i, j)),
            pl.BlockSpec((None, ts, tn), lambda bi, i, j: (bi, i, j + nj)),
            pl.BlockSpec((None, SUBLANE, tn), lambda bi, i, j: (bi, jnp.maximum(i * hb - 1, 0), j)),
            pl.BlockSpec((None, SUBLANE, tn), lambda bi, i, j: (bi, 0, j)),
            pl.BlockSpec((FFN_TAPS, tn), lambda bi, i, j: (0, j)),
            pl.BlockSpec((1, tn), lambda bi, i, j: (0, j)),
        ],
        out_specs=pl.BlockSpec((None, ts, tn), lambda bi, i, j: (bi, i, j)),
        out_shape=jax.ShapeDtypeStruct((b, l, D_FF), BF16),
        compiler_params=_cparams(("parallel", "parallel", "parallel")),
        name="ffn_act",
    )(up3, up3, up3, state8, conv_w, conv_b.reshape(1, D_FF))


def _gdn_kernel(qkv_ref, z_ref, sm_ref, cbuf_ref, s0_ref, cw_ref, alog_ref, dtb_ref, nw_ref,
                o_ref, sout_ref, s_scr, tail_scr, *, c, nsub):
    ci = pl.program_id(1)

    @pl.when(ci == 0)
    def _():
        s_scr[...] = s0_ref[...]
        tail_scr[...] = cbuf_ref[...]

    x = qkv_ref[...]
    taps = _shifted_taps(tail_scr[...], x, GDN_TAPS)
    rows = nsub * c
    tail_scr[...] = x[rows - SUBLANE:]
    cw = cw_ref[...]
    y = taps[0] * cw[0:1]
    for i in range(1, GDN_TAPS):
        y = y + taps[i] * cw[i:i + 1]
    y = _silu(y)

    sm = sm_ref[...]
    beta_all = jax.nn.sigmoid(sm)
    g_all = -jnp.exp(alog_ref[...]) * jax.nn.softplus(sm + dtb_ref[...])
    ri = _iota2((rows, rows), 0)
    rj = _iota2((rows, rows), 1)
    in_chunk_tri = jnp.where((ri >= rj) & ((ri >> _log2(c)) == (rj >> _log2(c))), 1.0, 0.0).astype(F32)
    gcum_all = dot_nn(in_chunk_tri, g_all)
    rows_of = _row_getter(gcum_all, rows)

    row = _iota2((c, c), 0)
    col = _iota2((c, c), 1)
    tri = row >= col
    strict = row > col
    z = z_ref[...]
    nw = nw_ref[...]
    heads = range(GDN_HEADS)
    units = [(q, h) for q in range(nsub) for h in heads]
    ms, aqk, rhs, qd, kd, gls = [], [], [], [], [], []
    for qi, h in units:
        rs = slice(qi * c, (qi + 1) * c)
        lo = h * GDN_D
        q = y[rs, lo:lo + GDN_D]
        k = y[rs, GROUP_W + lo:GROUP_W + lo + GDN_D]
        v = y[rs, 2 * GROUP_W + lo:2 * GROUP_W + lo + GDN_D]
        q = q * lax.rsqrt(jnp.sum(q * q, axis=-1, keepdims=True) + 1e-6) * (GDN_D ** -0.5)
        k = k * lax.rsqrt(jnp.sum(k * k, axis=-1, keepdims=True) + 1e-6)
        beta = beta_all[rs, SM_GDN_B + h:SM_GDN_B + h + 1]
        gc = gcum_all[rs, SM_GDN_A + h:SM_GDN_A + h + 1]
        gam = jnp.exp(jnp.where(tri, gc - rows_of(SM_GDN_A + h)[:, qi * c:(qi + 1) * c], -jnp.inf))
        kbeta = k * beta
        mq = dot3s(jnp.concatenate([kbeta, q], axis=0), _split(k), NT)
        ms.append(jnp.where(strict, mq[:c] * gam, 0.0))
        aqk.append(mq[c:] * gam)
        eg = jnp.exp(gc)
        gl = gc[c - 1:c]
        rhs.append(jnp.concatenate([kbeta * eg, v * beta], axis=1))
        qd.append(q * eg)
        kd.append(k * jnp.exp(gl - gc))
        gls.append(gl)
    packed = _inv_unit_lower_pairs([jnp.concatenate([ms[i], ms[i + 1]], axis=1) for i in range(0, len(ms), 2)], c)
    ts = [t[:, half * c:(half + 1) * c] for t in packed for half in range(2)]
    wu = [dot3s(t, _split(x), NN) for t, x in zip(ts, rhs)]
    state = [s_scr[h] for h in heads]
    for qi in range(nsub):
        rs = slice(qi * c, (qi + 1) * c)
        ix = [qi * GDN_HEADS + h for h in heads]
        wqs = [dot3s(jnp.concatenate([wu[i][:, :GDN_D], qd[i]], axis=0), _split(state[h]), NN)
               for h, i in zip(heads, ix)]
        v_new = [wu[i][:, GDN_D:] - wqs[h][:c] for h, i in zip(heads, ix)]
        outs = [wqs[h][c:] + dot1(aqk[i], v_new[h], NN) for h, i in zip(heads, ix)]
        state = [state[h] * jnp.exp(gls[i]) + dot3(_split(kd[i]), _split(v_new[h]), TN) for h, i in zip(heads, ix)]
        for h in heads:
            lo = h * GDN_D
            o = outs[h]
            o = o * lax.rsqrt(jnp.mean(o * o, axis=-1, keepdims=True) + NORM_EPS) * nw
            o = o * _silu(z[rs, lo:lo + GDN_D])
            o_ref[rs, lo:lo + GDN_D] = o.astype(o_ref.dtype)
    for h in heads:
        s_scr[h] = state[h]

    @pl.when(ci == pl.num_programs(1) - 1)
    def _():
        sout_ref[...] = s_scr[...]


def gdn_mixer(proj3, cbuf8, s0, conv_w, a_log, dt_bias, norm_w, c):
    b, l, _ = proj3.shape
    alog_row = jnp.zeros((1, LANE), F32).at[0, SM_GDN_A:SM_GDN_A + GDN_HEADS].set(a_log)
    dtb_row = jnp.zeros((1, LANE), F32).at[0, SM_GDN_A:SM_GDN_A + GDN_HEADS].set(dt_bias)
    w3 = 3 * GROUP_W
    const2 = lambda bi, ci: (0, 0)
    nsub = next(k for k in (4, 2, 1) if l % (k * c) == 0)
    rb = nsub * c
    return pl.pallas_call(
        functools.partial(_gdn_kernel, c=c, nsub=nsub),
        grid=(b, l // rb),
        in_specs=[
            pl.BlockSpec((None, rb, w3), lambda bi, ci: (bi, ci, COL_A_QKV // w3)),
            pl.BlockSpec((None, rb, GROUP_W), lambda bi, ci: (bi, ci, COL_A_Z // GROUP_W)),
            pl.BlockSpec((None, rb, LANE), lambda bi, ci: (bi, ci, COL_SMALL // LANE)),
            pl.BlockSpec((None, SUBLANE, w3), lambda bi, ci: (bi, 0, 0)),
            pl.BlockSpec((None, GDN_HEADS, GDN_D, GDN_D), lambda bi, ci: (bi, 0, 0, 0)),
            pl.BlockSpec((GDN_TAPS, w3), const2),
            pl.BlockSpec((1, LANE), const2),
            pl.BlockSpec((1, LANE), const2),
            pl.BlockSpec((1, GDN_D), const2),
        ],
        out_specs=[
            pl.BlockSpec((None, rb, GROUP_W), lambda bi, ci: (bi, ci, 0)),
            pl.BlockSpec((None, GDN_HEADS, GDN_D, GDN_D), lambda bi, ci: (bi, 0, 0, 0)),
        ],
        out_shape=[jax.ShapeDtypeStruct((b, l, GROUP_W), BF16),
                   jax.ShapeDtypeStruct((b, GDN_HEADS, GDN_D, GDN_D), F32)],
        scratch_shapes=[pltpu.VMEM((GDN_HEADS, GDN_D, GDN_D), F32), pltpu.VMEM((SUBLANE, w3), F32)],
        compiler_params=_cparams(("parallel", "arbitrary")),
        name="gdn_mixer",
    )(proj3, proj3, proj3, cbuf8, s0, conv_w, alog_row, dtb_row, norm_w.reshape(1, GDN_D))


def _rwkv_kernel(rkv_ref, lora_ref, sh_rkv_ref, sh_lora_ref, s0_ref, mu_rkv_ref, mu_lora_ref,
                 w0_ref, w2_ref, a0_ref, a2_ref, g2_ref, kk_ref, ka_ref, rk_ref, lnw_ref, lnb_ref,
                 o_ref, sout_ref, s_scr, tail_rkv, tail_lora, *, c, nsub):
    ci = pl.program_id(1)

    @pl.when(ci == 0)
    def _():
        s_scr[...] = s0_ref[...]
        tail_rkv[...] = sh_rkv_ref[...]
        tail_lora[...] = sh_lora_ref[...]

    x = rkv_ref[...]
    xl = lora_ref[...]
    prev = _shifted_taps(tail_rkv[...], x, 2)[0]
    prev_l = _shifted_taps(tail_lora[...], xl, 2)[0]
    tail_rkv[...] = x[nsub * c - SUBLANE:]
    tail_lora[...] = xl[nsub * c - SUBLANE:]
    zm = x + (prev - x) * mu_rkv_ref[...]
    zl = xl + (prev_l - xl) * mu_lora_ref[...]
    r = zm[:, 0:GROUP_W]
    k = zm[:, GROUP_W:2 * GROUP_W]
    v = zm[:, 2 * GROUP_W:3 * GROUP_W]
    wd = zl[:, 0:RWKV_W_LORA]
    ad = zl[:, RWKV_W_LORA:RWKV_W_LORA + RWKV_A_LORA]
    gd = zl[:, RWKV_W_LORA + RWKV_A_LORA:LORA_PAD]

    w_log = -jax.nn.softplus(-(w0_ref[...] + dot3s(jnp.tanh(wd), _split(w2_ref[...]), NN))) - 0.5
    logw = -jnp.exp(w_log)
    a = jax.nn.sigmoid(a0_ref[...] + dot3s(ad, _split(a2_ref[...]), NN))
    gate = dot1(jax.nn.sigmoid(gd), g2_ref[...], NN)
    kkv = k * kk_ref[...]
    k2 = k * (1.0 + (a - 1.0) * ka_ref[...])
    rows = nsub * c
    ri = _iota2((rows, rows), 0)
    rj = _iota2((rows, rows), 1)
    lc_sh = _log2(c)
    in_chunk_tri = jnp.where((ri >= rj) & ((ri >> lc_sh) == (rj >> lc_sh)), 1.0, 0.0).astype(F32)
    lcum = dot_nn(in_chunk_tri, logw)

    n = RWKV_HS
    pw = 2 * n
    rowc = _iota2((c, 2 * c), 0)
    colc = _iota2((c, 2 * c), 1) & (c - 1)
    strict2 = rowc > colc
    tri2 = rowc >= colc
    same_head = (_iota2((pw, pw), 0) < n) == (_iota2((pw, pw), 1) < n)
    ones_bd = jnp.where(same_head, 1.0, 0.0).astype(BF16)
    rk = rk_ref[...]
    lnw = lnw_ref[...]
    lnb = lnb_ref[...]
    pairs = range(RWKV_HEADS // 2)
    sl = [slice(p * pw, (p + 1) * pw) for p in pairs]
    units = [(slice(q * c, (q + 1) * c), sl[p]) for q in range(nsub) for p in pairs]
    kk_ss = [_head_sums(jnp.square(kkv[rs, ps]), ones_bd) for rs, ps in units]
    bonus_s = [_head_sums(r[rs, ps] * k2[rs, ps] * rk[:, ps], ones_bd) for rs, ps in units]
    lhs, x_bs, x_ks, bk_end, l_last = [], [], [], [], []
    for (rs, ps), ss in zip(units, kk_ss):
        kk = kkv[rs, ps] * lax.rsqrt(ss + 1e-6)
        lc = lcum[rs, ps]
        ll = lc[c - 1:c]
        p_inv = jnp.exp(-lc)
        a_t = -kk * jnp.exp(lc - logw[rs, ps])
        b_vec = kk * a[rs, ps]
        r_t = r[rs, ps] * jnp.exp(lc)
        p_end = jnp.exp(ll - lc)
        ar = jnp.concatenate([a_t, r_t], axis=0)
        x_bs.append(dot3s(ar, _split(_blockdiag_rows(b_vec * p_inv, n)), NT))
        x_ks.append(dot3s(ar, _split(_blockdiag_rows(k2[rs, ps] * p_inv, n)), NT))
        lhs.append(ar)
        bk_end.append(jnp.concatenate([b_vec * p_end, k2[rs, ps] * p_end], axis=0))
        l_last.append(ll)
    ms = [jnp.where(strict2, -x[:c], 0.0) for x in x_bs]
    av = [dot3s(jnp.where(strict2, x[:c], 0.0), _split(_blockdiag_rows(v[rs, ps], n)), NN)
          for x, (rs, ps) in zip(x_ks, units)]
    rbk = [jnp.concatenate([jnp.where(tri2, xb[c:], 0.0), jnp.where(tri2, xk[c:], 0.0)], axis=1)
           for xb, xk in zip(x_bs, x_ks)]
    ts = _inv_unit_lower_pairs(ms, c)
    state = [s_scr[p] for p in pairs]
    npair = len(pairs)
    for q in range(nsub):
        rs = slice(q * c, (q + 1) * c)
        ix = [q * npair + p for p in pairs]
        ars = [dot3s(lhs[i], _split(state[p]), NT) for p, i in zip(pairs, ix)]
        us = [dot3s(ts[i], _split(_blockdiag_rows(x[:c] + av[i], n)), NN) for x, i in zip(ars, ix)]
        vs = [v[rs, sl[p]] for p in pairs]
        ys = [ars[p][c:] + _dot(rbk[i].astype(BF16),
                                jnp.concatenate([_blockdiag_rows(us[p], n), _blockdiag_rows(vs[p], n)],
                                                axis=0).astype(BF16), NN, None)
              for p, i in zip(pairs, ix)]
        upd = [dot3(_split(jnp.concatenate([us[p], vs[p]], axis=0)), _split(bk_end[i]), TN)
               for p, i in zip(pairs, ix)]
        state = [jnp.where(same_head, state[p] * jnp.exp(l_last[i]) + upd[p], 0.0) for p, i in zip(pairs, ix)]
        devs = [y - _head_sums(y, ones_bd) * (1.0 / n) for y in ys]
        var = [_head_sums(jnp.square(d), ones_bd) * (1.0 / n) for d in devs]
        for p, i in zip(pairs, ix):
            yn = devs[p] * lax.rsqrt(var[p] + RWKV_GN_EPS) * lnw[:, sl[p]] + lnb[:, sl[p]]
            o_ref[rs, sl[p]] = ((yn + bonus_s[i] * vs[p]) * gate[rs, sl[p]]).astype(o_ref.dtype)
    for p in pairs:
        s_scr[p] = state[p]

    @pl.when(ci == pl.num_programs(1) - 1)
    def _():
        sout_ref[...] = s_scr[...]


def rwkv_mixer(proj3, sh_rkv8, sh_lora8, s0, prm, c):
    b, l, _ = proj3.shape
    nsub = next(k for k in (4, 2, 1) if l % (k * c) == 0)
    rb = nsub * c
    w3 = 3 * GROUP_W
    mu = prm['rwkv_mu']
    mu_rkv = mu[:w3].reshape(1, w3)
    mu_lora = jnp.pad(mu[w3:], (0, LORA_PAD - RWKV_LORA)).reshape(1, LORA_PAD)
    g2 = jnp.pad(prm['rwkv_g2'], ((0, G_LORA_PAD - RWKV_G_LORA), (0, 0)))
    row = lambda t: t.reshape(1, GROUP_W)
    const2 = lambda bi, ci: (0, 0)
    vec = pl.BlockSpec((1, GROUP_W), const2)
    n, npair = RWKV_HS, RWKV_HEADS // 2
    sp = s0.reshape(b, npair, 2, n, n)
    zero = jnp.zeros((b, npair, n, n), F32)
    s_pairs = jnp.concatenate([jnp.concatenate([sp[:, :, 0], zero], axis=-1),
                               jnp.concatenate([zero, sp[:, :, 1]], axis=-1)], axis=-2)
    state_spec = pl.BlockSpec((None, npair, 2 * n, 2 * n), lambda bi, ci: (bi, 0, 0, 0))
    o, s_new = pl.pallas_call(
        functools.partial(_rwkv_kernel, c=c, nsub=nsub),
        grid=(b, l // rb),
        in_specs=[
            pl.BlockSpec((None, rb, w3), lambda bi, ci: (bi, ci, COL_B_RKV // w3)),
            pl.BlockSpec((None, rb, LORA_PAD), lambda bi, ci: (bi, ci, COL_B_LORA // LORA_PAD)),
            pl.BlockSpec((None, SUBLANE, w3), lambda bi, ci: (bi, 0, 0)),
            pl.BlockSpec((None, SUBLANE, LORA_PAD), lambda bi, ci: (bi, 0, 0)),
            state_spec,
            pl.BlockSpec((1, w3), const2),
            pl.BlockSpec((1, LORA_PAD), const2),
            vec,
            pl.BlockSpec((RWKV_W_LORA, GROUP_W), const2),
            vec,
            pl.BlockSpec((RWKV_A_LORA, GROUP_W), const2),
            pl.BlockSpec((G_LORA_PAD, GROUP_W), const2),
            vec, vec, vec, vec, vec,
        ],
        out_specs=[
            pl.BlockSpec((None, rb, GROUP_W), lambda bi, ci: (bi, ci, 0)),
            state_spec,
        ],
        out_shape=[jax.ShapeDtypeStruct((b, l, GROUP_W), BF16),
                   jax.ShapeDtypeStruct((b, npair, 2 * n, 2 * n), F32)],
        scratch_shapes=[pltpu.VMEM((npair, 2 * n, 2 * n), F32),
                        pltpu.VMEM((SUBLANE, w3), F32), pltpu.VMEM((SUBLANE, LORA_PAD), F32)],
        compiler_params=_cparams(("parallel", "arbitrary")),
        name="rwkv_mixer",
    )(proj3, proj3, sh_rkv8, sh_lora8, s_pairs, mu_rkv, mu_lora,
      row(prm['rwkv_w0']), prm['rwkv_w2'], row(prm['rwkv_a0']), prm['rwkv_a2'], g2,
      row(prm['rwkv_k_k']), row(prm['rwkv_k_a']), row(prm['rwkv_r_k']),
      row(prm['rwkv_ln_w']), row(prm['rwkv_ln_b']))
    s_heads = jnp.stack([s_new[:, :, :n, :n], s_new[:, :, n:, n:]], axis=2)
    return o, s_heads.reshape(b, RWKV_HEADS, n, n)


def _ssd_kernel(z_ref, xbc_ref, sm_ref, cbuf_ref, s0_ref, cw_ref, cb_ref, dtb_ref, alog_ref, dsk_ref, nw_ref,
                o_ref, sout_ref, s_scr, tail_scr, y_scr, *, c):
    ci = pl.program_id(1)

    @pl.when(ci == 0)
    def _():
        s_scr[...] = s0_ref[...]
        tail_scr[...] = cbuf_ref[...]

    x = xbc_ref[...]
    taps = _shifted_taps(tail_scr[...], x, SSM_TAPS)
    tail_scr[...] = x[c - SUBLANE:]
    cw = cw_ref[...]
    y = taps[0] * cw[0:1]
    for i in range(1, SSM_TAPS):
        y = y + taps[i] * cw[i:i + 1]
    y = _silu(y + cb_ref[...])
    xs = y[:, 0:GROUP_W]
    gn = SSM_GROUPS * SSM_N

    sm = sm_ref[...]
    dt_all = jax.nn.softplus(sm + dtb_ref[...])
    da_all = dt_all * (-jnp.exp(alog_ref[...]))
    acs_all = _cumsum_rows(da_all, c)

    row = _iota2((c, c), 0)
    col = _iota2((c, c), 1)
    tri = row >= col
    z = z_ref[...]
    dsk = dsk_ref[...]
    hpg = SSM_HEADS // SSM_GROUPS
    gp = hpg * SSM_P
    rows_of = _row_getter(acs_all, c)
    for g in range(SSM_GROUPS):
        bm = y[:, GROUP_W + g * SSM_N:GROUP_W + (g + 1) * SSM_N]
        cm = y[:, GROUP_W + gn + g * SSM_N:GROUP_W + gn + (g + 1) * SSM_N]
        cb = dot1(cm, bm, NT)
        sg = s_scr[g]
        y_off = dot1(cm, sg, NT)
        xdec = []
        for rr in range(hpg):
            h = g * hpg + rr
            lo = h * SSM_P
            lane = SM_SSM_DT + h
            xs_h = xs[:, lo:lo + SSM_P]
            dt = dt_all[:, lane:lane + 1]
            acs = acs_all[:, lane:lane + 1]
            lmat = jnp.exp(jnp.where(tri, acs - rows_of(lane), -jnp.inf))
            xd = xs_h * dt
            a_last = acs[c - 1:c]
            xdec.append(xd * jnp.exp(a_last - acs))
            yh = dot1(cb * lmat, xd, NN) + y_off[:, rr * SSM_P:(rr + 1) * SSM_P] * jnp.exp(acs)
            yh = yh + xs_h * dsk[:, lane:lane + 1]
            y_scr[:, lo:lo + SSM_P] = yh * _silu(z[:, lo:lo + SSM_P])
        upd = dot3(_split(jnp.concatenate(xdec, axis=1)), _split(bm), TN)
        for rr in range(hpg):
            lane = SM_SSM_DT + g * hpg + rr
            dec = jnp.exp(acs_all[c - 1:c, lane:lane + 1])
            s_scr[g, rr * SSM_P:(rr + 1) * SSM_P, :] = (sg[rr * SSM_P:(rr + 1) * SSM_P] * dec
                                                         + upd[rr * SSM_P:(rr + 1) * SSM_P])

    gw = GROUP_W // SSM_GROUPS
    nw = nw_ref[...]
    for g in range(SSM_GROUPS):
        yg = y_scr[:, g * gw:(g + 1) * gw]
        yg = yg * lax.rsqrt(jnp.mean(yg * yg, axis=-1, keepdims=True) + NORM_EPS)
        o_ref[:, g * gw:(g + 1) * gw] = (yg * nw[:, g * gw:(g + 1) * gw]).astype(o_ref.dtype)

    @pl.when(ci == pl.num_programs(1) - 1)
    def _():
        sout_ref[...] = s_scr[...]


def ssd_mixer(proj3, cbuf8, s0, prm, c):
    b, l, _ = proj3.shape

    def lanes(t):
        return jnp.zeros((1, LANE), F32).at[0, SM_SSM_DT:SM_SSM_DT + SSM_HEADS].set(t)

    const2 = lambda bi, ci: (0, 0)
    small = pl.BlockSpec((1, LANE), const2)
    gshape = (SSM_GROUPS, SSM_HEADS // SSM_GROUPS * SSM_P, SSM_N)
    state_spec = pl.BlockSpec((None,) + gshape, lambda bi, ci: (bi, 0, 0, 0))
    o, s_new = pl.pallas_call(
        functools.partial(_ssd_kernel, c=c),
        grid=(b, l // c),
        in_specs=[
            pl.BlockSpec((None, c, GROUP_W), lambda bi, ci: (bi, ci, COL_C_Z // GROUP_W)),
            pl.BlockSpec((None, c, SSM_XBC), lambda bi, ci: (bi, ci, COL_C_XBC // SSM_XBC)),
            pl.BlockSpec((None, c, LANE), lambda bi, ci: (bi, ci, COL_SMALL // LANE)),
            pl.BlockSpec((None, SUBLANE, SSM_XBC), lambda bi, ci: (bi, 0, 0)),
            state_spec,
            pl.BlockSpec((SSM_TAPS, SSM_XBC), const2),
            pl.BlockSpec((1, SSM_XBC), const2),
            small, small, small,
            pl.BlockSpec((1, GROUP_W), const2),
        ],
        out_specs=[
            pl.BlockSpec((None, c, GROUP_W), lambda bi, ci: (bi, ci, 0)),
            state_spec,
        ],
        out_shape=[jax.ShapeDtypeStruct((b, l, GROUP_W), BF16),
                   jax.ShapeDtypeStruct((b,) + gshape, F32)],
        scratch_shapes=[pltpu.VMEM(gshape, F32), pltpu.VMEM((SUBLANE, SSM_XBC), F32),
                        pltpu.VMEM((c, GROUP_W), F32)],
        compiler_params=_cparams(("parallel", "arbitrary")),
        name="ssd_mixer",
    )(proj3, proj3, proj3, cbuf8, s0.reshape((b,) + gshape), prm['ssm_conv_w'],
      prm['ssm_conv_b'].reshape(1, SSM_XBC),
      lanes(prm['ssm_dt_bias']), lanes(prm['ssm_A_log']), lanes(prm['ssm_D']),
      prm['ssm_norm_w'].reshape(1, GROUP_W))
    return o, s_new.reshape(b, SSM_HEADS, SSM_P, SSM_N)


def _swa_weight(d):
    mult = jnp.zeros(d.shape, F32)
    for window, dil in SWA_PATTERNS:
        ok = (d >= 0) & (d <= window) & ((d & (dil - 1)) == 0)
        mult = mult + jnp.where(ok, 1.0, 0.0)
    return mult


def _swa_scores(q, k, d, slope):
    s = dot_nt(q.astype(BF16), k.astype(BF16), None) * (SWA_HD ** -0.5)
    mult = _swa_weight(d)
    s = s - slope * d.astype(F32)
    return jnp.where(mult > 0.0, s, NEG_INF), mult


def _swa_prompt_kernel(slopes_ref, q_ref, k_ref, v_ref, lw_ref, o_ref, m_scr, l_scr, acc_scr, *, t, sub):
    h = pl.program_id(1)
    qi = pl.program_id(2)
    ki = pl.program_id(3)

    @pl.when(ki == 0)
    def _():
        m_scr[...] = jnp.full(m_scr.shape, NEG_INF, F32)
        l_scr[...] = jnp.zeros(l_scr.shape, F32)
        acc_scr[...] = jnp.zeros(acc_scr.shape, F32)

    @pl.when(ki <= qi)
    def _():
        col = (ki * t + _iota2((1, t), 1)).astype(F32) * slopes_ref[h]
        kb = k_ref[...].astype(BF16)
        vb = v_ref[...].astype(BF16)
        def qk(r):
            q = (q_ref[r:r + sub, :] * (SWA_HD ** -0.5)).astype(BF16)
            return _dot(q, kb, NT, None)

        nxt = qk(0)
        for r in range(0, t, sub):
            rs = slice(r, r + sub)
            s = nxt + (lw_ref[qi - ki, rs, :] + col)
            if r + sub < t:
                nxt = qk(r + sub)
            m_old = m_scr[rs, :]
            m_new = jnp.maximum(m_old, jnp.max(s, axis=-1, keepdims=True))
            alpha = jnp.exp(m_old - m_new)
            p = jnp.exp(s - m_new)
            l_scr[rs, :] = alpha * l_scr[rs, :] + jnp.sum(p, axis=-1, keepdims=True)
            acc_scr[rs, :] = alpha * acc_scr[rs, :] + _dot(p.astype(BF16), vb, NN, None)
            m_scr[rs, :] = m_new

    @pl.when(ki == pl.num_programs(3) - 1)
    def _():
        o_ref[...] = (acc_scr[...] / l_scr[...]).astype(o_ref.dtype)


def _alibi_slopes():
    return jnp.asarray([2.0 ** (-8.0 * (i + 1) / SWA_HEADS) for i in range(SWA_HEADS)], F32)


def _swa_log_weight_tiles(n, t):
    d = (jnp.arange(n)[:, None, None] * t + jnp.arange(t)[None, :, None]) - jnp.arange(t)[None, None, :]
    mult = _swa_weight(d.astype(jnp.int32))
    return jnp.where(mult > 0.0, jnp.log(jnp.maximum(mult, 1.0)), NEG_INF)


def swa_prompt(proj3, k4, v4, li, t):
    b, l, _ = proj3.shape
    qc = COL_D_Q // SWA_HD
    n = l // t
    kv_spec = pl.BlockSpec((None, None, t, SWA_HD), lambda bi, h, qi, ki: (li, bi, jnp.minimum(ki, qi), h))

    return pl.pallas_call(
        functools.partial(_swa_prompt_kernel, t=t, sub=min(t, 256)),
        grid=(b, SWA_HEADS, n, n),
        in_specs=[
            pl.BlockSpec(memory_space=pltpu.SMEM),
            pl.BlockSpec((None, t, SWA_HD), lambda bi, h, qi, ki: (bi, qi, qc + h)),
            kv_spec,
            kv_spec,
            pl.BlockSpec((n, t, t), lambda bi, h, qi, ki: (0, 0, 0)),
        ],
        out_specs=pl.BlockSpec((None, t, SWA_HD), lambda bi, h, qi, ki: (bi, qi, h)),
        out_shape=jax.ShapeDtypeStruct((b, l, GROUP_W), BF16),
        scratch_shapes=[pltpu.VMEM((t, 1), F32), pltpu.VMEM((t, 1), F32), pltpu.VMEM((t, SWA_HD), F32)],
        compiler_params=_cparams(("parallel", "parallel", "parallel", "arbitrary")),
        name="swa_prompt",
    )(_alibi_slopes(), proj3, k4, v4, _swa_log_weight_tiles(n, t))


def _swa_sample_kernel(slopes_ref, q_ref, k_ref, v_ref, ck_ref, cv_ref, o_ref, *, t, wb):
    d_c = (wb + _iota2((t, wb), 0)) - _iota2((t, wb), 1)
    d_n = _iota2((t, t), 0) - _iota2((t, t), 1)
    for h in range(SWA_HEADS):
        hs = slice(h * SWA_HD, (h + 1) * SWA_HD)
        ck = ck_ref[pl.ds(h, wb, stride=SWA_HEADS), :]
        cv = cv_ref[pl.ds(h, wb, stride=SWA_HEADS), :]
        q = q_ref[:, hs]
        s_c, mult_c = _swa_scores(q, ck, d_c, slopes_ref[h])
        s_n, mult_n = _swa_scores(q, k_ref[:, hs], d_n, slopes_ref[h])
        m = jnp.maximum(jnp.max(s_c, axis=-1, keepdims=True), jnp.max(s_n, axis=-1, keepdims=True))
        p_c = jnp.exp(s_c - m) * mult_c
        p_n = jnp.exp(s_n - m) * mult_n
        den = jnp.sum(p_c, axis=-1, keepdims=True) + jnp.sum(p_n, axis=-1, keepdims=True)
        num = (dot_nn(p_c.astype(BF16), cv.astype(BF16), None)
               + dot_nn(p_n.astype(BF16), v_ref[:, hs].astype(BF16), None))
        o_ref[:, hs] = (num / den).astype(o_ref.dtype)


def swa_sample(proj3, k4, v4, li, cache_k_all, cache_v_all):
    b, t, _ = proj3.shape
    depth, _, wb = cache_k_all.shape[:3]
    ck = cache_k_all.reshape(depth, b, wb * SWA_HEADS, SWA_HD)
    cv = cache_v_all.reshape(depth, b, wb * SWA_HEADS, SWA_HD)
    new_spec = pl.BlockSpec((None, None, t, GROUP_W), lambda bi: (li, bi, 0, 0))
    cache_spec = pl.BlockSpec((None, None, wb * SWA_HEADS, SWA_HD), lambda bi: (li, bi, 0, 0))
    return pl.pallas_call(
        functools.partial(_swa_sample_kernel, t=t, wb=wb),
        grid=(b,),
        in_specs=[
            pl.BlockSpec(memory_space=pltpu.SMEM),
            pl.BlockSpec((None, t, GROUP_W), lambda bi: (bi, 0, COL_D_Q // GROUP_W)),
            new_spec,
            new_spec,
            cache_spec,
            cache_spec,
        ],
        out_specs=pl.BlockSpec((None, t, GROUP_W), lambda bi: (bi, 0, 0)),
        out_shape=jax.ShapeDtypeStruct((b, t, GROUP_W), BF16),
        compiler_params=_cparams(("parallel",)),
        name="swa_sample",
    )(_alibi_slopes(), proj3, k4, v4, ck, cv)


def _tiles(m):
    return (256, 1024) if m >= 1024 else (m, m)


def _front_pad_rows(t, rows=SUBLANE):
    return jnp.pad(t, ((0, 0), (rows - t.shape[1], 0), (0, 0)))


W_IN_SEGMENTS = (('a_qkv', 3 * GROUP_W), ('a_z', GROUP_W), ('a_b', GDN_HEADS), ('a_a', GDN_HEADS),
                 ('b_rkv', 3 * GROUP_W), ('b_lora', RWKV_LORA), ('c_z', GROUP_W), ('c_xbc', SSM_XBC),
                 ('c_dt', SSM_HEADS), ('d_q', GROUP_W), ('d_k', GROUP_W), ('d_v', GROUP_W))
W_IN_SRC = {}
_o = 0
for _name, _n in W_IN_SEGMENTS:
    W_IN_SRC[_name] = _o
    _o += _n
W_IN_WIDE = ((COL_A_QKV, 3 * GROUP_W, W_IN_SRC['a_qkv']), (COL_B_RKV, 3 * GROUP_W, W_IN_SRC['b_rkv']),
             (COL_D_Q, GROUP_W, W_IN_SRC['d_q']), (COL_A_Z, GROUP_W, W_IN_SRC['a_z']),
             (COL_C_Z, GROUP_W, W_IN_SRC['c_z']), (COL_C_XBC, SSM_XBC, W_IN_SRC['c_xbc']))
IN_TN = 512
assert COL_B_LORA % IN_TN == 0 and N_PROJ - COL_B_LORA == IN_TN


ROW_UNIT = 2 * SUBLANE


def _w_in_tile_rows():
    rows = []
    for j in range(N_PROJ // IN_TN - 1):
        c0 = j * IN_TN
        dest, width, src = next(s for s in W_IN_WIDE if s[0] <= c0 < s[0] + s[1])
        assert c0 + IN_TN <= dest + width and (src + c0 - dest) % ROW_UNIT == 0
        rows.append((src + (c0 - dest)) // ROW_UNIT)
    rows.append(0)
    return jnp.asarray(rows, jnp.int32)


W_IN_TAIL = ((0, 'b_lora', RWKV_LORA), (LORA_PAD + SM_GDN_B, 'a_b', 2 * GDN_HEADS),
             (LORA_PAD + SM_SSM_DT, 'c_dt', SSM_HEADS))
assert W_IN_SRC['a_a'] == W_IN_SRC['a_b'] + GDN_HEADS and SM_GDN_A == SM_GDN_B + GDN_HEADS


def _mm_in_kernel(rows_ref, x_ref, wt_ref, *rest, guest):
    del rows_ref
    if guest:
        *tail_refs, xg_ref, o_ref, og_ref, w_scr = rest
    else:
        *tail_refs, o_ref, w_scr = rest

    @pl.when(pl.program_id(1) == 0)
    def _():
        @pl.when(pl.program_id(0) < pl.num_programs(0) - 1)
        def _():
            w_scr[...] = wt_ref[0].astype(BF16)

        @pl.when(pl.program_id(0) == pl.num_programs(0) - 1)
        def _():
            w_scr[...] = jnp.zeros(w_scr.shape, BF16)
            for (row, _, n), ref in zip(W_IN_TAIL, tail_refs):
                w_scr[row:row + n, :] = ref[0].astype(BF16)

    if guest:
        @pl.when(pl.program_id(1) == 0)
        def _():
            og_ref[...] = _dot(xg_ref[...], w_scr[...], NT, None)

    o_ref[...] = _dot(x_ref[...], w_scr[...], NT, None)


def mm_in(h, w_in_t_all, li, tm, h_guest=None):
    m, kd = h.shape
    guest = h_guest is not None
    g_in, g_out, g_shape, g_args = [], [], [], []
    if guest:
        mg = h_guest.shape[0]
        g_in = [pl.BlockSpec((mg, kd), lambda j, i, rows: (0, 0))]
        g_out = [pl.BlockSpec((mg, IN_TN), lambda j, i, rows: (0, j))]
        g_shape = [jax.ShapeDtypeStruct((mg, N_PROJ), F32)]
        g_args = [h_guest]

    def rows_spec(n, start):
        assert start % ROW_UNIT == 0 and n % ROW_UNIT == 0
        return pl.BlockSpec((pl.Element(1), pl.Element(n), pl.Element(kd)),
                            lambda j, i, rows: (li, (start // ROW_UNIT) * ROW_UNIT, 0))

    grid_spec = pltpu.PrefetchScalarGridSpec(
        num_scalar_prefetch=1,
        grid=(N_PROJ // IN_TN, m // tm),
        in_specs=[
            pl.BlockSpec((tm, kd), lambda j, i, rows: (i, 0)),
            pl.BlockSpec((pl.Element(1), pl.Element(IN_TN), pl.Element(kd)),
                         lambda j, i, rows: (li, rows[j] * ROW_UNIT, 0)),
        ] + [rows_spec(n, W_IN_SRC[name]) for _, name, n in W_IN_TAIL] + g_in,
        out_specs=[pl.BlockSpec((tm, IN_TN), lambda j, i, rows: (i, j))] + g_out,
        scratch_shapes=[pltpu.VMEM((IN_TN, kd), BF16)],
    )
    out = pl.pallas_call(
        functools.partial(_mm_in_kernel, guest=guest),
        grid_spec=grid_spec,
        out_shape=[jax.ShapeDtypeStruct((m, N_PROJ), F32)] + g_shape,
        compiler_params=_cparams(("arbitrary", "arbitrary")),
        name="mm_in",
    )(_w_in_tile_rows(), h, *([w_in_t_all] * (1 + len(W_IN_TAIL))), *g_args)
    return (out[0], out[1]) if guest else (out[0], None)


def decoder_layer(x, prm, wts, past, swa_cache, chunks, li, depth, kv=(None, None), h=None, next_pre_w=None,
                  h_guest=None, proj=None):
    b, l, _ = x.shape
    m = b * l
    tr, tm = _tiles(m)
    x2 = x.reshape(m, D_MODEL)
    c_gdn, c_rwkv, c_ssd = chunks

    if h is None:
        h = rms_cast(x2, prm['norm_mix_pre'], tr)
    proj_guest = None
    if proj is None:
        proj, proj_guest = mm_in(h, wts['w_in_t_all'], li, tm, h_guest)
    proj3 = proj.reshape(b, l, N_PROJ)
    kbuf, vbuf = matmul_into(h, wts['w_in_t_all'], W_IN_SRC['d_k'], W_IN_SRC['d_v'], GROUP_W, kv, li, depth, tm,
                             256, "mm_kv")
    k4 = kbuf.reshape(depth, b, l, GROUP_W)
    v4 = vbuf.reshape(depth, b, l, GROUP_W)

    o_a, gdn_s = gdn_mixer(proj3, _front_pad_rows(past['gdn_conv']), past['gdn'], prm['gdn_conv_w'],
                           prm['gdn_A_log'], prm['gdn_dt_bias'], prm['gdn_norm_w'], c_gdn)
    shift = past['rwkv_shift'][:, None, :]
    sh_rkv8 = _front_pad_rows(shift[:, :, :3 * GROUP_W])
    sh_lora8 = _front_pad_rows(jnp.pad(shift[:, :, 3 * GROUP_W:], ((0, 0), (0, 0), (0, LORA_PAD - RWKV_LORA))))
    o_b, rwkv_s = rwkv_mixer(proj3, sh_rkv8, sh_lora8, past['rwkv'], prm, c_rwkv)
    o_c, ssm_s = ssd_mixer(proj3, _front_pad_rows(past['ssm_conv']), past['ssm'], prm, c_ssd)
    if swa_cache is None:
        o_d = swa_prompt(proj3, k4, v4, li, min(l, SWA_TILE))
    else:
        o_d = swa_sample(proj3, k4, v4, li, swa_cache[0], swa_cache[1])

    gdn_conv = proj3[:, l - (GDN_TAPS - 1):, COL_A_QKV:COL_A_QKV + 3 * GROUP_W]
    ssm_conv = proj3[:, l - (SSM_TAPS - 1):, COL_C_XBC:COL_C_XBC + SSM_XBC]
    rwkv_shift = jnp.concatenate([proj3[:, l - 1, COL_B_RKV:COL_B_RKV + 3 * GROUP_W],
                                  proj3[:, l - 1, COL_B_LORA:COL_B_LORA + RWKV_LORA]], axis=-1)

    y = matmul_groups([o.reshape(m, GROUP_W) for o in (o_a, o_b, o_c, o_d)], wts['w_out_all'], li, tm, 512,
                      "mm_out")
    x2, h2 = add_rms_cast(x2, y, prm['norm_mix_post'], prm['norm_ffn_pre'], tr)
    state8 = _front_pad_rows(past['ffn_conv'])
    if l % tm == 0:
        act, cst = ffn_up_act(h2, wts['w_up_all'], li, state8, prm['ffn_conv_w'], prm['ffn_conv_b'], l, tm, 256,
                              256)
        ffn_conv = cst[:, SUBLANE - (FFN_TAPS - 1):, :]
    else:
        up3 = matmul(h2, wts['w_up_all'], tm, 512, D_MODEL, "mm_up", li).reshape(b, l, 2 * D_FF)
        act = ffn_act(up3, state8, prm['ffn_conv_w'], prm['ffn_conv_b'], l, D_FF // 2).reshape(m, D_FF)
        ffn_conv = up3[:, l - (FFN_TAPS - 1):, :D_FF]
    y2 = matmul(act, wts['w_down_all'], tm, 512, D_FF // 2, "mm_down", li)
    if next_pre_w is None:
        x2, h_next = add_rms(x2, y2, prm['norm_ffn_post'], tr), None
    else:
        x2, h_next = add_rms_cast(x2, y2, prm['norm_ffn_post'], next_pre_w, tr)
    return x2.reshape(b, l, D_MODEL), h_next, (kbuf, vbuf), (gdn_s, gdn_conv, rwkv_s, rwkv_shift, ssm_s, ssm_conv,
                                                             ffn_conv), proj_guest


def _zero_past(bsz):
    return {
        'gdn': jnp.zeros((bsz, GDN_HEADS, GDN_D, GDN_D), F32),
        'gdn_conv': jnp.zeros((bsz, GDN_TAPS - 1, 3 * GROUP_W), F32),
        'rwkv': jnp.zeros((bsz, RWKV_HEADS, RWKV_HS, RWKV_HS), F32),
        'rwkv_shift': jnp.zeros((bsz, 3 * GROUP_W + RWKV_LORA), F32),
        'ssm': jnp.zeros((bsz, SSM_HEADS, SSM_P, SSM_N), F32),
        'ssm_conv': jnp.zeros((bsz, SSM_TAPS - 1, SSM_XBC), F32),
        'ffn_conv': jnp.zeros((bsz, FFN_TAPS - 1, D_FF), F32),
    }


PARAM_NAMES = ('norm_mix_pre', 'norm_mix_post', 'norm_ffn_pre', 'norm_ffn_post', 'w_in', 'w_out', 'gdn_conv_w',
               'gdn_A_log', 'gdn_dt_bias', 'gdn_norm_w', 'rwkv_mu', 'rwkv_w0', 'rwkv_w2', 'rwkv_a0', 'rwkv_a2',
               'rwkv_g2', 'rwkv_k_k', 'rwkv_k_a', 'rwkv_r_k', 'rwkv_ln_w', 'rwkv_ln_b', 'ssm_conv_w', 'ssm_conv_b',
               'ssm_dt_bias', 'ssm_A_log', 'ssm_D', 'ssm_norm_w', 'ffn_w_up', 'ffn_conv_w', 'ffn_conv_b',
               'ffn_w_down')


def kernel(x_prompt, x_sample, state_gdn, state_gdn_conv, state_rwkv, state_rwkv_shift, state_ssm, state_ssm_conv, cache_swa_k, cache_swa_v, state_ffn_conv, norm_mix_pre, norm_mix_post, norm_ffn_pre, norm_ffn_post, w_in, w_out, gdn_conv_w, gdn_A_log, gdn_dt_bias, gdn_norm_w, rwkv_mu, rwkv_w0, rwkv_w2, rwkv_a0, rwkv_a2, rwkv_g2, rwkv_k_k, rwkv_k_a, rwkv_r_k, rwkv_ln_w, rwkv_ln_b, ssm_conv_w, ssm_conv_b, ssm_dt_bias, ssm_A_log, ssm_D, ssm_norm_w, ffn_w_up, ffn_conv_w, ffn_conv_b, ffn_w_down):
    params = dict(zip(PARAM_NAMES, (norm_mix_pre, norm_mix_post, norm_ffn_pre, norm_ffn_post, w_in, w_out,
                                    gdn_conv_w, gdn_A_log, gdn_dt_bias, gdn_norm_w, rwkv_mu, rwkv_w0, rwkv_w2,
                                    rwkv_a0, rwkv_a2, rwkv_g2, rwkv_k_k, rwkv_k_a, rwkv_r_k, rwkv_ln_w, rwkv_ln_b,
                                    ssm_conv_w, ssm_conv_b, ssm_dt_bias, ssm_A_log, ssm_D, ssm_norm_w, ffn_w_up,
                                    ffn_conv_w, ffn_conv_b, ffn_w_down)))
    depth = w_in.shape[0]
    xp, xs = x_prompt, x_sample
    t_dec = x_sample.shape[1]
    prompt_states, sample_states = [], []
    hp = hs = None
    kvp = kvs = (None, None)
    wts = {'w_in_t_all': jnp.swapaxes(w_in, 1, 2), 'w_out_all': w_out, 'w_up_all': ffn_w_up,
           'w_down_all': ffn_w_down.astype(BF16)}
    for li in range(depth):
        prm = {k: v[li] for k, v in params.items()}
        nxt = norm_mix_pre[li + 1] if li + 1 < depth else None
        if hs is None:
            ms = xs.shape[0] * xs.shape[1]
            hs = rms_cast(xs.reshape(ms, D_MODEL), prm['norm_mix_pre'], ms)
        xp, hp, kvp, stp, proj_s = decoder_layer(xp, prm, wts, _zero_past(xp.shape[0]), None, (64, 64, 128),
                                                 li, depth, kvp, hp, nxt, h_guest=hs)
        past = {'gdn': state_gdn[li], 'gdn_conv': state_gdn_conv[li], 'rwkv': state_rwkv[li],
                'rwkv_shift': state_rwkv_shift[li], 'ssm': state_ssm[li], 'ssm_conv': state_ssm_conv[li],
                'ffn_conv': state_ffn_conv[li]}
        xs, hs, kvs, sts, _ = decoder_layer(xs, prm, wts, past, (cache_swa_k, cache_swa_v),
                                            (t_dec, t_dec, t_dec), li, depth, kvs, hs, nxt, proj=proj_s)
        prompt_states.append(stp)
        sample_states.append(sts)

    def window_rows(buf, x):
        bsz, l = x.shape[0], x.shape[1]
        rows = buf.reshape(depth, bsz, l, SWA_HEADS, SWA_HD)
        return rows[:, :, max(l - SWA_MAX_WINDOW, 0):]

    def outputs(states, kv, x):
        st = [jnp.stack(t) for t in zip(*states)]
        return (*st[:6], window_rows(kv[0], x), window_rows(kv[1], x), st[6])

    return (xp, xs, *outputs(prompt_states, kvp, x_prompt), *outputs(sample_states, kvs, x_sample))
```
